```python
import jax, jax.numpy as jnp
from jax import lax
import numpy as np

D_MODEL = 1024
BATCH = 32
SEQ = 256
DEPTH = 2
DEC_BATCH = 8
DEC_SEQ = 2048
PAST_LEN = 256

GRID_W = 64
N_MOD = 6
NORM_EPS = 1e-6
HEAD_A = 64
H_A = D_MODEL // HEAD_A
DECAY_LORA = 64
AAA_LORA = 64
GATE_LORA = 160
LNX_EPS = 64e-5
N_A_LAYERS = (DEPTH + 1) // 2
H_B = 4
DK_B = D_MODEL // H_B
DV_B = 2 * DK_B
RET_CHUNK = 128
ROPE_BASE = 10000.0
N_B_LAYERS = DEPTH // 2
N_EXPERTS = 64
TOP_K = 8
N_GROUPS = 8
TOPK_GROUPS = 4
D_EXPERT = 256
D_SHARED = 256
ROUTED_SCALE = 2.5
EXPERT_BLOCK = 128

kernel_name = 'bidir_rwkv7_retnet_moe_dit_step'


def rms_norm(x, g):
    xf = x.astype(jnp.float32)
    y = xf * lax.rsqrt(jnp.mean(xf * xf, -1, keepdims=True) + NORM_EPS)
    return (y * g.astype(jnp.float32)).astype(x.dtype)


def modulate(x, shift, scale):
    return x * (1 + scale) + shift


def shift_1d(x):
    half = x.shape[-1] // 2
    prev = jnp.pad(x[:, :-1, :half], ((0, 0), (1, 0), (0, 0)))
    nxt = jnp.pad(x[:, 1:, half:], ((0, 0), (0, 1), (0, 0)))
    return jnp.concatenate([prev, nxt], -1)


def shift_grid(x, rows):
    bsz, n_tok, d = x.shape
    q = d // 4
    xg = x.reshape(bsz, rows, GRID_W, d)
    left = jnp.pad(xg[:, :, :-1, :q], ((0, 0), (0, 0), (1, 0), (0, 0)))
    right = jnp.pad(xg[:, :, 1:, q:2 * q], ((0, 0), (0, 0), (0, 1), (0, 0)))
    up = jnp.pad(xg[:, :-1, :, 2 * q:3 * q], ((0, 0), (1, 0), (0, 0), (0, 0)))
    down = jnp.pad(xg[:, 1:, :, 3 * q:], ((0, 0), (0, 1), (0, 0), (0, 0)))
    return jnp.concatenate([left, right, up, down], -1).reshape(bsz, n_tok, d)


def axial_rope(x, rows):
    half = x.shape[-1] // 2
    n_freq = half // 2
    inv = ROPE_BASE ** (-jnp.arange(n_freq, dtype=jnp.float32) / n_freq)
    pos_r = jnp.repeat(jnp.arange(rows, dtype=jnp.float32), GRID_W)
    pos_c = jnp.tile(jnp.arange(GRID_W, dtype=jnp.float32), rows)

    def rot(xp, pos):
        ang = pos[:, None] * inv[None, :]
        cos = jnp.cos(ang)[None, :, None, :]
        sin = jnp.sin(ang)[None, :, None, :]
        x1, x2 = xp[..., :n_freq], xp[..., n_freq:]
        return jnp.concatenate([x1 * cos - x2 * sin, x1 * sin + x2 * cos], -1)

    return jnp.concatenate([rot(x[..., :half], pos_r), rot(x[..., half:], pos_c)], -1)


def rwkv_step(S, inp):
    r, w, k, v, kk, a = inp
    sa = jnp.einsum('bhvk,bhk->bhv', S, -kk)
    S = S * w[:, :, None, :] + sa[..., None] * (kk * a)[:, :, None, :] + v[..., None] * k[:, :, None, :]
    return S, jnp.einsum('bhvk,bhk->bhv', S, r)


def rwkv_mixer(h, h_shift, s0, mu, w0, w1, w2, a0, a1, a2, wrkv, wo, g1, g2, k_k, k_a, r_k, ln_w, ln_b):
    bsz, n_tok, d = h.shape
    f32 = jnp.float32
    heads = lambda t: t.astype(f32).reshape(bsz, n_tok, H_A, HEAD_A)
    xx = h_shift - h
    xr, xw, xk, xv, xa, xg = [h + xx * mu[i] for i in range(N_MOD)]
    r = heads(xr @ wrkv[0])
    k = heads(xk @ wrkv[1])
    v = heads(xv @ wrkv[2])
    g = (jax.nn.sigmoid(xg @ g1) @ g2).astype(f32)
    kk = k * k_k.astype(f32).reshape(H_A, HEAD_A)
    kk = kk / jnp.maximum(jnp.sqrt(jnp.sum(kk * kk, -1, keepdims=True)), 1e-12)
    k_a_h = k_a.astype(f32).reshape(H_A, HEAD_A)
    ys, bonuses, finals = [], [], []
    for d_ in range(2):
        w = -jax.nn.softplus(-(w0[d_] + jnp.tanh(xw @ w1[d_]) @ w2[d_]).astype(f32)) - 0.5
        decay = heads(jnp.exp(-jnp.exp(w)))
        a = heads(jax.nn.sigmoid((a0[d_] + (xa @ a1[d_]) @ a2[d_]).astype(f32)))
        kd = k * (1 + (a - 1) * k_a_h)
        seq_in = tuple(jnp.moveaxis(t, 1, 0) for t in (r, decay, kd, v, kk, a))
        s_fin, yd = lax.scan(rwkv_step, s0[:, d_].astype(f32), seq_in, reverse=(d_ == 1))
        ys.append(jnp.moveaxis(yd, 0, 1))
        bonuses.append(jnp.sum(r * kd * r_k.astype(f32), -1, keepdims=True) * v)
        finals.append(s_fin)
    y = ys[0] + ys[1]
    mean = jnp.mean(y, -1, keepdims=True)
    var = jnp.mean(jnp.square(y - mean), -1, keepdims=True)
    y = ((y - mean) * lax.rsqrt(var + LNX_EPS)).reshape(bsz, n_tok, d) * ln_w.astype(f32) + ln_b.astype(f32)
    y = (y + (bonuses[0] + bonuses[1]).reshape(bsz, n_tok, d)) * g
    return y.astype(h.dtype) @ wo, jnp.stack(finals, 1)


def retention_chunked(q, k, v, log_gamma, s0):
    bsz, n_tok, n_head, _ = q.shape
    dv = v.shape[-1]
    n_chunk = n_tok // RET_CHUNK
    chunks = lambda t: jnp.moveaxis(t.reshape(bsz, n_chunk, RET_CHUNK, n_head, t.shape[-1]), 1, 0)
    pos = jnp.arange(RET_CHUNK, dtype=jnp.float32)
    rel = pos[:, None] - pos[None, :]
    decay_mask = jnp.where(rel[None] >= 0, jnp.exp(jnp.maximum(rel, 0.0)[None] * log_gamma[:, None, None]), 0.0)
    q_decay = jnp.exp((pos + 1)[:, None] * log_gamma[None, :])
    k_decay = jnp.exp((RET_CHUNK - 1 - pos)[:, None] * log_gamma[None, :])
    chunk_decay = jnp.exp(RET_CHUNK * log_gamma)

    def step(S, inp):
        qc, kc, vc = inp
        scores = jnp.einsum('bihd,bjhd->bhij', qc, kc) * decay_mask[None]
        inner = jnp.einsum('bhij,bjhv->bihv', scores, vc)
        cross = jnp.einsum('bihd,bhdv->bihv', qc, S) * q_decay[None, :, :, None]
        S_new = S * chunk_decay[None, :, None, None] + jnp.einsum('bjhd,bjhv->bhdv', kc * k_decay[None, :, :, None], vc)
        return S_new, inner + cross

    s_fin, out = lax.scan(step, s0, (chunks(q), chunks(k), chunks(v)))
    return jnp.moveaxis(out, 0, 1).reshape(bsz, n_tok, n_head, dv), s_fin


def retention_mixer(h, s0, w_in, w_out, decay_logit, rows):
    bsz, n_tok, _ = h.shape
    f32 = jnp.float32
    proj = h @ w_in
    q, k, v, g = jnp.split(proj, [H_B * DK_B, 2 * H_B * DK_B, 2 * H_B * DK_B + H_B * DV_B], axis=-1)
    q = q.astype(f32).reshape(bsz, n_tok, H_B, DK_B)
    k = k.astype(f32).reshape(bsz, n_tok, H_B, DK_B) * DK_B ** -0.5
    v = v.astype(f32).reshape(bsz, n_tok, H_B, DV_B)
    if rows is not None:
        q = axial_rope(q, rows)
        k = axial_rope(k, rows)
    log_gamma = jax.nn.log_sigmoid(decay_logit.astype(f32))
    o_f, s_f = retention_chunked(q, k, v, log_gamma[0], s0[:, 0].astype(f32))
    o_b, s_b = retention_chunked(q[:, ::-1], k[:, ::-1], v[:, ::-1], log_gamma[1], s0[:, 1].astype(f32))
    o = o_f + o_b[:, ::-1]
    o = o * lax.rsqrt(jnp.mean(o * o, -1, keepdims=True) + NORM_EPS)
    y = jax.nn.silu(g.astype(f32)) * o.reshape(bsz, n_tok, H_B * DV_B)
    return y.astype(h.dtype) @ w_out, jnp.stack([s_f, s_b], 1)


def routed_experts(xf, idx, wts, w_gu, w_down):
    n_tok, d = xf.shape
    n_assign = n_tok * TOP_K
    flat_e = idx.reshape(-1)
    flat_t = jnp.arange(n_assign, dtype=jnp.int32) // TOP_K
    order = jnp.argsort(flat_e)
    e_sorted = flat_e[order]
    counts = jnp.bincount(flat_e, length=N_EXPERTS)
    padded = (counts + EXPERT_BLOCK - 1) // EXPERT_BLOCK * EXPERT_BLOCK
    pad_end = jnp.cumsum(padded)
    pad_start = pad_end - padded
    start = jnp.cumsum(counts) - counts
    dest = pad_start[e_sorted] + jnp.arange(n_assign, dtype=jnp.int32) - start[e_sorted]
    n_blocks = -(-n_assign // EXPERT_BLOCK) + N_EXPERTS
    n_slots = n_blocks * EXPERT_BLOCK
    tok_buf = jnp.full((n_slots,), n_tok, jnp.int32).at[dest].set(flat_t[order])
    wt_buf = jnp.zeros((n_slots,), jnp.float32).at[dest].set(wts.reshape(-1)[order])
    block_e = jnp.minimum(jnp.searchsorted(pad_end, jnp.arange(n_blocks, dtype=jnp.int32) * EXPERT_BLOCK, side='right'), N_EXPERTS - 1)
    x_pad = jnp.concatenate([xf, jnp.zeros((1, d), xf.dtype)], 0)

    def block(args):
        ids, e, bw = args
        xb = x_pad[ids]
        gate, up = jnp.split(xb @ w_gu[e], 2, axis=-1)
        return ((jax.nn.silu(gate) * up) @ w_down[e]) * bw[:, None].astype(xb.dtype)

    out = lax.map(block, (tok_buf.reshape(n_blocks, EXPERT_BLOCK), block_e, wt_buf.reshape(n_blocks, EXPERT_BLOCK)))
    y = jnp.zeros((n_tok + 1, d), out.dtype).at[tok_buf].add(out.reshape(n_slots, d))
    return y[:n_tok]


def moe_ffn(h, router, bias, w_gu, w_down, sh_gu, sh_down):
    bsz, n_tok, d = h.shape
    xf = h.reshape(bsz * n_tok, d)
    scores = jax.nn.sigmoid(xf.astype(jnp.float32) @ router.astype(jnp.float32))
    biased = scores + bias.astype(jnp.float32)
    per_group = N_EXPERTS // N_GROUPS
    group_score = lax.top_k(biased.reshape(-1, N_GROUPS, per_group), 2)[0].sum(-1)
    top_groups = lax.top_k(group_score, TOPK_GROUPS)[1]
    group_mask = jax.nn.one_hot(top_groups, N_GROUPS, dtype=jnp.float32).sum(1) > 0
    expert_mask = jnp.repeat(group_mask, per_group, axis=1)
    idx = lax.top_k(jnp.where(expert_mask, biased, -jnp.inf), TOP_K)[1]
    w = jnp.take_along_axis(scores, idx, axis=-1)
    w = w / jnp.sum(w, -1, keepdims=True) * ROUTED_SCALE
    routed = routed_experts(xf, idx, w, w_gu, w_down)
    gate, up = jnp.split(xf @ sh_gu, 2, axis=-1)
    shared = (jax.nn.silu(gate) * up) @ sh_down
    return (routed + shared).reshape(bsz, n_tok, d)


def setup_inputs(seed: int = 0) -> dict:
    key = jax.random.key(seed)
    ks = iter(jax.random.split(key, 48))
    nrm = lambda shape, scale: scale * jax.random.normal(next(ks), shape, jnp.float32)
    D = D_MODEL
    decay_init = jnp.log(2.0 ** (5.0 + jnp.arange(H_B, dtype=jnp.float32)) - 1.0)
    return {
        'x_prompt': nrm((BATCH, SEQ, D), 1.0),
        'x_sample': nrm((DEC_BATCH, DEC_SEQ, D), 1.0),
        'state_rwkv': nrm((DEC_BATCH, N_A_LAYERS, 2, H_A, HEAD_A, HEAD_A), 0.1),
        'state_ret': nrm((DEC_BATCH, N_B_LAYERS, 2, H_B, DK_B, DV_B), 0.1),
        'c': nrm((DEC_BATCH, D), 1.0),
        'c_ctx': nrm((D,), 1.0),
        'ada_w': nrm((DEPTH, D, N_MOD * D), 0.5 * D ** -0.5),
        'ada_b': nrm((DEPTH, N_MOD * D), 0.02),
        'norm_mix': 1.0 + nrm((DEPTH, D), 0.02),
        'norm_ffn': 1.0 + nrm((DEPTH, D), 0.02),
        'norm_final': 1.0 + nrm((D,), 0.02),
        'rwkv_mu': jax.random.uniform(next(ks), (N_A_LAYERS, N_MOD, D), jnp.float32),
        'rwkv_w0': -1.0 + nrm((N_A_LAYERS, 2, D), 0.5),
        'rwkv_w1': nrm((N_A_LAYERS, 2, D, DECAY_LORA), D ** -0.5),
        'rwkv_w2': nrm((N_A_LAYERS, 2, DECAY_LORA, D), 0.1 * DECAY_LORA ** -0.5),
        'rwkv_a0': nrm((N_A_LAYERS, 2, D), 0.5),
        'rwkv_a1': nrm((N_A_LAYERS, 2, D, AAA_LORA), D ** -0.5),
        'rwkv_a2': nrm((N_A_LAYERS, 2, AAA_LORA, D), 0.1 * AAA_LORA ** -0.5),
        'rwkv_wrkv': nrm((N_A_LAYERS, 3, D, D), D ** -0.5),
        'rwkv_wo': nrm((N_A_LAYERS, D, D), D ** -0.5),
        'rwkv_g1': nrm((N_A_LAYERS, D, GATE_LORA), D ** -0.5),
        'rwkv_g2': nrm((N_A_LAYERS, GATE_LORA, D), GATE_LORA ** -0.5),
        'rwkv_k_k': 0.85 + nrm((N_A_LAYERS, D), 0.05),
        'rwkv_k_a': 1.0 + nrm((N_A_LAYERS, D), 0.05),
        'rwkv_r_k': nrm((N_A_LAYERS, H_A, HEAD_A), 0.1),
        'rwkv_ln_w': 1.0 + nrm((N_A_LAYERS, D), 0.02),
        'rwkv_ln_b': nrm((N_A_LAYERS, D), 0.02),
        'ret_w_in': nrm((N_B_LAYERS, D, 2 * H_B * DK_B + 2 * H_B * DV_B), D ** -0.5),
        'ret_w_out': nrm((N_B_LAYERS, H_B * DV_B, D), (H_B * DV_B) ** -0.5),
        'ret_decay_logit': decay_init[None, None, :] + nrm((N_B_LAYERS, 2, H_B), 0.05),
        'moe_router': nrm((DEPTH, D, N_EXPERTS), D ** -0.5),
        'moe_bias': nrm((DEPTH, N_EXPERTS), 0.01),
        'moe_w_gu': nrm((DEPTH, N_EXPERTS, D, 2 * D_EXPERT), D ** -0.5),
        'moe_w_down': nrm((DEPTH, N_EXPERTS, D_EXPERT, D), D_EXPERT ** -0.5),
        'moe_sh_gu': nrm((DEPTH, D, 2 * D_SHARED), D ** -0.5),
        'moe_sh_down': nrm((DEPTH, D_SHARED, D), D_SHARED ** -0.5),
    }


def reference(x_prompt, x_sample, state_rwkv, state_ret, c, c_ctx, ada_w, ada_b, norm_mix, norm_ffn, norm_final,
              rwkv_mu, rwkv_w0, rwkv_w1, rwkv_w2, rwkv_a0, rwkv_a1, rwkv_a2, rwkv_wrkv, rwkv_wo, rwkv_g1, rwkv_g2,
              rwkv_k_k, rwkv_k_a, rwkv_r_k, rwkv_ln_w, rwkv_ln_b, ret_w_in, ret_w_out, ret_decay_logit,
              moe_router, moe_bias, moe_w_gu, moe_w_down, moe_sh_gu, moe_sh_down):
    rows = x_sample.shape[1] // GRID_W
    xp, xs = x_prompt, x_sample
    cond_ctx = jax.nn.silu(c_ctx)[None, :]
    cond_lat = jax.nn.silu(c)
    new_rwkv, new_ret = [], []
    for layer in range(DEPTH):
        j = layer // 2
        mod_p = jnp.split((cond_ctx @ ada_w[layer] + ada_b[layer])[:, None, :], N_MOD, axis=-1)
        mod_s = jnp.split((cond_lat @ ada_w[layer] + ada_b[layer])[:, None, :], N_MOD, axis=-1)
        hp = modulate(rms_norm(xp, norm_mix[layer]), mod_p[0], mod_p[1])
        hs = modulate(rms_norm(xs, norm_mix[layer]), mod_s[0], mod_s[1])
        if layer % 2 == 0:
            rw = (rwkv_mu[j], rwkv_w0[j], rwkv_w1[j], rwkv_w2[j], rwkv_a0[j], rwkv_a1[j], rwkv_a2[j], rwkv_wrkv[j],
                  rwkv_wo[j], rwkv_g1[j], rwkv_g2[j], rwkv_k_k[j], rwkv_k_a[j], rwkv_r_k[j], rwkv_ln_w[j], rwkv_ln_b[j])
            s_zero = jnp.zeros((xp.shape[0], 2, H_A, HEAD_A, HEAD_A), jnp.float32)
            op, sp = rwkv_mixer(hp, shift_1d(hp), s_zero, *rw)
            os_, _ = rwkv_mixer(hs, shift_grid(hs, rows), state_rwkv[:, j], *rw)
            new_rwkv.append(sp)
        else:
            s_zero = jnp.zeros((xp.shape[0], 2, H_B, DK_B, DV_B), jnp.float32)
            op, sp = retention_mixer(hp, s_zero, ret_w_in[j], ret_w_out[j], ret_decay_logit[j], None)
            os_, _ = retention_mixer(hs, state_ret[:, j], ret_w_in[j], ret_w_out[j], ret_decay_logit[j], rows)
            new_ret.append(sp)
        xp = xp + mod_p[2] * op
        xs = xs + mod_s[2] * os_
        moe_w = (moe_router[layer], moe_bias[layer], moe_w_gu[layer], moe_w_down[layer], moe_sh_gu[layer], moe_sh_down[layer])
        hp = modulate(rms_norm(xp, norm_ffn[layer]), mod_p[3], mod_p[4])
        hs = modulate(rms_norm(xs, norm_ffn[layer]), mod_s[3], mod_s[4])
        xp = xp + mod_p[5] * moe_ffn(hp, *moe_w)
        xs = xs + mod_s[5] * moe_ffn(hs, *moe_w)
    y_prompt = rms_norm(xp, norm_final)
    y_sample = rms_norm(xs, norm_final)
    new_state_rwkv = jnp.stack(new_rwkv, 1)
    new_state_ret = jnp.stack(new_ret, 1)
    return (y_prompt, y_sample, new_state_rwkv, new_state_ret)
```

```python
import functools

import jax
import jax.numpy as jnp
from jax import lax
from jax.experimental import pallas as pl
from jax.experimental.pallas import tpu as pltpu

F32, BF16, I32 = jnp.float32, jnp.bfloat16, jnp.int32

D = 1024
N_MOD = 6
NORM_EPS = 1e-6
GRID_W = 64
HEAD_A = 64
H_A = D // HEAD_A
LNX_EPS = 64e-5
RWKV_CHUNK = 64
N_PAIR = H_A // 2
H_B = 4
DK_B = D // H_B
DV_B = 2 * DK_B
RET_CHUNK = 128
ROPE_BASE = 10000.0
N_EXPERTS = 64
TOP_K = 8
N_GROUPS = 8
TOPK_GROUPS = 4
PER_GROUP = N_EXPERTS // N_GROUPS
D_EXPERT = 256
ROUTED_SCALE = 2.5
EXPERT_BLOCK = 128

TM = 256
LANES = 128
VMEM_LIMIT = 56 * 1024 * 1024


def _cparams(n_grid_axes, **kw):
    return pltpu.CompilerParams(dimension_semantics=("arbitrary",) * n_grid_axes, vmem_limit_bytes=VMEM_LIMIT, **kw)


def _bdot(a, b):
    return jnp.dot(a.astype(BF16), b.astype(BF16), preferred_element_type=F32)


def _bdot_nt(a, b):
    return lax.dot_general(a.astype(BF16), b.astype(BF16), (((1,), (1,)), ((), ())), preferred_element_type=F32)


def _bdot_tn(a, b):
    return lax.dot_general(a.astype(BF16), b.astype(BF16), (((0,), (0,)), ((), ())), preferred_element_type=F32)


def _split3(x):
    hi = x.astype(BF16)
    r1 = x - hi.astype(F32)
    mid = r1.astype(BF16)
    lo = (r1 - mid.astype(F32)).astype(BF16)
    return hi, mid, lo


def _dot_f32(a, b):
    ah, am, al = _split3(a)
    bh, bm, bl = _split3(b)
    d = lambda x, y: jnp.dot(x, y, preferred_element_type=F32)
    return d(ah, bh) + (d(ah, bm) + d(am, bh)) + (d(ah, bl) + d(al, bh) + d(am, bm))


def _dot_f32_rhs01(a, m01):
    ah, am, al = _split3(a)
    d = lambda x: jnp.dot(x, m01, preferred_element_type=F32)
    return d(ah) + d(am) + d(al)


def _dot_f32_lhs01(m01, b):
    bh, bm, bl = _split3(b)
    d = lambda x: jnp.dot(m01, x, preferred_element_type=F32)
    return d(bh) + d(bm) + d(bl)


def _norm_mod(x, g, shift, scale):
    ms = jnp.mean(x * x, axis=-1, keepdims=True)
    y = x * lax.rsqrt(ms + NORM_EPS) * g
    return y * (1.0 + scale) + shift


def _silu(x):
    return x * jax.nn.sigmoid(x)


def _mod_row(i, p_tiles, tiles_per_seq):
    return jnp.where(i < p_tiles, 0, 1 + (i - p_tiles) // tiles_per_seq)


def _mod_kernel(cond_ref, w_ref, b_ref, o_ref):
    o_ref[...] = _dot_f32(_silu(cond_ref[...]), w_ref[...]) + b_ref[...]


def _modulation(cond, ada_w, ada_b):
    depth, _, n6 = ada_w.shape
    tn = 1536
    return pl.pallas_call(
        _mod_kernel,
        grid=(depth, n6 // tn),
        in_specs=[
            pl.BlockSpec(cond.shape, lambda l, j: (0, 0)),
            pl.BlockSpec((None, D, tn), lambda l, j: (l, 0, j)),
            pl.BlockSpec((None, 1, tn), lambda l, j: (l, 0, j)),
        ],
        out_specs=pl.BlockSpec((None, cond.shape[0], tn), lambda l, j: (l, 0, j)),
        out_shape=jax.ShapeDtypeStruct((depth, cond.shape[0], n6), F32),
        compiler_params=_cparams(2),
        name="adaln_mod",
    )(cond, ada_w, ada_b.reshape(depth, 1, n6))


def _rwkv_pre_kernel(x_ref, xu_ref, xd_ref, mod_ref, g_ref, mu_ref, wrkv_ref, g1_ref, g2_ref, w1_ref, w2_ref,
                     w0_ref, a1_ref, a2_ref, a0_ref, kk_ref, ka_ref, rk_ref, hsum_ref, hexp_ref,
                     r_o, v_o, kk_o, lw_o, kd_o, b_o, gate_o, bonus_o, *, p_tiles, tiles_per_seq):
    i = pl.program_id(0)
    is_p = i < p_tiles
    sub = (i - p_tiles) % tiles_per_seq
    mod = mod_ref[...]
    shift, scale = mod[:, 0:D], mod[:, D:2 * D]
    g = g_ref[...]
    h = _norm_mod(x_ref[...], g, shift, scale)
    hu = _norm_mod(xu_ref[...], g, shift, scale)
    hd = _norm_mod(xd_ref[...], g, shift, scale)

    q = D // 4
    row = lax.broadcasted_iota(I32, (TM, 1), 0)
    per = jnp.where(is_p, TM, GRID_W)
    pos = row & (per - 1)

    def prev1(a):
        return jnp.where(pos == 0, 0.0, pltpu.roll(a, 1, 0))

    def next1(a):
        return jnp.where(pos == per - 1, 0.0, pltpu.roll(a, TM - 1, 0))

    h0, h1, h2, h3 = (h[:, j * q:(j + 1) * q] for j in range(4))
    up = jnp.concatenate([jnp.where(sub == 0, 0.0, hu[:, 2 * q:3 * q]), h2[0:TM - GRID_W]], axis=0)
    down = jnp.concatenate([h3[GRID_W:TM], jnp.where(sub == tiles_per_seq - 1, 0.0, hd[:, 3 * q:4 * q])], axis=0)
    s0 = prev1(h0)
    s1 = jnp.where(is_p, prev1(h1), next1(h1))
    s2 = jnp.where(is_p, next1(h2), up)
    s3 = jnp.where(is_p, next1(h3), down)
    xx = jnp.concatenate([s0, s1, s2, s3], axis=1) - h

    mu = mu_ref[...]
    mix = lambda j: h + xx * mu[j:j + 1]
    r = _bdot(mix(0), wrkv_ref[0])
    k = _bdot(mix(2), wrkv_ref[1])
    v = _bdot(mix(3), wrkv_ref[2])
    gate = _bdot(jax.nn.sigmoid(_bdot(mix(5), g1_ref[...])), g2_ref[...])
    w_all = w0_ref[...] + _bdot(jnp.tanh(_bdot(mix(1), w1_ref[...])), w2_ref[...])
    a_all = jax.nn.sigmoid(a0_ref[...] + _bdot(_bdot(mix(4), a1_ref[...]), a2_ref[...]))

    hsum, hexp = hsum_ref[...], hexp_ref[...]
    head_sum = lambda t: _dot_f32_rhs01(_dot_f32_rhs01(t, hsum), hexp)

    kkr = k * kk_ref[...]
    kk = kkr / jnp.maximum(jnp.sqrt(head_sum(kkr * kkr)), 1e-12)
    ka = ka_ref[...]
    kd_sum = jnp.zeros_like(k)
    for d in range(2):
        wd = w_all[:, d * D:(d + 1) * D]
        z = -wd
        softplus = jnp.maximum(z, 0.0) + jnp.log(1.0 + jnp.exp(-jnp.abs(z)))
        lw_o[d] = -jnp.exp(-softplus - 0.5)
        a = a_all[:, d * D:(d + 1) * D]
        kd = k * (1.0 + (a - 1.0) * ka)
        kd_o[d] = kd
        b_o[d] = kk * a
        kd_sum = kd_sum + kd
    r_o[...] = r
    v_o[...] = v
    kk_o[...] = kk
    gate_o[...] = gate
    bonus_o[...] = head_sum(r * kd_sum * rk_ref[...]) * v


def _rwkv_pre(x, mod, p_tiles, tiles_per_seq, norm_g, wts):
    n = x.shape[0]
    n_tiles = n // TM
    hb = TM // GRID_W
    n_hblk = n // GRID_W
    row = lambda a: a.reshape(1, -1)
    full = lambda a: pl.BlockSpec(a.shape, lambda i: (0,) * a.ndim)
    tok = pl.BlockSpec((TM, D), lambda i: (i, 0))
    tok2 = pl.BlockSpec((2, TM, D), lambda i: (0, i, 0))
    consts = [row(norm_g), wts["mu"], wts["wrkv"], wts["g1"], wts["g2"], wts["w1"], wts["w2"], wts["w0"],
              wts["a1"], wts["a2"], wts["a0"], row(wts["k_k"]), row(wts["k_a"]), row(wts["r_k"]),
              wts["hsum"], wts["hexp"]]
    return pl.pallas_call(
        functools.partial(_rwkv_pre_kernel, p_tiles=p_tiles, tiles_per_seq=tiles_per_seq),
        grid=(n_tiles,),
        in_specs=[
            tok,
            pl.BlockSpec((GRID_W, D), lambda i: (jnp.maximum(i * hb - 1, 0), 0)),
            pl.BlockSpec((GRID_W, D), lambda i: (jnp.minimum(i * hb + hb, n_hblk - 1), 0)),
            pl.BlockSpec((None, 1, N_MOD * D), lambda i: (_mod_row(i, p_tiles, tiles_per_seq), 0, 0)),
        ] + [full(a) for a in consts],
        out_specs=[tok, tok, tok, tok2, tok2, tok2, tok, tok],
        out_shape=[jax.ShapeDtypeStruct((n, D), F32)] * 3 + [jax.ShapeDtypeStruct((2, n, D), F32)] * 3
        + [jax.ShapeDtypeStruct((n, D), F32)] * 2,
        compiler_params=_cparams(1),
        name="rwkv_pre",
    )(x, x, x, mod, *consts)


class _ScanItems:
    def __init__(self, n_ctx, ctx_chunks, n_lat, lat_chunks):
        self.n_ctx, self.ctx_chunks, self.n_lat, self.lat_chunks = n_ctx, ctx_chunks, n_lat, lat_chunks
        self.ctx_items = n_ctx * ctx_chunks
        self.n_items = self.ctx_items + n_lat * lat_chunks

    def decode(self, j):
        is_ctx = j < self.ctx_items
        jl = jnp.maximum(j - self.ctx_items, 0)
        c = jnp.where(is_ctx, j % self.ctx_chunks, jl % self.lat_chunks)
        return is_ctx, c, jnp.where(is_ctx, self.ctx_chunks, self.lat_chunks)

    def block(self, d, j):
        _, c, nc = self.decode(j)
        return jnp.where(d == 0, j, j - c + (nc - 1 - c))

    def ctx_seq(self, j):
        return jnp.minimum(j // self.ctx_chunks, self.n_ctx - 1)

    def lat_seq(self, j):
        return jnp.maximum(j - self.ctx_items, 0) // self.lat_chunks


def _rwkv_scan_kernel(r_ref, v_ref, kk_ref, lw_ref, kd_ref, b_ref, s0_ref, y_ref, sf_ref, s_scr, *, items):
    d = pl.program_id(0)
    is_ctx, c, nc = items.decode(pl.program_id(1))
    rev = d == 1
    ch = RWKV_CHUNK

    @pl.when(c == 0)
    def _():
        s_scr[...] = jnp.where(is_ctx, 0.0, s0_ref[...])

    ti = lax.broadcasted_iota(I32, (ch, ch), 0)
    tj = lax.broadcasted_iota(I32, (ch, ch), 1)
    tri = jnp.where(jnp.where(rev, ti - tj, tj - ti) <= 0, 1.0, 0.0).astype(BF16)
    lw = lw_ref[...]
    cum = _dot_f32_lhs01(tri, lw)
    last = jnp.where(rev, cum[0:1], cum[ch - 1:ch])
    kk = kk_ref[...]
    kd = kd_ref[...]
    b = b_ref[...]
    e_neg = jnp.exp(-cum)
    e_rel = jnp.exp(last - cum)
    at = -kk * jnp.exp(cum - lw)
    rt = r_ref[...] * jnp.exp(cum)
    bt = b * e_neg
    kt = kd * e_neg
    bh = b * e_rel
    kh = kd * e_rel
    g_all = jnp.exp(last)
    v = v_ref[...]

    si = lax.broadcasted_iota(I32, (2 * ch, 2 * ch), 0)
    sj = lax.broadcasted_iota(I32, (2 * ch, 2 * ch), 1)
    same = (si < ch) == (sj < ch)
    ui, uj = si & (ch - 1), sj & (ch - 1)
    order = jnp.where(rev, ui - uj, uj - ui)
    strict = same & (order < 0)
    incl = same & (order <= 0)
    first = lax.broadcasted_iota(I32, (1, LANES), 1) < HEAD_A

    def stack(xp):
        return jnp.concatenate([jnp.where(first, xp, 0.0), jnp.where(first, 0.0, xp)], axis=0)

    for p in range(N_PAIR):
        sl = slice(p * LANES, (p + 1) * LANES)
        ps = jnp.concatenate([stack(at[:, sl]), stack(rt[:, sl])], axis=0)
        qs = jnp.concatenate([stack(bt[:, sl]), stack(kt[:, sl])], axis=0)
        m1 = _bdot_nt(ps, qs)
        l_ab = jnp.where(strict, m1[0:2 * ch, 0:2 * ch], 0.0)
        l_ak = jnp.where(strict, m1[0:2 * ch, 2 * ch:4 * ch], 0.0)
        m_rb = jnp.where(incl, m1[2 * ch:4 * ch, 0:2 * ch], 0.0)
        m_rk = jnp.where(incl, m1[2 * ch:4 * ch, 2 * ch:4 * ch], 0.0)
        s = s_scr[p]
        m2 = _bdot_nt(jnp.concatenate([at[:, sl], rt[:, sl]], axis=0), s)
        vs = stack(v[:, sl])
        x = stack(m2[0:ch]) + _bdot(l_ak, vs)
        nm = l_ab
        x = x + _bdot(nm, x)
        for _ in range(5):
            nm = _bdot(nm, nm)
            x = x + _bdot(nm, x)
        uv = jnp.concatenate([x, vs], axis=0)
        ys = _bdot(jnp.concatenate([m_rb, m_rk], axis=1), uv)
        y_ref[:, sl] = ys[0:ch] + ys[ch:2 * ch] + m2[ch:2 * ch]
        qh = jnp.concatenate([stack(bh[:, sl]), stack(kh[:, sl])], axis=0)
        s_scr[p] = s * g_all[:, sl] + _bdot_tn(uv, qh)

    @pl.when(is_ctx & (c == nc - 1))
    def _():
        sf_ref[...] = s_scr[...]


def _rwkv_scan(r, v, kk, lw, kd, b, s0, items):
    n = r.shape[0]
    ch = RWKV_CHUNK
    tok = pl.BlockSpec((ch, D), lambda d, j: (items.block(d, j), 0))
    tokd = pl.BlockSpec((None, ch, D), lambda d, j: (d, items.block(d, j), 0))
    st = lambda seq: pl.BlockSpec((None, None, N_PAIR, LANES, LANES), lambda d, j: (d, seq(j), 0, 0, 0))
    return pl.pallas_call(
        functools.partial(_rwkv_scan_kernel, items=items),
        grid=(2, items.n_items),
        in_specs=[tok, tok, tok, tokd, tokd, tokd, st(items.lat_seq)],
        out_specs=[tokd, st(items.ctx_seq)],
        out_shape=[jax.ShapeDtypeStruct((2, n, D), F32),
                   jax.ShapeDtypeStruct((2, items.n_ctx, N_PAIR, LANES, LANES), F32)],
        scratch_shapes=[pltpu.VMEM((N_PAIR, LANES, LANES), F32)],
        compiler_params=_cparams(2),
        name="rwkv_scan",
    )(r, v, kk, lw, kd, b, s0)


def _rwkv_post_kernel(x_ref, y_ref, gate_ref, bonus_ref, mod_ref, lnw_ref, lnb_ref, wo_ref, hsum_ref, hexp_ref, o_ref):
    hsum, hexp = hsum_ref[...], hexp_ref[...]
    head_mean = lambda t: _dot_f32_rhs01(_dot_f32_rhs01(t, hsum), hexp) * (1.0 / HEAD_A)
    y = y_ref[0] + y_ref[1]
    yc = y - head_mean(y)
    yn = yc * lax.rsqrt(head_mean(yc * yc) + LNX_EPS)
    z = (yn * lnw_ref[...] + lnb_ref[...] + bonus_ref[...]) * gate_ref[...]
    o_ref[...] = x_ref[...] + mod_ref[:, 2 * D:3 * D] * _bdot(z, wo_ref[...])


def _rwkv_post(x, y, gate, bonus, mod, p_tiles, tiles_per_seq, ln_w, ln_b, wo, hsum, hexp):
    n = x.shape[0]
    full = lambda a: pl.BlockSpec(a.shape, lambda i: (0,) * a.ndim)
    tok = pl.BlockSpec((TM, D), lambda i: (i, 0))
    consts = [ln_w.reshape(1, D), ln_b.reshape(1, D), wo, hsum, hexp]
    return pl.pallas_call(
        _rwkv_post_kernel,
        grid=(n // TM,),
        in_specs=[tok, pl.BlockSpec((2, TM, D), lambda i: (0, i, 0)), tok, tok,
                  pl.BlockSpec((None, 1, N_MOD * D), lambda i: (_mod_row(i, p_tiles, tiles_per_seq), 0, 0))]
        + [full(a) for a in consts],
        out_specs=tok,
        out_shape=jax.ShapeDtypeStruct((n, D), F32),
        compiler_params=_cparams(1),
        name="rwkv_post",
    )(x, y, gate, bonus, mod, *consts)


def _ret_pre_kernel(x_ref, mod_ref, g_ref, cos_ref, sin_ref, win_ref, q_o, k_o, v_o, gate_o):
    mod = mod_ref[...]
    h = _norm_mod(x_ref[...], g_ref[...], mod[:, 0:D], mod[:, D:2 * D])
    proj = _bdot(h, win_ref[...])
    cos, sin = cos_ref[...], sin_ref[...]

    def rope(t):
        outs = []
        for j in range(D // LANES):
            tj = t[:, j * LANES:(j + 1) * LANES]
            cj = cos[:, (j % 2) * LANES:(j % 2 + 1) * LANES]
            sj = sin[:, (j % 2) * LANES:(j % 2 + 1) * LANES]
            outs.append(tj * cj + pltpu.roll(tj, LANES // 2, 1) * sj)
        return jnp.concatenate(outs, axis=1)

    q_o[...] = rope(proj[:, 0:D]).astype(BF16)
    k_o[...] = rope(proj[:, D:2 * D] * (DK_B ** -0.5)).astype(BF16)
    v_o[...] = proj[:, 2 * D:4 * D].astype(BF16)
    gate_o[...] = proj[:, 4 * D:6 * D]


def _ret_pre(x, mod, p_tiles, tiles_per_seq, norm_g, cos_t, sin_t, w_in):
    n = x.shape[0]
    full = lambda a: pl.BlockSpec(a.shape, lambda i: (0,) * a.ndim)
    tok = lambda w: pl.BlockSpec((TM, w), lambda i: (i, 0))
    tab = pl.BlockSpec((TM, DK_B), lambda i: (jnp.where(i < p_tiles, 0, 1 + (i - p_tiles) % tiles_per_seq), 0))
    return pl.pallas_call(
        _ret_pre_kernel,
        grid=(n // TM,),
        in_specs=[tok(D), pl.BlockSpec((None, 1, N_MOD * D), lambda i: (_mod_row(i, p_tiles, tiles_per_seq), 0, 0)),
                  full(norm_g.reshape(1, D)), tab, tab, full(w_in)],
        out_specs=[tok(D), tok(D), tok(2 * D), tok(2 * D)],
        out_shape=[jax.ShapeDtypeStruct((n, D), BF16), jax.ShapeDtypeStruct((n, D), BF16),
                   jax.ShapeDtypeStruct((n, 2 * D), BF16), jax.ShapeDtypeStruct((n, 2 * D), F32)],
        compiler_params=_cparams(1),
        name="ret_pre",
    )(x, mod, norm_g.reshape(1, D), cos_t, sin_t, w_in)


def _ret_scan_kernel(lg_ref, q_ref, k_ref, v_ref, s0_ref, o_ref, sf_ref, s_scr, *, items):
    d = pl.program_id(0)
    is_ctx, c, nc = items.decode(pl.program_id(1))
    rev = d == 1
    ch = RET_CHUNK

    @pl.when(c == 0)
    def _():
        s_scr[...] = jnp.where(is_ctx, 0.0, s0_ref[...])

    ti = lax.broadcasted_iota(I32, (ch, ch), 0)
    tj = lax.broadcasted_iota(I32, (ch, ch), 1)
    rel = jnp.where(rev, tj - ti, ti - tj).astype(F32)
    steps_q = jnp.where(rev, ch - ti, ti + 1).astype(F32)
    steps_k = jnp.where(rev, ti, ch - 1 - ti).astype(F32)
    for hd in range(H_B):
        lg = lg_ref[d, hd]
        mask = jnp.where(rel >= 0, jnp.exp(jnp.maximum(rel, 0.0) * lg), 0.0)
        q_dec = jnp.exp(steps_q * lg)
        k_dec = jnp.exp(steps_k * lg)
        chunk_dec = jnp.exp(jnp.full((1, DV_B), float(ch), F32) * lg)
        qh = q_ref[:, hd * DK_B:(hd + 1) * DK_B]
        kh = k_ref[:, hd * DK_B:(hd + 1) * DK_B]
        vh = v_ref[:, hd * DV_B:(hd + 1) * DV_B]
        s = s_scr[hd]
        scores = _bdot_nt(qh, kh) * mask
        inner = _bdot(scores, vh)
        cross = _bdot(qh, s) * jnp.concatenate([q_dec] * (DV_B // ch), axis=1)
        o_ref[:, hd * DV_B:(hd + 1) * DV_B] = inner + cross
        kdec = kh.astype(F32) * jnp.concatenate([k_dec] * (DK_B // ch), axis=1)
        s_scr[hd] = s * chunk_dec + _bdot_tn(kdec, vh)

    @pl.when(is_ctx & (c == nc - 1))
    def _():
        sf_ref[...] = s_scr[...]


def _ret_scan(log_gamma, q, k, v, s0, items):
    n = q.shape[0]
    ch = RET_CHUNK
    tok = lambda w: pl.BlockSpec((ch, w), lambda d, j: (items.block(d, j), 0))
    st = lambda seq: pl.BlockSpec((None, None, H_B, DK_B, DV_B), lambda d, j: (d, seq(j), 0, 0, 0))
    return pl.pallas_call(
        functools.partial(_ret_scan_kernel, items=items),
        grid=(2, items.n_items),
        in_specs=[pl.BlockSpec(memory_space=pltpu.SMEM), tok(D), tok(D), tok(2 * D), st(items.lat_seq)],
        out_specs=[pl.BlockSpec((None, ch, 2 * D), lambda d, j: (d, items.block(d, j), 0)), st(items.ctx_seq)],
        out_shape=[jax.ShapeDtypeStruct((2, n, 2 * D), F32),
                   jax.ShapeDtypeStruct((2, items.n_ctx, H_B, DK_B, DV_B), F32)],
        scratch_shapes=[pltpu.VMEM((H_B, DK_B, DV_B), F32)],
        compiler_params=_cparams(2),
        name="ret_scan",
    )(log_gamma, q, k, v, s0)


def _ret_post_kernel(x_ref, o_ref, gate_ref, mod_ref, wout_ref, out_ref):
    o = o_ref[0] + o_ref[1]
    parts = []
    for hd in range(H_B):
        oh = o[:, hd * DV_B:(hd + 1) * DV_B]
        parts.append(oh * lax.rsqrt(jnp.mean(oh * oh, axis=-1, keepdims=True) + NORM_EPS))
    y = _silu(gate_ref[...]) * jnp.concatenate(parts, axis=1)
    out_ref[...] = x_ref[...] + mod_ref[:, 2 * D:3 * D] * _bdot(y, wout_ref[...])


def _ret_post(x, o, gate, mod, p_tiles, tiles_per_seq, w_out):
    n = x.shape[0]
    tok = lambda w: pl.BlockSpec((TM, w), lambda i: (i, 0))
    return pl.pallas_call(
        _ret_post_kernel,
        grid=(n // TM,),
        in_specs=[tok(D), pl.BlockSpec((2, TM, 2 * D), lambda i: (0, i, 0)), tok(2 * D),
                  pl.BlockSpec((None, 1, N_MOD * D), lambda i: (_mod_row(i, p_tiles, tiles_per_seq), 0, 0)),
                  pl.BlockSpec(w_out.shape, lambda i: (0, 0))],
        out_specs=tok(D),
        out_shape=jax.ShapeDtypeStruct((n, D), F32),
        compiler_params=_cparams(1),
        name="ret_post",
    )(x, o, gate, mod, w_out)


def _moe_route_kernel(x_ref, mod_ref, g_ref, router_ref, bias_ref, h_o, e_o, w_o, p_o, cnt_o, carry):
    i = pl.program_id(0)

    @pl.when(i == 0)
    def _():
        carry[...] = jnp.zeros_like(carry)

    mod = mod_ref[...]
    h = _norm_mod(x_ref[...], g_ref[...], mod[:, 3 * D:4 * D], mod[:, 4 * D:5 * D])
    h_o[...] = h
    lane = lax.broadcasted_iota(I32, (TM, LANES), 1)
    valid = lane < N_EXPERTS
    neg = -jnp.inf
    scores = jax.nn.sigmoid(_dot_f32(h, router_ref[...]))
    biased = jnp.where(valid, scores + bias_ref[...], neg)

    def group_reduce(t, op):
        s = 1
        while s < PER_GROUP:
            partner = jnp.where((lane & s) == 0, pltpu.roll(t, LANES - s, 1), pltpu.roll(t, s, 1))
            t = op(t, partner)
            s *= 2
        return t

    lane_f = lane.astype(F32)
    group_f = jnp.floor(lane_f * (1.0 / PER_GROUP))

    def first_lane_of_max(t):
        m = jnp.max(t, axis=-1, keepdims=True)
        return jnp.min(jnp.where(t == m, lane_f, float(LANES)), axis=-1, keepdims=True)

    m1 = group_reduce(biased, jnp.maximum)
    first1 = group_reduce(jnp.where(biased == m1, lane_f, float(LANES)), jnp.minimum)
    m2 = group_reduce(jnp.where(lane_f == first1, neg, biased), jnp.maximum)
    gscore = jnp.where(valid, m1 + m2, neg)
    cand = jnp.full((TM, LANES), neg, F32)
    for _ in range(TOPK_GROUPS):
        gsel = group_f == jnp.floor(first_lane_of_max(gscore) * (1.0 / PER_GROUP))
        cand = jnp.where(gsel, biased, cand)
        gscore = jnp.where(gsel, neg, gscore)
    hits = []
    sel01 = jnp.zeros((TM, LANES), F32)
    e_cols = jnp.zeros((TM, LANES), F32)
    for j in range(TOP_K):
        fl = first_lane_of_max(cand)
        hit = lane_f == fl
        hits.append(hit)
        sel01 = jnp.where(hit, 1.0, sel01)
        cand = jnp.where(hit, neg, cand)
        e_cols = jnp.where(lane == j, fl, e_cols)
    wsum = jnp.sum(sel01 * scores, axis=-1, keepdims=True)

    ri = lax.broadcasted_iota(I32, (TM, TM), 0)
    rj = lax.broadcasted_iota(I32, (TM, TM), 1)
    below = jnp.where(rj < ri, 1.0, 0.0).astype(BF16)
    rank = jnp.dot(below, sel01.astype(BF16), preferred_element_type=F32) + carry[...]
    carry[...] = carry[...] + jnp.sum(sel01, axis=0, keepdims=True)
    w_cols = jnp.zeros((TM, LANES), F32)
    p_cols = jnp.zeros((TM, LANES), F32)
    for j in range(TOP_K):
        wj = jnp.sum(jnp.where(hits[j], scores, 0.0), axis=-1, keepdims=True)
        w_cols = jnp.where(lane == j, wj / wsum * ROUTED_SCALE, w_cols)
        p_cols = jnp.where(lane == j, jnp.sum(jnp.where(hits[j], rank, 0.0), axis=-1, keepdims=True), p_cols)
    e_o[...] = e_cols.astype(I32)
    w_o[...] = w_cols
    p_o[...] = p_cols.astype(I32)
    cnt_o[...] = jnp.broadcast_to(carry[...], cnt_o.shape)


def _moe_route(x, mod, p_tiles, tiles_per_seq, norm_g, router, bias):
    n = x.shape[0]
    full = lambda a: pl.BlockSpec(a.shape, lambda i: (0,) * a.ndim)
    tok = lambda w: pl.BlockSpec((TM, w), lambda i: (i, 0))
    router_p = jnp.pad(router, ((0, 0), (0, LANES - N_EXPERTS)))
    bias_p = jnp.pad(bias, (0, LANES - N_EXPERTS)).reshape(1, LANES)
    return pl.pallas_call(
        _moe_route_kernel,
        grid=(n // TM,),
        in_specs=[tok(D), pl.BlockSpec((None, 1, N_MOD * D), lambda i: (_mod_row(i, p_tiles, tiles_per_seq), 0, 0)),
                  full(norm_g.reshape(1, D)), full(router_p), full(bias_p)],
        out_specs=[tok(D), tok(LANES), tok(LANES), tok(LANES), pl.BlockSpec((8, LANES), lambda i: (0, 0))],
        out_shape=[jax.ShapeDtypeStruct((n, D), F32), jax.ShapeDtypeStruct((n, LANES), I32),
                   jax.ShapeDtypeStruct((n, LANES), F32), jax.ShapeDtypeStruct((n, LANES), I32),
                   jax.ShapeDtypeStruct((8, LANES), F32)],
        scratch_shapes=[pltpu.VMEM((1, LANES), F32)],
        compiler_params=_cparams(1),
        name="moe_route",
    )(x, mod, norm_g.reshape(1, D), router_p, bias_p)


DISPATCH_TOK = 512


def _moe_dispatch_kernel(dest_ref, h_hbm, xs_in, xs_hbm, sem):
    del xs_in
    i = pl.program_id(0)
    n_dma = DISPATCH_TOK * TOP_K

    def copy(j):
        t = i * DISPATCH_TOK + j // TOP_K
        return pltpu.make_async_copy(h_hbm.at[pl.ds(t, 1)], xs_hbm.at[pl.ds(dest_ref[j], 1)], sem)

    def issue(j, carry):
        copy(j).start()
        return carry

    def drain(j, carry):
        copy(j).wait()
        return carry

    lax.fori_loop(0, n_dma, issue, 0)
    lax.fori_loop(0, n_dma, drain, 0)


def _moe_dispatch(dest_flat, h, xs_init):
    n = h.shape[0]
    return pl.pallas_call(
        _moe_dispatch_kernel,
        grid=(n // DISPATCH_TOK,),
        in_specs=[pl.BlockSpec((DISPATCH_TOK * TOP_K,), lambda i: (i,), memory_space=pltpu.SMEM),
                  pl.BlockSpec(memory_space=pl.ANY), pl.BlockSpec(memory_space=pl.ANY)],
        out_specs=pl.BlockSpec(memory_space=pl.ANY),
        out_shape=jax.ShapeDtypeStruct(xs_init.shape, xs_init.dtype),
        scratch_shapes=[pltpu.SemaphoreType.DMA],
        input_output_aliases={2: 0},
        compiler_params=_cparams(1, has_side_effects=True),
        name="moe_dispatch",
    )(dest_flat, h, xs_init)


def _moe_expert_kernel(be_ref, xs_ref, wgu_ref, wdn_ref, o_ref):
    del be_ref
    gu = _bdot(xs_ref[...], wgu_ref[...])
    act = _silu(gu[:, 0:D_EXPERT]) * gu[:, D_EXPERT:2 * D_EXPERT]
    o_ref[...] = _bdot(act, wdn_ref[...])


def _moe_expert(block_e, xs, w_gu, w_down):
    n_slots = xs.shape[0]
    n_blocks = n_slots // EXPERT_BLOCK
    grid_spec = pltpu.PrefetchScalarGridSpec(
        num_scalar_prefetch=1,
        grid=(n_blocks,),
        in_specs=[pl.BlockSpec((EXPERT_BLOCK, D), lambda i, be: (i, 0)),
                  pl.BlockSpec((None, D, 2 * D_EXPERT), lambda i, be: (be[i], 0, 0)),
                  pl.BlockSpec((None, D_EXPERT, D), lambda i, be: (be[i], 0, 0))],
        out_specs=pl.BlockSpec((EXPERT_BLOCK, D), lambda i, be: (i, 0)),
    )
    return pl.pallas_call(
        _moe_expert_kernel,
        grid_spec=grid_spec,
        out_shape=jax.ShapeDtypeStruct((n_slots, D), F32),
        compiler_params=_cparams(1),
        name="moe_expert",
    )(block_e, xs, w_gu, w_down)


COMBINE_TOK = 128


def _moe_combine_kernel(dest_ref, x_ref, h_ref, w_ref, mod_ref, shgu_ref, shdn_ref, gfin_ref, ys_hbm, o_ref, buf, sem,
                        *, final_norm):
    n_dma = COMBINE_TOK * TOP_K

    def copy(j):
        t = j // TOP_K
        kq = j % TOP_K
        return pltpu.make_async_copy(ys_hbm.at[pl.ds(dest_ref[j], 1)], buf.at[kq, pl.ds(t, 1)], sem)

    def issue(j, carry):
        copy(j).start()
        return carry

    def drain(j, carry):
        copy(j).wait()
        return carry

    lax.fori_loop(0, n_dma, issue, 0)
    gu = _bdot(h_ref[...], shgu_ref[...])
    acc = _bdot(_silu(gu[:, 0:D_EXPERT]) * gu[:, D_EXPERT:2 * D_EXPERT], shdn_ref[...])
    lax.fori_loop(0, n_dma, drain, 0)
    w = w_ref[...]
    for kq in range(TOP_K):
        acc = acc + buf[kq] * w[:, kq:kq + 1]
    out = x_ref[...] + mod_ref[:, 5 * D:6 * D] * acc
    if final_norm:
        out = out * lax.rsqrt(jnp.mean(out * out, axis=-1, keepdims=True) + NORM_EPS) * gfin_ref[...]
    o_ref[...] = out


def _moe_combine(dest_flat, x, h, w_cols, mod, p_tiles, tiles_per_seq, sh_gu, sh_down, g_final, ys, final_norm):
    n = x.shape[0]
    ratio = TM // COMBINE_TOK
    tok = lambda w: pl.BlockSpec((COMBINE_TOK, w), lambda i: (i, 0))
    full = lambda a: pl.BlockSpec(a.shape, lambda i: (0,) * a.ndim)
    return pl.pallas_call(
        functools.partial(_moe_combine_kernel, final_norm=final_norm),
        grid=(n // COMBINE_TOK,),
        in_specs=[pl.BlockSpec((COMBINE_TOK * TOP_K,), lambda i: (i,), memory_space=pltpu.SMEM),
                  tok(D), tok(D), tok(LANES),
                  pl.BlockSpec((None, 1, N_MOD * D), lambda i: (_mod_row(i // ratio, p_tiles, tiles_per_seq), 0, 0)),
                  full(sh_gu), full(sh_down), full(g_final),
                  pl.BlockSpec(memory_space=pl.ANY)],
        out_specs=tok(D),
        out_shape=jax.ShapeDtypeStruct((n, D), F32),
        scratch_shapes=[pltpu.VMEM((TOP_K, COMBINE_TOK, D), F32), pltpu.SemaphoreType.DMA],
        compiler_params=_cparams(1),
        name="moe_combine",
    )(dest_flat, x, h, w_cols, mod, sh_gu, sh_down, g_final, ys)


def _moe_layer(x, mod, p_tiles, tiles_per_seq, norm_g, router, bias, w_gu, w_down, sh_gu, sh_down, g_final, final_norm):
    n = x.shape[0]
    h, e_cols, w_cols, p_cols, counts = _moe_route(x, mod, p_tiles, tiles_per_seq, norm_g, router, bias)
    counts = counts[0, :N_EXPERTS].astype(I32)
    padded = (counts + EXPERT_BLOCK - 1) // EXPERT_BLOCK * EXPERT_BLOCK
    pad_end = jnp.cumsum(padded)
    pad_start = pad_end - padded
    dest = (pad_start[e_cols[:, :TOP_K]] + p_cols[:, :TOP_K]).reshape(-1)
    n_blocks = n * TOP_K // EXPERT_BLOCK + N_EXPERTS
    block_e = jnp.minimum(jnp.searchsorted(pad_end, jnp.arange(n_blocks, dtype=I32) * EXPERT_BLOCK, side="right"),
                          N_EXPERTS - 1).astype(I32)
    xs = _moe_dispatch(dest, h, jnp.zeros((n_blocks * EXPERT_BLOCK, D), F32))
    ys = _moe_expert(block_e, xs, w_gu, w_down)
    return _moe_combine(dest, x, h, w_cols, mod, p_tiles, tiles_per_seq, sh_gu, sh_down, g_final.reshape(1, D), ys,
                        final_norm)


def _pair_pack(s):
    lead = s.shape[:-3]
    s = s.reshape(lead + (N_PAIR, 2, HEAD_A, HEAD_A))
    z = jnp.zeros_like(s[..., 0, :, :])
    top = jnp.concatenate([s[..., 0, :, :], z], axis=-1)
    bot = jnp.concatenate([z, s[..., 1, :, :]], axis=-1)
    return jnp.concatenate([top, bot], axis=-2)


def _pair_unpack(s):
    a = s[..., :HEAD_A, :HEAD_A]
    b = s[..., HEAD_A:, HEAD_A:]
    return jnp.stack([a, b], axis=-3).reshape(s.shape[:-3] + (H_A, HEAD_A, HEAD_A))


def _rope_tables(rows):
    half = DK_B // 2
    n_freq = half // 2
    inv = ROPE_BASE ** (-jnp.arange(n_freq, dtype=F32) / n_freq)
    pos_r = jnp.repeat(jnp.arange(rows, dtype=F32), GRID_W)
    pos_c = jnp.tile(jnp.arange(GRID_W, dtype=F32), rows)

    def tab(pos):
        ang = pos[:, None] * inv[None, :]
        c, s = jnp.cos(ang), jnp.sin(ang)
        return jnp.concatenate([c, c], -1), jnp.concatenate([-s, s], -1)

    cr, sr = tab(pos_r)
    cc, sc = tab(pos_c)
    cos = jnp.concatenate([cr, cc], -1)
    sin = jnp.concatenate([sr, sc], -1)
    cos = jnp.concatenate([jnp.ones((TM, DK_B), F32), cos], 0)
    sin = jnp.concatenate([jnp.zeros((TM, DK_B), F32), sin], 0)
    return cos, sin


def kernel(x_prompt, x_sample, state_rwkv, state_ret, c, c_ctx, ada_w, ada_b, norm_mix, norm_ffn, norm_final, rwkv_mu, rwkv_w0, rwkv_w1, rwkv_w2, rwkv_a0, rwkv_a1, rwkv_a2, rwkv_wrkv, rwkv_wo, rwkv_g1, rwkv_g2, rwkv_k_k, rwkv_k_a, rwkv_r_k, rwkv_ln_w, rwkv_ln_b, ret_w_in, ret_w_out, ret_decay_logit, moe_router, moe_bias, moe_w_gu, moe_w_down, moe_sh_gu, moe_sh_down):
    bp, tp, _ = x_prompt.shape
    bs, ts, _ = x_sample.shape
    n_p, n_s = bp * tp, bs * ts
    assert tp == TM and ts % TM == 0 and TM % GRID_W == 0
    p_tiles = n_p // TM
    tiles_per_seq = ts // TM
    rows = ts // GRID_W
    x = jnp.concatenate([x_prompt.reshape(n_p, D), x_sample.reshape(n_s, D)], axis=0)
    n = n_p + n_s

    n_cond = 16
    cond = jnp.zeros((n_cond, D), F32).at[0].set(c_ctx).at[1:1 + bs].set(c)
    mod = _modulation(cond, ada_w, ada_b).reshape(ada_w.shape[0], n_cond, 1, N_MOD * D)

    head_of = jnp.arange(D, dtype=I32) // HEAD_A
    hsum = (head_of[:, None] == jnp.arange(LANES, dtype=I32)[None, :]).astype(BF16)
    hexp = hsum.T

    gl = rwkv_g1.shape[-1]
    glp = -(-gl // LANES) * LANES
    zb = lambda a: jnp.zeros_like(a)
    wts = {
        "mu": jnp.pad(rwkv_mu[0], ((0, 8 - N_MOD), (0, 0))),
        "wrkv": rwkv_wrkv[0].astype(BF16),
        "g1": jnp.pad(rwkv_g1[0], ((0, 0), (0, glp - gl))).astype(BF16),
        "g2": jnp.pad(rwkv_g2[0], ((0, glp - gl), (0, 0))).astype(BF16),
        "w1": jnp.concatenate([rwkv_w1[0, 0], rwkv_w1[0, 1]], axis=1).astype(BF16),
        "w2": jnp.concatenate([jnp.concatenate([rwkv_w2[0, 0], zb(rwkv_w2[0, 1])], 1),
                               jnp.concatenate([zb(rwkv_w2[0, 0]), rwkv_w2[0, 1]], 1)], 0).astype(BF16),
        "w0": rwkv_w0[0].reshape(1, 2 * D),
        "a1": jnp.concatenate([rwkv_a1[0, 0], rwkv_a1[0, 1]], axis=1).astype(BF16),
        "a2": jnp.concatenate([jnp.concatenate([rwkv_a2[0, 0], zb(rwkv_a2[0, 1])], 1),
                               jnp.concatenate([zb(rwkv_a2[0, 0]), rwkv_a2[0, 1]], 1)], 0).astype(BF16),
        "a0": rwkv_a0[0].reshape(1, 2 * D),
        "k_k": rwkv_k_k[0], "k_a": rwkv_k_a[0], "r_k": rwkv_r_k[0].reshape(D),
        "hsum": hsum, "hexp": hexp,
    }
    r, v, kk, lw, kd, b, gate, bonus = _rwkv_pre(x, mod[0], p_tiles, tiles_per_seq, norm_mix[0], wts)
    s0_lat = _pair_pack(jnp.moveaxis(state_rwkv[:, 0], 1, 0))
    y2, s_fin = _rwkv_scan(r, v, kk, lw, kd, b, s0_lat,
                           _ScanItems(bp, tp // RWKV_CHUNK, bs, ts // RWKV_CHUNK))
    new_state_rwkv = jnp.moveaxis(_pair_unpack(s_fin), 0, 1)[:, None]
    x = _rwkv_post(x, y2, gate, bonus, mod[0], p_tiles, tiles_per_seq, rwkv_ln_w[0], rwkv_ln_b[0],
                   rwkv_wo[0].astype(BF16), hsum, hexp)
    x = _moe_layer(x, mod[0], p_tiles, tiles_per_seq, norm_ffn[0], moe_router[0], moe_bias[0],
                   moe_w_gu[0].astype(BF16), moe_w_down[0].astype(BF16), moe_sh_gu[0].astype(BF16),
                   moe_sh_down[0].astype(BF16), norm_final, False)

    cos_t, sin_t = _rope_tables(rows)
    q, k, vv, rgate = _ret_pre(x, mod[1], p_tiles, tiles_per_seq, norm_mix[1], cos_t, sin_t, ret_w_in[0].astype(BF16))
    log_gamma = jax.nn.log_sigmoid(ret_decay_logit[0].astype(F32))
    o2, r_fin = _ret_scan(log_gamma, q, k, vv, jnp.moveaxis(state_ret[:, 0], 1, 0),
                          _ScanItems(bp, tp // RET_CHUNK, bs, ts // RET_CHUNK))
    new_state_ret = jnp.moveaxis(r_fin, 0, 1)[:, None]
    x = _ret_post(x, o2, rgate, mod[1], p_tiles, tiles_per_seq, ret_w_out[0].astype(BF16))
    x = _moe_layer(x, mod[1], p_tiles, tiles_per_seq, norm_ffn[1], moe_router[1], moe_bias[1],
                   moe_w_gu[1].astype(BF16), moe_w_down[1].astype(BF16), moe_sh_gu[1].astype(BF16),
                   moe_sh_down[1].astype(BF16), norm_final, True)

    return (x[:n_p].reshape(bp, tp, D), x[n_p:].reshape(bs, ts, D), new_state_rwkv, new_state_ret)
```

```python
import functools

import jax
import jax.numpy as jnp
from jax import lax
from jax.experimental import pallas as pl
from jax.experimental.pallas import tpu as pltpu

F32, BF16, I32 = jnp.float32, jnp.bfloat16, jnp.int32

D = 1024
N_MOD = 6
NORM_EPS = 1e-6
GRID_W = 64
HEAD_A = 64
H_A = D // HEAD_A
LNX_EPS = 64e-5
RWKV_CHUNK = 64
N_PAIR = H_A // 2
H_B = 4
DK_B = D // H_B
DV_B = 2 * DK_B
RET_CHUNK = 128
ROPE_BASE = 10000.0
N_EXPERTS = 64
TOP_K = 8
N_GROUPS = 8
TOPK_GROUPS = 4
PER_GROUP = N_EXPERTS // N_GROUPS
D_EXPERT = 256
ROUTED_SCALE = 2.5
EXPERT_BLOCK = 128

TM = 256
LANES = 128
ROW_SUB = D // LANES
VMEM_LIMIT = 56 * 1024 * 1024


def _cparams(n_grid_axes, **kw):
    return pltpu.CompilerParams(dimension_semantics=("arbitrary",) * n_grid_axes, vmem_limit_bytes=VMEM_LIMIT, **kw)


def _bdot(a, b):
    return jnp.dot(a.astype(BF16), b.astype(BF16), preferred_element_type=F32)


def _bdot_nt(a, b):
    return lax.dot_general(a.astype(BF16), b.astype(BF16), (((1,), (1,)), ((), ())), preferred_element_type=F32)


def _bdot_tn(a, b):
    return lax.dot_general(a.astype(BF16), b.astype(BF16), (((0,), (0,)), ((), ())), preferred_element_type=F32)


def _split3(x):
    hi = x.astype(BF16)
    r1 = x - hi.astype(F32)
    mid = r1.astype(BF16)
    lo = (r1 - mid.astype(F32)).astype(BF16)
    return hi, mid, lo


def _dot_f32(a, b):
    ah, am, al = _split3(a)
    bh, bm, bl = _split3(b)
    d = lambda x, y: jnp.dot(x, y, preferred_element_type=F32)
    return d(ah, bh) + (d(ah, bm) + d(am, bh)) + (d(ah, bl) + d(al, bh) + d(am, bm))


def _dot_f32_rhs01(a, m01):
    ah, am, al = _split3(a)
    d = lambda x: jnp.dot(x, m01, preferred_element_type=F32)
    return d(ah) + d(am) + d(al)


def _dot_f32_lhs01(m01, b):
    bh, bm, bl = _split3(b)
    d = lambda x: jnp.dot(m01, x, preferred_element_type=F32)
    return d(bh) + d(bm) + d(bl)


def _norm_mod(x, g, shift, scale):
    ms = jnp.mean(x * x, axis=-1, keepdims=True)
    y = x * lax.rsqrt(ms + NORM_EPS) * g
    return y * (1.0 + scale) + shift


def _silu(x):
    return x * jax.nn.sigmoid(x)


def _rows_load(ref):
    return jnp.concatenate([ref[:, j, :] for j in range(ROW_SUB)], axis=1)


def _rows_store(ref, val):
    for j in range(ROW_SUB):
        ref[:, j, :] = val[:, j * LANES:(j + 1) * LANES]


def _mod_row(i, p_tiles, tiles_per_seq):
    return jnp.where(i < p_tiles, 0, 1 + (i - p_tiles) // tiles_per_seq)


def _mod_kernel(cond_ref, w_ref, b_ref, o_ref):
    o_ref[...] = _dot_f32(_silu(cond_ref[...]), w_ref[...]) + b_ref[...]


def _modulation(cond, ada_w, ada_b):
    depth, _, n6 = ada_w.shape
    tn = 1536
    return pl.pallas_call(
        _mod_kernel,
        grid=(depth, n6 // tn),
        in_specs=[
            pl.BlockSpec(cond.shape, lambda l, j: (0, 0)),
            pl.BlockSpec((None, D, tn), lambda l, j: (l, 0, j)),
            pl.BlockSpec((None, 1, tn), lambda l, j: (l, 0, j)),
        ],
        out_specs=pl.BlockSpec((None, cond.shape[0], tn), lambda l, j: (l, 0, j)),
        out_shape=jax.ShapeDtypeStruct((depth, cond.shape[0], n6), F32),
        compiler_params=_cparams(2),
        name="adaln_mod",
    )(cond, ada_w, ada_b.reshape(depth, 1, n6))


def _rwkv_pre_kernel(x_ref, xu_ref, xd_ref, mod_ref, g_ref, mu_ref, wrkv_ref, g1_ref, g2_ref, w1_ref, w2_ref,
                     w0_ref, a1_ref, a2_ref, a0_ref, kk_ref, ka_ref, rk_ref, hsum_ref, hexp_ref,
                     r_o, v_o, kk_o, lw_o, kd_o, b_o, gate_o, bonus_o, *, p_tiles, tiles_per_seq):
    i = pl.program_id(0)
    is_p = i < p_tiles
    sub = (i - p_tiles) % tiles_per_seq
    mod = mod_ref[...]
    shift, scale = mod[:, 0:D], mod[:, D:2 * D]
    g = g_ref[...]
    h = _norm_mod(x_ref[...], g, shift, scale)
    hu = _norm_mod(xu_ref[...], g, shift, scale)
    hd = _norm_mod(xd_ref[...], g, shift, scale)

    q = D // 4
    row = lax.broadcasted_iota(I32, (TM, 1), 0)
    per = jnp.where(is_p, TM, GRID_W)
    pos = row & (per - 1)

    def prev1(a):
        return jnp.where(pos == 0, 0.0, pltpu.roll(a, 1, 0))

    def next1(a):
        return jnp.where(pos == per - 1, 0.0, pltpu.roll(a, TM - 1, 0))

    h0, h1, h2, h3 = (h[:, j * q:(j + 1) * q] for j in range(4))
    up = jnp.concatenate([jnp.where(sub == 0, 0.0, hu[:, 2 * q:3 * q]), h2[0:TM - GRID_W]], axis=0)
    down = jnp.concatenate([h3[GRID_W:TM], jnp.where(sub == tiles_per_seq - 1, 0.0, hd[:, 3 * q:4 * q])], axis=0)
    s0 = prev1(h0)
    s1 = jnp.where(is_p, prev1(h1), next1(h1))
    s2 = jnp.where(is_p, next1(h2), up)
    s3 = jnp.where(is_p, next1(h3), down)
    xx = jnp.concatenate([s0, s1, s2, s3], axis=1) - h

    mu = mu_ref[...]
    mix = lambda j: h + xx * mu[j:j + 1]
    r = _bdot(mix(0), wrkv_ref[0])
    k = _bdot(mix(2), wrkv_ref[1])
    v = _bdot(mix(3), wrkv_ref[2])
    gate = _bdot(jax.nn.sigmoid(_bdot(mix(5), g1_ref[...])), g2_ref[...])
    w_all = w0_ref[...] + _bdot(jnp.tanh(_bdot(mix(1), w1_ref[...])), w2_ref[...])
    a_all = jax.nn.sigmoid(a0_ref[...] + _bdot(_bdot(mix(4), a1_ref[...]), a2_ref[...]))

    hsum, hexp = hsum_ref[...], hexp_ref[...]
    head_sum = lambda t: _dot_f32_rhs01(_dot_f32_rhs01(t, hsum), hexp)

    kkr = k * kk_ref[...]
    kk = kkr / jnp.maximum(jnp.sqrt(head_sum(kkr * kkr)), 1e-12)
    ka = ka_ref[...]
    kd_sum = jnp.zeros_like(k)
    for d in range(2):
        wd = w_all[:, d * D:(d + 1) * D]
        z = -wd
        softplus = jnp.maximum(z, 0.0) + jnp.log(1.0 + jnp.exp(-jnp.abs(z)))
        lw_o[d] = -jnp.exp(-softplus - 0.5)
        a = a_all[:, d * D:(d + 1) * D]
        kd = k * (1.0 + (a - 1.0) * ka)
        kd_o[d] = kd
        b_o[d] = kk * a
        kd_sum = kd_sum + kd
    r_o[...] = r
    v_o[...] = v
    kk_o[...] = kk
    gate_o[...] = gate
    bonus_o[...] = head_sum(r * kd_sum * rk_ref[...]) * v


def _rwkv_pre(x, mod, p_tiles, tiles_per_seq, norm_g, wts):
    n = x.shape[0]
    n_tiles = n // TM
    hb = TM // GRID_W
    n_hblk = n // GRID_W
    row = lambda a: a.reshape(1, -1)
    full = lambda a: pl.BlockSpec(a.shape, lambda i: (0,) * a.ndim)
    tok = pl.BlockSpec((TM, D), lambda i: (i, 0))
    tok2 = pl.BlockSpec((2, TM, D), lambda i: (0, i, 0))
    consts = [row(norm_g), wts["mu"], wts["wrkv"], wts["g1"], wts["g2"], wts["w1"], wts["w2"], wts["w0"],
              wts["a1"], wts["a2"], wts["a0"], row(wts["k_k"]), row(wts["k_a"]), row(wts["r_k"]),
              wts["hsum"], wts["hexp"]]
    return pl.pallas_call(
        functools.partial(_rwkv_pre_kernel, p_tiles=p_tiles, tiles_per_seq=tiles_per_seq),
        grid=(n_tiles,),
        in_specs=[
            tok,
            pl.BlockSpec((GRID_W, D), lambda i: (jnp.maximum(i * hb - 1, 0), 0)),
            pl.BlockSpec((GRID_W, D), lambda i: (jnp.minimum(i * hb + hb, n_hblk - 1), 0)),
            pl.BlockSpec((None, 1, N_MOD * D), lambda i: (_mod_row(i, p_tiles, tiles_per_seq), 0, 0)),
        ] + [full(a) for a in consts],
        out_specs=[tok, tok, tok, tok2, tok2, tok2, tok, tok],
        out_shape=[jax.ShapeDtypeStruct((n, D), F32)] * 3 + [jax.ShapeDtypeStruct((2, n, D), F32)] * 3
        + [jax.ShapeDtypeStruct((n, D), F32)] * 2,
        compiler_params=_cparams(1),
        name="rwkv_pre",
    )(x, x, x, mod, *consts)


class _ScanItems:
    def __init__(self, n_ctx, ctx_chunks, n_lat, lat_chunks):
        self.n_ctx, self.ctx_chunks, self.n_lat, self.lat_chunks = n_ctx, ctx_chunks, n_lat, lat_chunks
        self.ctx_items = n_ctx * ctx_chunks
        self.n_items = self.ctx_items + n_lat * lat_chunks

    def decode(self, j):
        is_ctx = j < self.ctx_items
        jl = jnp.maximum(j - self.ctx_items, 0)
        c = jnp.where(is_ctx, j % self.ctx_chunks, jl % self.lat_chunks)
        return is_ctx, c, jnp.where(is_ctx, self.ctx_chunks, self.lat_chunks)

    def block(self, d, j):
        _, c, nc = self.decode(j)
        return jnp.where(d == 0, j, j - c + (nc - 1 - c))

    def ctx_seq(self, j):
        return jnp.minimum(j // self.ctx_chunks, self.n_ctx - 1)

    def lat_seq(self, j):
        return jnp.maximum(j - self.ctx_items, 0) // self.lat_chunks


def _rwkv_scan_kernel(r_ref, v_ref, kk_ref, lw_ref, kd_ref, b_ref, s0_ref, y_ref, sf_ref, s_scr, *, items):
    d = pl.program_id(0)
    is_ctx, c, nc = items.decode(pl.program_id(1))
    rev = d == 1
    ch = RWKV_CHUNK

    @pl.when(c == 0)
    def _():
        s_scr[...] = jnp.where(is_ctx, 0.0, s0_ref[...])

    ti = lax.broadcasted_iota(I32, (ch, ch), 0)
    tj = lax.broadcasted_iota(I32, (ch, ch), 1)
    tri = jnp.where(jnp.where(rev, ti - tj, tj - ti) <= 0, 1.0, 0.0).astype(BF16)
    lw = lw_ref[...]
    cum = _dot_f32_lhs01(tri, lw)
    last = jnp.where(rev, cum[0:1], cum[ch - 1:ch])
    kk = kk_ref[...]
    kd = kd_ref[...]
    b = b_ref[...]
    e_neg = jnp.exp(-cum)
    e_rel = jnp.exp(last - cum)
    at = -kk * jnp.exp(cum - lw)
    rt = r_ref[...] * jnp.exp(cum)
    bt = b * e_neg
    kt = kd * e_neg
    bh = b * e_rel
    kh = kd * e_rel
    g_all = jnp.exp(last)
    v = v_ref[...]

    si = lax.broadcasted_iota(I32, (2 * ch, 2 * ch), 0)
    sj = lax.broadcasted_iota(I32, (2 * ch, 2 * ch), 1)
    same = (si < ch) == (sj < ch)
    ui, uj = si & (ch - 1), sj & (ch - 1)
    order = jnp.where(rev, ui - uj, uj - ui)
    strict = same & (order < 0)
    incl = same & (order <= 0)
    first = lax.broadcasted_iota(I32, (1, LANES), 1) < HEAD_A

    def stack(xp):
        return jnp.concatenate([jnp.where(first, xp, 0.0), jnp.where(first, 0.0, xp)], axis=0)

    for p in range(N_PAIR):
        sl = slice(p * LANES, (p + 1) * LANES)
        ps = jnp.concatenate([stack(at[:, sl]), stack(rt[:, sl])], axis=0)
        qs = jnp.concatenate([stack(bt[:, sl]), stack(kt[:, sl])], axis=0)
        m1 = _bdot_nt(ps, qs)
        l_ab = jnp.where(strict, m1[0:2 * ch, 0:2 * ch], 0.0)
        l_ak = jnp.where(strict, m1[0:2 * ch, 2 * ch:4 * ch], 0.0)
        m_rb = jnp.where(incl, m1[2 * ch:4 * ch, 0:2 * ch], 0.0)
        m_rk = jnp.where(incl, m1[2 * ch:4 * ch, 2 * ch:4 * ch], 0.0)
        s = s_scr[p]
        m2 = _bdot_nt(jnp.concatenate([at[:, sl], rt[:, sl]], axis=0), s)
        vs = stack(v[:, sl])
        x = stack(m2[0:ch]) + _bdot(l_ak, vs)
        nm = l_ab
        x = x + _bdot(nm, x)
        for _ in range(5):
            nm = _bdot(nm, nm)
            x = x + _bdot(nm, x)
        uv = jnp.concatenate([x, vs], axis=0)
        ys = _bdot(jnp.concatenate([m_rb, m_rk], axis=1), uv)
        y_ref[:, sl] = ys[0:ch] + ys[ch:2 * ch] + m2[ch:2 * ch]
        qh = jnp.concatenate([stack(bh[:, sl]), stack(kh[:, sl])], axis=0)
        s_scr[p] = s * g_all[:, sl] + _bdot_tn(uv, qh)

    @pl.when(is_ctx & (c == nc - 1))
    def _():
        sf_ref[...] = s_scr[...]


def _rwkv_scan(r, v, kk, lw, kd, b, s0, items):
    n = r.shape[0]
    ch = RWKV_CHUNK
    tok = pl.BlockSpec((ch, D), lambda d, j: (items.block(d, j), 0))
    tokd = pl.BlockSpec((None, ch, D), lambda d, j: (d, items.block(d, j), 0))
    st = lambda seq: pl.BlockSpec((None, None, N_PAIR, LANES, LANES), lambda d, j: (d, seq(j), 0, 0, 0))
    return pl.pallas_call(
        functools.partial(_rwkv_scan_kernel, items=items),
        grid=(2, items.n_items),
        in_specs=[tok, tok, tok, tokd, tokd, tokd, st(items.lat_seq)],
        out_specs=[tokd, st(items.ctx_seq)],
        out_shape=[jax.ShapeDtypeStruct((2, n, D), F32),
                   jax.ShapeDtypeStruct((2, items.n_ctx, N_PAIR, LANES, LANES), F32)],
        scratch_shapes=[pltpu.VMEM((N_PAIR, LANES, LANES), F32)],
        compiler_params=_cparams(2),
        name="rwkv_scan",
    )(r, v, kk, lw, kd, b, s0)


def _rwkv_post_kernel(x_ref, y_ref, gate_ref, bonus_ref, mod_ref, lnw_ref, lnb_ref, wo_ref, hsum_ref, hexp_ref, o_ref):
    hsum, hexp = hsum_ref[...], hexp_ref[...]
    head_mean = lambda t: _dot_f32_rhs01(_dot_f32_rhs01(t, hsum), hexp) * (1.0 / HEAD_A)
    y = y_ref[0] + y_ref[1]
    yc = y - head_mean(y)
    yn = yc * lax.rsqrt(head_mean(yc * yc) + LNX_EPS)
    z = (yn * lnw_ref[...] + lnb_ref[...] + bonus_ref[...]) * gate_ref[...]
    o_ref[...] = x_ref[...] + mod_ref[:, 2 * D:3 * D] * _bdot(z, wo_ref[...])


def _rwkv_post(x, y, gate, bonus, mod, p_tiles, tiles_per_seq, ln_w, ln_b, wo, hsum, hexp):
    n = x.shape[0]
    full = lambda a: pl.BlockSpec(a.shape, lambda i: (0,) * a.ndim)
    tok = pl.BlockSpec((TM, D), lambda i: (i, 0))
    consts = [ln_w.reshape(1, D), ln_b.reshape(1, D), wo, hsum, hexp]
    return pl.pallas_call(
        _rwkv_post_kernel,
        grid=(n // TM,),
        in_specs=[tok, pl.BlockSpec((2, TM, D), lambda i: (0, i, 0)), tok, tok,
                  pl.BlockSpec((None, 1, N_MOD * D), lambda i: (_mod_row(i, p_tiles, tiles_per_seq), 0, 0))]
        + [full(a) for a in consts],
        out_specs=tok,
        out_shape=jax.ShapeDtypeStruct((n, D), F32),
        compiler_params=_cparams(1),
        name="rwkv_post",
    )(x, y, gate, bonus, mod, *consts)


def _ret_pre_kernel(x_ref, mod_ref, g_ref, cos_ref, sin_ref, win_ref, q_o, k_o, v_o, gate_o):
    mod = mod_ref[...]
    h = _norm_mod(x_ref[...], g_ref[...], mod[:, 0:D], mod[:, D:2 * D])
    proj = _bdot(h, win_ref[...])
    cos, sin = cos_ref[...], sin_ref[...]

    def rope(t):
        outs = []
        for j in range(D // LANES):
            tj = t[:, j * LANES:(j + 1) * LANES]
            cj = cos[:, (j % 2) * LANES:(j % 2 + 1) * LANES]
            sj = sin[:, (j % 2) * LANES:(j % 2 + 1) * LANES]
            outs.append(tj * cj + pltpu.roll(tj, LANES // 2, 1) * sj)
        return jnp.concatenate(outs, axis=1)

    q_o[...] = rope(proj[:, 0:D]).astype(BF16)
    k_o[...] = rope(proj[:, D:2 * D] * (DK_B ** -0.5)).astype(BF16)
    v_o[...] = proj[:, 2 * D:4 * D].astype(BF16)
    gate_o[...] = proj[:, 4 * D:6 * D]


def _ret_pre(x, mod, p_tiles, tiles_per_seq, norm_g, cos_t, sin_t, w_in):
    n = x.shape[0]
    full = lambda a: pl.BlockSpec(a.shape, lambda i: (0,) * a.ndim)
    tok = lambda w: pl.BlockSpec((TM, w), lambda i: (i, 0))
    tab = pl.BlockSpec((TM, DK_B), lambda i: (jnp.where(i < p_tiles, 0, 1 + (i - p_tiles) % tiles_per_seq), 0))
    return pl.pallas_call(
        _ret_pre_kernel,
        grid=(n // TM,),
        in_specs=[tok(D), pl.BlockSpec((None, 1, N_MOD * D), lambda i: (_mod_row(i, p_tiles, tiles_per_seq), 0, 0)),
                  full(norm_g.reshape(1, D)), tab, tab, full(w_in)],
        out_specs=[tok(D), tok(D), tok(2 * D), tok(2 * D)],
        out_shape=[jax.ShapeDtypeStruct((n, D), BF16), jax.ShapeDtypeStruct((n, D), BF16),
                   jax.ShapeDtypeStruct((n, 2 * D), BF16), jax.ShapeDtypeStruct((n, 2 * D), F32)],
        compiler_params=_cparams(1),
        name="ret_pre",
    )(x, mod, norm_g.reshape(1, D), cos_t, sin_t, w_in)


def _ret_scan_kernel(lg_ref, q_ref, k_ref, v_ref, s0_ref, o_ref, sf_ref, s_scr, *, items):
    d = pl.program_id(0)
    is_ctx, c, nc = items.decode(pl.program_id(1))
    rev = d == 1
    ch = RET_CHUNK

    @pl.when(c == 0)
    def _():
        s_scr[...] = jnp.where(is_ctx, 0.0, s0_ref[...])

    ti = lax.broadcasted_iota(I32, (ch, ch), 0)
    tj = lax.broadcasted_iota(I32, (ch, ch), 1)
    rel = jnp.where(rev, tj - ti, ti - tj).astype(F32)
    steps_q = jnp.where(rev, ch - ti, ti + 1).astype(F32)
    steps_k = jnp.where(rev, ti, ch - 1 - ti).astype(F32)
    for hd in range(H_B):
        lg = lg_ref[d, hd]
        mask = jnp.where(rel >= 0, jnp.exp(jnp.maximum(rel, 0.0) * lg), 0.0)
        q_dec = jnp.exp(steps_q * lg)
        k_dec = jnp.exp(steps_k * lg)
        chunk_dec = jnp.exp(jnp.full((1, DV_B), float(ch), F32) * lg)
        qh = q_ref[:, hd * DK_B:(hd + 1) * DK_B]
        kh = k_ref[:, hd * DK_B:(hd + 1) * DK_B]
        vh = v_ref[:, hd * DV_B:(hd + 1) * DV_B]
        s = s_scr[hd]
        scores = _bdot_nt(qh, kh) * mask
        inner = _bdot(scores, vh)
        cross = _bdot(qh, s) * jnp.concatenate([q_dec] * (DV_B // ch), axis=1)
        o_ref[:, hd * DV_B:(hd + 1) * DV_B] = inner + cross
        kdec = kh.astype(F32) * jnp.concatenate([k_dec] * (DK_B // ch), axis=1)
        s_scr[hd] = s * chunk_dec + _bdot_tn(kdec, vh)

    @pl.when(is_ctx & (c == nc - 1))
    def _():
        sf_ref[...] = s_scr[...]


def _ret_scan(log_gamma, q, k, v, s0, items):
    n = q.shape[0]
    ch = RET_CHUNK
    tok = lambda w: pl.BlockSpec((ch, w), lambda d, j: (items.block(d, j), 0))
    st = lambda seq: pl.BlockSpec((None, None, H_B, DK_B, DV_B), lambda d, j: (d, seq(j), 0, 0, 0))
    return pl.pallas_call(
        functools.partial(_ret_scan_kernel, items=items),
        grid=(2, items.n_items),
        in_specs=[pl.BlockSpec(memory_space=pltpu.SMEM), tok(D), tok(D), tok(2 * D), st(items.lat_seq)],
        out_specs=[pl.BlockSpec((None, ch, 2 * D), lambda d, j: (d, items.block(d, j), 0)), st(items.ctx_seq)],
        out_shape=[jax.ShapeDtypeStruct((2, n, 2 * D), F32),
                   jax.ShapeDtypeStruct((2, items.n_ctx, H_B, DK_B, DV_B), F32)],
        scratch_shapes=[pltpu.VMEM((H_B, DK_B, DV_B), F32)],
        compiler_params=_cparams(2),
        name="ret_scan",
    )(log_gamma, q, k, v, s0)


def _ret_post_kernel(x_ref, o_ref, gate_ref, mod_ref, wout_ref, out_ref):
    o = o_ref[0] + o_ref[1]
    parts = []
    for hd in range(H_B):
        oh = o[:, hd * DV_B:(hd + 1) * DV_B]
        parts.append(oh * lax.rsqrt(jnp.mean(oh * oh, axis=-1, keepdims=True) + NORM_EPS))
    y = _silu(gate_ref[...]) * jnp.concatenate(parts, axis=1)
    out_ref[...] = x_ref[...] + mod_ref[:, 2 * D:3 * D] * _bdot(y, wout_ref[...])


def _ret_post(x, o, gate, mod, p_tiles, tiles_per_seq, w_out):
    n = x.shape[0]
    tok = lambda w: pl.BlockSpec((TM, w), lambda i: (i, 0))
    return pl.pallas_call(
        _ret_post_kernel,
        grid=(n // TM,),
        in_specs=[tok(D), pl.BlockSpec((2, TM, 2 * D), lambda i: (0, i, 0)), tok(2 * D),
                  pl.BlockSpec((None, 1, N_MOD * D), lambda i: (_mod_row(i, p_tiles, tiles_per_seq), 0, 0)),
                  pl.BlockSpec(w_out.shape, lambda i: (0, 0))],
        out_specs=tok(D),
        out_shape=jax.ShapeDtypeStruct((n, D), F32),
        compiler_params=_cparams(1),
        name="ret_post",
    )(x, o, gate, mod, w_out)


def _moe_route_kernel(x_ref, mod_ref, g_ref, router_ref, bias_ref, h_o, e_o, w_o, p_o, cnt_o, carry):
    i = pl.program_id(0)

    @pl.when(i == 0)
    def _():
        carry[...] = jnp.zeros_like(carry)

    mod = mod_ref[...]
    h = _norm_mod(x_ref[...], g_ref[...], mod[:, 3 * D:4 * D], mod[:, 4 * D:5 * D])
    _rows_store(h_o, h)
    lane =lax.broadcasted_iota(I32, (TM, LANES), 1)
    valid = lane < N_EXPERTS
    neg = -jnp.inf
    scores = jax.nn.sigmoid(_dot_f32(h, router_ref[...]))
    biased = jnp.where(valid, scores + bias_ref[...], neg)

    def group_reduce(t, op):
        s = 1
        while s < PER_GROUP:
            partner = jnp.where((lane & s) == 0, pltpu.roll(t, LANES - s, 1), pltpu.roll(t, s, 1))
            t = op(t, partner)
            s *= 2
        return t

    lane_f = lane.astype(F32)
    group_f = jnp.floor(lane_f * (1.0 / PER_GROUP))

    def first_lane_of_max(t):
        m = jnp.max(t, axis=-1, keepdims=True)
        return jnp.min(jnp.where(t == m, lane_f, float(LANES)), axis=-1, keepdims=True)

    m1 = group_reduce(biased, jnp.maximum)
    first1 = group_reduce(jnp.where(biased == m1, lane_f, float(LANES)), jnp.minimum)
    m2 = group_reduce(jnp.where(lane_f == first1, neg, biased), jnp.maximum)
    gscore = jnp.where(valid, m1 + m2, neg)
    cand = jnp.full((TM, LANES), neg, F32)
    for _ in range(TOPK_GROUPS):
        gsel = group_f == jnp.floor(first_lane_of_max(gscore) * (1.0 / PER_GROUP))
        cand = jnp.where(gsel, biased, cand)
        gscore = jnp.where(gsel, neg, gscore)
    hits = []
    sel01 = jnp.zeros((TM, LANES), F32)
    e_cols = jnp.zeros((TM, LANES), F32)
    for j in range(TOP_K):
        fl = first_lane_of_max(cand)
        hit = lane_f == fl
        hits.append(hit)
        sel01 = jnp.where(hit, 1.0, sel01)
        cand = jnp.where(hit, neg, cand)
        e_cols = jnp.where(lane == j, fl, e_cols)
    wsum = jnp.sum(sel01 * scores, axis=-1, keepdims=True)

    ri = lax.broadcasted_iota(I32, (TM, TM), 0)
    rj = lax.broadcasted_iota(I32, (TM, TM), 1)
    below = jnp.where(rj < ri, 1.0, 0.0).astype(BF16)
    rank = jnp.dot(below, sel01.astype(BF16), preferred_element_type=F32) + carry[...]
    carry[...] = carry[...] + jnp.sum(sel01, axis=0, keepdims=True)
    w_cols = jnp.zeros((TM, LANES), F32)
    p_cols = jnp.zeros((TM, LANES), F32)
    for j in range(TOP_K):
        wj = jnp.sum(jnp.where(hits[j], scores, 0.0), axis=-1, keepdims=True)
        w_cols = jnp.where(lane == j, wj / wsum * ROUTED_SCALE, w_cols)
        p_cols = jnp.where(lane == j, jnp.sum(jnp.where(hits[j], rank, 0.0), axis=-1, keepdims=True), p_cols)
    e_o[...] = e_cols.astype(I32)
    w_o[...] = w_cols
    p_o[...] = p_cols.astype(I32)
    cnt_o[...] = jnp.broadcast_to(carry[...], cnt_o.shape)


def _moe_route(x, mod, p_tiles, tiles_per_seq, norm_g, router, bias):
    n = x.shape[0]
    full = lambda a: pl.BlockSpec(a.shape, lambda i: (0,) * a.ndim)
    tok = lambda w: pl.BlockSpec((TM, w), lambda i: (i, 0))
    router_p = jnp.pad(router, ((0, 0), (0, LANES - N_EXPERTS)))
    bias_p = jnp.pad(bias, (0, LANES - N_EXPERTS)).reshape(1, LANES)
    return pl.pallas_call(
        _moe_route_kernel,
        grid=(n // TM,),
        in_specs=[tok(D), pl.BlockSpec((None, 1, N_MOD * D), lambda i: (_mod_row(i, p_tiles, tiles_per_seq), 0, 0)),
                  full(norm_g.reshape(1, D)), full(router_p), full(bias_p)],
        out_specs=[pl.BlockSpec((TM, ROW_SUB, LANES), lambda i: (i, 0, 0)), tok(LANES), tok(LANES), tok(LANES),
                   pl.BlockSpec((8, LANES), lambda i: (0, 0))],
        out_shape=[jax.ShapeDtypeStruct((n, ROW_SUB, LANES), F32), jax.ShapeDtypeStruct((n, LANES), I32),
                   jax.ShapeDtypeStruct((n, LANES), F32), jax.ShapeDtypeStruct((n, LANES), I32),
                   jax.ShapeDtypeStruct((8, LANES), F32)],
        scratch_shapes=[pltpu.VMEM((1, LANES), F32)],
        compiler_params=_cparams(1),
        name="moe_route",
    )(x, mod, norm_g.reshape(1, D), router_p, bias_p)


DISPATCH_TOK = 512


def _moe_dispatch_kernel(dest_ref, h_ref, xs_in, xs_hbm, sem):
    del xs_in

    def copy(t, kq):
        return pltpu.make_async_copy(h_ref.at[t], xs_hbm.at[dest_ref[t * TOP_K + kq]], sem)

    def issue(t, carry):
        for kq in range(TOP_K):
            copy(t, kq).start()
        return carry

    def drain(t, carry):
        for kq in range(TOP_K):
            copy(t, kq).wait()
        return carry

    lax.fori_loop(0, DISPATCH_TOK, issue, 0)
    lax.fori_loop(0, DISPATCH_TOK, drain, 0)


def _moe_dispatch(dest_flat, h, xs_init):
    n = h.shape[0]
    return pl.pallas_call(
        _moe_dispatch_kernel,
        grid=(n // DISPATCH_TOK,),
        in_specs=[pl.BlockSpec((DISPATCH_TOK * TOP_K,), lambda i: (i,), memory_space=pltpu.SMEM),
                  pl.BlockSpec((DISPATCH_TOK, ROW_SUB, LANES), lambda i: (i, 0, 0)),
                  pl.BlockSpec(memory_space=pl.ANY)],
        out_specs=pl.BlockSpec(memory_space=pl.ANY),
        out_shape=jax.ShapeDtypeStruct(xs_init.shape, xs_init.dtype),
        scratch_shapes=[pltpu.SemaphoreType.DMA],
        input_output_aliases={2: 0},
        compiler_params=_cparams(1, has_side_effects=True),
        name="moe_dispatch",
    )(dest_flat, h, xs_init)


def _moe_expert_kernel(be_ref, xs_ref, wgu_ref, wdn_ref, o_ref):
    del be_ref
    gu = _bdot(_rows_load(xs_ref), wgu_ref[...])
    act = _silu(gu[:, 0:D_EXPERT]) * gu[:, D_EXPERT:2 * D_EXPERT]
    _rows_store(o_ref, _bdot(act, wdn_ref[...]))


def _moe_expert(block_e, xs, w_gu, w_down):
    n_slots = xs.shape[0]
    n_blocks = n_slots // EXPERT_BLOCK
    rows = pl.BlockSpec((EXPERT_BLOCK, ROW_SUB, LANES), lambda i, be: (i, 0, 0))
    grid_spec = pltpu.PrefetchScalarGridSpec(
        num_scalar_prefetch=1,
        grid=(n_blocks,),
        in_specs=[rows,
                  pl.BlockSpec((None, D, 2 * D_EXPERT), lambda i, be: (be[i], 0, 0)),
                  pl.BlockSpec((None, D_EXPERT, D), lambda i, be: (be[i], 0, 0))],
        out_specs=rows,
    )
    return pl.pallas_call(
        _moe_expert_kernel,
        grid_spec=grid_spec,
        out_shape=jax.ShapeDtypeStruct((n_slots, ROW_SUB, LANES), F32),
        compiler_params=_cparams(1),
        name="moe_expert",
    )(block_e, xs, w_gu, w_down)


COMBINE_TOK = 128


def _moe_combine_kernel(dest_ref, x_ref, h_ref, w_ref, mod_ref, shgu_ref, shdn_ref, gfin_ref, ys_hbm, o_ref, buf, sem,
                        *, final_norm):
    def copy(t, kq):
        return pltpu.make_async_copy(ys_hbm.at[dest_ref[t * TOP_K + kq]], buf.at[kq, t], sem)

    def issue(t, carry):
        for kq in range(TOP_K):
            copy(t, kq).start()
        return carry

    def drain(t, carry):
        for kq in range(TOP_K):
            copy(t, kq).wait()
        return carry

    lax.fori_loop(0, COMBINE_TOK, issue, 0)
    gu = _bdot(_rows_load(h_ref), shgu_ref[...])
    acc = _bdot(_silu(gu[:, 0:D_EXPERT]) * gu[:, D_EXPERT:2 * D_EXPERT], shdn_ref[...])
    lax.fori_loop(0, COMBINE_TOK, drain, 0)
    w = w_ref[...]
    for kq in range(TOP_K):
        acc = acc + _rows_load(buf.at[kq]) * w[:, kq:kq + 1]
    out = x_ref[...] + mod_ref[:, 5 * D:6 * D] * acc
    if final_norm:
        out = out * lax.rsqrt(jnp.mean(out * out, axis=-1, keepdims=True) + NORM_EPS) * gfin_ref[...]
    o_ref[...] = out


def _moe_combine(dest_flat, x, h, w_cols, mod, p_tiles, tiles_per_seq, sh_gu, sh_down, g_final, ys, final_norm):
    n = x.shape[0]
    ratio = TM // COMBINE_TOK
    tok = lambda w: pl.BlockSpec((COMBINE_TOK, w), lambda i: (i, 0))
    full = lambda a: pl.BlockSpec(a.shape, lambda i: (0,) * a.ndim)
    return pl.pallas_call(
        functools.partial(_moe_combine_kernel, final_norm=final_norm),
        grid=(n // COMBINE_TOK,),
        in_specs=[pl.BlockSpec((COMBINE_TOK * TOP_K,), lambda i: (i,), memory_space=pltpu.SMEM),
                  tok(D), pl.BlockSpec((COMBINE_TOK, ROW_SUB, LANES), lambda i: (i, 0, 0)), tok(LANES),
                  pl.BlockSpec((None, 1, N_MOD * D), lambda i: (_mod_row(i // ratio, p_tiles, tiles_per_seq), 0, 0)),
                  full(sh_gu), full(sh_down), full(g_final),
                  pl.BlockSpec(memory_space=pl.ANY)],
        out_specs=tok(D),
        out_shape=jax.ShapeDtypeStruct((n, D), F32),
        scratch_shapes=[pltpu.VMEM((TOP_K, COMBINE_TOK, ROW_SUB, LANES), F32), pltpu.SemaphoreType.DMA],
        compiler_params=_cparams(1),
        name="moe_combine",
    )(dest_flat, x, h, w_cols, mod, sh_gu, sh_down, g_final, ys)


def _moe_layer(x, mod, p_tiles, tiles_per_seq, norm_g, router, bias, w_gu, w_down, sh_gu, sh_down, g_final, final_norm):
    n = x.shape[0]
    h, e_cols, w_cols, p_cols, counts = _moe_route(x, mod, p_tiles, tiles_per_seq, norm_g, router, bias)
    counts = counts[0, :N_EXPERTS].astype(I32)
    padded = (counts + EXPERT_BLOCK - 1) // EXPERT_BLOCK * EXPERT_BLOCK
    pad_end = jnp.cumsum(padded)
    pad_start = pad_end - padded
    dest = (pad_start[e_cols[:, :TOP_K]] + p_cols[:, :TOP_K]).reshape(-1)
    n_blocks = n * TOP_K // EXPERT_BLOCK + N_EXPERTS
    block_start = jnp.arange(n_blocks, dtype=I32) * EXPERT_BLOCK
    block_e = jnp.minimum(jnp.sum((pad_end[None, :] <= block_start[:, None]).astype(I32), axis=1), N_EXPERTS - 1)
    xs = _moe_dispatch(dest, h, jnp.zeros((n_blocks * EXPERT_BLOCK, ROW_SUB, LANES), F32))
    ys = _moe_expert(block_e, xs, w_gu, w_down)
    return _moe_combine(dest, x, h, w_cols, mod, p_tiles, tiles_per_seq, sh_gu, sh_down, g_final.reshape(1, D), ys,
                        final_norm)


def _pair_pack(s):
    lead = s.shape[:-3]
    s = s.reshape(lead + (N_PAIR, 2, HEAD_A, HEAD_A))
    z = jnp.zeros_like(s[..., 0, :, :])
    top = jnp.concatenate([s[..., 0, :, :], z], axis=-1)
    bot = jnp.concatenate([z, s[..., 1, :, :]], axis=-1)
    return jnp.concatenate([top, bot], axis=-2)


def _pair_unpack(s):
    a = s[..., :HEAD_A, :HEAD_A]
    b = s[..., HEAD_A:, HEAD_A:]
    return jnp.stack([a, b], axis=-3).reshape(s.shape[:-3] + (H_A, HEAD_A, HEAD_A))


def _rope_tables(rows):
    half = DK_B // 2
    n_freq = half // 2
    inv = ROPE_BASE ** (-jnp.arange(n_freq, dtype=F32) / n_freq)
    pos_r = jnp.repeat(jnp.arange(rows, dtype=F32), GRID_W)
    pos_c = jnp.tile(jnp.arange(GRID_W, dtype=F32), rows)

    def tab(pos):
        ang = pos[:, None] * inv[None, :]
        c, s = jnp.cos(ang), jnp.sin(ang)
        return jnp.concatenate([c, c], -1), jnp.concatenate([-s, s], -1)

    cr, sr = tab(pos_r)
    cc, sc = tab(pos_c)
    cos = jnp.concatenate([cr, cc], -1)
    sin = jnp.concatenate([sr, sc], -1)
    cos = jnp.concatenate([jnp.ones((TM, DK_B), F32), cos], 0)
    sin = jnp.concatenate([jnp.zeros((TM, DK_B), F32), sin], 0)
    return cos, sin


def kernel(x_prompt, x_sample, state_rwkv, state_ret, c, c_ctx, ada_w, ada_b, norm_mix, norm_ffn, norm_final, rwkv_mu, rwkv_w0, rwkv_w1, rwkv_w2, rwkv_a0, rwkv_a1, rwkv_a2, rwkv_wrkv, rwkv_wo, rwkv_g1, rwkv_g2, rwkv_k_k, rwkv_k_a, rwkv_r_k, rwkv_ln_w, rwkv_ln_b, ret_w_in, ret_w_out, ret_decay_logit, moe_router, moe_bias, moe_w_gu, moe_w_down, moe_sh_gu, moe_sh_down):
    bp, tp, _ = x_prompt.shape
    bs, ts, _ = x_sample.shape
    n_p, n_s = bp * tp, bs * ts
    assert tp == TM and ts % TM == 0 and TM % GRID_W == 0
    p_tiles = n_p // TM
    tiles_per_seq = ts // TM
    rows = ts // GRID_W
    x = jnp.concatenate([x_prompt.reshape(n_p, D), x_sample.reshape(n_s, D)], axis=0)
    n = n_p + n_s

    n_cond = 16
    cond = jnp.zeros((n_cond, D), F32).at[0].set(c_ctx).at[1:1 + bs].set(c)
    mod = _modulation(cond, ada_w, ada_b).reshape(ada_w.shape[0], n_cond, 1, N_MOD * D)

    head_of = jnp.arange(D, dtype=I32) // HEAD_A
    hsum = (head_of[:, None] == jnp.arange(LANES, dtype=I32)[None, :]).astype(BF16)
    hexp = hsum.T

    gl = rwkv_g1.shape[-1]
    glp = -(-gl // LANES) * LANES
    zb = lambda a: jnp.zeros_like(a)
    wts = {
        "mu": jnp.pad(rwkv_mu[0], ((0, 8 - N_MOD), (0, 0))),
        "wrkv": rwkv_wrkv[0].astype(BF16),
        "g1": jnp.pad(rwkv_g1[0], ((0, 0), (0, glp - gl))).astype(BF16),
        "g2": jnp.pad(rwkv_g2[0], ((0, glp - gl), (0, 0))).astype(BF16),
        "w1": jnp.concatenate([rwkv_w1[0, 0], rwkv_w1[0, 1]], axis=1).astype(BF16),
        "w2": jnp.concatenate([jnp.concatenate([rwkv_w2[0, 0], zb(rwkv_w2[0, 1])], 1),
                               jnp.concatenate([zb(rwkv_w2[0, 0]), rwkv_w2[0, 1]], 1)], 0).astype(BF16),
        "w0": rwkv_w0[0].reshape(1, 2 * D),
        "a1": jnp.concatenate([rwkv_a1[0, 0], rwkv_a1[0, 1]], axis=1).astype(BF16),
        "a2": jnp.concatenate([jnp.concatenate([rwkv_a2[0, 0], zb(rwkv_a2[0, 1])], 1),
                               jnp.concatenate([zb(rwkv_a2[0, 0]), rwkv_a2[0, 1]], 1)], 0).astype(BF16),
        "a0": rwkv_a0[0].reshape(1, 2 * D),
        "k_k": rwkv_k_k[0], "k_a": rwkv_k_a[0], "r_k": rwkv_r_k[0].reshape(D),
        "hsum": hsum, "hexp": hexp,
    }
    r, v, kk, lw, kd, b, gate, bonus = _rwkv_pre(x, mod[0], p_tiles, tiles_per_seq, norm_mix[0], wts)
    s0_lat = _pair_pack(jnp.moveaxis(state_rwkv[:, 0], 1, 0))
    y2, s_fin = _rwkv_scan(r, v, kk, lw, kd, b, s0_lat,
                           _ScanItems(bp, tp // RWKV_CHUNK, bs, ts // RWKV_CHUNK))
    new_state_rwkv = jnp.moveaxis(_pair_unpack(s_fin), 0, 1)[:, None]
    x = _rwkv_post(x, y2, gate, bonus, mod[0], p_tiles, tiles_per_seq, rwkv_ln_w[0], rwkv_ln_b[0],
                   rwkv_wo[0].astype(BF16), hsum, hexp)
    x = _moe_layer(x, mod[0], p_tiles, tiles_per_seq, norm_ffn[0], moe_router[0], moe_bias[0],
                   moe_w_gu[0].astype(BF16), moe_w_down[0].astype(BF16), moe_sh_gu[0].astype(BF16),
                   moe_sh_down[0].astype(BF16), norm_final, False)

    cos_t, sin_t = _rope_tables(rows)
    q, k, vv, rgate = _ret_pre(x, mod[1], p_tiles, tiles_per_seq, norm_mix[1], cos_t, sin_t, ret_w_in[0].astype(BF16))
    log_gamma = jax.nn.log_sigmoid(ret_decay_logit[0].astype(F32))
    o2, r_fin = _ret_scan(log_gamma, q, k, vv, jnp.moveaxis(state_ret[:, 0], 1, 0),
                          _ScanItems(bp, tp // RET_CHUNK, bs, ts // RET_CHUNK))
    new_state_ret = jnp.moveaxis(r_fin, 0, 1)[:, None]
    x = _ret_post(x, o2, rgate, mod[1], p_tiles, tiles_per_seq, ret_w_out[0].astype(BF16))
    x = _moe_layer(x, mod[1], p_tiles, tiles_per_seq, norm_ffn[1], moe_router[1], moe_bias[1],
                   moe_w_gu[1].astype(BF16), moe_w_down[1].astype(BF16), moe_sh_gu[1].astype(BF16),
                   moe_sh_down[1].astype(BF16), norm_final, True)

    return (x[:n_p].reshape(bp, tp, D), x[n_p:].reshape(bs, ts, D), new_state_rwkv, new_state_ret)
```

```python
import functools

import jax
import jax.numpy as jnp
from jax import lax
from jax.experimental import pallas as pl
from jax.experimental.pallas import tpu as pltpu

F32, BF16, I32 = jnp.float32, jnp.bfloat16, jnp.int32

D = 1024
N_MOD = 6
NORM_EPS = 1e-6
GRID_W = 64
HEAD_A = 64
H_A = D // HEAD_A
LNX_EPS = 64e-5
RWKV_CHUNK = 64
N_PAIR = H_A // 2
H_B = 4
DK_B = D // H_B
DV_B = 2 * DK_B
RET_CHUNK = 128
ROPE_BASE = 10000.0
N_EXPERTS = 64
TOP_K = 8
N_GROUPS = 8
TOPK_GROUPS = 4
PER_GROUP = N_EXPERTS // N_GROUPS
D_EXPERT = 256
ROUTED_SCALE = 2.5
EXPERT_BLOCK = 128

TM = 256
LANES = 128
ROW_SUB = D // LANES
VMEM_LIMIT = 56 * 1024 * 1024


def _cparams(n_grid_axes, **kw):
    return pltpu.CompilerParams(dimension_semantics=("arbitrary",) * n_grid_axes, vmem_limit_bytes=VMEM_LIMIT, **kw)


def _bdot(a, b):
    return jnp.dot(a.astype(BF16), b.astype(BF16), preferred_element_type=F32)


def _bdot_nt(a, b):
    return lax.dot_general(a.astype(BF16), b.astype(BF16), (((1,), (1,)), ((), ())), preferred_element_type=F32)


def _bdot_tn(a, b):
    return lax.dot_general(a.astype(BF16), b.astype(BF16), (((0,), (0,)), ((), ())), preferred_element_type=F32)


def _split3(x):
    hi = x.astype(BF16)
    r1 = x - hi.astype(F32)
    mid = r1.astype(BF16)
    lo = (r1 - mid.astype(F32)).astype(BF16)
    return hi, mid, lo


def _dot_f32(a, b):
    ah, am, al = _split3(a)
    bh, bm, bl = _split3(b)
    d = lambda x, y: jnp.dot(x, y, preferred_element_type=F32)
    return d(ah, bh) + (d(ah, bm) + d(am, bh)) + (d(ah, bl) + d(al, bh) + d(am, bm))


def _dot_f32_rhs01(a, m01):
    ah, am, al = _split3(a)
    d = lambda x: jnp.dot(x, m01, preferred_element_type=F32)
    return d(ah) + d(am) + d(al)


def _dot_f32_lhs01(m01, b):
    bh, bm, bl = _split3(b)
    d = lambda x: jnp.dot(m01, x, preferred_element_type=F32)
    return d(bh) + d(bm) + d(bl)


def _norm_mod(x, g, shift, scale):
    ms = jnp.mean(x * x, axis=-1, keepdims=True)
    y = x * lax.rsqrt(ms + NORM_EPS) * g
    return y * (1.0 + scale) + shift


def _silu(x):
    return x * jax.nn.sigmoid(x)


def _rows_load(ref):
    m = ref.shape[0] // ROW_SUB
    return jnp.concatenate([ref[pl.ds(j, m, stride=ROW_SUB), :] for j in range(ROW_SUB)], axis=1)


def _rows_store(ref, val):
    m = val.shape[0]
    for j in range(ROW_SUB):
        ref[pl.ds(j, m, stride=ROW_SUB), :] = val[:, j * LANES:(j + 1) * LANES]


def _row_tile(ref, idx):
    return ref.at[pl.ds(pl.multiple_of(idx * ROW_SUB, ROW_SUB), ROW_SUB)]


def _mod_row(i, p_tiles, tiles_per_seq):
    return jnp.where(i < p_tiles, 0, 1 + (i - p_tiles) // tiles_per_seq)


def _mod_kernel(cond_ref, w_ref, b_ref, o_ref):
    o_ref[...] = _dot_f32(_silu(cond_ref[...]), w_ref[...]) + b_ref[...]


def _modulation(cond, ada_w, ada_b):
    depth, _, n6 = ada_w.shape
    tn = 1536
    return pl.pallas_call(
        _mod_kernel,
        grid=(depth, n6 // tn),
        in_specs=[
            pl.BlockSpec(cond.shape, lambda l, j: (0, 0)),
            pl.BlockSpec((None, D, tn), lambda l, j: (l, 0, j)),
            pl.BlockSpec((None, 1, tn), lambda l, j: (l, 0, j)),
        ],
        out_specs=pl.BlockSpec((None, cond.shape[0], tn), lambda l, j: (l, 0, j)),
        out_shape=jax.ShapeDtypeStruct((depth, cond.shape[0], n6), F32),
        compiler_params=_cparams(2),
        name="adaln_mod",
    )(cond, ada_w, ada_b.reshape(depth, 1, n6))


def _rwkv_pre_kernel(x_ref, xu_ref, xd_ref, mod_ref, g_ref, mu_ref, wrkv_ref, g1_ref, g2_ref, w1_ref, w2_ref,
                     w0_ref, a1_ref, a2_ref, a0_ref, kk_ref, ka_ref, rk_ref, hsum_ref, hexp_ref,
                     r_o, v_o, kk_o, lw_o, kd_o, b_o, gate_o, bonus_o, *, p_tiles, tiles_per_seq):
    i = pl.program_id(0)
    is_p = i < p_tiles
    sub = (i - p_tiles) % tiles_per_seq
    mod = mod_ref[...]
    shift, scale = mod[:, 0:D], mod[:, D:2 * D]
    g = g_ref[...]
    h = _norm_mod(x_ref[...], g, shift, scale)
    hu = _norm_mod(xu_ref[...], g, shift, scale)
    hd = _norm_mod(xd_ref[...], g, shift, scale)

    q = D // 4
    row = lax.broadcasted_iota(I32, (TM, 1), 0)
    per = jnp.where(is_p, TM, GRID_W)
    pos = row & (per - 1)

    def prev1(a):
        return jnp.where(pos == 0, 0.0, pltpu.roll(a, 1, 0))

    def next1(a):
        return jnp.where(pos == per - 1, 0.0, pltpu.roll(a, TM - 1, 0))

    h0, h1, h2, h3 = (h[:, j * q:(j + 1) * q] for j in range(4))
    up = jnp.concatenate([jnp.where(sub == 0, 0.0, hu[:, 2 * q:3 * q]), h2[0:TM - GRID_W]], axis=0)
    down = jnp.concatenate([h3[GRID_W:TM], jnp.where(sub == tiles_per_seq - 1, 0.0, hd[:, 3 * q:4 * q])], axis=0)
    s0 = prev1(h0)
    s1 = jnp.where(is_p, prev1(h1), next1(h1))
    s2 = jnp.where(is_p, next1(h2), up)
    s3 = jnp.where(is_p, next1(h3), down)
    xx = jnp.concatenate([s0, s1, s2, s3], axis=1) - h

    mu = mu_ref[...]
    mix = lambda j: h + xx * mu[j:j + 1]
    r = _bdot(mix(0), wrkv_ref[0])
    k = _bdot(mix(2), wrkv_ref[1])
    v = _bdot(mix(3), wrkv_ref[2])
    gate = _bdot(jax.nn.sigmoid(_bdot(mix(5), g1_ref[...])), g2_ref[...])
    w_all = w0_ref[...] + _bdot(jnp.tanh(_bdot(mix(1), w1_ref[...])), w2_ref[...])
    a_all = jax.nn.sigmoid(a0_ref[...] + _bdot(_bdot(mix(4), a1_ref[...]), a2_ref[...]))

    hsum, hexp = hsum_ref[...], hexp_ref[...]
    head_sum = lambda t: _dot_f32_rhs01(_dot_f32_rhs01(t, hsum), hexp)

    kkr = k * kk_ref[...]
    kk = kkr / jnp.maximum(jnp.sqrt(head_sum(kkr * kkr)), 1e-12)
    ka = ka_ref[...]
    kd_sum = jnp.zeros_like(k)
    for d in range(2):
        wd = w_all[:, d * D:(d + 1) * D]
        z = -wd
        softplus = jnp.maximum(z, 0.0) + jnp.log(1.0 + jnp.exp(-jnp.abs(z)))
        lw_o[d] = -jnp.exp(-softplus - 0.5)
        a = a_all[:, d * D:(d + 1) * D]
        kd = k * (1.0 + (a - 1.0) * ka)
        kd_o[d] = kd
        b_o[d] = kk * a
        kd_sum = kd_sum + kd
    r_o[...] = r
    v_o[...] = v
    kk_o[...] = kk
    gate_o[...] = gate
    bonus_o[...] = head_sum(r * kd_sum * rk_ref[...]) * v


def _rwkv_pre(x, mod, p_tiles, tiles_per_seq, norm_g, wts):
    n = x.shape[0]
    n_tiles = n // TM
    hb = TM // GRID_W
    n_hblk = n // GRID_W
    row = lambda a: a.reshape(1, -1)
    full = lambda a: pl.BlockSpec(a.shape, lambda i: (0,) * a.ndim)
    tok = pl.BlockSpec((TM, D), lambda i: (i, 0))
    tok2 = pl.BlockSpec((2, TM, D), lambda i: (0, i, 0))
    consts = [row(norm_g), wts["mu"], wts["wrkv"], wts["g1"], wts["g2"], wts["w1"], wts["w2"], wts["w0"],
              wts["a1"], wts["a2"], wts["a0"], row(wts["k_k"]), row(wts["k_a"]), row(wts["r_k"]),
              wts["hsum"], wts["hexp"]]
    return pl.pallas_call(
        functools.partial(_rwkv_pre_kernel, p_tiles=p_tiles, tiles_per_seq=tiles_per_seq),
        grid=(n_tiles,),
        in_specs=[
            tok,
            pl.BlockSpec((GRID_W, D), lambda i: (jnp.maximum(i * hb - 1, 0), 0)),
            pl.BlockSpec((GRID_W, D), lambda i: (jnp.minimum(i * hb + hb, n_hblk - 1), 0)),
            pl.BlockSpec((None, 1, N_MOD * D), lambda i: (_mod_row(i, p_tiles, tiles_per_seq), 0, 0)),
        ] + [full(a) for a in consts],
        out_specs=[tok, tok, tok, tok2, tok2, tok2, tok, tok],
        out_shape=[jax.ShapeDtypeStruct((n, D), F32)] * 3 + [jax.ShapeDtypeStruct((2, n, D), F32)] * 3
        + [jax.ShapeDtypeStruct((n, D), F32)] * 2,
        compiler_params=_cparams(1),
        name="rwkv_pre",
    )(x, x, x, mod, *consts)


class _ScanItems:
    def __init__(self, n_ctx, ctx_chunks, n_lat, lat_chunks):
        self.n_ctx, self.ctx_chunks, self.n_lat, self.lat_chunks = n_ctx, ctx_chunks, n_lat, lat_chunks
        self.ctx_items = n_ctx * ctx_chunks
        self.n_items = self.ctx_items + n_lat * lat_chunks

    def decode(self, j):
        is_ctx = j < self.ctx_items
        jl = jnp.maximum(j - self.ctx_items, 0)
        c = jnp.where(is_ctx, j % self.ctx_chunks, jl % self.lat_chunks)
        return is_ctx, c, jnp.where(is_ctx, self.ctx_chunks, self.lat_chunks)

    def block(self, d, j):
        _, c, nc = self.decode(j)
        return jnp.where(d == 0, j, j - c + (nc - 1 - c))

    def ctx_seq(self, j):
        return jnp.minimum(j // self.ctx_chunks, self.n_ctx - 1)

    def lat_seq(self, j):
        return jnp.maximum(j - self.ctx_items, 0) // self.lat_chunks


def _rwkv_scan_kernel(r_ref, v_ref, kk_ref, lw_ref, kd_ref, b_ref, s0_ref, y_ref, sf_ref, s_scr, *, items):
    d = pl.program_id(0)
    is_ctx, c, nc = items.decode(pl.program_id(1))
    rev = d == 1
    ch = RWKV_CHUNK

    @pl.when(c == 0)
    def _():
        s_scr[...] = jnp.where(is_ctx, 0.0, s0_ref[...])

    ti = lax.broadcasted_iota(I32, (ch, ch), 0)
    tj = lax.broadcasted_iota(I32, (ch, ch), 1)
    tri = jnp.where(jnp.where(rev, ti - tj, tj - ti) <= 0, 1.0, 0.0).astype(BF16)
    lw = lw_ref[...]
    cum = _dot_f32_lhs01(tri, lw)
    last = jnp.where(rev, cum[0:1], cum[ch - 1:ch])
    kk = kk_ref[...]
    kd = kd_ref[...]
    b = b_ref[...]
    e_neg = jnp.exp(-cum)
    e_rel = jnp.exp(last - cum)
    at = -kk * jnp.exp(cum - lw)
    rt = r_ref[...] * jnp.exp(cum)
    bt = b * e_neg
    kt = kd * e_neg
    bh = b * e_rel
    kh = kd * e_rel
    g_all = jnp.exp(last)
    v = v_ref[...]

    si = lax.broadcasted_iota(I32, (2 * ch, 2 * ch), 0)
    sj = lax.broadcasted_iota(I32, (2 * ch, 2 * ch), 1)
    same = (si < ch) == (sj < ch)
    ui, uj = si & (ch - 1), sj & (ch - 1)
    order = jnp.where(rev, ui - uj, uj - ui)
    strict = same & (order < 0)
    incl = same & (order <= 0)
    first = lax.broadcasted_iota(I32, (1, LANES), 1) < HEAD_A

    def stack(xp):
        return jnp.concatenate([jnp.where(first, xp, 0.0), jnp.where(first, 0.0, xp)], axis=0)

    pairs = range(N_PAIR)
    sls = [slice(p * LANES, (p + 1) * LANES) for p in pairs]
    s_old = [s_scr[p] for p in pairs]
    m1 = [_bdot_nt(jnp.concatenate([stack(at[:, sl]), stack(rt[:, sl])], axis=0),
                   jnp.concatenate([stack(bt[:, sl]), stack(kt[:, sl])], axis=0)) for sl in sls]
    m2 = [_bdot_nt(jnp.concatenate([at[:, sl], rt[:, sl]], axis=0), s_old[p])
          for p, sl in enumerate(sls)]
    vs = [stack(v[:, sl]) for sl in sls]
    x = [stack(m2[p][0:ch]) + _bdot(jnp.where(strict, m1[p][0:2 * ch, 2 * ch:4 * ch], 0.0), vs[p]) for p in pairs]
    nm = [jnp.where(strict, m1[p][0:2 * ch, 0:2 * ch], 0.0).astype(BF16) for p in pairs]
    x = [x[p] + _bdot(nm[p], x[p]) for p in pairs]
    for _ in range(5):
        nm = [_bdot(nm[p], nm[p]).astype(BF16) for p in pairs]
        x = [x[p] + _bdot(nm[p], x[p]) for p in pairs]
    uv = [jnp.concatenate([x[p], vs[p]], axis=0).astype(BF16) for p in pairs]
    ys = [_bdot(jnp.concatenate([jnp.where(incl, m1[p][2 * ch:4 * ch, 0:2 * ch], 0.0),
                                 jnp.where(incl, m1[p][2 * ch:4 * ch, 2 * ch:4 * ch], 0.0)], axis=1), uv[p])
          for p in pairs]
    s_new = [s_old[p] * g_all[:, sl]
             + _bdot_tn(uv[p], jnp.concatenate([stack(bh[:, sl]), stack(kh[:, sl])], axis=0))
             for p, sl in enumerate(sls)]
    for p, sl in enumerate(sls):
        y_ref[:, sl] = ys[p][0:ch] + ys[p][ch:2 * ch] + m2[p][ch:2 * ch]
    for p in pairs:
        s_scr[p] = s_new[p]

    @pl.when(is_ctx & (c == nc - 1))
    def _():
        sf_ref[...] = s_scr[...]


def _rwkv_scan(r, v, kk, lw, kd, b, s0, items):
    n = r.shape[0]
    ch = RWKV_CHUNK
    tok = pl.BlockSpec((ch, D), lambda d, j: (items.block(d, j), 0))
    tokd = pl.BlockSpec((None, ch, D), lambda d, j: (d, items.block(d, j), 0))
    st = lambda seq: pl.BlockSpec((None, None, N_PAIR, LANES, LANES), lambda d, j: (d, seq(j), 0, 0, 0))
    return pl.pallas_call(
        functools.partial(_rwkv_scan_kernel, items=items),
        grid=(2, items.n_items),
        in_specs=[tok, tok, tok, tokd, tokd, tokd, st(items.lat_seq)],
        out_specs=[tokd, st(items.ctx_seq)],
        out_shape=[jax.ShapeDtypeStruct((2, n, D), F32),
                   jax.ShapeDtypeStruct((2, items.n_ctx, N_PAIR, LANES, LANES), F32)],
        scratch_shapes=[pltpu.VMEM((N_PAIR, LANES, LANES), F32)],
        compiler_params=_cparams(2),
        name="rwkv_scan",
    )(r, v, kk, lw, kd, b, s0)


def _rwkv_post_kernel(x_ref, y_ref, gate_ref, bonus_ref, mod_ref, lnw_ref, lnb_ref, wo_ref, hsum_ref, hexp_ref, o_ref):
    hsum, hexp = hsum_ref[...], hexp_ref[...]
    head_mean = lambda t: _dot_f32_rhs01(_dot_f32_rhs01(t, hsum), hexp) * (1.0 / HEAD_A)
    y = y_ref[0] + y_ref[1]
    yc = y - head_mean(y)
    yn = yc * lax.rsqrt(head_mean(yc * yc) + LNX_EPS)
    z = (yn * lnw_ref[...] + lnb_ref[...] + bonus_ref[...]) * gate_ref[...]
    o_ref[...] = x_ref[...] + mod_ref[:, 2 * D:3 * D] * _bdot(z, wo_ref[...])


def _rwkv_post(x, y, gate, bonus, mod, p_tiles, tiles_per_seq, ln_w, ln_b, wo, hsum, hexp):
    n = x.shape[0]
    full = lambda a: pl.BlockSpec(a.shape, lambda i: (0,) * a.ndim)
    tok = pl.BlockSpec((TM, D), lambda i: (i, 0))
    consts = [ln_w.reshape(1, D), ln_b.reshape(1, D), wo, hsum, hexp]
    return pl.pallas_call(
        _rwkv_post_kernel,
        grid=(n // TM,),
        in_specs=[tok, pl.BlockSpec((2, TM, D), lambda i: (0, i, 0)), tok, tok,
                  pl.BlockSpec((None, 1, N_MOD * D), lambda i: (_mod_row(i, p_tiles, tiles_per_seq), 0, 0))]
        + [full(a) for a in consts],
        out_specs=tok,
        out_shape=jax.ShapeDtypeStruct((n, D), F32),
        compiler_params=_cparams(1),
        name="rwkv_post",
    )(x, y, gate, bonus, mod, *consts)


def _ret_pre_kernel(x_ref, mod_ref, g_ref, cos_ref, sin_ref, win_ref, q_o, k_o, v_o, gate_o):
    mod = mod_ref[...]
    h = _norm_mod(x_ref[...], g_ref[...], mod[:, 0:D], mod[:, D:2 * D])
    proj = _bdot(h, win_ref[...])
    cos, sin = cos_ref[...], sin_ref[...]

    def rope(t):
        outs = []
        for j in range(D // LANES):
            tj = t[:, j * LANES:(j + 1) * LANES]
            cj = cos[:, (j % 2) * LANES:(j % 2 + 1) * LANES]
            sj = sin[:, (j % 2) * LANES:(j % 2 + 1) * LANES]
            outs.append(tj * cj + pltpu.roll(tj, LANES // 2, 1) * sj)
        return jnp.concatenate(outs, axis=1)

    q_o[...] = rope(proj[:, 0:D]).astype(BF16)
    k_o[...] = rope(proj[:, D:2 * D] * (DK_B ** -0.5)).astype(BF16)
    v_o[...] = proj[:, 2 * D:4 * D].astype(BF16)
    gate_o[...] = proj[:, 4 * D:6 * D]


def _ret_pre(x, mod, p_tiles, tiles_per_seq, norm_g, cos_t, sin_t, w_in):
    n = x.shape[0]
    full = lambda a: pl.BlockSpec(a.shape, lambda i: (0,) * a.ndim)
    tok = lambda w: pl.BlockSpec((TM, w), lambda i: (i, 0))
    tab = pl.BlockSpec((TM, DK_B), lambda i: (jnp.where(i < p_tiles, 0, 1 + (i - p_tiles) % tiles_per_seq), 0))
    return pl.pallas_call(
        _ret_pre_kernel,
        grid=(n // TM,),
        in_specs=[tok(D), pl.BlockSpec((None, 1, N_MOD * D), lambda i: (_mod_row(i, p_tiles, tiles_per_seq), 0, 0)),
                  full(norm_g.reshape(1, D)), tab, tab, full(w_in)],
        out_specs=[tok(D), tok(D), tok(2 * D), tok(2 * D)],
        out_shape=[jax.ShapeDtypeStruct((n, D), BF16), jax.ShapeDtypeStruct((n, D), BF16),
                   jax.ShapeDtypeStruct((n, 2 * D), BF16), jax.ShapeDtypeStruct((n, 2 * D), F32)],
        compiler_params=_cparams(1),
        name="ret_pre",
    )(x, mod, norm_g.reshape(1, D), cos_t, sin_t, w_in)


def _ret_scan_kernel(lg_ref, q_ref, k_ref, v_ref, s0_ref, o_ref, sf_ref, s_scr, *, items):
    d = pl.program_id(0)
    is_ctx, c, nc = items.decode(pl.program_id(1))
    rev = d == 1
    ch = RET_CHUNK

    @pl.when(c == 0)
    def _():
        s_scr[...] = jnp.where(is_ctx, 0.0, s0_ref[...])

    ti = lax.broadcasted_iota(I32, (ch, ch), 0)
    tj = lax.broadcasted_iota(I32, (ch, ch), 1)
    rel = jnp.where(rev, tj - ti, ti - tj).astype(F32)
    steps_q = jnp.where(rev, ch - ti, ti + 1).astype(F32)
    steps_k = jnp.where(rev, ti, ch - 1 - ti).astype(F32)
    for hd in range(H_B):
        lg = lg_ref[d, hd]
        mask = jnp.where(rel >= 0, jnp.exp(jnp.maximum(rel, 0.0) * lg), 0.0)
        q_dec = jnp.exp(steps_q * lg)
        k_dec = jnp.exp(steps_k * lg)
        chunk_dec = jnp.exp(jnp.full((1, DV_B), float(ch), F32) * lg)
        qh = q_ref[:, hd * DK_B:(hd + 1) * DK_B]
        kh = k_ref[:, hd * DK_B:(hd + 1) * DK_B]
        vh = v_ref[:, hd * DV_B:(hd + 1) * DV_B]
        s = s_scr[hd]
        scores = _bdot_nt(qh, kh) * mask
        inner = _bdot(scores, vh)
        cross = _bdot(qh, s) * jnp.concatenate([q_dec] * (DV_B // ch), axis=1)
        o_ref[:, hd * DV_B:(hd + 1) * DV_B] = inner + cross
        kdec = kh.astype(F32) * jnp.concatenate([k_dec] * (DK_B // ch), axis=1)
        s_scr[hd] = s * chunk_dec + _bdot_tn(kdec, vh)

    @pl.when(is_ctx & (c == nc - 1))
    def _():
        sf_ref[...] = s_scr[...]


def _ret_scan(log_gamma, q, k, v, s0, items):
    n = q.shape[0]
    ch = RET_CHUNK
    tok = lambda w: pl.BlockSpec((ch, w), lambda d, j: (items.block(d, j), 0))
    st = lambda seq: pl.BlockSpec((None, None, H_B, DK_B, DV_B), lambda d, j: (d, seq(j), 0, 0, 0))
    return pl.pallas_call(
        functools.partial(_ret_scan_kernel, items=items),
        grid=(2, items.n_items),
        in_specs=[pl.BlockSpec(memory_space=pltpu.SMEM), tok(D), tok(D), tok(2 * D), st(items.lat_seq)],
        out_specs=[pl.BlockSpec((None, ch, 2 * D), lambda d, j: (d, items.block(d, j), 0)), st(items.ctx_seq)],
        out_shape=[jax.ShapeDtypeStruct((2, n, 2 * D), F32),
                   jax.ShapeDtypeStruct((2, items.n_ctx, H_B, DK_B, DV_B), F32)],
        scratch_shapes=[pltpu.VMEM((H_B, DK_B, DV_B), F32)],
        compiler_params=_cparams(2),
        name="ret_scan",
    )(log_gamma, q, k, v, s0)


def _ret_post_kernel(x_ref, o_ref, gate_ref, mod_ref, wout_ref, out_ref):
    o = o_ref[0] + o_ref[1]
    parts = []
    for hd in range(H_B):
        oh = o[:, hd * DV_B:(hd + 1) * DV_B]
        parts.append(oh * lax.rsqrt(jnp.mean(oh * oh, axis=-1, keepdims=True) + NORM_EPS))
    y = _silu(gate_ref[...]) * jnp.concatenate(parts, axis=1)
    out_ref[...] = x_ref[...] + mod_ref[:, 2 * D:3 * D] * _bdot(y, wout_ref[...])


def _ret_post(x, o, gate, mod, p_tiles, tiles_per_seq, w_out):
    n = x.shape[0]
    tok = lambda w: pl.BlockSpec((TM, w), lambda i: (i, 0))
    return pl.pallas_call(
        _ret_post_kernel,
        grid=(n // TM,),
        in_specs=[tok(D), pl.BlockSpec((2, TM, 2 * D), lambda i: (0, i, 0)), tok(2 * D),
                  pl.BlockSpec((None, 1, N_MOD * D), lambda i: (_mod_row(i, p_tiles, tiles_per_seq), 0, 0)),
                  pl.BlockSpec(w_out.shape, lambda i: (0, 0))],
        out_specs=tok(D),
        out_shape=jax.ShapeDtypeStruct((n, D), F32),
        compiler_params=_cparams(1),
        name="ret_post",
    )(x, o, gate, mod, w_out)


def _moe_route_kernel(x_ref, mod_ref, g_ref, router_ref, bias_ref, h_o, e_o, w_o, p_o, cnt_o, carry):
    i = pl.program_id(0)

    @pl.when(i == 0)
    def _():
        carry[...] = jnp.zeros_like(carry)

    mod = mod_ref[...]
    h = _norm_mod(x_ref[...], g_ref[...], mod[:, 3 * D:4 * D], mod[:, 4 * D:5 * D])
    _rows_store(h_o, h)
    lane =lax.broadcasted_iota(I32, (TM, LANES), 1)
    valid = lane < N_EXPERTS
    neg = -jnp.inf
    scores = jax.nn.sigmoid(_dot_f32(h, router_ref[...]))
    biased = jnp.where(valid, scores + bias_ref[...], neg)

    def group_reduce(t, op):
        s = 1
        while s < PER_GROUP:
            partner = jnp.where((lane & s) == 0, pltpu.roll(t, LANES - s, 1), pltpu.roll(t, s, 1))
            t = op(t, partner)
            s *= 2
        return t

    lane_f = lane.astype(F32)
    group_f = jnp.floor(lane_f * (1.0 / PER_GROUP))

    def first_lane_of_max(t):
        m = jnp.max(t, axis=-1, keepdims=True)
        return jnp.min(jnp.where(t == m, lane_f, float(LANES)), axis=-1, keepdims=True)

    m1 = group_reduce(biased, jnp.maximum)
    first1 = group_reduce(jnp.where(biased == m1, lane_f, float(LANES)), jnp.minimum)
    m2 = group_reduce(jnp.where(lane_f == first1, neg, biased), jnp.maximum)
    gscore = jnp.where(valid, m1 + m2, neg)
    cand = jnp.full((TM, LANES), neg, F32)
    for _ in range(TOPK_GROUPS):
        gsel = group_f == jnp.floor(first_lane_of_max(gscore) * (1.0 / PER_GROUP))
        cand = jnp.where(gsel, biased, cand)
        gscore = jnp.where(gsel, neg, gscore)
    hits = []
    sel01 = jnp.zeros((TM, LANES), F32)
    e_cols = jnp.zeros((TM, LANES), F32)
    for j in range(TOP_K):
        fl = first_lane_of_max(cand)
        hit = lane_f == fl
        hits.append(hit)
        sel01 = jnp.where(hit, 1.0, sel01)
        cand = jnp.where(hit, neg, cand)
        e_cols = jnp.where(lane == j, fl, e_cols)
    wsum = jnp.sum(sel01 * scores, axis=-1, keepdims=True)

    ri = lax.broadcasted_iota(I32, (TM, TM), 0)
    rj = lax.broadcasted_iota(I32, (TM, TM), 1)
    below = jnp.where(rj < ri, 1.0, 0.0).astype(BF16)
    rank = jnp.dot(below, sel01.astype(BF16), preferred_element_type=F32) + carry[...]
    carry[...] = carry[...] + jnp.sum(sel01, axis=0, keepdims=True)
    w_cols = jnp.zeros((TM, LANES), F32)
    p_cols = jnp.zeros((TM, LANES), F32)
    for j in range(TOP_K):
        wj = jnp.sum(jnp.where(hits[j], scores, 0.0), axis=-1, keepdims=True)
        w_cols = jnp.where(lane == j, wj / wsum * ROUTED_SCALE, w_cols)
        p_cols = jnp.where(lane == j, jnp.sum(jnp.where(hits[j], rank, 0.0), axis=-1, keepdims=True), p_cols)
    e_o[...] = e_cols.astype(I32)
    w_o[...] = w_cols
    p_o[...] = p_cols.astype(I32)
    cnt_o[...] = jnp.broadcast_to(carry[...], cnt_o.shape)


def _moe_route(x, mod, p_tiles, tiles_per_seq, norm_g, router, bias):
    n = x.shape[0]
    full = lambda a: pl.BlockSpec(a.shape, lambda i: (0,) * a.ndim)
    tok = lambda w: pl.BlockSpec((TM, w), lambda i: (i, 0))
    router_p = jnp.pad(router, ((0, 0), (0, LANES - N_EXPERTS)))
    bias_p = jnp.pad(bias, (0, LANES - N_EXPERTS)).reshape(1, LANES)
    return pl.pallas_call(
        _moe_route_kernel,
        grid=(n // TM,),
        in_specs=[tok(D), pl.BlockSpec((None, 1, N_MOD * D), lambda i: (_mod_row(i, p_tiles, tiles_per_seq), 0, 0)),
                  full(norm_g.reshape(1, D)), full(router_p), full(bias_p)],
        out_specs=[pl.BlockSpec((TM * ROW_SUB, LANES), lambda i: (i, 0)), tok(LANES), tok(LANES), tok(LANES),
                   pl.BlockSpec((8, LANES), lambda i: (0, 0))],
        out_shape=[jax.ShapeDtypeStruct((n * ROW_SUB, LANES), F32), jax.ShapeDtypeStruct((n, LANES), I32),
                   jax.ShapeDtypeStruct((n, LANES), F32), jax.ShapeDtypeStruct((n, LANES), I32),
                   jax.ShapeDtypeStruct((8, LANES), F32)],
        scratch_shapes=[pltpu.VMEM((1, LANES), F32)],
        compiler_params=_cparams(1),
        name="moe_route",
    )(x, mod, norm_g.reshape(1, D), router_p, bias_p)


DISPATCH_TOK = 512


def _moe_dispatch_kernel(dest_ref, h_ref, xs_in, xs_hbm, sem):
    del xs_in

    def copy(t, kq):
        return pltpu.make_async_copy(_row_tile(h_ref, t), _row_tile(xs_hbm, dest_ref[t * TOP_K + kq]), sem)

    def issue(t, carry):
        for kq in range(TOP_K):
            copy(t, kq).start()
        return carry

    def drain(t, carry):
        for kq in range(TOP_K):
            copy(t, kq).wait()
        return carry

    lax.fori_loop(0, DISPATCH_TOK, issue, 0)
    lax.fori_loop(0, DISPATCH_TOK, drain, 0)


def _moe_dispatch(dest_flat, h, xs_init):
    n = h.shape[0] // ROW_SUB
    return pl.pallas_call(
        _moe_dispatch_kernel,
        grid=(n // DISPATCH_TOK,),
        in_specs=[pl.BlockSpec((DISPATCH_TOK * TOP_K,), lambda i: (i,), memory_space=pltpu.SMEM),
                  pl.BlockSpec((DISPATCH_TOK * ROW_SUB, LANES), lambda i: (i, 0)),
                  pl.BlockSpec(memory_space=pl.ANY)],
        out_specs=pl.BlockSpec(memory_space=pl.ANY),
        out_shape=jax.ShapeDtypeStruct(xs_init.shape, xs_init.dtype),
        scratch_shapes=[pltpu.SemaphoreType.DMA],
        input_output_aliases={2: 0},
        compiler_params=_cparams(1, has_side_effects=True),
        name="moe_dispatch",
    )(dest_flat, h, xs_init)


def _moe_expert_kernel(be_ref, xs_ref, wgu_ref, wdn_ref, o_ref):
    del be_ref
    gu = _bdot(_rows_load(xs_ref), wgu_ref[...])
    act = _silu(gu[:, 0:D_EXPERT]) * gu[:, D_EXPERT:2 * D_EXPERT]
    _rows_store(o_ref, _bdot(act, wdn_ref[...]))


def _moe_expert(block_e, xs, w_gu, w_down):
    n_slots = xs.shape[0] // ROW_SUB
    n_blocks = n_slots // EXPERT_BLOCK
    rows = pl.BlockSpec((EXPERT_BLOCK * ROW_SUB, LANES), lambda i, be: (i, 0))
    grid_spec = pltpu.PrefetchScalarGridSpec(
        num_scalar_prefetch=1,
        grid=(n_blocks,),
        in_specs=[rows,
                  pl.BlockSpec((None, D, 2 * D_EXPERT), lambda i, be: (be[i], 0, 0)),
                  pl.BlockSpec((None, D_EXPERT, D), lambda i, be: (be[i], 0, 0))],
        out_specs=rows,
    )
    return pl.pallas_call(
        _moe_expert_kernel,
        grid_spec=grid_spec,
        out_shape=jax.ShapeDtypeStruct((n_slots * ROW_SUB, LANES), F32),
        compiler_params=_cparams(1),
        name="moe_expert",
    )(block_e, xs, w_gu, w_down)


COMBINE_TOK = 128


def _moe_combine_kernel(dest_ref, x_ref, h_ref, w_ref, mod_ref, shgu_ref, shdn_ref, gfin_ref, ys_hbm, o_ref, buf, sem,
                        *, final_norm):
    def copy(t, kq):
        return pltpu.make_async_copy(_row_tile(ys_hbm, dest_ref[t * TOP_K + kq]), _row_tile(buf.at[kq], t), sem)

    def issue(t, carry):
        for kq in range(TOP_K):
            copy(t, kq).start()
        return carry

    def drain(t, carry):
        for kq in range(TOP_K):
            copy(t, kq).wait()
        return carry

    lax.fori_loop(0, COMBINE_TOK, issue, 0)
    gu = _bdot(_rows_load(h_ref), shgu_ref[...])
    acc = _bdot(_silu(gu[:, 0:D_EXPERT]) * gu[:, D_EXPERT:2 * D_EXPERT], shdn_ref[...])
    lax.fori_loop(0, COMBINE_TOK, drain, 0)
    w = w_ref[...]
    for kq in range(TOP_K):
        acc = acc + _rows_load(buf.at[kq]) * w[:, kq:kq + 1]
    out = x_ref[...] + mod_ref[:, 5 * D:6 * D] * acc
    if final_norm:
        out = out * lax.rsqrt(jnp.mean(out * out, axis=-1, keepdims=True) + NORM_EPS) * gfin_ref[...]
    o_ref[...] = out


def _moe_combine(dest_flat, x, h, w_cols, mod, p_tiles, tiles_per_seq, sh_gu, sh_down, g_final, ys, final_norm):
    n = x.shape[0]
    ratio = TM // COMBINE_TOK
    tok = lambda w: pl.BlockSpec((COMBINE_TOK, w), lambda i: (i, 0))
    full = lambda a: pl.BlockSpec(a.shape, lambda i: (0,) * a.ndim)
    return pl.pallas_call(
        functools.partial(_moe_combine_kernel, final_norm=final_norm),
        grid=(n // COMBINE_TOK,),
        in_specs=[pl.BlockSpec((COMBINE_TOK * TOP_K,), lambda i: (i,), memory_space=pltpu.SMEM),
                  tok(D), pl.BlockSpec((COMBINE_TOK * ROW_SUB, LANES), lambda i: (i, 0)), tok(LANES),
                  pl.BlockSpec((None, 1, N_MOD * D), lambda i: (_mod_row(i // ratio, p_tiles, tiles_per_seq), 0, 0)),
                  full(sh_gu), full(sh_down), full(g_final),
                  pl.BlockSpec(memory_space=pl.ANY)],
        out_specs=tok(D),
        out_shape=jax.ShapeDtypeStruct((n, D), F32),
        scratch_shapes=[pltpu.VMEM((TOP_K, COMBINE_TOK * ROW_SUB, LANES), F32), pltpu.SemaphoreType.DMA],
        compiler_params=_cparams(1),
        name="moe_combine",
    )(dest_flat, x, h, w_cols, mod, sh_gu, sh_down, g_final, ys)


def _moe_layer(x, mod, p_tiles, tiles_per_seq, norm_g, router, bias, w_gu, w_down, sh_gu, sh_down, g_final, final_norm):
    n = x.shape[0]
    h, e_cols, w_cols, p_cols, counts = _moe_route(x, mod, p_tiles, tiles_per_seq, norm_g, router, bias)
    counts = counts[0, :N_EXPERTS].astype(I32)
    padded = (counts + EXPERT_BLOCK - 1) // EXPERT_BLOCK * EXPERT_BLOCK
    pad_end = jnp.cumsum(padded)
    pad_start = pad_end - padded
    dest = (pad_start[e_cols[:, :TOP_K]] + p_cols[:, :TOP_K]).reshape(-1)
    n_blocks = n * TOP_K // EXPERT_BLOCK + N_EXPERTS
    block_start = jnp.arange(n_blocks, dtype=I32) * EXPERT_BLOCK
    block_e = jnp.minimum(jnp.sum((pad_end[None, :] <= block_start[:, None]).astype(I32), axis=1), N_EXPERTS - 1)
    xs = _moe_dispatch(dest, h, jnp.zeros((n_blocks * EXPERT_BLOCK * ROW_SUB, LANES), F32))
    ys = _moe_expert(block_e, xs, w_gu, w_down)
    return _moe_combine(dest, x, h, w_cols, mod, p_tiles, tiles_per_seq, sh_gu, sh_down, g_final.reshape(1, D), ys,
                        final_norm)


def _pair_pack(s):
    lead = s.shape[:-3]
    s = s.reshape(lead + (N_PAIR, 2, HEAD_A, HEAD_A))
    z = jnp.zeros_like(s[..., 0, :, :])
    top = jnp.concatenate([s[..., 0, :, :], z], axis=-1)
    bot = jnp.concatenate([z, s[..., 1, :, :]], axis=-1)
    return jnp.concatenate([top, bot], axis=-2)


def _pair_unpack(s):
    a = s[..., :HEAD_A, :HEAD_A]
    b = s[..., HEAD_A:, HEAD_A:]
    return jnp.stack([a, b], axis=-3).reshape(s.shape[:-3] + (H_A, HEAD_A, HEAD_A))


def _rope_tables(rows):
    half = DK_B // 2
    n_freq = half // 2
    inv = ROPE_BASE ** (-jnp.arange(n_freq, dtype=F32) / n_freq)
    pos_r = jnp.repeat(jnp.arange(rows, dtype=F32), GRID_W)
    pos_c = jnp.tile(jnp.arange(GRID_W, dtype=F32), rows)

    def tab(pos):
        ang = pos[:, None] * inv[None, :]
        c, s = jnp.cos(ang), jnp.sin(ang)
        return jnp.concatenate([c, c], -1), jnp.concatenate([-s, s], -1)

    cr, sr = tab(pos_r)
    cc, sc = tab(pos_c)
    cos = jnp.concatenate([cr, cc], -1)
    sin = jnp.concatenate([sr, sc], -1)
    cos = jnp.concatenate([jnp.ones((TM, DK_B), F32), cos], 0)
    sin = jnp.concatenate([jnp.zeros((TM, DK_B), F32), sin], 0)
    return cos, sin


def kernel(x_prompt, x_sample, state_rwkv, state_ret, c, c_ctx, ada_w, ada_b, norm_mix, norm_ffn, norm_final, rwkv_mu, rwkv_w0, rwkv_w1, rwkv_w2, rwkv_a0, rwkv_a1, rwkv_a2, rwkv_wrkv, rwkv_wo, rwkv_g1, rwkv_g2, rwkv_k_k, rwkv_k_a, rwkv_r_k, rwkv_ln_w, rwkv_ln_b, ret_w_in, ret_w_out, ret_decay_logit, moe_router, moe_bias, moe_w_gu, moe_w_down, moe_sh_gu, moe_sh_down):
    bp, tp, _ = x_prompt.shape
    bs, ts, _ = x_sample.shape
    n_p, n_s = bp * tp, bs * ts
    assert tp == TM and ts % TM == 0 and TM % GRID_W == 0
    p_tiles = n_p // TM
    tiles_per_seq = ts // TM
    rows = ts // GRID_W
    x = jnp.concatenate([x_prompt.reshape(n_p, D), x_sample.reshape(n_s, D)], axis=0)
    n = n_p + n_s

    n_cond = 16
    cond = jnp.zeros((n_cond, D), F32).at[0].set(c_ctx).at[1:1 + bs].set(c)
    mod = _modulation(cond, ada_w, ada_b).reshape(ada_w.shape[0], n_cond, 1, N_MOD * D)

    head_of = jnp.arange(D, dtype=I32) // HEAD_A
    hsum = (head_of[:, None] == jnp.arange(LANES, dtype=I32)[None, :]).astype(BF16)
    hexp = hsum.T

    gl = rwkv_g1.shape[-1]
    glp = -(-gl // LANES) * LANES
    zb = lambda a: jnp.zeros_like(a)
    wts = {
        "mu": jnp.pad(rwkv_mu[0], ((0, 8 - N_MOD), (0, 0))),
        "wrkv": rwkv_wrkv[0].astype(BF16),
        "g1": jnp.pad(rwkv_g1[0], ((0, 0), (0, glp - gl))).astype(BF16),
        "g2": jnp.pad(rwkv_g2[0], ((0, glp - gl), (0, 0))).astype(BF16),
        "w1": jnp.concatenate([rwkv_w1[0, 0], rwkv_w1[0, 1]], axis=1).astype(BF16),
        "w2": jnp.concatenate([jnp.concatenate([rwkv_w2[0, 0], zb(rwkv_w2[0, 1])], 1),
                               jnp.concatenate([zb(rwkv_w2[0, 0]), rwkv_w2[0, 1]], 1)], 0).astype(BF16),
        "w0": rwkv_w0[0].reshape(1, 2 * D),
        "a1": jnp.concatenate([rwkv_a1[0, 0], rwkv_a1[0, 1]], axis=1).astype(BF16),
        "a2": jnp.concatenate([jnp.concatenate([rwkv_a2[0, 0], zb(rwkv_a2[0, 1])], 1),
                               jnp.concatenate([zb(rwkv_a2[0, 0]), rwkv_a2[0, 1]], 1)], 0).astype(BF16),
        "a0": rwkv_a0[0].reshape(1, 2 * D),
        "k_k": rwkv_k_k[0], "k_a": rwkv_k_a[0], "r_k": rwkv_r_k[0].reshape(D),
        "hsum": hsum, "hexp": hexp,
    }
    r, v, kk, lw, kd, b, gate, bonus = _rwkv_pre(x, mod[0], p_tiles, tiles_per_seq, norm_mix[0], wts)
    s0_lat = _pair_pack(jnp.moveaxis(state_rwkv[:, 0], 1, 0))
    y2, s_fin = _rwkv_scan(r, v, kk, lw, kd, b, s0_lat,
                           _ScanItems(bp, tp // RWKV_CHUNK, bs, ts // RWKV_CHUNK))
    new_state_rwkv = jnp.moveaxis(_pair_unpack(s_fin), 0, 1)[:, None]
    x = _rwkv_post(x, y2, gate, bonus, mod[0], p_tiles, tiles_per_seq, rwkv_ln_w[0], rwkv_ln_b[0],
                   rwkv_wo[0].astype(BF16), hsum, hexp)
    x = _moe_layer(x, mod[0], p_tiles, tiles_per_seq, norm_ffn[0], moe_router[0], moe_bias[0],
                   moe_w_gu[0].astype(BF16), moe_w_down[0].astype(BF16), moe_sh_gu[0].astype(BF16),
                   moe_sh_down[0].astype(BF16), norm_final, False)

    cos_t, sin_t = _rope_tables(rows)
    q, k, vv, rgate = _ret_pre(x, mod[1], p_tiles, tiles_per_seq, norm_mix[1], cos_t, sin_t, ret_w_in[0].astype(BF16))
    log_gamma = jax.nn.log_sigmoid(ret_decay_logit[0].astype(F32))
    o2, r_fin = _ret_scan(log_gamma, q, k, vv, jnp.moveaxis(state_ret[:, 0], 1, 0),
                          _ScanItems(bp, tp // RET_CHUNK, bs, ts // RET_CHUNK))
    new_state_ret = jnp.moveaxis(r_fin, 0, 1)[:, None]
    x = _ret_post(x, o2, rgate, mod[1], p_tiles, tiles_per_seq, ret_w_out[0].astype(BF16))
    x = _moe_layer(x, mod[1], p_tiles, tiles_per_seq, norm_ffn[1], moe_router[1], moe_bias[1],
                   moe_w_gu[1].astype(BF16), moe_w_down[1].astype(BF16), moe_sh_gu[1].astype(BF16),
                   moe_sh_down[1].astype(BF16), norm_final, True)

    return (x[:n_p].reshape(bp, tp, D), x[n_p:].reshape(bs, ts, D), new_state_rwkv, new_state_ret)
```

```python
import functools

import jax
import jax.numpy as jnp
from jax import lax
from jax.experimental import pallas as pl
from jax.experimental.pallas import tpu as pltpu

F32, BF16, I32 = jnp.float32, jnp.bfloat16, jnp.int32

D = 1024
N_MOD = 6
NORM_EPS = 1e-6
GRID_W = 64
HEAD_A = 64
H_A = D // HEAD_A
LNX_EPS = 64e-5
RWKV_CHUNK = 64
N_PAIR = H_A // 2
H_B = 4
DK_B = D // H_B
DV_B = 2 * DK_B
RET_CHUNK = 128
ROPE_BASE = 10000.0
N_EXPERTS = 64
TOP_K = 8
N_GROUPS = 8
TOPK_GROUPS = 4
PER_GROUP = N_EXPERTS // N_GROUPS
D_EXPERT = 256
ROUTED_SCALE = 2.5
EXPERT_BLOCK = 256

TM = 256
LANES = 128
ROW_SUB = D // LANES
VMEM_LIMIT = 56 * 1024 * 1024


def _cparams(n_grid_axes, **kw):
    return pltpu.CompilerParams(dimension_semantics=("arbitrary",) * n_grid_axes, vmem_limit_bytes=VMEM_LIMIT, **kw)


def _bdot(a, b):
    return jnp.dot(a.astype(BF16), b.astype(BF16), preferred_element_type=F32)


def _bdot_nt(a, b):
    return lax.dot_general(a.astype(BF16), b.astype(BF16), (((1,), (1,)), ((), ())), preferred_element_type=F32)


def _bdot_tn(a, b):
    return lax.dot_general(a.astype(BF16), b.astype(BF16), (((0,), (0,)), ((), ())), preferred_element_type=F32)


def _split3(x):
    hi = x.astype(BF16)
    r1 = x - hi.astype(F32)
    mid = r1.astype(BF16)
    lo = (r1 - mid.astype(F32)).astype(BF16)
    return hi, mid, lo


def _dot_f32(a, b):
    ah, am, al = _split3(a)
    bh, bm, bl = _split3(b)
    d = lambda x, y: jnp.dot(x, y, preferred_element_type=F32)
    return d(ah, bh) + (d(ah, bm) + d(am, bh)) + (d(ah, bl) + d(al, bh) + d(am, bm))


def _dot_f32_rhs01(a, m01):
    ah, am, al = _split3(a)
    d = lambda x: jnp.dot(x, m01, preferred_element_type=F32)
    return d(ah) + d(am) + d(al)


def _dot_f32_lhs01(m01, b):
    bh, bm, bl = _split3(b)
    d = lambda x: jnp.dot(m01, x, preferred_element_type=F32)
    return d(bh) + d(bm) + d(bl)


def _norm_mod(x, g, shift, scale):
    ms = jnp.mean(x * x, axis=-1, keepdims=True)
    y = x * lax.rsqrt(ms + NORM_EPS) * g
    return y * (1.0 + scale) + shift


def _silu(x):
    return x * jax.nn.sigmoid(x)


def _rows_load(ref):
    m = ref.shape[0] // ROW_SUB
    return jnp.concatenate([ref[pl.ds(j, m, stride=ROW_SUB), :] for j in range(ROW_SUB)], axis=1)


def _rows_store(ref, val):
    m = val.shape[0]
    for j in range(ROW_SUB):
        ref[pl.ds(j, m, stride=ROW_SUB), :] = val[:, j * LANES:(j + 1) * LANES]


def _row_tile(ref, idx):
    return ref.at[pl.ds(pl.multiple_of(idx * ROW_SUB, ROW_SUB), ROW_SUB)]


def _mod_row(i, p_tiles, tiles_per_seq):
    return jnp.where(i < p_tiles, 0, 1 + (i - p_tiles) // tiles_per_seq)


def _mod_kernel(cond_ref, w_ref, b_ref, o_ref):
    o_ref[...] = _dot_f32(_silu(cond_ref[...]), w_ref[...]) + b_ref[...]


def _modulation(cond, ada_w, ada_b):
    depth, _, n6 = ada_w.shape
    tn = 1536
    return pl.pallas_call(
        _mod_kernel,
        grid=(depth, n6 // tn),
        in_specs=[
            pl.BlockSpec(cond.shape, lambda l, j: (0, 0)),
            pl.BlockSpec((None, D, tn), lambda l, j: (l, 0, j)),
            pl.BlockSpec((None, 1, tn), lambda l, j: (l, 0, j)),
        ],
        out_specs=pl.BlockSpec((None, cond.shape[0], tn), lambda l, j: (l, 0, j)),
        out_shape=jax.ShapeDtypeStruct((depth, cond.shape[0], n6), F32),
        compiler_params=_cparams(2),
        name="adaln_mod",
    )(cond, ada_w, ada_b.reshape(depth, 1, n6))


def _rwkv_pre_kernel(x_ref, xu_ref, xd_ref, mod_ref, g_ref, mu_ref, wrkv_ref, g1_ref, g2_ref, w1_ref, w2_ref,
                     w0_ref, a1_ref, a2_ref, a0_ref, kk_ref, ka_ref, rk_ref, hsum_ref, hexp_ref,
                     r_o, v_o, kk_o, lw_o, kd_o, b_o, gate_o, bonus_o, *, p_tiles, tiles_per_seq):
    i = pl.program_id(0)
    is_p = i < p_tiles
    sub = (i - p_tiles) % tiles_per_seq
    mod = mod_ref[...]
    shift, scale = mod[:, 0:D], mod[:, D:2 * D]
    g = g_ref[...]
    h = _norm_mod(x_ref[...], g, shift, scale)
    hu = _norm_mod(xu_ref[...], g, shift, scale)
    hd = _norm_mod(xd_ref[...], g, shift, scale)

    q = D // 4
    row = lax.broadcasted_iota(I32, (TM, 1), 0)
    per = jnp.where(is_p, TM, GRID_W)
    pos = row & (per - 1)

    def prev1(a):
        return jnp.where(pos == 0, 0.0, pltpu.roll(a, 1, 0))

    def next1(a):
        return jnp.where(pos == per - 1, 0.0, pltpu.roll(a, TM - 1, 0))

    h0, h1, h2, h3 = (h[:, j * q:(j + 1) * q] for j in range(4))
    up = jnp.concatenate([jnp.where(sub == 0, 0.0, hu[:, 2 * q:3 * q]), h2[0:TM - GRID_W]], axis=0)
    down = jnp.concatenate([h3[GRID_W:TM], jnp.where(sub == tiles_per_seq - 1, 0.0, hd[:, 3 * q:4 * q])], axis=0)
    s0 = prev1(h0)
    s1 = jnp.where(is_p, prev1(h1), next1(h1))
    s2 = jnp.where(is_p, next1(h2), up)
    s3 = jnp.where(is_p, next1(h3), down)
    xx = jnp.concatenate([s0, s1, s2, s3], axis=1) - h

    mu = mu_ref[...]
    mix = lambda j: h + xx * mu[j:j + 1]
    r = _bdot(mix(0), wrkv_ref[0])
    k = _bdot(mix(2), wrkv_ref[1])
    v = _bdot(mix(3), wrkv_ref[2])
    gate = _bdot(jax.nn.sigmoid(_bdot(mix(5), g1_ref[...])), g2_ref[...])
    w_all = w0_ref[...] + _bdot(jnp.tanh(_bdot(mix(1), w1_ref[...])), w2_ref[...])
    a_all = jax.nn.sigmoid(a0_ref[...] + _bdot(_bdot(mix(4), a1_ref[...]), a2_ref[...]))

    hsum, hexp = hsum_ref[...], hexp_ref[...]
    head_sum = lambda t: _dot_f32_rhs01(_dot_f32_rhs01(t, hsum), hexp)

    kkr = k * kk_ref[...]
    kk = kkr / jnp.maximum(jnp.sqrt(head_sum(kkr * kkr)), 1e-12)
    ka = ka_ref[...]
    kd_sum = jnp.zeros_like(k)
    for d in range(2):
        wd = w_all[:, d * D:(d + 1) * D]
        z = -wd
        softplus = jnp.maximum(z, 0.0) + jnp.log(1.0 + jnp.exp(-jnp.abs(z)))
        lw_o[d] = -jnp.exp(-softplus - 0.5)
        a = a_all[:, d * D:(d + 1) * D]
        kd = k * (1.0 + (a - 1.0) * ka)
        kd_o[d] = kd
        b_o[d] = kk * a
        kd_sum = kd_sum + kd
    r_o[...] = r
    v_o[...] = v
    kk_o[...] = kk
    gate_o[...] = gate
    bonus_o[...] = head_sum(r * kd_sum * rk_ref[...]) * v


def _rwkv_pre(x, mod, p_tiles, tiles_per_seq, norm_g, wts):
    n = x.shape[0]
    n_tiles = n // TM
    hb = TM // GRID_W
    n_hblk = n // GRID_W
    row = lambda a: a.reshape(1, -1)
    full = lambda a: pl.BlockSpec(a.shape, lambda i: (0,) * a.ndim)
    tok = pl.BlockSpec((TM, D), lambda i: (i, 0))
    tok2 = pl.BlockSpec((2, TM, D), lambda i: (0, i, 0))
    consts = [row(norm_g), wts["mu"], wts["wrkv"], wts["g1"], wts["g2"], wts["w1"], wts["w2"], wts["w0"],
              wts["a1"], wts["a2"], wts["a0"], row(wts["k_k"]), row(wts["k_a"]), row(wts["r_k"]),
              wts["hsum"], wts["hexp"]]
    return pl.pallas_call(
        functools.partial(_rwkv_pre_kernel, p_tiles=p_tiles, tiles_per_seq=tiles_per_seq),
        grid=(n_tiles,),
        in_specs=[
            tok,
            pl.BlockSpec((GRID_W, D), lambda i: (jnp.maximum(i * hb - 1, 0), 0)),
            pl.BlockSpec((GRID_W, D), lambda i: (jnp.minimum(i * hb + hb, n_hblk - 1), 0)),
            pl.BlockSpec((None, 1, N_MOD * D), lambda i: (_mod_row(i, p_tiles, tiles_per_seq), 0, 0)),
        ] + [full(a) for a in consts],
        out_specs=[tok, tok, tok, tok2, tok2, tok2, tok, tok],
        out_shape=[jax.ShapeDtypeStruct((n, D), F32)] * 3 + [jax.ShapeDtypeStruct((2, n, D), F32)] * 3
        + [jax.ShapeDtypeStruct((n, D), F32)] * 2,
        compiler_params=_cparams(1),
        name="rwkv_pre",
    )(x, x, x, mod, *consts)


class _ScanItems:
    def __init__(self, n_ctx, ctx_chunks, n_lat, lat_chunks):
        self.n_ctx, self.ctx_chunks, self.n_lat, self.lat_chunks = n_ctx, ctx_chunks, n_lat, lat_chunks
        self.ctx_items = n_ctx * ctx_chunks
        self.n_items = self.ctx_items + n_lat * lat_chunks

    def decode(self, j):
        is_ctx = j < self.ctx_items
        jl = jnp.maximum(j - self.ctx_items, 0)
        c = jnp.where(is_ctx, j % self.ctx_chunks, jl % self.lat_chunks)
        return is_ctx, c, jnp.where(is_ctx, self.ctx_chunks, self.lat_chunks)

    def block(self, d, j):
        _, c, nc = self.decode(j)
        return jnp.where(d == 0, j, j - c + (nc - 1 - c))

    def ctx_seq(self, j):
        return jnp.minimum(j // self.ctx_chunks, self.n_ctx - 1)

    def lat_seq(self, j):
        return jnp.maximum(j - self.ctx_items, 0) // self.lat_chunks


def _rwkv_scan_kernel(r_ref, v_ref, kk_ref, lw_ref, kd_ref, b_ref, s0_ref, y_ref, sf_ref, s_scr, *, items):
    d = pl.program_id(0)
    is_ctx, c, nc = items.decode(pl.program_id(1))
    rev = d == 1
    ch = RWKV_CHUNK

    @pl.when(c == 0)
    def _():
        s_scr[...] = jnp.where(is_ctx, 0.0, s0_ref[...])

    ti = lax.broadcasted_iota(I32, (ch, ch), 0)
    tj = lax.broadcasted_iota(I32, (ch, ch), 1)
    tri = jnp.where(jnp.where(rev, ti - tj, tj - ti) <= 0, 1.0, 0.0).astype(BF16)
    lw = lw_ref[...]
    cum = _dot_f32_lhs01(tri, lw)
    last = jnp.where(rev, cum[0:1], cum[ch - 1:ch])
    kk = kk_ref[...]
    kd = kd_ref[...]
    b = b_ref[...]
    e_neg = jnp.exp(-cum)
    e_rel = jnp.exp(last - cum)
    at = -kk * jnp.exp(cum - lw)
    rt = r_ref[...] * jnp.exp(cum)
    bt = b * e_neg
    kt = kd * e_neg
    bh = b * e_rel
    kh = kd * e_rel
    g_all = jnp.exp(last)
    v = v_ref[...]

    si = lax.broadcasted_iota(I32, (2 * ch, 2 * ch), 0)
    sj = lax.broadcasted_iota(I32, (2 * ch, 2 * ch), 1)
    same = (si < ch) == (sj < ch)
    ui, uj = si & (ch - 1), sj & (ch - 1)
    order = jnp.where(rev, ui - uj, uj - ui)
    strict = same & (order < 0)
    incl = same & (order <= 0)
    first = lax.broadcasted_iota(I32, (1, LANES), 1) < HEAD_A

    def stack(xp):
        return jnp.concatenate([jnp.where(first, xp, 0.0), jnp.where(first, 0.0, xp)], axis=0)

    pairs = range(N_PAIR)
    sls = [slice(p * LANES, (p + 1) * LANES) for p in pairs]
    s_old = [s_scr[p] for p in pairs]
    m1 = [_bdot_nt(jnp.concatenate([stack(at[:, sl]), stack(rt[:, sl])], axis=0),
                   jnp.concatenate([stack(bt[:, sl]), stack(kt[:, sl])], axis=0)) for sl in sls]
    m2 = [_bdot_nt(jnp.concatenate([at[:, sl], rt[:, sl]], axis=0), s_old[p])
          for p, sl in enumerate(sls)]
    vs = [stack(v[:, sl]) for sl in sls]
    x = [stack(m2[p][0:ch]) + _bdot(jnp.where(strict, m1[p][0:2 * ch, 2 * ch:4 * ch], 0.0), vs[p]) for p in pairs]
    nm = [jnp.where(strict, m1[p][0:2 * ch, 0:2 * ch], 0.0).astype(BF16) for p in pairs]
    x = [x[p] + _bdot(nm[p], x[p]) for p in pairs]
    for _ in range(5):
        nm = [_bdot(nm[p], nm[p]).astype(BF16) for p in pairs]
        x = [x[p] + _bdot(nm[p], x[p]) for p in pairs]
    uv = [jnp.concatenate([x[p], vs[p]], axis=0).astype(BF16) for p in pairs]
    ys = [_bdot(jnp.concatenate([jnp.where(incl, m1[p][2 * ch:4 * ch, 0:2 * ch], 0.0),
                                 jnp.where(incl, m1[p][2 * ch:4 * ch, 2 * ch:4 * ch], 0.0)], axis=1), uv[p])
          for p in pairs]
    s_new = [s_old[p] * g_all[:, sl]
             + _bdot_tn(uv[p], jnp.concatenate([stack(bh[:, sl]), stack(kh[:, sl])], axis=0))
             for p, sl in enumerate(sls)]
    for p, sl in enumerate(sls):
        y_ref[:, sl] = ys[p][0:ch] + ys[p][ch:2 * ch] + m2[p][ch:2 * ch]
    for p in pairs:
        s_scr[p] = s_new[p]

    @pl.when(is_ctx & (c == nc - 1))
    def _():
        sf_ref[...] = s_scr[...]


def _rwkv_scan(r, v, kk, lw, kd, b, s0, items):
    n = r.shape[0]
    ch = RWKV_CHUNK
    tok = pl.BlockSpec((ch, D), lambda d, j: (items.block(d, j), 0))
    tokd = pl.BlockSpec((None, ch, D), lambda d, j: (d, items.block(d, j), 0))
    st = lambda seq: pl.BlockSpec((None, None, N_PAIR, LANES, LANES), lambda d, j: (d, seq(j), 0, 0, 0))
    return pl.pallas_call(
        functools.partial(_rwkv_scan_kernel, items=items),
        grid=(2, items.n_items),
        in_specs=[tok, tok, tok, tokd, tokd, tokd, st(items.lat_seq)],
        out_specs=[tokd, st(items.ctx_seq)],
        out_shape=[jax.ShapeDtypeStruct((2, n, D), F32),
                   jax.ShapeDtypeStruct((2, items.n_ctx, N_PAIR, LANES, LANES), F32)],
        scratch_shapes=[pltpu.VMEM((N_PAIR, LANES, LANES), F32)],
        compiler_params=_cparams(2),
        name="rwkv_scan",
    )(r, v, kk, lw, kd, b, s0)


def _rwkv_post_kernel(x_ref, y_ref, gate_ref, bonus_ref, mod_ref, lnw_ref, lnb_ref, wo_ref, hsum_ref, hexp_ref, o_ref):
    hsum, hexp = hsum_ref[...], hexp_ref[...]
    head_mean = lambda t: _dot_f32_rhs01(_dot_f32_rhs01(t, hsum), hexp) * (1.0 / HEAD_A)
    y = y_ref[0] + y_ref[1]
    yc = y - head_mean(y)
    yn = yc * lax.rsqrt(head_mean(yc * yc) + LNX_EPS)
    z = (yn * lnw_ref[...] + lnb_ref[...] + bonus_ref[...]) * gate_ref[...]
    o_ref[...] = x_ref[...] + mod_ref[:, 2 * D:3 * D] * _bdot(z, wo_ref[...])


def _rwkv_post(x, y, gate, bonus, mod, p_tiles, tiles_per_seq, ln_w, ln_b, wo, hsum, hexp):
    n = x.shape[0]
    full = lambda a: pl.BlockSpec(a.shape, lambda i: (0,) * a.ndim)
    tok = pl.BlockSpec((TM, D), lambda i: (i, 0))
    consts = [ln_w.reshape(1, D), ln_b.reshape(1, D), wo, hsum, hexp]
    return pl.pallas_call(
        _rwkv_post_kernel,
        grid=(n // TM,),
        in_specs=[tok, pl.BlockSpec((2, TM, D), lambda i: (0, i, 0)), tok, tok,
                  pl.BlockSpec((None, 1, N_MOD * D), lambda i: (_mod_row(i, p_tiles, tiles_per_seq), 0, 0))]
        + [full(a) for a in consts],
        out_specs=tok,
        out_shape=jax.ShapeDtypeStruct((n, D), F32),
        compiler_params=_cparams(1),
        name="rwkv_post",
    )(x, y, gate, bonus, mod, *consts)


def _ret_pre_kernel(x_ref, mod_ref, g_ref, cos_ref, sin_ref, win_ref, q_o, k_o, v_o, gate_o):
    mod = mod_ref[...]
    h = _norm_mod(x_ref[...], g_ref[...], mod[:, 0:D], mod[:, D:2 * D])
    proj = _bdot(h, win_ref[...])
    cos, sin = cos_ref[...], sin_ref[...]

    def rope(t):
        outs = []
        for j in range(D // LANES):
            tj = t[:, j * LANES:(j + 1) * LANES]
            cj = cos[:, (j % 2) * LANES:(j % 2 + 1) * LANES]
            sj = sin[:, (j % 2) * LANES:(j % 2 + 1) * LANES]
            outs.append(tj * cj + pltpu.roll(tj, LANES // 2, 1) * sj)
        return jnp.concatenate(outs, axis=1)

    q_o[...] = rope(proj[:, 0:D]).astype(BF16)
    k_o[...] = rope(proj[:, D:2 * D] * (DK_B ** -0.5)).astype(BF16)
    v_o[...] = proj[:, 2 * D:4 * D].astype(BF16)
    gate_o[...] = proj[:, 4 * D:6 * D]


def _ret_pre(x, mod, p_tiles, tiles_per_seq, norm_g, cos_t, sin_t, w_in):
    n = x.shape[0]
    full = lambda a: pl.BlockSpec(a.shape, lambda i: (0,) * a.ndim)
    tok = lambda w: pl.BlockSpec((TM, w), lambda i: (i, 0))
    tab = pl.BlockSpec((TM, DK_B), lambda i: (jnp.where(i < p_tiles, 0, 1 + (i - p_tiles) % tiles_per_seq), 0))
    return pl.pallas_call(
        _ret_pre_kernel,
        grid=(n // TM,),
        in_specs=[tok(D), pl.BlockSpec((None, 1, N_MOD * D), lambda i: (_mod_row(i, p_tiles, tiles_per_seq), 0, 0)),
                  full(norm_g.reshape(1, D)), tab, tab, full(w_in)],
        out_specs=[tok(D), tok(D), tok(2 * D), tok(2 * D)],
        out_shape=[jax.ShapeDtypeStruct((n, D), BF16), jax.ShapeDtypeStruct((n, D), BF16),
                   jax.ShapeDtypeStruct((n, 2 * D), BF16), jax.ShapeDtypeStruct((n, 2 * D), F32)],
        compiler_params=_cparams(1),
        name="ret_pre",
    )(x, mod, norm_g.reshape(1, D), cos_t, sin_t, w_in)


def _ret_scan_kernel(lg_ref, q_ref, k_ref, v_ref, s0_ref, o_ref, sf_ref, s_scr, *, items):
    d = pl.program_id(0)
    is_ctx, c, nc = items.decode(pl.program_id(1))
    rev = d == 1
    ch = RET_CHUNK

    @pl.when(c == 0)
    def _():
        s_scr[...] = jnp.where(is_ctx, 0.0, s0_ref[...])

    ti = lax.broadcasted_iota(I32, (ch, ch), 0)
    tj = lax.broadcasted_iota(I32, (ch, ch), 1)
    rel = jnp.where(rev, tj - ti, ti - tj).astype(F32)
    steps_q = jnp.where(rev, ch - ti, ti + 1).astype(F32)
    steps_k = jnp.where(rev, ti, ch - 1 - ti).astype(F32)
    for hd in range(H_B):
        lg = lg_ref[d, hd]
        mask = jnp.where(rel >= 0, jnp.exp(jnp.maximum(rel, 0.0) * lg), 0.0)
        q_dec = jnp.exp(steps_q * lg)
        k_dec = jnp.exp(steps_k * lg)
        chunk_dec = jnp.exp(jnp.full((1, DV_B), float(ch), F32) * lg)
        qh = q_ref[:, hd * DK_B:(hd + 1) * DK_B]
        kh = k_ref[:, hd * DK_B:(hd + 1) * DK_B]
        vh = v_ref[:, hd * DV_B:(hd + 1) * DV_B]
        s = s_scr[hd]
        scores = _bdot_nt(qh, kh) * mask
        inner = _bdot(scores, vh)
        cross = _bdot(qh, s) * jnp.concatenate([q_dec] * (DV_B // ch), axis=1)
        o_ref[:, hd * DV_B:(hd + 1) * DV_B] = inner + cross
        kdec = kh.astype(F32) * jnp.concatenate([k_dec] * (DK_B // ch), axis=1)
        s_scr[hd] = s * chunk_dec + _bdot_tn(kdec, vh)

    @pl.when(is_ctx & (c == nc - 1))
    def _():
        sf_ref[...] = s_scr[...]


def _ret_scan(log_gamma, q, k, v, s0, items):
    n = q.shape[0]
    ch = RET_CHUNK
    tok = lambda w: pl.BlockSpec((ch, w), lambda d, j: (items.block(d, j), 0))
    st = lambda seq: pl.BlockSpec((None, None, H_B, DK_B, DV_B), lambda d, j: (d, seq(j), 0, 0, 0))
    return pl.pallas_call(
        functools.partial(_ret_scan_kernel, items=items),
        grid=(2, items.n_items),
        in_specs=[pl.BlockSpec(memory_space=pltpu.SMEM), tok(D), tok(D), tok(2 * D), st(items.lat_seq)],
        out_specs=[pl.BlockSpec((None, ch, 2 * D), lambda d, j: (d, items.block(d, j), 0)), st(items.ctx_seq)],
        out_shape=[jax.ShapeDtypeStruct((2, n, 2 * D), F32),
                   jax.ShapeDtypeStruct((2, items.n_ctx, H_B, DK_B, DV_B), F32)],
        scratch_shapes=[pltpu.VMEM((H_B, DK_B, DV_B), F32)],
        compiler_params=_cparams(2),
        name="ret_scan",
    )(log_gamma, q, k, v, s0)


def _ret_post_kernel(x_ref, o_ref, gate_ref, mod_ref, wout_ref, out_ref):
    o = o_ref[0] + o_ref[1]
    parts = []
    for hd in range(H_B):
        oh = o[:, hd * DV_B:(hd + 1) * DV_B]
        parts.append(oh * lax.rsqrt(jnp.mean(oh * oh, axis=-1, keepdims=True) + NORM_EPS))
    y = _silu(gate_ref[...]) * jnp.concatenate(parts, axis=1)
    out_ref[...] = x_ref[...] + mod_ref[:, 2 * D:3 * D] * _bdot(y, wout_ref[...])


def _ret_post(x, o, gate, mod, p_tiles, tiles_per_seq, w_out):
    n = x.shape[0]
    tok = lambda w: pl.BlockSpec((TM, w), lambda i: (i, 0))
    return pl.pallas_call(
        _ret_post_kernel,
        grid=(n // TM,),
        in_specs=[tok(D), pl.BlockSpec((2, TM, 2 * D), lambda i: (0, i, 0)), tok(2 * D),
                  pl.BlockSpec((None, 1, N_MOD * D), lambda i: (_mod_row(i, p_tiles, tiles_per_seq), 0, 0)),
                  pl.BlockSpec(w_out.shape, lambda i: (0, 0))],
        out_specs=tok(D),
        out_shape=jax.ShapeDtypeStruct((n, D), F32),
        compiler_params=_cparams(1),
        name="ret_post",
    )(x, o, gate, mod, w_out)


def _moe_route_kernel(x_ref, mod_ref, g_ref, router_ref, bias_ref, h_o, e_o, w_o, p_o, cnt_o, carry):
    i = pl.program_id(0)

    @pl.when(i == 0)
    def _():
        carry[...] = jnp.zeros_like(carry)

    mod = mod_ref[...]
    h = _norm_mod(x_ref[...], g_ref[...], mod[:, 3 * D:4 * D], mod[:, 4 * D:5 * D])
    _rows_store(h_o, h)
    lane =lax.broadcasted_iota(I32, (TM, LANES), 1)
    valid = lane < N_EXPERTS
    neg = -jnp.inf
    scores = jax.nn.sigmoid(_dot_f32(h, router_ref[...]))
    biased = jnp.where(valid, scores + bias_ref[...], neg)

    def group_reduce(t, op):
        s = 1
        while s < PER_GROUP:
            partner = jnp.where((lane & s) == 0, pltpu.roll(t, LANES - s, 1), pltpu.roll(t, s, 1))
            t = op(t, partner)
            s *= 2
        return t

    lane_f = lane.astype(F32)
    group_f = jnp.floor(lane_f * (1.0 / PER_GROUP))

    def first_lane_of_max(t):
        m = jnp.max(t, axis=-1, keepdims=True)
        return jnp.min(jnp.where(t == m, lane_f, float(LANES)), axis=-1, keepdims=True)

    m1 = group_reduce(biased, jnp.maximum)
    first1 = group_reduce(jnp.where(biased == m1, lane_f, float(LANES)), jnp.minimum)
    m2 = group_reduce(jnp.where(lane_f == first1, neg, biased), jnp.maximum)
    gscore = jnp.where(valid, m1 + m2, neg)
    cand = jnp.full((TM, LANES), neg, F32)
    for _ in range(TOPK_GROUPS):
        gsel = group_f == jnp.floor(first_lane_of_max(gscore) * (1.0 / PER_GROUP))
        cand = jnp.where(gsel, biased, cand)
        gscore = jnp.where(gsel, neg, gscore)
    hits = []
    sel01 = jnp.zeros((TM, LANES), F32)
    e_cols = jnp.zeros((TM, LANES), F32)
    for j in range(TOP_K):
        fl = first_lane_of_max(cand)
        hit = lane_f == fl
        hits.append(hit)
        sel01 = jnp.where(hit, 1.0, sel01)
        cand = jnp.where(hit, neg, cand)
        e_cols = jnp.where(lane == j, fl, e_cols)
    wsum = jnp.sum(sel01 * scores, axis=-1, keepdims=True)

    ri = lax.broadcasted_iota(I32, (TM, TM), 0)
    rj = lax.broadcasted_iota(I32, (TM, TM), 1)
    below = jnp.where(rj < ri, 1.0, 0.0).astype(BF16)
    rank = jnp.dot(below, sel01.astype(BF16), preferred_element_type=F32) + carry[...]
    carry[...] = carry[...] + jnp.sum(sel01, axis=0, keepdims=True)
    w_cols = jnp.zeros((TM, LANES), F32)
    p_cols = jnp.zeros((TM, LANES), F32)
    for j in range(TOP_K):
        wj = jnp.sum(jnp.where(hits[j], scores, 0.0), axis=-1, keepdims=True)
        w_cols = jnp.where(lane == j, wj / wsum * ROUTED_SCALE, w_cols)
        p_cols = jnp.where(lane == j, jnp.sum(jnp.where(hits[j], rank, 0.0), axis=-1, keepdims=True), p_cols)
    e_o[...] = e_cols.astype(I32)
    w_o[...] = w_cols
    p_o[...] = p_cols.astype(I32)
    cnt_o[...] = jnp.broadcast_to(carry[...], cnt_o.shape)


def _moe_route(x, mod, p_tiles, tiles_per_seq, norm_g, router, bias):
    n = x.shape[0]
    full = lambda a: pl.BlockSpec(a.shape, lambda i: (0,) * a.ndim)
    tok = lambda w: pl.BlockSpec((TM, w), lambda i: (i, 0))
    router_p = jnp.pad(router, ((0, 0), (0, LANES - N_EXPERTS)))
    bias_p = jnp.pad(bias, (0, LANES - N_EXPERTS)).reshape(1, LANES)
    return pl.pallas_call(
        _moe_route_kernel,
        grid=(n // TM,),
        in_specs=[tok(D), pl.BlockSpec((None, 1, N_MOD * D), lambda i: (_mod_row(i, p_tiles, tiles_per_seq), 0, 0)),
                  full(norm_g.reshape(1, D)), full(router_p), full(bias_p)],
        out_specs=[pl.BlockSpec((TM * ROW_SUB, LANES), lambda i: (i, 0)), tok(LANES), tok(LANES), tok(LANES),
                   pl.BlockSpec((8, LANES), lambda i: (0, 0))],
        out_shape=[jax.ShapeDtypeStruct((n * ROW_SUB, LANES), F32), jax.ShapeDtypeStruct((n, LANES), I32),
                   jax.ShapeDtypeStruct((n, LANES), F32), jax.ShapeDtypeStruct((n, LANES), I32),
                   jax.ShapeDtypeStruct((8, LANES), F32)],
        scratch_shapes=[pltpu.VMEM((1, LANES), F32)],
        compiler_params=_cparams(1),
        name="moe_route",
    )(x, mod, norm_g.reshape(1, D), router_p, bias_p)


DISPATCH_TOK = 512


def _moe_dispatch_kernel(pad_lo_ref, pad_n_ref, dest_ref, h_ref, xs_hbm, zero_scr, sem):
    @pl.when(pl.program_id(0) == 0)
    def _():
        zero_scr[...] = jnp.zeros_like(zero_scr)

        def pad_copy(e, s):
            return pltpu.make_async_copy(zero_scr, _row_tile(xs_hbm, pad_lo_ref[e] + s), sem)

        def per_expert(e, carry):
            def issue(s, c):
                pad_copy(e, s).start()
                return c

            def drain(s, c):
                pad_copy(e, s).wait()
                return c

            lax.fori_loop(0, pad_n_ref[e], issue, 0)
            lax.fori_loop(0, pad_n_ref[e], drain, 0)
            return carry

        lax.fori_loop(0, N_EXPERTS, per_expert, 0)

    def copy(t, kq):
        return pltpu.make_async_copy(_row_tile(h_ref, t), _row_tile(xs_hbm, dest_ref[t * TOP_K + kq]), sem)

    def issue(t, carry):
        for kq in range(TOP_K):
            copy(t, kq).start(priority=kq % 2)
        return carry

    def drain(t, carry):
        for kq in range(TOP_K):
            copy(t, kq).wait()
        return carry

    lax.fori_loop(0, DISPATCH_TOK, issue, 0)
    lax.fori_loop(0, DISPATCH_TOK, drain, 0)


def _moe_dispatch(pad_lo, pad_n, dest_flat, h, n_slots):
    n = h.shape[0] // ROW_SUB
    grid_spec = pltpu.PrefetchScalarGridSpec(
        num_scalar_prefetch=2,
        grid=(n // DISPATCH_TOK,),
        in_specs=[pl.BlockSpec((DISPATCH_TOK * TOP_K,), lambda i, lo, cnt: (i,), memory_space=pltpu.SMEM),
                  pl.BlockSpec((DISPATCH_TOK * ROW_SUB, LANES), lambda i, lo, cnt: (i, 0))],
        out_specs=pl.BlockSpec(memory_space=pl.ANY),
        scratch_shapes=[pltpu.VMEM((ROW_SUB, LANES), F32), pltpu.SemaphoreType.DMA],
    )
    return pl.pallas_call(
        _moe_dispatch_kernel,
        grid_spec=grid_spec,
        out_shape=jax.ShapeDtypeStruct((n_slots * ROW_SUB, LANES), F32),
        compiler_params=_cparams(1, has_side_effects=True),
        name="moe_dispatch",
    )(pad_lo, pad_n, dest_flat, h)


def _moe_expert_kernel(be_ref, xs_ref, wgu_ref, wdn_ref, o_ref):
    del be_ref
    gu = _bdot(_rows_load(xs_ref), wgu_ref[...])
    act = _silu(gu[:, 0:D_EXPERT]) * gu[:, D_EXPERT:2 * D_EXPERT]
    _rows_store(o_ref, _bdot(act, wdn_ref[...]))


def _moe_expert(block_e, xs, w_gu, w_down):
    n_slots = xs.shape[0] // ROW_SUB
    n_blocks = n_slots // EXPERT_BLOCK
    rows = pl.BlockSpec((EXPERT_BLOCK * ROW_SUB, LANES), lambda i, be: (i, 0))
    grid_spec = pltpu.PrefetchScalarGridSpec(
        num_scalar_prefetch=1,
        grid=(n_blocks,),
        in_specs=[rows,
                  pl.BlockSpec((None, D, 2 * D_EXPERT), lambda i, be: (be[i], 0, 0)),
                  pl.BlockSpec((None, D_EXPERT, D), lambda i, be: (be[i], 0, 0))],
        out_specs=rows,
    )
    return pl.pallas_call(
        _moe_expert_kernel,
        grid_spec=grid_spec,
        out_shape=jax.ShapeDtypeStruct((n_slots * ROW_SUB, LANES), F32),
        compiler_params=_cparams(1),
        name="moe_expert",
    )(block_e, xs, w_gu, w_down)


COMBINE_TOK = 128


def _moe_combine_kernel(dest_ref, x_ref, h_ref, w_ref, mod_ref, shgu_ref, shdn_ref, gfin_ref, ys_hbm, o_ref, buf, sem,
                        *, final_norm):
    def copy(t, kq):
        return pltpu.make_async_copy(_row_tile(ys_hbm, dest_ref[t * TOP_K + kq]), _row_tile(buf.at[kq], t), sem)

    def issue(t, carry):
        for kq in range(TOP_K):
            copy(t, kq).start(priority=kq % 2)
        return carry

    def drain(t, carry):
        for kq in range(TOP_K):
            copy(t, kq).wait()
        return carry

    lax.fori_loop(0, COMBINE_TOK, issue, 0)
    gu = _bdot(_rows_load(h_ref), shgu_ref[...])
    acc = _bdot(_silu(gu[:, 0:D_EXPERT]) * gu[:, D_EXPERT:2 * D_EXPERT], shdn_ref[...])
    lax.fori_loop(0, COMBINE_TOK, drain, 0)
    w = w_ref[...]
    for kq in range(TOP_K):
        acc = acc + _rows_load(buf.at[kq]) * w[:, kq:kq + 1]
    out = x_ref[...] + mod_ref[:, 5 * D:6 * D] * acc
    if final_norm:
        out = out * lax.rsqrt(jnp.mean(out * out, axis=-1, keepdims=True) + NORM_EPS) * gfin_ref[...]
    o_ref[...] = out


def _moe_combine(dest_flat, x, h, w_cols, mod, p_tiles, tiles_per_seq, sh_gu, sh_down, g_final, ys, final_norm):
    n = x.shape[0]
    ratio = TM // COMBINE_TOK
    tok = lambda w: pl.BlockSpec((COMBINE_TOK, w), lambda i: (i, 0))
    full = lambda a: pl.BlockSpec(a.shape, lambda i: (0,) * a.ndim)
    return pl.pallas_call(
        functools.partial(_moe_combine_kernel, final_norm=final_norm),
        grid=(n // COMBINE_TOK,),
        in_specs=[pl.BlockSpec((COMBINE_TOK * TOP_K,), lambda i: (i,), memory_space=pltpu.SMEM),
                  tok(D), pl.BlockSpec((COMBINE_TOK * ROW_SUB, LANES), lambda i: (i, 0)), tok(LANES),
                  pl.BlockSpec((None, 1, N_MOD * D), lambda i: (_mod_row(i // ratio, p_tiles, tiles_per_seq), 0, 0)),
                  full(sh_gu), full(sh_down), full(g_final),
                  pl.BlockSpec(memory_space=pl.ANY)],
        out_specs=tok(D),
        out_shape=jax.ShapeDtypeStruct((n, D), F32),
        scratch_shapes=[pltpu.VMEM((TOP_K, COMBINE_TOK * ROW_SUB, LANES), F32), pltpu.SemaphoreType.DMA],
        compiler_params=_cparams(1),
        name="moe_combine",
    )(dest_flat, x, h, w_cols, mod, sh_gu, sh_down, g_final, ys)


def _moe_layer(x, mod, p_tiles, tiles_per_seq, norm_g, router, bias, w_gu, w_down, sh_gu, sh_down, g_final, final_norm):
    n = x.shape[0]
    h, e_cols, w_cols, p_cols, counts = _moe_route(x, mod, p_tiles, tiles_per_seq, norm_g, router, bias)
    counts = counts[0, :N_EXPERTS].astype(I32)
    padded = (counts + EXPERT_BLOCK - 1) // EXPERT_BLOCK * EXPERT_BLOCK
    pad_end = jnp.cumsum(padded)
    pad_start = pad_end - padded
    dest = (pad_start[e_cols[:, :TOP_K]] + p_cols[:, :TOP_K]).reshape(-1)
    n_blocks = n * TOP_K // EXPERT_BLOCK + N_EXPERTS
    n_slots = n_blocks * EXPERT_BLOCK
    block_start = jnp.arange(n_blocks, dtype=I32) * EXPERT_BLOCK
    block_e = jnp.minimum(jnp.sum((pad_end[None, :] <= block_start[:, None]).astype(I32), axis=1), N_EXPERTS - 1)
    pad_n = (padded - counts).at[N_EXPERTS - 1].add(n_slots - pad_end[N_EXPERTS - 1])
    xs = _moe_dispatch(pad_start + counts, pad_n, dest, h, n_slots)
    ys = _moe_expert(block_e, xs, w_gu, w_down)
    return _moe_combine(dest, x, h, w_cols, mod, p_tiles, tiles_per_seq, sh_gu, sh_down, g_final.reshape(1, D), ys,
                        final_norm)


def _pair_pack(s):
    lead = s.shape[:-3]
    s = s.reshape(lead + (N_PAIR, 2, HEAD_A, HEAD_A))
    z = jnp.zeros_like(s[..., 0, :, :])
    top = jnp.concatenate([s[..., 0, :, :], z], axis=-1)
    bot = jnp.concatenate([z, s[..., 1, :, :]], axis=-1)
    return jnp.concatenate([top, bot], axis=-2)


def _pair_unpack(s):
    a = s[..., :HEAD_A, :HEAD_A]
    b = s[..., HEAD_A:, HEAD_A:]
    return jnp.stack([a, b], axis=-3).reshape(s.shape[:-3] + (H_A, HEAD_A, HEAD_A))


def _rope_tables(rows):
    half = DK_B // 2
    n_freq = half // 2
    inv = ROPE_BASE ** (-jnp.arange(n_freq, dtype=F32) / n_freq)
    pos_r = jnp.repeat(jnp.arange(rows, dtype=F32), GRID_W)
    pos_c = jnp.tile(jnp.arange(GRID_W, dtype=F32), rows)

    def tab(pos):
        ang = pos[:, None] * inv[None, :]
        c, s = jnp.cos(ang), jnp.sin(ang)
        return jnp.concatenate([c, c], -1), jnp.concatenate([-s, s], -1)

    cr, sr = tab(pos_r)
    cc, sc = tab(pos_c)
    cos = jnp.concatenate([cr, cc], -1)
    sin = jnp.concatenate([sr, sc], -1)
    cos = jnp.concatenate([jnp.ones((TM, DK_B), F32), cos], 0)
    sin = jnp.concatenate([jnp.zeros((TM, DK_B), F32), sin], 0)
    return cos, sin


def kernel(x_prompt, x_sample, state_rwkv, state_ret, c, c_ctx, ada_w, ada_b, norm_mix, norm_ffn, norm_final, rwkv_mu, rwkv_w0, rwkv_w1, rwkv_w2, rwkv_a0, rwkv_a1, rwkv_a2, rwkv_wrkv, rwkv_wo, rwkv_g1, rwkv_g2, rwkv_k_k, rwkv_k_a, rwkv_r_k, rwkv_ln_w, rwkv_ln_b, ret_w_in, ret_w_out, ret_decay_logit, moe_router, moe_bias, moe_w_gu, moe_w_down, moe_sh_gu, moe_sh_down):
    bp, tp, _ = x_prompt.shape
    bs, ts, _ = x_sample.shape
    n_p, n_s = bp * tp, bs * ts
    assert tp == TM and ts % TM == 0 and TM % GRID_W == 0
    p_tiles = n_p // TM
    tiles_per_seq = ts // TM
    rows = ts // GRID_W
    x = jnp.concatenate([x_prompt.reshape(n_p, D), x_sample.reshape(n_s, D)], axis=0)
    n = n_p + n_s

    n_cond = 16
    cond = jnp.zeros((n_cond, D), F32).at[0].set(c_ctx).at[1:1 + bs].set(c)
    mod = _modulation(cond, ada_w, ada_b).reshape(ada_w.shape[0], n_cond, 1, N_MOD * D)

    head_of = jnp.arange(D, dtype=I32) // HEAD_A
    hsum = (head_of[:, None] == jnp.arange(LANES, dtype=I32)[None, :]).astype(BF16)
    hexp = hsum.T

    gl = rwkv_g1.shape[-1]
    glp = -(-gl // LANES) * LANES
    zb = lambda a: jnp.zeros_like(a)
    wts = {
        "mu": jnp.pad(rwkv_mu[0], ((0, 8 - N_MOD), (0, 0))),
        "wrkv": rwkv_wrkv[0].astype(BF16),
        "g1": jnp.pad(rwkv_g1[0], ((0, 0), (0, glp - gl))).astype(BF16),
        "g2": jnp.pad(rwkv_g2[0], ((0, glp - gl), (0, 0))).astype(BF16),
        "w1": jnp.concatenate([rwkv_w1[0, 0], rwkv_w1[0, 1]], axis=1).astype(BF16),
        "w2": jnp.concatenate([jnp.concatenate([rwkv_w2[0, 0], zb(rwkv_w2[0, 1])], 1),
                               jnp.concatenate([zb(rwkv_w2[0, 0]), rwkv_w2[0, 1]], 1)], 0).astype(BF16),
        "w0": rwkv_w0[0].reshape(1, 2 * D),
        "a1": jnp.concatenate([rwkv_a1[0, 0], rwkv_a1[0, 1]], axis=1).astype(BF16),
        "a2": jnp.concatenate([jnp.concatenate([rwkv_a2[0, 0], zb(rwkv_a2[0, 1])], 1),
                               jnp.concatenate([zb(rwkv_a2[0, 0]), rwkv_a2[0, 1]], 1)], 0).astype(BF16),
        "a0": rwkv_a0[0].reshape(1, 2 * D),
        "k_k": rwkv_k_k[0], "k_a": rwkv_k_a[0], "r_k": rwkv_r_k[0].reshape(D),
        "hsum": hsum, "hexp": hexp,
    }
    r, v, kk, lw, kd, b, gate, bonus = _rwkv_pre(x, mod[0], p_tiles, tiles_per_seq, norm_mix[0], wts)
    s0_lat = _pair_pack(jnp.moveaxis(state_rwkv[:, 0], 1, 0))
    y2, s_fin = _rwkv_scan(r, v, kk, lw, kd, b, s0_lat,
                           _ScanItems(bp, tp // RWKV_CHUNK, bs, ts // RWKV_CHUNK))
    new_state_rwkv = jnp.moveaxis(_pair_unpack(s_fin), 0, 1)[:, None]
    x = _rwkv_post(x, y2, gate, bonus, mod[0], p_tiles, tiles_per_seq, rwkv_ln_w[0], rwkv_ln_b[0],
                   rwkv_wo[0].astype(BF16), hsum, hexp)
    x = _moe_layer(x, mod[0], p_tiles, tiles_per_seq, norm_ffn[0], moe_router[0], moe_bias[0],
                   moe_w_gu[0].astype(BF16), moe_w_down[0].astype(BF16), moe_sh_gu[0].astype(BF16),
                   moe_sh_down[0].astype(BF16), norm_final, False)

    cos_t, sin_t = _rope_tables(rows)
    q, k, vv, rgate = _ret_pre(x, mod[1], p_tiles, tiles_per_seq, norm_mix[1], cos_t, sin_t, ret_w_in[0].astype(BF16))
    log_gamma = jax.nn.log_sigmoid(ret_decay_logit[0].astype(F32))
    o2, r_fin = _ret_scan(log_gamma, q, k, vv, jnp.moveaxis(state_ret[:, 0], 1, 0),
                          _ScanItems(bp, tp // RET_CHUNK, bs, ts // RET_CHUNK))
    new_state_ret = jnp.moveaxis(r_fin, 0, 1)[:, None]
    x = _ret_post(x, o2, rgate, mod[1], p_tiles, tiles_per_seq, ret_w_out[0].astype(BF16))
    x = _moe_layer(x, mod[1], p_tiles, tiles_per_seq, norm_ffn[1], moe_router[1], moe_bias[1],
                   moe_w_gu[1].astype(BF16), moe_w_down[1].astype(BF16), moe_sh_gu[1].astype(BF16),
                   moe_sh_down[1].astype(BF16), norm_final, True)

    return (x[:n_p].reshape(bp, tp, D), x[n_p:].reshape(bs, ts, D), new_state_rwkv, new_state_ret)
```

```python
import functools

import jax
import jax.numpy as jnp
from jax import lax
from jax.experimental import pallas as pl
from jax.experimental.pallas import tpu as pltpu

F32, BF16, I32 = jnp.float32, jnp.bfloat16, jnp.int32

D = 1024
N_MOD = 6
NORM_EPS = 1e-6
GRID_W = 64
HEAD_A = 64
H_A = D // HEAD_A
LNX_EPS = 64e-5
RWKV_CHUNK = 64
N_PAIR = H_A // 2
H_B = 4
DK_B = D // H_B
DV_B = 2 * DK_B
RET_CHUNK = 128
ROPE_BASE = 10000.0
N_EXPERTS = 64
TOP_K = 8
N_GROUPS = 8
TOPK_GROUPS = 4
PER_GROUP = N_EXPERTS // N_GROUPS
D_EXPERT = 256
ROUTED_SCALE = 2.5
EXPERT_BLOCK = 256

TM = 256
LANES = 128
ROW_SUB = D // LANES
VMEM_LIMIT = 56 * 1024 * 1024


def _cparams(n_grid_axes, **kw):
    return pltpu.CompilerParams(dimension_semantics=("arbitrary",) * n_grid_axes, vmem_limit_bytes=VMEM_LIMIT, **kw)


def _bdot(a, b):
    return jnp.dot(a.astype(BF16), b.astype(BF16), preferred_element_type=F32)


def _bdot_nt(a, b):
    return lax.dot_general(a.astype(BF16), b.astype(BF16), (((1,), (1,)), ((), ())), preferred_element_type=F32)


def _bdot_tn(a, b):
    return lax.dot_general(a.astype(BF16), b.astype(BF16), (((0,), (0,)), ((), ())), preferred_element_type=F32)


def _split3(x):
    hi = x.astype(BF16)
    r1 = x - hi.astype(F32)
    mid = r1.astype(BF16)
    lo = (r1 - mid.astype(F32)).astype(BF16)
    return hi, mid, lo


def _dot_f32(a, b):
    ah, am, al = _split3(a)
    bh, bm, bl = _split3(b)
    d = lambda x, y: jnp.dot(x, y, preferred_element_type=F32)
    return d(ah, bh) + (d(ah, bm) + d(am, bh)) + (d(ah, bl) + d(al, bh) + d(am, bm))


def _dot_f32_rhs01(a, m01):
    ah, am, al = _split3(a)
    d = lambda x: jnp.dot(x, m01, preferred_element_type=F32)
    return d(ah) + d(am) + d(al)


def _dot_f32_lhs01(m01, b):
    bh, bm, bl = _split3(b)
    d = lambda x: jnp.dot(m01, x, preferred_element_type=F32)
    return d(bh) + d(bm) + d(bl)


def _norm_mod(x, g, shift, scale):
    ms = jnp.mean(x * x, axis=-1, keepdims=True)
    y = x * lax.rsqrt(ms + NORM_EPS) * g
    return y * (1.0 + scale) + shift


def _silu(x):
    return x * jax.nn.sigmoid(x)


def _rows_load(ref):
    m = ref.shape[0] // ROW_SUB
    return jnp.concatenate([ref[pl.ds(j, m, stride=ROW_SUB), :] for j in range(ROW_SUB)], axis=1)


def _rows_store(ref, val):
    m = val.shape[0]
    for j in range(ROW_SUB):
        ref[pl.ds(j, m, stride=ROW_SUB), :] = val[:, j * LANES:(j + 1) * LANES]


def _row_tile(ref, idx):
    return ref.at[pl.ds(pl.multiple_of(idx * ROW_SUB, ROW_SUB), ROW_SUB)]


def _mod_row(i, p_tiles, tiles_per_seq):
    return jnp.where(i < p_tiles, 0, 1 + (i - p_tiles) // tiles_per_seq)


def _mod_kernel(cond_ref, w_ref, b_ref, o_ref):
    o_ref[...] = _dot_f32(_silu(cond_ref[...]), w_ref[...]) + b_ref[...]


def _modulation(cond, ada_w, ada_b):
    depth, _, n6 = ada_w.shape
    tn = 1536
    return pl.pallas_call(
        _mod_kernel,
        grid=(depth, n6 // tn),
        in_specs=[
            pl.BlockSpec(cond.shape, lambda l, j: (0, 0)),
            pl.BlockSpec((None, D, tn), lambda l, j: (l, 0, j)),
            pl.BlockSpec((None, 1, tn), lambda l, j: (l, 0, j)),
        ],
        out_specs=pl.BlockSpec((None, cond.shape[0], tn), lambda l, j: (l, 0, j)),
        out_shape=jax.ShapeDtypeStruct((depth, cond.shape[0], n6), F32),
        compiler_params=_cparams(2),
        name="adaln_mod",
    )(cond, ada_w, ada_b.reshape(depth, 1, n6))


def _rwkv_pre_kernel(x_ref, xu_ref, xd_ref, mod_ref, g_ref, mu_ref, wrkv_ref, g1_ref, g2_ref, w1_ref, w2_ref,
                     w0_ref, a1_ref, a2_ref, a0_ref, kk_ref, ka_ref, rk_ref, hsum_ref, hexp_ref,
                     r_o, v_o, kk_o, lw_o, kd_o, b_o, gate_o, bonus_o, *, p_tiles, tiles_per_seq):
    i = pl.program_id(0)
    is_p = i < p_tiles
    sub = (i - p_tiles) % tiles_per_seq
    mod = mod_ref[...]
    shift, scale = mod[:, 0:D], mod[:, D:2 * D]
    g = g_ref[...]
    h = _norm_mod(x_ref[...], g, shift, scale)
    hu = _norm_mod(xu_ref[...], g, shift, scale)
    hd = _norm_mod(xd_ref[...], g, shift, scale)

    q = D // 4
    row = lax.broadcasted_iota(I32, (TM, 1), 0)
    per = jnp.where(is_p, TM, GRID_W)
    pos = row & (per - 1)

    def prev1(a):
        return jnp.where(pos == 0, 0.0, pltpu.roll(a, 1, 0))

    def next1(a):
        return jnp.where(pos == per - 1, 0.0, pltpu.roll(a, TM - 1, 0))

    h0, h1, h2, h3 = (h[:, j * q:(j + 1) * q] for j in range(4))
    up = jnp.concatenate([jnp.where(sub == 0, 0.0, hu[:, 2 * q:3 * q]), h2[0:TM - GRID_W]], axis=0)
    down = jnp.concatenate([h3[GRID_W:TM], jnp.where(sub == tiles_per_seq - 1, 0.0, hd[:, 3 * q:4 * q])], axis=0)
    s0 = prev1(h0)
    s1 = jnp.where(is_p, prev1(h1), next1(h1))
    s2 = jnp.where(is_p, next1(h2), up)
    s3 = jnp.where(is_p, next1(h3), down)
    xx = jnp.concatenate([s0, s1, s2, s3], axis=1) - h

    mu = mu_ref[...]
    mix = lambda j: h + xx * mu[j:j + 1]
    r = _bdot(mix(0), wrkv_ref[0])
    k = _bdot(mix(2), wrkv_ref[1])
    v = _bdot(mix(3), wrkv_ref[2])
    gate = _bdot(jax.nn.sigmoid(_bdot(mix(5), g1_ref[...])), g2_ref[...])
    w_all = w0_ref[...] + _bdot(jnp.tanh(_bdot(mix(1), w1_ref[...])), w2_ref[...])
    a_all = jax.nn.sigmoid(a0_ref[...] + _bdot(_bdot(mix(4), a1_ref[...]), a2_ref[...]))

    hsum, hexp = hsum_ref[...], hexp_ref[...]
    head_sum = lambda t: _dot_f32_rhs01(_dot_f32_rhs01(t, hsum), hexp)

    kkr = k * kk_ref[...]
    kk = kkr / jnp.maximum(jnp.sqrt(head_sum(kkr * kkr)), 1e-12)
    ka = ka_ref[...]
    kd_sum = jnp.zeros_like(k)
    for d in range(2):
        wd = w_all[:, d * D:(d + 1) * D]
        z = -wd
        softplus = jnp.maximum(z, 0.0) + jnp.log(1.0 + jnp.exp(-jnp.abs(z)))
        lw_o[d] = -jnp.exp(-softplus - 0.5)
        a = a_all[:, d * D:(d + 1) * D]
        kd = k * (1.0 + (a - 1.0) * ka)
        kd_o[d] = kd
        b_o[d] = kk * a
        kd_sum = kd_sum + kd
    r_o[...] = r
    v_o[...] = v
    kk_o[...] = kk
    gate_o[...] = gate
    bonus_o[...] = head_sum(r * kd_sum * rk_ref[...]) * v


def _rwkv_pre(x, mod, p_tiles, tiles_per_seq, norm_g, wts):
    n = x.shape[0]
    n_tiles = n // TM
    hb = TM // GRID_W
    n_hblk = n // GRID_W
    row = lambda a: a.reshape(1, -1)
    full = lambda a: pl.BlockSpec(a.shape, lambda i: (0,) * a.ndim)
    tok = pl.BlockSpec((TM, D), lambda i: (i, 0))
    tok2 = pl.BlockSpec((2, TM, D), lambda i: (0, i, 0))
    consts = [row(norm_g), wts["mu"], wts["wrkv"], wts["g1"], wts["g2"], wts["w1"], wts["w2"], wts["w0"],
              wts["a1"], wts["a2"], wts["a0"], row(wts["k_k"]), row(wts["k_a"]), row(wts["r_k"]),
              wts["hsum"], wts["hexp"]]
    return pl.pallas_call(
        functools.partial(_rwkv_pre_kernel, p_tiles=p_tiles, tiles_per_seq=tiles_per_seq),
        grid=(n_tiles,),
        in_specs=[
            tok,
            pl.BlockSpec((GRID_W, D), lambda i: (jnp.maximum(i * hb - 1, 0), 0)),
            pl.BlockSpec((GRID_W, D), lambda i: (jnp.minimum(i * hb + hb, n_hblk - 1), 0)),
            pl.BlockSpec((None, 1, N_MOD * D), lambda i: (_mod_row(i, p_tiles, tiles_per_seq), 0, 0)),
        ] + [full(a) for a in consts],
        out_specs=[tok, tok, tok, tok2, tok2, tok2, tok, tok],
        out_shape=[jax.ShapeDtypeStruct((n, D), F32)] * 3 + [jax.ShapeDtypeStruct((2, n, D), F32)] * 3
        + [jax.ShapeDtypeStruct((n, D), F32)] * 2,
        compiler_params=_cparams(1),
        name="rwkv_pre",
    )(x, x, x, mod, *consts)


class _ScanItems:
    def __init__(self, n_ctx, ctx_chunks, n_lat, lat_chunks):
        self.n_ctx, self.ctx_chunks, self.n_lat, self.lat_chunks = n_ctx, ctx_chunks, n_lat, lat_chunks
        self.ctx_items = n_ctx * ctx_chunks
        self.n_items = self.ctx_items + n_lat * lat_chunks

    def decode(self, j):
        is_ctx = j < self.ctx_items
        jl = jnp.maximum(j - self.ctx_items, 0)
        c = jnp.where(is_ctx, j % self.ctx_chunks, jl % self.lat_chunks)
        return is_ctx, c, jnp.where(is_ctx, self.ctx_chunks, self.lat_chunks)

    def block(self, d, j):
        _, c, nc = self.decode(j)
        return jnp.where(d == 0, j, j - c + (nc - 1 - c))

    def ctx_seq(self, j):
        return jnp.minimum(j // self.ctx_chunks, self.n_ctx - 1)

    def lat_seq(self, j):
        return jnp.maximum(j - self.ctx_items, 0) // self.lat_chunks


def _rwkv_scan_kernel(r_ref, v_ref, kk_ref, lw_ref, kd_ref, b_ref, s0_ref, y_ref, sf_ref, s_scr, *, items):
    d = pl.program_id(0)
    is_ctx, c, nc = items.decode(pl.program_id(1))
    rev = d == 1
    ch = RWKV_CHUNK

    @pl.when(c == 0)
    def _():
        s_scr[...] = jnp.where(is_ctx, 0.0, s0_ref[...])

    ti = lax.broadcasted_iota(I32, (ch, ch), 0)
    tj = lax.broadcasted_iota(I32, (ch, ch), 1)
    tri = jnp.where(jnp.where(rev, ti - tj, tj - ti) <= 0, 1.0, 0.0).astype(BF16)
    lw = lw_ref[...]
    cum = _dot_f32_lhs01(tri, lw)
    last = jnp.where(rev, cum[0:1], cum[ch - 1:ch])
    kk = kk_ref[...]
    kd = kd_ref[...]
    b = b_ref[...]
    e_neg = jnp.exp(-cum)
    e_rel = jnp.exp(last - cum)
    at = -kk * jnp.exp(cum - lw)
    rt = r_ref[...] * jnp.exp(cum)
    bt = b * e_neg
    kt = kd * e_neg
    bh = b * e_rel
    kh = kd * e_rel
    g_all = jnp.exp(last)
    v = v_ref[...]

    si = lax.broadcasted_iota(I32, (2 * ch, 2 * ch), 0)
    sj = lax.broadcasted_iota(I32, (2 * ch, 2 * ch), 1)
    same = (si < ch) == (sj < ch)
    ui, uj = si & (ch - 1), sj & (ch - 1)
    order = jnp.where(rev, ui - uj, uj - ui)
    strict = same & (order < 0)
    incl = same & (order <= 0)
    first = lax.broadcasted_iota(I32, (1, LANES), 1) < HEAD_A

    def stack(xp):
        return jnp.concatenate([jnp.where(first, xp, 0.0), jnp.where(first, 0.0, xp)], axis=0)

    pairs = range(N_PAIR)
    sls = [slice(p * LANES, (p + 1) * LANES) for p in pairs]
    s_old = [s_scr[p] for p in pairs]
    m1 = [_bdot_nt(jnp.concatenate([stack(at[:, sl]), stack(rt[:, sl])], axis=0),
                   jnp.concatenate([stack(bt[:, sl]), stack(kt[:, sl])], axis=0)) for sl in sls]
    m2 = [_bdot_nt(jnp.concatenate([at[:, sl], rt[:, sl]], axis=0), s_old[p])
          for p, sl in enumerate(sls)]
    vs = [stack(v[:, sl]) for sl in sls]
    x = [stack(m2[p][0:ch]) + _bdot(jnp.where(strict, m1[p][0:2 * ch, 2 * ch:4 * ch], 0.0), vs[p]) for p in pairs]
    nm = [jnp.where(strict, m1[p][0:2 * ch, 0:2 * ch], 0.0).astype(BF16) for p in pairs]
    x = [x[p] + _bdot(nm[p], x[p]) for p in pairs]
    for _ in range(5):
        nm = [_bdot(nm[p], nm[p]).astype(BF16) for p in pairs]
        x = [x[p] + _bdot(nm[p], x[p]) for p in pairs]
    uv = [jnp.concatenate([x[p], vs[p]], axis=0).astype(BF16) for p in pairs]
    ys = [_bdot(jnp.concatenate([jnp.where(incl, m1[p][2 * ch:4 * ch, 0:2 * ch], 0.0),
                                 jnp.where(incl, m1[p][2 * ch:4 * ch, 2 * ch:4 * ch], 0.0)], axis=1), uv[p])
          for p in pairs]
    s_new = [s_old[p] * g_all[:, sl]
             + _bdot_tn(uv[p], jnp.concatenate([stack(bh[:, sl]), stack(kh[:, sl])], axis=0))
             for p, sl in enumerate(sls)]
    for p, sl in enumerate(sls):
        y_ref[:, sl] = ys[p][0:ch] + ys[p][ch:2 * ch] + m2[p][ch:2 * ch]
    for p in pairs:
        s_scr[p] = s_new[p]

    @pl.when(is_ctx & (c == nc - 1))
    def _():
        sf_ref[...] = s_scr[...]


def _rwkv_scan(r, v, kk, lw, kd, b, s0, items):
    n = r.shape[0]
    ch = RWKV_CHUNK
    tok = pl.BlockSpec((ch, D), lambda d, j: (items.block(d, j), 0))
    tokd = pl.BlockSpec((None, ch, D), lambda d, j: (d, items.block(d, j), 0))
    st = lambda seq: pl.BlockSpec((None, None, N_PAIR, LANES, LANES), lambda d, j: (d, seq(j), 0, 0, 0))
    return pl.pallas_call(
        functools.partial(_rwkv_scan_kernel, items=items),
        grid=(2, items.n_items),
        in_specs=[tok, tok, tok, tokd, tokd, tokd, st(items.lat_seq)],
        out_specs=[tokd, st(items.ctx_seq)],
        out_shape=[jax.ShapeDtypeStruct((2, n, D), F32),
                   jax.ShapeDtypeStruct((2, items.n_ctx, N_PAIR, LANES, LANES), F32)],
        scratch_shapes=[pltpu.VMEM((N_PAIR, LANES, LANES), F32)],
        compiler_params=_cparams(2),
        name="rwkv_scan",
    )(r, v, kk, lw, kd, b, s0)


def _rwkv_post_kernel(x_ref, y_ref, gate_ref, bonus_ref, mod_ref, lnw_ref, lnb_ref, wo_ref, hsum_ref, hexp_ref, o_ref):
    hsum, hexp = hsum_ref[...], hexp_ref[...]
    head_mean = lambda t: _dot_f32_rhs01(_dot_f32_rhs01(t, hsum), hexp) * (1.0 / HEAD_A)
    y = y_ref[0] + y_ref[1]
    yc = y - head_mean(y)
    yn = yc * lax.rsqrt(head_mean(yc * yc) + LNX_EPS)
    z = (yn * lnw_ref[...] + lnb_ref[...] + bonus_ref[...]) * gate_ref[...]
    o_ref[...] = x_ref[...] + mod_ref[:, 2 * D:3 * D] * _bdot(z, wo_ref[...])


def _rwkv_post(x, y, gate, bonus, mod, p_tiles, tiles_per_seq, ln_w, ln_b, wo, hsum, hexp):
    n = x.shape[0]
    full = lambda a: pl.BlockSpec(a.shape, lambda i: (0,) * a.ndim)
    tok = pl.BlockSpec((TM, D), lambda i: (i, 0))
    consts = [ln_w.reshape(1, D), ln_b.reshape(1, D), wo, hsum, hexp]
    return pl.pallas_call(
        _rwkv_post_kernel,
        grid=(n // TM,),
        in_specs=[tok, pl.BlockSpec((2, TM, D), lambda i: (0, i, 0)), tok, tok,
                  pl.BlockSpec((None, 1, N_MOD * D), lambda i: (_mod_row(i, p_tiles, tiles_per_seq), 0, 0))]
        + [full(a) for a in consts],
        out_specs=tok,
        out_shape=jax.ShapeDtypeStruct((n, D), F32),
        compiler_params=_cparams(1),
        name="rwkv_post",
    )(x, y, gate, bonus, mod, *consts)


def _ret_pre_kernel(x_ref, mod_ref, g_ref, cos_ref, sin_ref, win_ref, q_o, k_o, v_o, gate_o):
    mod = mod_ref[...]
    h = _norm_mod(x_ref[...], g_ref[...], mod[:, 0:D], mod[:, D:2 * D])
    proj = _bdot(h, win_ref[...])
    cos, sin = cos_ref[...], sin_ref[...]

    def rope(t):
        outs = []
        for j in range(D // LANES):
            tj = t[:, j * LANES:(j + 1) * LANES]
            cj = cos[:, (j % 2) * LANES:(j % 2 + 1) * LANES]
            sj = sin[:, (j % 2) * LANES:(j % 2 + 1) * LANES]
            outs.append(tj * cj + pltpu.roll(tj, LANES // 2, 1) * sj)
        return jnp.concatenate(outs, axis=1)

    q_o[...] = rope(proj[:, 0:D]).astype(BF16)
    k_o[...] = rope(proj[:, D:2 * D] * (DK_B ** -0.5)).astype(BF16)
    v_o[...] = proj[:, 2 * D:4 * D].astype(BF16)
    gate_o[...] = proj[:, 4 * D:6 * D]


def _ret_pre(x, mod, p_tiles, tiles_per_seq, norm_g, cos_t, sin_t, w_in):
    n = x.shape[0]
    full = lambda a: pl.BlockSpec(a.shape, lambda i: (0,) * a.ndim)
    tok = lambda w: pl.BlockSpec((TM, w), lambda i: (i, 0))
    tab = pl.BlockSpec((TM, DK_B), lambda i: (jnp.where(i < p_tiles, 0, 1 + (i - p_tiles) % tiles_per_seq), 0))
    return pl.pallas_call(
        _ret_pre_kernel,
        grid=(n // TM,),
        in_specs=[tok(D), pl.BlockSpec((None, 1, N_MOD * D), lambda i: (_mod_row(i, p_tiles, tiles_per_seq), 0, 0)),
                  full(norm_g.reshape(1, D)), tab, tab, full(w_in)],
        out_specs=[tok(D), tok(D), tok(2 * D), tok(2 * D)],
        out_shape=[jax.ShapeDtypeStruct((n, D), BF16), jax.ShapeDtypeStruct((n, D), BF16),
                   jax.ShapeDtypeStruct((n, 2 * D), BF16), jax.ShapeDtypeStruct((n, 2 * D), F32)],
        compiler_params=_cparams(1),
        name="ret_pre",
    )(x, mod, norm_g.reshape(1, D), cos_t, sin_t, w_in)


def _ret_scan_kernel(lg_ref, q_ref, k_ref, v_ref, s0_ref, o_ref, sf_ref, s_scr, *, items):
    d = pl.program_id(0)
    is_ctx, c, nc = items.decode(pl.program_id(1))
    rev = d == 1
    ch = RET_CHUNK

    @pl.when(c == 0)
    def _():
        s_scr[...] = jnp.where(is_ctx, 0.0, s0_ref[...])

    ti = lax.broadcasted_iota(I32, (ch, ch), 0)
    tj = lax.broadcasted_iota(I32, (ch, ch), 1)
    rel = jnp.where(rev, tj - ti, ti - tj).astype(F32)
    steps_q = jnp.where(rev, ch - ti, ti + 1).astype(F32)
    steps_k = jnp.where(rev, ti, ch - 1 - ti).astype(F32)
    for hd in range(H_B):
        lg = lg_ref[d, hd]
        mask = jnp.where(rel >= 0, jnp.exp(jnp.maximum(rel, 0.0) * lg), 0.0)
        q_dec = jnp.exp(steps_q * lg)
        k_dec = jnp.exp(steps_k * lg)
        chunk_dec = jnp.exp(jnp.full((1, DV_B), float(ch), F32) * lg)
        qh = q_ref[:, hd * DK_B:(hd + 1) * DK_B]
        kh = k_ref[:, hd * DK_B:(hd + 1) * DK_B]
        vh = v_ref[:, hd * DV_B:(hd + 1) * DV_B]
        s = s_scr[hd]
        scores = _bdot_nt(qh, kh) * mask
        inner = _bdot(scores, vh)
        cross = _bdot(qh, s) * jnp.concatenate([q_dec] * (DV_B // ch), axis=1)
        o_ref[:, hd * DV_B:(hd + 1) * DV_B] = inner + cross
        kdec = kh.astype(F32) * jnp.concatenate([k_dec] * (DK_B // ch), axis=1)
        s_scr[hd] = s * chunk_dec + _bdot_tn(kdec, vh)

    @pl.when(is_ctx & (c == nc - 1))
    def _():
        sf_ref[...] = s_scr[...]


def _ret_scan(log_gamma, q, k, v, s0, items):
    n = q.shape[0]
    ch = RET_CHUNK
    tok = lambda w: pl.BlockSpec((ch, w), lambda d, j: (items.block(d, j), 0))
    st = lambda seq: pl.BlockSpec((None, None, H_B, DK_B, DV_B), lambda d, j: (d, seq(j), 0, 0, 0))
    return pl.pallas_call(
        functools.partial(_ret_scan_kernel, items=items),
        grid=(2, items.n_items),
        in_specs=[pl.BlockSpec(memory_space=pltpu.SMEM), tok(D), tok(D), tok(2 * D), st(items.lat_seq)],
        out_specs=[pl.BlockSpec((None, ch, 2 * D), lambda d, j: (d, items.block(d, j), 0)), st(items.ctx_seq)],
        out_shape=[jax.ShapeDtypeStruct((2, n, 2 * D), F32),
                   jax.ShapeDtypeStruct((2, items.n_ctx, H_B, DK_B, DV_B), F32)],
        scratch_shapes=[pltpu.VMEM((H_B, DK_B, DV_B), F32)],
        compiler_params=_cparams(2),
        name="ret_scan",
    )(log_gamma, q, k, v, s0)


def _ret_post_kernel(x_ref, o_ref, gate_ref, mod_ref, wout_ref, out_ref):
    o = o_ref[0] + o_ref[1]
    parts = []
    for hd in range(H_B):
        oh = o[:, hd * DV_B:(hd + 1) * DV_B]
        parts.append(oh * lax.rsqrt(jnp.mean(oh * oh, axis=-1, keepdims=True) + NORM_EPS))
    y = _silu(gate_ref[...]) * jnp.concatenate(parts, axis=1)
    out_ref[...] = x_ref[...] + mod_ref[:, 2 * D:3 * D] * _bdot(y, wout_ref[...])


def _ret_post(x, o, gate, mod, p_tiles, tiles_per_seq, w_out):
    n = x.shape[0]
    tok = lambda w: pl.BlockSpec((TM, w), lambda i: (i, 0))
    return pl.pallas_call(
        _ret_post_kernel,
        grid=(n // TM,),
        in_specs=[tok(D), pl.BlockSpec((2, TM, 2 * D), lambda i: (0, i, 0)), tok(2 * D),
                  pl.BlockSpec((None, 1, N_MOD * D), lambda i: (_mod_row(i, p_tiles, tiles_per_seq), 0, 0)),
                  pl.BlockSpec(w_out.shape, lambda i: (0, 0))],
        out_specs=tok(D),
        out_shape=jax.ShapeDtypeStruct((n, D), F32),
        compiler_params=_cparams(1),
        name="ret_post",
    )(x, o, gate, mod, w_out)


def _moe_route_kernel(x_ref, mod_ref, g_ref, router_ref, bias_ref, h_o, e_o, w_o, p_o, cnt_o, carry):
    i = pl.program_id(0)

    @pl.when(i == 0)
    def _():
        carry[...] = jnp.zeros_like(carry)

    mod = mod_ref[...]
    h = _norm_mod(x_ref[...], g_ref[...], mod[:, 3 * D:4 * D], mod[:, 4 * D:5 * D])
    _rows_store(h_o, h)
    lane =lax.broadcasted_iota(I32, (TM, LANES), 1)
    valid = lane < N_EXPERTS
    neg = -jnp.inf
    scores = jax.nn.sigmoid(_dot_f32(h, router_ref[...]))
    biased = jnp.where(valid, scores + bias_ref[...], neg)

    def group_reduce(t, op):
        s = 1
        while s < PER_GROUP:
            partner = jnp.where((lane & s) == 0, pltpu.roll(t, LANES - s, 1), pltpu.roll(t, s, 1))
            t = op(t, partner)
            s *= 2
        return t

    lane_f = lane.astype(F32)
    group_f = jnp.floor(lane_f * (1.0 / PER_GROUP))

    def first_lane_of_max(t):
        m = jnp.max(t, axis=-1, keepdims=True)
        return jnp.min(jnp.where(t == m, lane_f, float(LANES)), axis=-1, keepdims=True)

    m1 = group_reduce(biased, jnp.maximum)
    first1 = group_reduce(jnp.where(biased == m1, lane_f, float(LANES)), jnp.minimum)
    m2 = group_reduce(jnp.where(lane_f == first1, neg, biased), jnp.maximum)
    gscore = jnp.where(valid, m1 + m2, neg)
    cand = jnp.full((TM, LANES), neg, F32)
    for _ in range(TOPK_GROUPS):
        gsel = group_f == jnp.floor(first_lane_of_max(gscore) * (1.0 / PER_GROUP))
        cand = jnp.where(gsel, biased, cand)
        gscore = jnp.where(gsel, neg, gscore)
    hits = []
    sel01 = jnp.zeros((TM, LANES), F32)
    e_cols = jnp.zeros((TM, LANES), F32)
    for j in range(TOP_K):
        fl = first_lane_of_max(cand)
        hit = lane_f == fl
        hits.append(hit)
        sel01 = jnp.where(hit, 1.0, sel01)
        cand = jnp.where(hit, neg, cand)
        e_cols = jnp.where(lane == j, fl, e_cols)
    wsum = jnp.sum(sel01 * scores, axis=-1, keepdims=True)

    ri = lax.broadcasted_iota(I32, (TM, TM), 0)
    rj = lax.broadcasted_iota(I32, (TM, TM), 1)
    below = jnp.where(rj < ri, 1.0, 0.0).astype(BF16)
    rank = jnp.dot(below, sel01.astype(BF16), preferred_element_type=F32) + carry[...]
    carry[...] = carry[...] + jnp.sum(sel01, axis=0, keepdims=True)
    w_cols = jnp.zeros((TM, LANES), F32)
    p_cols = jnp.zeros((TM, LANES), F32)
    for j in range(TOP_K):
        wj = jnp.sum(jnp.where(hits[j], scores, 0.0), axis=-1, keepdims=True)
        w_cols = jnp.where(lane == j, wj / wsum * ROUTED_SCALE, w_cols)
        p_cols = jnp.where(lane == j, jnp.sum(jnp.where(hits[j], rank, 0.0), axis=-1, keepdims=True), p_cols)
    e_o[...] = e_cols.astype(I32)
    w_o[...] = w_cols
    p_o[...] = p_cols.astype(I32)
    cnt_o[...] = jnp.broadcast_to(carry[...], cnt_o.shape)


def _moe_route(x, mod, p_tiles, tiles_per_seq, norm_g, router, bias):
    n = x.shape[0]
    full = lambda a: pl.BlockSpec(a.shape, lambda i: (0,) * a.ndim)
    tok = lambda w: pl.BlockSpec((TM, w), lambda i: (i, 0))
    router_p = jnp.pad(router, ((0, 0), (0, LANES - N_EXPERTS)))
    bias_p = jnp.pad(bias, (0, LANES - N_EXPERTS)).reshape(1, LANES)
    return pl.pallas_call(
        _moe_route_kernel,
        grid=(n // TM,),
        in_specs=[tok(D), pl.BlockSpec((None, 1, N_MOD * D), lambda i: (_mod_row(i, p_tiles, tiles_per_seq), 0, 0)),
                  full(norm_g.reshape(1, D)), full(router_p), full(bias_p)],
        out_specs=[pl.BlockSpec((TM * ROW_SUB, LANES), lambda i: (i, 0)), tok(LANES), tok(LANES), tok(LANES),
                   pl.BlockSpec((8, LANES), lambda i: (0, 0))],
        out_shape=[jax.ShapeDtypeStruct((n * ROW_SUB, LANES), F32), jax.ShapeDtypeStruct((n, LANES), I32),
                   jax.ShapeDtypeStruct((n, LANES), F32), jax.ShapeDtypeStruct((n, LANES), I32),
                   jax.ShapeDtypeStruct((8, LANES), F32)],
        scratch_shapes=[pltpu.VMEM((1, LANES), F32)],
        compiler_params=_cparams(1),
        name="moe_route",
    )(x, mod, norm_g.reshape(1, D), router_p, bias_p)


def _moe_expert_kernel(be_ref, tok_ref, tok_next_ref, h_hbm, wgu_ref, wdn_ref, o_ref, xbuf, wgu_bf, wdn_bf, sems):
    i = pl.program_id(0)
    slot = i % 2
    group = TOK_SMEM_BLOCK // EXPERT_BLOCK

    def row_copy(idx_ref, blk, r, s):
        tok = idx_ref[(blk % group) * EXPERT_BLOCK + r]
        return pltpu.make_async_copy(_row_tile(h_hbm, tok), _row_tile(xbuf.at[s], r), sems.at[s])

    def gather_start(idx_ref, blk, s):
        def body(g, carry):
            for u in range(8):
                row_copy(idx_ref, blk, g * 8 + u, s).start(priority=u % 2)
            return carry
        lax.fori_loop(0, EXPERT_BLOCK // 8, body, 0)

    def gather_wait(idx_ref, blk, s):
        def body(r, carry):
            row_copy(idx_ref, blk, r, s).wait()
            return carry
        lax.fori_loop(0, EXPERT_BLOCK, body, 0)

    @pl.when(i == 0)
    def _():
        gather_start(tok_ref, i, slot)

    @pl.when(i + 1 < pl.num_programs(0))
    def _():
        gather_start(tok_next_ref, i + 1, 1 - slot)

    @pl.when((i == 0) | (be_ref[i] != be_ref[jnp.maximum(i - 1, 0)]))
    def _():
        wgu_bf[...] = wgu_ref[...].astype(BF16)
        wdn_bf[...] = wdn_ref[...].astype(BF16)

    gather_wait(tok_ref, i, slot)
    gu = _bdot(_rows_load(xbuf.at[slot]), wgu_bf[...])
    act = _silu(gu[:, 0:D_EXPERT]) * gu[:, D_EXPERT:2 * D_EXPERT]
    _rows_store(o_ref, _bdot(act, wdn_bf[...]))


TOK_SMEM_BLOCK = 1024


def _moe_expert(block_e, slot_tok, h, w_gu, w_down, layer):
    n_slots = slot_tok.shape[0]
    n_blocks = n_slots // EXPERT_BLOCK
    group = TOK_SMEM_BLOCK // EXPERT_BLOCK
    grid_spec = pltpu.PrefetchScalarGridSpec(
        num_scalar_prefetch=1,
        grid=(n_blocks,),
        in_specs=[pl.BlockSpec((TOK_SMEM_BLOCK,), lambda i, be: (i // group,), memory_space=pltpu.SMEM),
                  pl.BlockSpec((TOK_SMEM_BLOCK,), lambda i, be: (jnp.minimum(i + 1, n_blocks - 1) // group,),
                               memory_space=pltpu.SMEM),
                  pl.BlockSpec(memory_space=pl.ANY),
                  pl.BlockSpec((None, None, D, 2 * D_EXPERT), lambda i, be: (layer, be[i], 0, 0)),
                  pl.BlockSpec((None, None, D_EXPERT, D), lambda i, be: (layer, be[i], 0, 0))],
        out_specs=pl.BlockSpec((EXPERT_BLOCK * ROW_SUB, LANES), lambda i, be: (i, 0)),
        scratch_shapes=[pltpu.VMEM((2, EXPERT_BLOCK * ROW_SUB, LANES), F32),
                        pltpu.VMEM((D, 2 * D_EXPERT), BF16), pltpu.VMEM((D_EXPERT, D), BF16),
                        pltpu.SemaphoreType.DMA((2,))],
    )
    return pl.pallas_call(
        _moe_expert_kernel,
        grid_spec=grid_spec,
        out_shape=jax.ShapeDtypeStruct((n_slots * ROW_SUB, LANES), F32),
        compiler_params=_cparams(1),
        name="moe_expert",
    )(block_e, slot_tok, slot_tok, h, w_gu, w_down)


COMBINE_TOK = 128


def _moe_combine_kernel(dest_ref, x_ref, h_ref, w_ref, mod_ref, shgu_ref, shdn_ref, gfin_ref, ys_hbm, o_ref, buf, sem,
                        *, final_norm):
    def copy(t, kq):
        return pltpu.make_async_copy(_row_tile(ys_hbm, dest_ref[t * TOP_K + kq]), _row_tile(buf.at[kq], t), sem)

    def issue(t, carry):
        for kq in range(TOP_K):
            copy(t, kq).start(priority=kq % 2)
        return carry

    def drain(t, carry):
        for kq in range(TOP_K):
            copy(t, kq).wait()
        return carry

    lax.fori_loop(0, COMBINE_TOK, issue, 0)
    gu = _bdot(_rows_load(h_ref), shgu_ref[...])
    acc = _bdot(_silu(gu[:, 0:D_EXPERT]) * gu[:, D_EXPERT:2 * D_EXPERT], shdn_ref[...])
    lax.fori_loop(0, COMBINE_TOK, drain, 0)
    w = w_ref[...]
    for kq in range(TOP_K):
        acc = acc + _rows_load(buf.at[kq]) * w[:, kq:kq + 1]
    out = x_ref[...] + mod_ref[:, 5 * D:6 * D] * acc
    if final_norm:
        out = out * lax.rsqrt(jnp.mean(out * out, axis=-1, keepdims=True) + NORM_EPS) * gfin_ref[...]
    o_ref[...] = out


def _moe_combine(dest_flat, x, h, w_cols, mod, p_tiles, tiles_per_seq, sh_gu, sh_down, g_final, ys, final_norm):
    n = x.shape[0]
    ratio = TM // COMBINE_TOK
    tok = lambda w: pl.BlockSpec((COMBINE_TOK, w), lambda i: (i, 0))
    full = lambda a: pl.BlockSpec(a.shape, lambda i: (0,) * a.ndim)
    return pl.pallas_call(
        functools.partial(_moe_combine_kernel, final_norm=final_norm),
        grid=(n // COMBINE_TOK,),
        in_specs=[pl.BlockSpec((COMBINE_TOK * TOP_K,), lambda i: (i,), memory_space=pltpu.SMEM),
                  tok(D), pl.BlockSpec((COMBINE_TOK * ROW_SUB, LANES), lambda i: (i, 0)), tok(LANES),
                  pl.BlockSpec((None, 1, N_MOD * D), lambda i: (_mod_row(i // ratio, p_tiles, tiles_per_seq), 0, 0)),
                  full(sh_gu), full(sh_down), full(g_final),
                  pl.BlockSpec(memory_space=pl.ANY)],
        out_specs=tok(D),
        out_shape=jax.ShapeDtypeStruct((n, D), F32),
        scratch_shapes=[pltpu.VMEM((TOP_K, COMBINE_TOK * ROW_SUB, LANES), F32), pltpu.SemaphoreType.DMA],
        compiler_params=_cparams(1),
        name="moe_combine",
    )(dest_flat, x, h, w_cols, mod, sh_gu, sh_down, g_final, ys)


def _moe_layer(x, mod, p_tiles, tiles_per_seq, norm_g, router, bias, w_gu, w_down, layer, sh_gu, sh_down, g_final,
               final_norm):
    n = x.shape[0]
    h, e_cols, w_cols, p_cols, counts = _moe_route(x, mod, p_tiles, tiles_per_seq, norm_g, router, bias)
    counts = counts[0, :N_EXPERTS].astype(I32)
    padded = (counts + EXPERT_BLOCK - 1) // EXPERT_BLOCK * EXPERT_BLOCK
    pad_end = jnp.cumsum(padded)
    pad_start = pad_end - padded
    dest = (pad_start[e_cols[:, :TOP_K]] + p_cols[:, :TOP_K]).reshape(-1)
    n_blocks = n * TOP_K // EXPERT_BLOCK + N_EXPERTS
    n_slots = n_blocks * EXPERT_BLOCK
    block_start = jnp.arange(n_blocks, dtype=I32) * EXPERT_BLOCK
    block_e = jnp.minimum(jnp.sum((pad_end[None, :] <= block_start[:, None]).astype(I32), axis=1), N_EXPERTS - 1)
    slot_tok = jnp.zeros((n_slots,), I32).at[dest].set(jnp.arange(n * TOP_K, dtype=I32) // TOP_K)
    ys = _moe_expert(block_e, slot_tok, h, w_gu, w_down, layer)
    return _moe_combine(dest, x, h, w_cols, mod, p_tiles, tiles_per_seq, sh_gu, sh_down, g_final.reshape(1, D), ys,
                        final_norm)


def _pair_pack(s):
    lead = s.shape[:-3]
    s = s.reshape(lead + (N_PAIR, 2, HEAD_A, HEAD_A))
    z = jnp.zeros_like(s[..., 0, :, :])
    top = jnp.concatenate([s[..., 0, :, :], z], axis=-1)
    bot = jnp.concatenate([z, s[..., 1, :, :]], axis=-1)
    return jnp.concatenate([top, bot], axis=-2)


def _pair_unpack(s):
    a = s[..., :HEAD_A, :HEAD_A]
    b = s[..., HEAD_A:, HEAD_A:]
    return jnp.stack([a, b], axis=-3).reshape(s.shape[:-3] + (H_A, HEAD_A, HEAD_A))


def _rope_tables(rows):
    half = DK_B // 2
    n_freq = half // 2
    inv = ROPE_BASE ** (-jnp.arange(n_freq, dtype=F32) / n_freq)
    pos_r = jnp.repeat(jnp.arange(rows, dtype=F32), GRID_W)
    pos_c = jnp.tile(jnp.arange(GRID_W, dtype=F32), rows)

    def tab(pos):
        ang = pos[:, None] * inv[None, :]
        c, s = jnp.cos(ang), jnp.sin(ang)
        return jnp.concatenate([c, c], -1), jnp.concatenate([-s, s], -1)

    cr, sr = tab(pos_r)
    cc, sc = tab(pos_c)
    cos = jnp.concatenate([cr, cc], -1)
    sin = jnp.concatenate([sr, sc], -1)
    cos = jnp.concatenate([jnp.ones((TM, DK_B), F32), cos], 0)
    sin = jnp.concatenate([jnp.zeros((TM, DK_B), F32), sin], 0)
    return cos, sin


def kernel(x_prompt, x_sample, state_rwkv, state_ret, c, c_ctx, ada_w, ada_b, norm_mix, norm_ffn, norm_final, rwkv_mu, rwkv_w0, rwkv_w1, rwkv_w2, rwkv_a0, rwkv_a1, rwkv_a2, rwkv_wrkv, rwkv_wo, rwkv_g1, rwkv_g2, rwkv_k_k, rwkv_k_a, rwkv_r_k, rwkv_ln_w, rwkv_ln_b, ret_w_in, ret_w_out, ret_decay_logit, moe_router, moe_bias, moe_w_gu, moe_w_down, moe_sh_gu, moe_sh_down):
    bp, tp, _ = x_prompt.shape
    bs, ts, _ = x_sample.shape
    n_p, n_s = bp * tp, bs * ts
    assert tp == TM and ts % TM == 0 and TM % GRID_W == 0
    p_tiles = n_p // TM
    tiles_per_seq = ts // TM
    rows = ts // GRID_W
    x = jnp.concatenate([x_prompt.reshape(n_p, D), x_sample.reshape(n_s, D)], axis=0)
    n = n_p + n_s

    n_cond = 16
    cond = jnp.zeros((n_cond, D), F32).at[0].set(c_ctx).at[1:1 + bs].set(c)
    mod = _modulation(cond, ada_w, ada_b).reshape(ada_w.shape[0], n_cond, 1, N_MOD * D)

    head_of = jnp.arange(D, dtype=I32) // HEAD_A
    hsum = (head_of[:, None] == jnp.arange(LANES, dtype=I32)[None, :]).astype(BF16)
    hexp = hsum.T

    gl = rwkv_g1.shape[-1]
    glp = -(-gl // LANES) * LANES
    zb = lambda a: jnp.zeros_like(a)
    wts = {
        "mu": jnp.pad(rwkv_mu[0], ((0, 8 - N_MOD), (0, 0))),
        "wrkv": rwkv_wrkv[0].astype(BF16),
        "g1": jnp.pad(rwkv_g1[0], ((0, 0), (0, glp - gl))).astype(BF16),
        "g2": jnp.pad(rwkv_g2[0], ((0, glp - gl), (0, 0))).astype(BF16),
        "w1": jnp.concatenate([rwkv_w1[0, 0], rwkv_w1[0, 1]], axis=1).astype(BF16),
        "w2": jnp.concatenate([jnp.concatenate([rwkv_w2[0, 0], zb(rwkv_w2[0, 1])], 1),
                               jnp.concatenate([zb(rwkv_w2[0, 0]), rwkv_w2[0, 1]], 1)], 0).astype(BF16),
        "w0": rwkv_w0[0].reshape(1, 2 * D),
        "a1": jnp.concatenate([rwkv_a1[0, 0], rwkv_a1[0, 1]], axis=1).astype(BF16),
        "a2": jnp.concatenate([jnp.concatenate([rwkv_a2[0, 0], zb(rwkv_a2[0, 1])], 1),
                               jnp.concatenate([zb(rwkv_a2[0, 0]), rwkv_a2[0, 1]], 1)], 0).astype(BF16),
        "a0": rwkv_a0[0].reshape(1, 2 * D),
        "k_k": rwkv_k_k[0], "k_a": rwkv_k_a[0], "r_k": rwkv_r_k[0].reshape(D),
        "hsum": hsum, "hexp": hexp,
    }
    r, v, kk, lw, kd, b, gate, bonus = _rwkv_pre(x, mod[0], p_tiles, tiles_per_seq, norm_mix[0], wts)
    s0_lat = _pair_pack(jnp.moveaxis(state_rwkv[:, 0], 1, 0))
    y2, s_fin = _rwkv_scan(r, v, kk, lw, kd, b, s0_lat,
                           _ScanItems(bp, tp // RWKV_CHUNK, bs, ts // RWKV_CHUNK))
    new_state_rwkv = jnp.moveaxis(_pair_unpack(s_fin), 0, 1)[:, None]
    x = _rwkv_post(x, y2, gate, bonus, mod[0], p_tiles, tiles_per_seq, rwkv_ln_w[0], rwkv_ln_b[0],
                   rwkv_wo[0].astype(BF16), hsum, hexp)
    x = _moe_layer(x, mod[0], p_tiles, tiles_per_seq, norm_ffn[0], moe_router[0], moe_bias[0],
                   moe_w_gu, moe_w_down, 0, moe_sh_gu[0].astype(BF16),
                   moe_sh_down[0].astype(BF16), norm_final, False)

    cos_t, sin_t = _rope_tables(rows)
    q, k, vv, rgate = _ret_pre(x, mod[1], p_tiles, tiles_per_seq, norm_mix[1], cos_t, sin_t, ret_w_in[0].astype(BF16))
    log_gamma = jax.nn.log_sigmoid(ret_decay_logit[0].astype(F32))
    o2, r_fin = _ret_scan(log_gamma, q, k, vv, jnp.moveaxis(state_ret[:, 0], 1, 0),
                          _ScanItems(bp, tp // RET_CHUNK, bs, ts // RET_CHUNK))
    new_state_ret = jnp.moveaxis(r_fin, 0, 1)[:, None]
    x = _ret_post(x, o2, rgate, mod[1], p_tiles, tiles_per_seq, ret_w_out[0].astype(BF16))
    x = _moe_layer(x, mod[1], p_tiles, tiles_per_seq, norm_ffn[1], moe_router[1], moe_bias[1],
                   moe_w_gu, moe_w_down, 1, moe_sh_gu[1].astype(BF16),
                   moe_sh_down[1].astype(BF16), norm_final, True)

    return (x[:n_p].reshape(bp, tp, D), x[n_p:].reshape(bs, ts, D), new_state_rwkv, new_state_ret)
```

```python
import functools

import jax
import jax.numpy as jnp
from jax import lax
from jax.experimental import pallas as pl
from jax.experimental.pallas import tpu as pltpu

F32, BF16, I32 = jnp.float32, jnp.bfloat16, jnp.int32

D = 1024
N_MOD = 6
NORM_EPS = 1e-6
GRID_W = 64
HEAD_A = 64
H_A = D // HEAD_A
LNX_EPS = 64e-5
RWKV_CHUNK = 64
N_PAIR = H_A // 2
H_B = 4
DK_B = D // H_B
DV_B = 2 * DK_B
RET_CHUNK = 128
ROPE_BASE = 10000.0
N_EXPERTS = 64
TOP_K = 8
N_GROUPS = 8
TOPK_GROUPS = 4
PER_GROUP = N_EXPERTS // N_GROUPS
D_EXPERT = 256
ROUTED_SCALE = 2.5
EXPERT_BLOCK = 256

TM = 256
LANES = 128
ROW_SUB = D // LANES
VMEM_LIMIT = 56 * 1024 * 1024


def _cparams(n_grid_axes, **kw):
    return pltpu.CompilerParams(dimension_semantics=("arbitrary",) * n_grid_axes, vmem_limit_bytes=VMEM_LIMIT, **kw)


def _bdot(a, b):
    return jnp.dot(a.astype(BF16), b.astype(BF16), preferred_element_type=F32)


def _bdot_nt(a, b):
    return lax.dot_general(a.astype(BF16), b.astype(BF16), (((1,), (1,)), ((), ())), preferred_element_type=F32)


def _bdot_tn(a, b):
    return lax.dot_general(a.astype(BF16), b.astype(BF16), (((0,), (0,)), ((), ())), preferred_element_type=F32)


def _split3(x):
    hi = x.astype(BF16)
    r1 = x - hi.astype(F32)
    mid = r1.astype(BF16)
    lo = (r1 - mid.astype(F32)).astype(BF16)
    return hi, mid, lo


def _dot_f32(a, b):
    ah, am, al = _split3(a)
    bh, bm, bl = _split3(b)
    d = lambda x, y: jnp.dot(x, y, preferred_element_type=F32)
    return d(ah, bh) + (d(ah, bm) + d(am, bh)) + (d(ah, bl) + d(al, bh) + d(am, bm))


def _dot_f32_rhs01(a, m01):
    ah, am, al = _split3(a)
    d = lambda x: jnp.dot(x, m01, preferred_element_type=F32)
    return d(ah) + d(am) + d(al)


def _dot_f32_lhs01(m01, b):
    bh, bm, bl = _split3(b)
    d = lambda x: jnp.dot(m01, x, preferred_element_type=F32)
    return d(bh) + d(bm) + d(bl)


def _norm_mod(x, g, shift, scale):
    ms = jnp.mean(x * x, axis=-1, keepdims=True)
    y = x * lax.rsqrt(ms + NORM_EPS) * g
    return y * (1.0 + scale) + shift


def _silu(x):
    return x * jax.nn.sigmoid(x)


def _rows_load(ref):
    m = ref.shape[0] // ROW_SUB
    return jnp.concatenate([ref[pl.ds(j, m, stride=ROW_SUB), :] for j in range(ROW_SUB)], axis=1)


def _rows_store(ref, val):
    m = val.shape[0]
    for j in range(ROW_SUB):
        ref[pl.ds(j, m, stride=ROW_SUB), :] = val[:, j * LANES:(j + 1) * LANES]


def _row_tile(ref, idx):
    return ref.at[pl.ds(pl.multiple_of(idx * ROW_SUB, ROW_SUB), ROW_SUB)]


def _mod_row(i, p_tiles, tiles_per_seq):
    return jnp.where(i < p_tiles, 0, 1 + (i - p_tiles) // tiles_per_seq)


def _mod_kernel(cond_ref, w_ref, b_ref, o_ref):
    o_ref[...] = _dot_f32(_silu(cond_ref[...]), w_ref[...]) + b_ref[...]


def _modulation(cond, ada_w, ada_b):
    depth, _, n6 = ada_w.shape
    tn = 1536
    return pl.pallas_call(
        _mod_kernel,
        grid=(depth, n6 // tn),
        in_specs=[
            pl.BlockSpec(cond.shape, lambda l, j: (0, 0)),
            pl.BlockSpec((None, D, tn), lambda l, j: (l, 0, j)),
            pl.BlockSpec((None, 1, tn), lambda l, j: (l, 0, j)),
        ],
        out_specs=pl.BlockSpec((None, cond.shape[0], tn), lambda l, j: (l, 0, j)),
        out_shape=jax.ShapeDtypeStruct((depth, cond.shape[0], n6), F32),
        compiler_params=_cparams(2),
        name="adaln_mod",
    )(cond, ada_w, ada_b.reshape(depth, 1, n6))


def _rwkv_pre_kernel(x_ref, xu_ref, xd_ref, mod_ref, g_ref, mu_ref, wrkv_ref, g1_ref, g2_ref, w1_ref, w2_ref,
                     w0_ref, a1_ref, a2_ref, a0_ref, kk_ref, ka_ref, rk_ref, hsum_ref, hexp_ref,
                     r_o, v_o, kk_o, lw_o, kd_o, b_o, gate_o, bonus_o, *, p_tiles, tiles_per_seq):
    i = pl.program_id(0)
    is_p = i < p_tiles
    sub = (i - p_tiles) % tiles_per_seq
    mod = mod_ref[...]
    shift, scale = mod[:, 0:D], mod[:, D:2 * D]
    g = g_ref[...]
    h = _norm_mod(x_ref[...], g, shift, scale)
    hu = _norm_mod(xu_ref[...], g, shift, scale)
    hd = _norm_mod(xd_ref[...], g, shift, scale)

    q = D // 4
    row = lax.broadcasted_iota(I32, (TM, 1), 0)
    per = jnp.where(is_p, TM, GRID_W)
    pos = row & (per - 1)

    def prev1(a):
        return jnp.where(pos == 0, 0.0, pltpu.roll(a, 1, 0))

    def next1(a):
        return jnp.where(pos == per - 1, 0.0, pltpu.roll(a, TM - 1, 0))

    h0, h1, h2, h3 = (h[:, j * q:(j + 1) * q] for j in range(4))
    up = jnp.concatenate([jnp.where(sub == 0, 0.0, hu[:, 2 * q:3 * q]), h2[0:TM - GRID_W]], axis=0)
    down = jnp.concatenate([h3[GRID_W:TM], jnp.where(sub == tiles_per_seq - 1, 0.0, hd[:, 3 * q:4 * q])], axis=0)
    s0 = prev1(h0)
    s1 = jnp.where(is_p, prev1(h1), next1(h1))
    s2 = jnp.where(is_p, next1(h2), up)
    s3 = jnp.where(is_p, next1(h3), down)
    xx = jnp.concatenate([s0, s1, s2, s3], axis=1) - h

    mu = mu_ref[...]
    mix = lambda j: h + xx * mu[j:j + 1]
    r = _bdot(mix(0), wrkv_ref[0])
    k = _bdot(mix(2), wrkv_ref[1])
    v = _bdot(mix(3), wrkv_ref[2])
    gate = _bdot(jax.nn.sigmoid(_bdot(mix(5), g1_ref[...])), g2_ref[...])
    w_all = w0_ref[...] + _bdot(jnp.tanh(_bdot(mix(1), w1_ref[...])), w2_ref[...])
    a_all = jax.nn.sigmoid(a0_ref[...] + _bdot(_bdot(mix(4), a1_ref[...]), a2_ref[...]))

    hsum, hexp = hsum_ref[...], hexp_ref[...]
    head_sum = lambda t: _dot_f32_rhs01(_dot_f32_rhs01(t, hsum), hexp)

    kkr = k * kk_ref[...]
    kk = kkr / jnp.maximum(jnp.sqrt(head_sum(kkr * kkr)), 1e-12)
    ka = ka_ref[...]
    kd_sum = jnp.zeros_like(k)
    for d in range(2):
        wd = w_all[:, d * D:(d + 1) * D]
        z = -wd
        softplus = jnp.maximum(z, 0.0) + jnp.log(1.0 + jnp.exp(-jnp.abs(z)))
        lw_o[d] = -jnp.exp(-softplus - 0.5)
        a = a_all[:, d * D:(d + 1) * D]
        kd = k * (1.0 + (a - 1.0) * ka)
        kd_o[d] = kd
        b_o[d] = kk * a
        kd_sum = kd_sum + kd
    r_o[...] = r
    v_o[...] = v
    kk_o[...] = kk
    gate_o[...] = gate
    bonus_o[...] = head_sum(r * kd_sum * rk_ref[...]) * v


def _rwkv_pre(x, mod, p_tiles, tiles_per_seq, norm_g, wts):
    n = x.shape[0]
    n_tiles = n // TM
    hb = TM // GRID_W
    n_hblk = n // GRID_W
    row = lambda a: a.reshape(1, -1)
    full = lambda a: pl.BlockSpec(a.shape, lambda i: (0,) * a.ndim)
    tok = pl.BlockSpec((TM, D), lambda i: (i, 0))
    tok2 = pl.BlockSpec((2, TM, D), lambda i: (0, i, 0))
    consts = [row(norm_g), wts["mu"], wts["wrkv"], wts["g1"], wts["g2"], wts["w1"], wts["w2"], wts["w0"],
              wts["a1"], wts["a2"], wts["a0"], row(wts["k_k"]), row(wts["k_a"]), row(wts["r_k"]),
              wts["hsum"], wts["hexp"]]
    return pl.pallas_call(
        functools.partial(_rwkv_pre_kernel, p_tiles=p_tiles, tiles_per_seq=tiles_per_seq),
        grid=(n_tiles,),
        in_specs=[
            tok,
            pl.BlockSpec((GRID_W, D), lambda i: (jnp.maximum(i * hb - 1, 0), 0)),
            pl.BlockSpec((GRID_W, D), lambda i: (jnp.minimum(i * hb + hb, n_hblk - 1), 0)),
            pl.BlockSpec((None, 1, N_MOD * D), lambda i: (_mod_row(i, p_tiles, tiles_per_seq), 0, 0)),
        ] + [full(a) for a in consts],
        out_specs=[tok, tok, tok, tok2, tok2, tok2, tok, tok],
        out_shape=[jax.ShapeDtypeStruct((n, D), F32)] * 3 + [jax.ShapeDtypeStruct((2, n, D), F32)] * 3
        + [jax.ShapeDtypeStruct((n, D), F32)] * 2,
        compiler_params=_cparams(1),
        name="rwkv_pre",
    )(x, x, x, mod, *consts)


class _ScanItems:
    def __init__(self, n_ctx, ctx_chunks, n_lat, lat_chunks):
        self.n_ctx, self.ctx_chunks, self.n_lat, self.lat_chunks = n_ctx, ctx_chunks, n_lat, lat_chunks
        self.ctx_items = n_ctx * ctx_chunks
        self.n_items = self.ctx_items + n_lat * lat_chunks

    def decode(self, j):
        is_ctx = j < self.ctx_items
        jl = jnp.maximum(j - self.ctx_items, 0)
        c = jnp.where(is_ctx, j % self.ctx_chunks, jl % self.lat_chunks)
        return is_ctx, c, jnp.where(is_ctx, self.ctx_chunks, self.lat_chunks)

    def block(self, d, j):
        _, c, nc = self.decode(j)
        return jnp.where(d == 0, j, j - c + (nc - 1 - c))

    def ctx_seq(self, j):
        return jnp.minimum(j // self.ctx_chunks, self.n_ctx - 1)

    def lat_seq(self, j):
        return jnp.maximum(j - self.ctx_items, 0) // self.lat_chunks


def _rwkv_scan_kernel(r_ref, v_ref, kk_ref, lw_ref, kd_ref, b_ref, s0_ref, y_ref, sf_ref, s_scr, *, items):
    d = pl.program_id(0)
    is_ctx, c, nc = items.decode(pl.program_id(1))
    rev = d == 1
    ch = RWKV_CHUNK

    @pl.when(c == 0)
    def _():
        s_scr[...] = jnp.where(is_ctx, 0.0, s0_ref[...])

    ti = lax.broadcasted_iota(I32, (ch, ch), 0)
    tj = lax.broadcasted_iota(I32, (ch, ch), 1)
    tri = jnp.where(jnp.where(rev, ti - tj, tj - ti) <= 0, 1.0, 0.0).astype(BF16)
    lw = lw_ref[...]
    cum = _dot_f32_lhs01(tri, lw)
    last = jnp.where(rev, cum[0:1], cum[ch - 1:ch])
    kk = kk_ref[...]
    kd = kd_ref[...]
    b = b_ref[...]
    e_neg = jnp.exp(-cum)
    e_rel = jnp.exp(last - cum)
    at = -kk * jnp.exp(cum - lw)
    rt = r_ref[...] * jnp.exp(cum)
    bt = b * e_neg
    kt = kd * e_neg
    bh = b * e_rel
    kh = kd * e_rel
    g_all = jnp.exp(last)
    v = v_ref[...]

    si = lax.broadcasted_iota(I32, (2 * ch, 2 * ch), 0)
    sj = lax.broadcasted_iota(I32, (2 * ch, 2 * ch), 1)
    same = (si < ch) == (sj < ch)
    ui, uj = si & (ch - 1), sj & (ch - 1)
    order = jnp.where(rev, ui - uj, uj - ui)
    strict = same & (order < 0)
    incl = same & (order <= 0)
    first = lax.broadcasted_iota(I32, (1, LANES), 1) < HEAD_A

    def stack(xp):
        return jnp.concatenate([jnp.where(first, xp, 0.0), jnp.where(first, 0.0, xp)], axis=0)

    pairs = range(N_PAIR)
    sls = [slice(p * LANES, (p + 1) * LANES) for p in pairs]
    s_old = [s_scr[p] for p in pairs]
    m1 = [_bdot_nt(jnp.concatenate([stack(at[:, sl]), stack(rt[:, sl])], axis=0),
                   jnp.concatenate([stack(bt[:, sl]), stack(kt[:, sl])], axis=0)) for sl in sls]
    m2 = [_bdot_nt(jnp.concatenate([at[:, sl], rt[:, sl]], axis=0), s_old[p])
          for p, sl in enumerate(sls)]
    vs = [stack(v[:, sl]) for sl in sls]
    x = [stack(m2[p][0:ch]) + _bdot(jnp.where(strict, m1[p][0:2 * ch, 2 * ch:4 * ch], 0.0), vs[p]) for p in pairs]
    nm = [jnp.where(strict, m1[p][0:2 * ch, 0:2 * ch], 0.0).astype(BF16) for p in pairs]
    x = [x[p] + _bdot(nm[p], x[p]) for p in pairs]
    for _ in range(5):
        nm = [_bdot(nm[p], nm[p]).astype(BF16) for p in pairs]
        x = [x[p] + _bdot(nm[p], x[p]) for p in pairs]
    uv = [jnp.concatenate([x[p], vs[p]], axis=0).astype(BF16) for p in pairs]
    ys = [_bdot(jnp.concatenate([jnp.where(incl, m1[p][2 * ch:4 * ch, 0:2 * ch], 0.0),
                                 jnp.where(incl, m1[p][2 * ch:4 * ch, 2 * ch:4 * ch], 0.0)], axis=1), uv[p])
          for p in pairs]
    s_new = [s_old[p] * g_all[:, sl]
             + _bdot_tn(uv[p], jnp.concatenate([stack(bh[:, sl]), stack(kh[:, sl])], axis=0))
             for p, sl in enumerate(sls)]
    for p, sl in enumerate(sls):
        y_ref[:, sl] = ys[p][0:ch] + ys[p][ch:2 * ch] + m2[p][ch:2 * ch]
    for p in pairs:
        s_scr[p] = s_new[p]

    @pl.when(is_ctx & (c == nc - 1))
    def _():
        sf_ref[...] = s_scr[...]


def _rwkv_scan(r, v, kk, lw, kd, b, s0, items):
    n = r.shape[0]
    ch = RWKV_CHUNK
    tok = pl.BlockSpec((ch, D), lambda d, j: (items.block(d, j), 0))
    tokd = pl.BlockSpec((None, ch, D), lambda d, j: (d, items.block(d, j), 0))
    st = lambda seq: pl.BlockSpec((None, None, N_PAIR, LANES, LANES), lambda d, j: (d, seq(j), 0, 0, 0))
    return pl.pallas_call(
        functools.partial(_rwkv_scan_kernel, items=items),
        grid=(2, items.n_items),
        in_specs=[tok, tok, tok, tokd, tokd, tokd, st(items.lat_seq)],
        out_specs=[tokd, st(items.ctx_seq)],
        out_shape=[jax.ShapeDtypeStruct((2, n, D), F32),
                   jax.ShapeDtypeStruct((2, items.n_ctx, N_PAIR, LANES, LANES), F32)],
        scratch_shapes=[pltpu.VMEM((N_PAIR, LANES, LANES), F32)],
        compiler_params=_cparams(2),
        name="rwkv_scan",
    )(r, v, kk, lw, kd, b, s0)


def _rwkv_post_kernel(x_ref, y_ref, gate_ref, bonus_ref, mod_ref, lnw_ref, lnb_ref, wo_ref, hsum_ref, hexp_ref, o_ref):
    hsum, hexp = hsum_ref[...], hexp_ref[...]
    head_mean = lambda t: _dot_f32_rhs01(_dot_f32_rhs01(t, hsum), hexp) * (1.0 / HEAD_A)
    y = y_ref[0] + y_ref[1]
    yc = y - head_mean(y)
    yn = yc * lax.rsqrt(head_mean(yc * yc) + LNX_EPS)
    z = (yn * lnw_ref[...] + lnb_ref[...] + bonus_ref[...]) * gate_ref[...]
    o_ref[...] = x_ref[...] + mod_ref[:, 2 * D:3 * D] * _bdot(z, wo_ref[...])


def _rwkv_post(x, y, gate, bonus, mod, p_tiles, tiles_per_seq, ln_w, ln_b, wo, hsum, hexp):
    n = x.shape[0]
    full = lambda a: pl.BlockSpec(a.shape, lambda i: (0,) * a.ndim)
    tok = pl.BlockSpec((TM, D), lambda i: (i, 0))
    consts = [ln_w.reshape(1, D), ln_b.reshape(1, D), wo, hsum, hexp]
    return pl.pallas_call(
        _rwkv_post_kernel,
        grid=(n // TM,),
        in_specs=[tok, pl.BlockSpec((2, TM, D), lambda i: (0, i, 0)), tok, tok,
                  pl.BlockSpec((None, 1, N_MOD * D), lambda i: (_mod_row(i, p_tiles, tiles_per_seq), 0, 0))]
        + [full(a) for a in consts],
        out_specs=tok,
        out_shape=jax.ShapeDtypeStruct((n, D), F32),
        compiler_params=_cparams(1),
        name="rwkv_post",
    )(x, y, gate, bonus, mod, *consts)


def _ret_pre_kernel(x_ref, mod_ref, g_ref, cos_ref, sin_ref, win_ref, q_o, k_o, v_o, gate_o):
    mod = mod_ref[...]
    h = _norm_mod(x_ref[...], g_ref[...], mod[:, 0:D], mod[:, D:2 * D])
    proj = _bdot(h, win_ref[...])
    cos, sin = cos_ref[...], sin_ref[...]

    def rope(t):
        outs = []
        for j in range(D // LANES):
            tj = t[:, j * LANES:(j + 1) * LANES]
            cj = cos[:, (j % 2) * LANES:(j % 2 + 1) * LANES]
            sj = sin[:, (j % 2) * LANES:(j % 2 + 1) * LANES]
            outs.append(tj * cj + pltpu.roll(tj, LANES // 2, 1) * sj)
        return jnp.concatenate(outs, axis=1)

    q_o[...] = rope(proj[:, 0:D]).astype(BF16)
    k_o[...] = rope(proj[:, D:2 * D] * (DK_B ** -0.5)).astype(BF16)
    v_o[...] = proj[:, 2 * D:4 * D].astype(BF16)
    gate_o[...] = proj[:, 4 * D:6 * D]


def _ret_pre(x, mod, p_tiles, tiles_per_seq, norm_g, cos_t, sin_t, w_in):
    n = x.shape[0]
    full = lambda a: pl.BlockSpec(a.shape, lambda i: (0,) * a.ndim)
    tok = lambda w: pl.BlockSpec((TM, w), lambda i: (i, 0))
    tab = pl.BlockSpec((TM, DK_B), lambda i: (jnp.where(i < p_tiles, 0, 1 + (i - p_tiles) % tiles_per_seq), 0))
    return pl.pallas_call(
        _ret_pre_kernel,
        grid=(n // TM,),
        in_specs=[tok(D), pl.BlockSpec((None, 1, N_MOD * D), lambda i: (_mod_row(i, p_tiles, tiles_per_seq), 0, 0)),
                  full(norm_g.reshape(1, D)), tab, tab, full(w_in)],
        out_specs=[tok(D), tok(D), tok(2 * D), tok(2 * D)],
        out_shape=[jax.ShapeDtypeStruct((n, D), BF16), jax.ShapeDtypeStruct((n, D), BF16),
                   jax.ShapeDtypeStruct((n, 2 * D), BF16), jax.ShapeDtypeStruct((n, 2 * D), F32)],
        compiler_params=_cparams(1),
        name="ret_pre",
    )(x, mod, norm_g.reshape(1, D), cos_t, sin_t, w_in)


def _ret_scan_kernel(lg_ref, q_ref, k_ref, v_ref, s0_ref, o_ref, sf_ref, s_scr, *, items):
    d = pl.program_id(0)
    is_ctx, c, nc = items.decode(pl.program_id(1))
    rev = d == 1
    ch = RET_CHUNK

    @pl.when(c == 0)
    def _():
        s_scr[...] = jnp.where(is_ctx, 0.0, s0_ref[...])

    ti = lax.broadcasted_iota(I32, (ch, ch), 0)
    tj = lax.broadcasted_iota(I32, (ch, ch), 1)
    rel = jnp.where(rev, tj - ti, ti - tj).astype(F32)
    steps_q = jnp.where(rev, ch - ti, ti + 1).astype(F32)
    steps_k = jnp.where(rev, ti, ch - 1 - ti).astype(F32)
    for hd in range(H_B):
        lg = lg_ref[d, hd]
        mask = jnp.where(rel >= 0, jnp.exp(jnp.maximum(rel, 0.0) * lg), 0.0)
        q_dec = jnp.exp(steps_q * lg)
        k_dec = jnp.exp(steps_k * lg)
        chunk_dec = jnp.exp(jnp.full((1, DV_B), float(ch), F32) * lg)
        qh = q_ref[:, hd * DK_B:(hd + 1) * DK_B]
        kh = k_ref[:, hd * DK_B:(hd + 1) * DK_B]
        vh = v_ref[:, hd * DV_B:(hd + 1) * DV_B]
        s = s_scr[hd]
        scores = _bdot_nt(qh, kh) * mask
        inner = _bdot(scores, vh)
        cross = _bdot(qh, s) * jnp.concatenate([q_dec] * (DV_B // ch), axis=1)
        o_ref[:, hd * DV_B:(hd + 1) * DV_B] = inner + cross
        kdec = kh.astype(F32) * jnp.concatenate([k_dec] * (DK_B // ch), axis=1)
        s_scr[hd] = s * chunk_dec + _bdot_tn(kdec, vh)

    @pl.when(is_ctx & (c == nc - 1))
    def _():
        sf_ref[...] = s_scr[...]


def _ret_scan(log_gamma, q, k, v, s0, items):
    n = q.shape[0]
    ch = RET_CHUNK
    tok = lambda w: pl.BlockSpec((ch, w), lambda d, j: (items.block(d, j), 0))
    st = lambda seq: pl.BlockSpec((None, None, H_B, DK_B, DV_B), lambda d, j: (d, seq(j), 0, 0, 0))
    return pl.pallas_call(
        functools.partial(_ret_scan_kernel, items=items),
        grid=(2, items.n_items),
        in_specs=[pl.BlockSpec(memory_space=pltpu.SMEM), tok(D), tok(D), tok(2 * D), st(items.lat_seq)],
        out_specs=[pl.BlockSpec((None, ch, 2 * D), lambda d, j: (d, items.block(d, j), 0)), st(items.ctx_seq)],
        out_shape=[jax.ShapeDtypeStruct((2, n, 2 * D), F32),
                   jax.ShapeDtypeStruct((2, items.n_ctx, H_B, DK_B, DV_B), F32)],
        scratch_shapes=[pltpu.VMEM((H_B, DK_B, DV_B), F32)],
        compiler_params=_cparams(2),
        name="ret_scan",
    )(log_gamma, q, k, v, s0)


def _ret_post_kernel(x_ref, o_ref, gate_ref, mod_ref, wout_ref, out_ref):
    o = o_ref[0] + o_ref[1]
    parts = []
    for hd in range(H_B):
        oh = o[:, hd * DV_B:(hd + 1) * DV_B]
        parts.append(oh * lax.rsqrt(jnp.mean(oh * oh, axis=-1, keepdims=True) + NORM_EPS))
    y = _silu(gate_ref[...]) * jnp.concatenate(parts, axis=1)
    out_ref[...] = x_ref[...] + mod_ref[:, 2 * D:3 * D] * _bdot(y, wout_ref[...])


def _ret_post(x, o, gate, mod, p_tiles, tiles_per_seq, w_out):
    n = x.shape[0]
    tok = lambda w: pl.BlockSpec((TM, w), lambda i: (i, 0))
    return pl.pallas_call(
        _ret_post_kernel,
        grid=(n // TM,),
        in_specs=[tok(D), pl.BlockSpec((2, TM, 2 * D), lambda i: (0, i, 0)), tok(2 * D),
                  pl.BlockSpec((None, 1, N_MOD * D), lambda i: (_mod_row(i, p_tiles, tiles_per_seq), 0, 0)),
                  pl.BlockSpec(w_out.shape, lambda i: (0, 0))],
        out_specs=tok(D),
        out_shape=jax.ShapeDtypeStruct((n, D), F32),
        compiler_params=_cparams(1),
        name="ret_post",
    )(x, o, gate, mod, w_out)


def _moe_route_kernel(x_ref, mod_ref, g_ref, router_ref, bias_ref, h_o, e_o, w_o, p_o, cnt_o, carry):
    i = pl.program_id(0)

    @pl.when(i == 0)
    def _():
        carry[...] = jnp.zeros_like(carry)

    mod = mod_ref[...]
    h = _norm_mod(x_ref[...], g_ref[...], mod[:, 3 * D:4 * D], mod[:, 4 * D:5 * D])
    _rows_store(h_o, h)
    lane =lax.broadcasted_iota(I32, (TM, LANES), 1)
    valid = lane < N_EXPERTS
    neg = -jnp.inf
    scores = jax.nn.sigmoid(_dot_f32(h, router_ref[...]))
    biased = jnp.where(valid, scores + bias_ref[...], neg)

    def group_reduce(t, op):
        s = 1
        while s < PER_GROUP:
            partner = jnp.where((lane & s) == 0, pltpu.roll(t, LANES - s, 1), pltpu.roll(t, s, 1))
            t = op(t, partner)
            s *= 2
        return t

    lane_f = lane.astype(F32)
    group_f = jnp.floor(lane_f * (1.0 / PER_GROUP))

    def first_lane_of_max(t):
        m = jnp.max(t, axis=-1, keepdims=True)
        return jnp.min(jnp.where(t == m, lane_f, float(LANES)), axis=-1, keepdims=True)

    m1 = group_reduce(biased, jnp.maximum)
    first1 = group_reduce(jnp.where(biased == m1, lane_f, float(LANES)), jnp.minimum)
    m2 = group_reduce(jnp.where(lane_f == first1, neg, biased), jnp.maximum)
    gscore = jnp.where(valid, m1 + m2, neg)
    cand = jnp.full((TM, LANES), neg, F32)
    for _ in range(TOPK_GROUPS):
        gsel = group_f == jnp.floor(first_lane_of_max(gscore) * (1.0 / PER_GROUP))
        cand = jnp.where(gsel, biased, cand)
        gscore = jnp.where(gsel, neg, gscore)
    hits = []
    sel01 = jnp.zeros((TM, LANES), F32)
    e_cols = jnp.zeros((TM, LANES), F32)
    for j in range(TOP_K):
        fl = first_lane_of_max(cand)
        hit = lane_f == fl
        hits.append(hit)
        sel01 = jnp.where(hit, 1.0, sel01)
        cand = jnp.where(hit, neg, cand)
        e_cols = jnp.where(lane == j, fl, e_cols)
    wsum = jnp.sum(sel01 * scores, axis=-1, keepdims=True)

    ri = lax.broadcasted_iota(I32, (TM, TM), 0)
    rj = lax.broadcasted_iota(I32, (TM, TM), 1)
    below = jnp.where(rj < ri, 1.0, 0.0).astype(BF16)
    rank = jnp.dot(below, sel01.astype(BF16), preferred_element_type=F32) + carry[...]
    carry[...] = carry[...] + jnp.sum(sel01, axis=0, keepdims=True)
    w_cols = jnp.zeros((TM, LANES), F32)
    p_cols = jnp.zeros((TM, LANES), F32)
    for j in range(TOP_K):
        wj = jnp.sum(jnp.where(hits[j], scores, 0.0), axis=-1, keepdims=True)
        w_cols = jnp.where(lane == j, wj / wsum * ROUTED_SCALE, w_cols)
        p_cols = jnp.where(lane == j, jnp.sum(jnp.where(hits[j], rank, 0.0), axis=-1, keepdims=True), p_cols)
    e_o[...] = e_cols.astype(I32)
    w_o[...] = w_cols
    p_o[...] = p_cols.astype(I32)
    cnt_o[...] = jnp.broadcast_to(carry[...], cnt_o.shape)


def _moe_route(x, mod, p_tiles, tiles_per_seq, norm_g, router, bias):
    n = x.shape[0]
    full = lambda a: pl.BlockSpec(a.shape, lambda i: (0,) * a.ndim)
    tok = lambda w: pl.BlockSpec((TM, w), lambda i: (i, 0))
    router_p = jnp.pad(router, ((0, 0), (0, LANES - N_EXPERTS)))
    bias_p = jnp.pad(bias, (0, LANES - N_EXPERTS)).reshape(1, LANES)
    return pl.pallas_call(
        _moe_route_kernel,
        grid=(n // TM,),
        in_specs=[tok(D), pl.BlockSpec((None, 1, N_MOD * D), lambda i: (_mod_row(i, p_tiles, tiles_per_seq), 0, 0)),
                  full(norm_g.reshape(1, D)), full(router_p), full(bias_p)],
        out_specs=[pl.BlockSpec((TM * ROW_SUB, LANES), lambda i: (i, 0)), tok(LANES), tok(LANES), tok(LANES),
                   pl.BlockSpec((8, LANES), lambda i: (0, 0))],
        out_shape=[jax.ShapeDtypeStruct((n * ROW_SUB, LANES), F32), jax.ShapeDtypeStruct((n, LANES), I32),
                   jax.ShapeDtypeStruct((n, LANES), F32), jax.ShapeDtypeStruct((n, LANES), I32),
                   jax.ShapeDtypeStruct((8, LANES), F32)],
        scratch_shapes=[pltpu.VMEM((1, LANES), F32)],
        compiler_params=_cparams(1),
        name="moe_route",
    )(x, mod, norm_g.reshape(1, D), router_p, bias_p)


DISPATCH_TOK = 512


def _moe_dispatch_kernel(pad_lo_ref, pad_n_ref, dest_ref, h_ref, xs_hbm, zero_scr, sem):
    @pl.when(pl.program_id(0) == 0)
    def _():
        zero_scr[...] = jnp.zeros_like(zero_scr)

        def pad_copy(e, s):
            return pltpu.make_async_copy(zero_scr, _row_tile(xs_hbm, pad_lo_ref[e] + s), sem)

        def per_expert(e, carry):
            def issue(s, c):
                pad_copy(e, s).start()
                return c

            def drain(s, c):
                pad_copy(e, s).wait()
                return c

            lax.fori_loop(0, pad_n_ref[e], issue, 0)
            lax.fori_loop(0, pad_n_ref[e], drain, 0)
            return carry

        lax.fori_loop(0, N_EXPERTS, per_expert, 0)

    def copy(t, kq):
        return pltpu.make_async_copy(_row_tile(h_ref, t), _row_tile(xs_hbm, dest_ref[t * TOP_K + kq]), sem)

    def issue(t, carry):
        for kq in range(TOP_K):
            copy(t, kq).start(priority=kq % 2)
        return carry

    def drain(t, carry):
        for kq in range(TOP_K):
            copy(t, kq).wait()
        return carry

    lax.fori_loop(0, DISPATCH_TOK, issue, 0)
    lax.fori_loop(0, DISPATCH_TOK, drain, 0)


def _moe_dispatch(pad_lo, pad_n, dest_flat, h, n_slots):
    n = h.shape[0] // ROW_SUB
    grid_spec = pltpu.PrefetchScalarGridSpec(
        num_scalar_prefetch=2,
        grid=(n // DISPATCH_TOK,),
        in_specs=[pl.BlockSpec((DISPATCH_TOK * TOP_K,), lambda i, lo, cnt: (i,), memory_space=pltpu.SMEM),
                  pl.BlockSpec((DISPATCH_TOK * ROW_SUB, LANES), lambda i, lo, cnt: (i, 0))],
        out_specs=pl.BlockSpec(memory_space=pl.ANY),
        scratch_shapes=[pltpu.VMEM((ROW_SUB, LANES), F32), pltpu.SemaphoreType.DMA],
    )
    return pl.pallas_call(
        _moe_dispatch_kernel,
        grid_spec=grid_spec,
        out_shape=jax.ShapeDtypeStruct((n_slots * ROW_SUB, LANES), F32),
        compiler_params=_cparams(1, has_side_effects=True),
        name="moe_dispatch",
    )(pad_lo, pad_n, dest_flat, h)


def _moe_expert_kernel(be_ref, xs_ref, wgu_ref, wdn_ref, o_ref, wgu_bf, wdn_bf):
    i = pl.program_id(0)

    @pl.when((i == 0) | (be_ref[i] != be_ref[jnp.maximum(i - 1, 0)]))
    def _():
        wgu_bf[...] = wgu_ref[...].astype(BF16)
        wdn_bf[...] = wdn_ref[...].astype(BF16)

    gu = _bdot(_rows_load(xs_ref), wgu_bf[...])
    act = _silu(gu[:, 0:D_EXPERT]) * gu[:, D_EXPERT:2 * D_EXPERT]
    _rows_store(o_ref, _bdot(act, wdn_bf[...]))


def _moe_expert(block_e, xs, w_gu, w_down, layer):
    n_slots = xs.shape[0] // ROW_SUB
    n_blocks = n_slots // EXPERT_BLOCK
    rows = pl.BlockSpec((EXPERT_BLOCK * ROW_SUB, LANES), lambda i, be: (i, 0))
    grid_spec = pltpu.PrefetchScalarGridSpec(
        num_scalar_prefetch=1,
        grid=(n_blocks,),
        in_specs=[rows,
                  pl.BlockSpec((None, None, D, 2 * D_EXPERT), lambda i, be: (layer, be[i], 0, 0)),
                  pl.BlockSpec((None, None, D_EXPERT, D), lambda i, be: (layer, be[i], 0, 0))],
        out_specs=rows,
        scratch_shapes=[pltpu.VMEM((D, 2 * D_EXPERT), BF16), pltpu.VMEM((D_EXPERT, D), BF16)],
    )
    return pl.pallas_call(
        _moe_expert_kernel,
        grid_spec=grid_spec,
        out_shape=jax.ShapeDtypeStruct((n_slots * ROW_SUB, LANES), F32),
        compiler_params=_cparams(1),
        name="moe_expert",
    )(block_e, xs, w_gu, w_down)


COMBINE_TOK = 128


def _moe_combine_kernel(dest_ref, x_ref, h_ref, w_ref, mod_ref, shgu_ref, shdn_ref, gfin_ref, ys_hbm, o_ref, buf, sem,
                        *, final_norm):
    def copy(t, kq):
        return pltpu.make_async_copy(_row_tile(ys_hbm, dest_ref[t * TOP_K + kq]), _row_tile(buf.at[kq], t), sem)

    def issue(t, carry):
        for kq in range(TOP_K):
            copy(t, kq).start(priority=kq % 2)
        return carry

    def drain(t, carry):
        for kq in range(TOP_K):
            copy(t, kq).wait()
        return carry

    lax.fori_loop(0, COMBINE_TOK, issue, 0)
    gu = _bdot(_rows_load(h_ref), shgu_ref[...])
    acc = _bdot(_silu(gu[:, 0:D_EXPERT]) * gu[:, D_EXPERT:2 * D_EXPERT], shdn_ref[...])
    lax.fori_loop(0, COMBINE_TOK, drain, 0)
    w = w_ref[...]
    for kq in range(TOP_K):
        acc = acc + _rows_load(buf.at[kq]) * w[:, kq:kq + 1]
    out = x_ref[...] + mod_ref[:, 5 * D:6 * D] * acc
    if final_norm:
        out = out * lax.rsqrt(jnp.mean(out * out, axis=-1, keepdims=True) + NORM_EPS) * gfin_ref[...]
    o_ref[...] = out


def _moe_combine(dest_flat, x, h, w_cols, mod, p_tiles, tiles_per_seq, sh_gu, sh_down, g_final, ys, final_norm):
    n = x.shape[0]
    ratio = TM // COMBINE_TOK
    tok = lambda w: pl.BlockSpec((COMBINE_TOK, w), lambda i: (i, 0))
    full = lambda a: pl.BlockSpec(a.shape, lambda i: (0,) * a.ndim)
    return pl.pallas_call(
        functools.partial(_moe_combine_kernel, final_norm=final_norm),
        grid=(n // COMBINE_TOK,),
        in_specs=[pl.BlockSpec((COMBINE_TOK * TOP_K,), lambda i: (i,), memory_space=pltpu.SMEM),
                  tok(D), pl.BlockSpec((COMBINE_TOK * ROW_SUB, LANES), lambda i: (i, 0)), tok(LANES),
                  pl.BlockSpec((None, 1, N_MOD * D), lambda i: (_mod_row(i // ratio, p_tiles, tiles_per_seq), 0, 0)),
                  full(sh_gu), full(sh_down), full(g_final),
                  pl.BlockSpec(memory_space=pl.ANY)],
        out_specs=tok(D),
        out_shape=jax.ShapeDtypeStruct((n, D), F32),
        scratch_shapes=[pltpu.VMEM((TOP_K, COMBINE_TOK * ROW_SUB, LANES), F32), pltpu.SemaphoreType.DMA],
        compiler_params=_cparams(1),
        name="moe_combine",
    )(dest_flat, x, h, w_cols, mod, sh_gu, sh_down, g_final, ys)


def _moe_layer(x, mod, p_tiles, tiles_per_seq, norm_g, router, bias, w_gu, w_down, layer, sh_gu, sh_down, g_final,
               final_norm):
    n = x.shape[0]
    h, e_cols, w_cols, p_cols, counts = _moe_route(x, mod, p_tiles, tiles_per_seq, norm_g, router, bias)
    counts = counts[0, :N_EXPERTS].astype(I32)
    padded = (counts + EXPERT_BLOCK - 1) // EXPERT_BLOCK * EXPERT_BLOCK
    pad_end = jnp.cumsum(padded)
    pad_start = pad_end - padded
    dest = (pad_start[e_cols[:, :TOP_K]] + p_cols[:, :TOP_K]).reshape(-1)
    n_blocks = n * TOP_K // EXPERT_BLOCK + N_EXPERTS
    n_slots = n_blocks * EXPERT_BLOCK
    block_start = jnp.arange(n_blocks, dtype=I32) * EXPERT_BLOCK
    block_e = jnp.minimum(jnp.sum((pad_end[None, :] <= block_start[:, None]).astype(I32), axis=1), N_EXPERTS - 1)
    pad_n = (padded - counts).at[N_EXPERTS - 1].add(n_slots - pad_end[N_EXPERTS - 1])
    xs = _moe_dispatch(pad_start + counts, pad_n, dest, h, n_slots)
    ys = _moe_expert(block_e, xs, w_gu, w_down, layer)
    return _moe_combine(dest, x, h, w_cols, mod, p_tiles, tiles_per_seq, sh_gu, sh_down, g_final.reshape(1, D), ys,
                        final_norm)


def _pair_pack(s):
    lead = s.shape[:-3]
    s = s.reshape(lead + (N_PAIR, 2, HEAD_A, HEAD_A))
    z = jnp.zeros_like(s[..., 0, :, :])
    top = jnp.concatenate([s[..., 0, :, :], z], axis=-1)
    bot = jnp.concatenate([z, s[..., 1, :, :]], axis=-1)
    return jnp.concatenate([top, bot], axis=-2)


def _pair_unpack(s):
    a = s[..., :HEAD_A, :HEAD_A]
    b = s[..., HEAD_A:, HEAD_A:]
    return jnp.stack([a, b], axis=-3).reshape(s.shape[:-3] + (H_A, HEAD_A, HEAD_A))


def _rope_tables(rows):
    half = DK_B // 2
    n_freq = half // 2
    inv = ROPE_BASE ** (-jnp.arange(n_freq, dtype=F32) / n_freq)
    pos_r = jnp.repeat(jnp.arange(rows, dtype=F32), GRID_W)
    pos_c = jnp.tile(jnp.arange(GRID_W, dtype=F32), rows)

    def tab(pos):
        ang = pos[:, None] * inv[None, :]
        c, s = jnp.cos(ang), jnp.sin(ang)
        return jnp.concatenate([c, c], -1), jnp.concatenate([-s, s], -1)

    cr, sr = tab(pos_r)
    cc, sc = tab(pos_c)
    cos = jnp.concatenate([cr, cc], -1)
    sin = jnp.concatenate([sr, sc], -1)
    cos = jnp.concatenate([jnp.ones((TM, DK_B), F32), cos], 0)
    sin = jnp.concatenate([jnp.zeros((TM, DK_B), F32), sin], 0)
    return cos, sin


def kernel(x_prompt, x_sample, state_rwkv, state_ret, c, c_ctx, ada_w, ada_b, norm_mix, norm_ffn, norm_final, rwkv_mu, rwkv_w0, rwkv_w1, rwkv_w2, rwkv_a0, rwkv_a1, rwkv_a2, rwkv_wrkv, rwkv_wo, rwkv_g1, rwkv_g2, rwkv_k_k, rwkv_k_a, rwkv_r_k, rwkv_ln_w, rwkv_ln_b, ret_w_in, ret_w_out, ret_decay_logit, moe_router, moe_bias, moe_w_gu, moe_w_down, moe_sh_gu, moe_sh_down):
    bp, tp, _ = x_prompt.shape
    bs, ts, _ = x_sample.shape
    n_p, n_s = bp * tp, bs * ts
    assert tp == TM and ts % TM == 0 and TM % GRID_W == 0
    p_tiles = n_p // TM
    tiles_per_seq = ts // TM
    rows = ts // GRID_W
    x = jnp.concatenate([x_prompt.reshape(n_p, D), x_sample.reshape(n_s, D)], axis=0)
    n = n_p + n_s

    n_cond = 16
    cond = jnp.zeros((n_cond, D), F32).at[0].set(c_ctx).at[1:1 + bs].set(c)
    mod = _modulation(cond, ada_w, ada_b).reshape(ada_w.shape[0], n_cond, 1, N_MOD * D)

    head_of = jnp.arange(D, dtype=I32) // HEAD_A
    hsum = (head_of[:, None] == jnp.arange(LANES, dtype=I32)[None, :]).astype(BF16)
    hexp = hsum.T

    gl = rwkv_g1.shape[-1]
    glp = -(-gl // LANES) * LANES
    zb = lambda a: jnp.zeros_like(a)
    wts = {
        "mu": jnp.pad(rwkv_mu[0], ((0, 8 - N_MOD), (0, 0))),
        "wrkv": rwkv_wrkv[0].astype(BF16),
        "g1": jnp.pad(rwkv_g1[0], ((0, 0), (0, glp - gl))).astype(BF16),
        "g2": jnp.pad(rwkv_g2[0], ((0, glp - gl), (0, 0))).astype(BF16),
        "w1": jnp.concatenate([rwkv_w1[0, 0], rwkv_w1[0, 1]], axis=1).astype(BF16),
        "w2": jnp.concatenate([jnp.concatenate([rwkv_w2[0, 0], zb(rwkv_w2[0, 1])], 1),
                               jnp.concatenate([zb(rwkv_w2[0, 0]), rwkv_w2[0, 1]], 1)], 0).astype(BF16),
        "w0": rwkv_w0[0].reshape(1, 2 * D),
        "a1": jnp.concatenate([rwkv_a1[0, 0], rwkv_a1[0, 1]], axis=1).astype(BF16),
        "a2": jnp.concatenate([jnp.concatenate([rwkv_a2[0, 0], zb(rwkv_a2[0, 1])], 1),
                               jnp.concatenate([zb(rwkv_a2[0, 0]), rwkv_a2[0, 1]], 1)], 0).astype(BF16),
        "a0": rwkv_a0[0].reshape(1, 2 * D),
        "k_k": rwkv_k_k[0], "k_a": rwkv_k_a[0], "r_k": rwkv_r_k[0].reshape(D),
        "hsum": hsum, "hexp": hexp,
    }
    r, v, kk, lw, kd, b, gate, bonus = _rwkv_pre(x, mod[0], p_tiles, tiles_per_seq, norm_mix[0], wts)
    s0_lat = _pair_pack(jnp.moveaxis(state_rwkv[:, 0], 1, 0))
    y2, s_fin = _rwkv_scan(r, v, kk, lw, kd, b, s0_lat,
                           _ScanItems(bp, tp // RWKV_CHUNK, bs, ts // RWKV_CHUNK))
    new_state_rwkv = jnp.moveaxis(_pair_unpack(s_fin), 0, 1)[:, None]
    x = _rwkv_post(x, y2, gate, bonus, mod[0], p_tiles, tiles_per_seq, rwkv_ln_w[0], rwkv_ln_b[0],
                   rwkv_wo[0].astype(BF16), hsum, hexp)
    x = _moe_layer(x, mod[0], p_tiles, tiles_per_seq, norm_ffn[0], moe_router[0], moe_bias[0],
                   moe_w_gu, moe_w_down, 0, moe_sh_gu[0].astype(BF16),
                   moe_sh_down[0].astype(BF16), norm_final, False)

    cos_t, sin_t = _rope_tables(rows)
    q, k, vv, rgate = _ret_pre(x, mod[1], p_tiles, tiles_per_seq, norm_mix[1], cos_t, sin_t, ret_w_in[0].astype(BF16))
    log_gamma = jax.nn.log_sigmoid(ret_decay_logit[0].astype(F32))
    o2, r_fin = _ret_scan(log_gamma, q, k, vv, jnp.moveaxis(state_ret[:, 0], 1, 0),
                          _ScanItems(bp, tp // RET_CHUNK, bs, ts // RET_CHUNK))
    new_state_ret = jnp.moveaxis(r_fin, 0, 1)[:, None]
    x = _ret_post(x, o2, rgate, mod[1], p_tiles, tiles_per_seq, ret_w_out[0].astype(BF16))
    x = _moe_layer(x, mod[1], p_tiles, tiles_per_seq, norm_ffn[1], moe_router[1], moe_bias[1],
                   moe_w_gu, moe_w_down, 1, moe_sh_gu[1].astype(BF16),
                   moe_sh_down[1].astype(BF16), norm_final, True)

    return (x[:n_p].reshape(bp, tp, D), x[n_p:].reshape(bs, ts, D), new_state_rwkv, new_state_ret)
```

```python
import functools

import jax
import jax.numpy as jnp
from jax import lax
from jax.experimental import pallas as pl
from jax.experimental.pallas import tpu as pltpu

F32, BF16, I32 = jnp.float32, jnp.bfloat16, jnp.int32

D = 1024
N_MOD = 6
NORM_EPS = 1e-6
GRID_W = 64
HEAD_A = 64
H_A = D // HEAD_A
LNX_EPS = 64e-5
RWKV_CHUNK = 64
N_PAIR = H_A // 2
H_B = 4
DK_B = D // H_B
DV_B = 2 * DK_B
RET_CHUNK = 128
ROPE_BASE = 10000.0
N_EXPERTS = 64
TOP_K = 8
N_GROUPS = 8
TOPK_GROUPS = 4
PER_GROUP = N_EXPERTS // N_GROUPS
D_EXPERT = 256
ROUTED_SCALE = 2.5
EXPERT_BLOCK = 256

TM = 256
LANES = 128
ROW_SUB = D // LANES
VMEM_LIMIT = 56 * 1024 * 1024


def _cparams(n_grid_axes, **kw):
    return pltpu.CompilerParams(dimension_semantics=("arbitrary",) * n_grid_axes, vmem_limit_bytes=VMEM_LIMIT, **kw)


def _bdot(a, b):
    return jnp.dot(a.astype(BF16), b.astype(BF16), preferred_element_type=F32)


def _bdot_nt(a, b):
    return lax.dot_general(a.astype(BF16), b.astype(BF16), (((1,), (1,)), ((), ())), preferred_element_type=F32)


def _bdot_tn(a, b):
    return lax.dot_general(a.astype(BF16), b.astype(BF16), (((0,), (0,)), ((), ())), preferred_element_type=F32)


def _split3(x):
    hi = x.astype(BF16)
    r1 = x - hi.astype(F32)
    mid = r1.astype(BF16)
    lo = (r1 - mid.astype(F32)).astype(BF16)
    return hi, mid, lo


def _dot_f32(a, b):
    ah, am, al = _split3(a)
    bh, bm, bl = _split3(b)
    d = lambda x, y: jnp.dot(x, y, preferred_element_type=F32)
    return d(ah, bh) + (d(ah, bm) + d(am, bh)) + (d(ah, bl) + d(al, bh) + d(am, bm))


def _dot_f32_rhs01(a, m01):
    ah, am, al = _split3(a)
    d = lambda x: jnp.dot(x, m01, preferred_element_type=F32)
    return d(ah) + d(am) + d(al)


def _dot_f32_lhs01(m01, b):
    bh, bm, bl = _split3(b)
    d = lambda x: jnp.dot(m01, x, preferred_element_type=F32)
    return d(bh) + d(bm) + d(bl)


def _norm_mod(x, g, shift, scale):
    ms = jnp.mean(x * x, axis=-1, keepdims=True)
    y = x * lax.rsqrt(ms + NORM_EPS) * g
    return y * (1.0 + scale) + shift


def _silu(x):
    return x * jax.nn.sigmoid(x)


def _rows_load(ref):
    m = ref.shape[0] // ROW_SUB
    return jnp.concatenate([ref[pl.ds(j, m, stride=ROW_SUB), :] for j in range(ROW_SUB)], axis=1)


def _rows_store(ref, val):
    m = val.shape[0]
    for j in range(ROW_SUB):
        ref[pl.ds(j, m, stride=ROW_SUB), :] = val[:, j * LANES:(j + 1) * LANES]


def _row_tile(ref, idx):
    return ref.at[pl.ds(pl.multiple_of(idx * ROW_SUB, ROW_SUB), ROW_SUB)]


def _mod_row(i, p_tiles, tiles_per_seq):
    return jnp.where(i < p_tiles, 0, 1 + (i - p_tiles) // tiles_per_seq)


def _mod_kernel(cond_ref, w_ref, b_ref, o_ref):
    o_ref[...] = _dot_f32(_silu(cond_ref[...]), w_ref[...]) + b_ref[...]


def _modulation(cond, ada_w, ada_b):
    depth, _, n6 = ada_w.shape
    tn = 1536
    return pl.pallas_call(
        _mod_kernel,
        grid=(depth, n6 // tn),
        in_specs=[
            pl.BlockSpec(cond.shape, lambda l, j: (0, 0)),
            pl.BlockSpec((None, D, tn), lambda l, j: (l, 0, j)),
            pl.BlockSpec((None, 1, tn), lambda l, j: (l, 0, j)),
        ],
        out_specs=pl.BlockSpec((None, cond.shape[0], tn), lambda l, j: (l, 0, j)),
        out_shape=jax.ShapeDtypeStruct((depth, cond.shape[0], n6), F32),
        compiler_params=_cparams(2),
        name="adaln_mod",
    )(cond, ada_w, ada_b.reshape(depth, 1, n6))


def _rwkv_pre_kernel(x_ref, xu_ref, xd_ref, mod_ref, g_ref, mu_ref, wrkv_ref, g1_ref, g2_ref, w1_ref, w2_ref,
                     w0_ref, a1_ref, a2_ref, a0_ref, kk_ref, ka_ref, rk_ref, hsum_ref, hexp_ref,
                     r_o, v_o, kk_o, lw_o, kd_o, b_o, gate_o, bonus_o, *, p_tiles, tiles_per_seq):
    i = pl.program_id(0)
    is_p = i < p_tiles
    sub = (i - p_tiles) % tiles_per_seq
    mod = mod_ref[...]
    shift, scale = mod[:, 0:D], mod[:, D:2 * D]
    g = g_ref[...]
    h = _norm_mod(x_ref[...], g, shift, scale)
    hu = _norm_mod(xu_ref[...], g, shift, scale)
    hd = _norm_mod(xd_ref[...], g, shift, scale)

    q = D // 4
    row = lax.broadcasted_iota(I32, (TM, 1), 0)
    per = jnp.where(is_p, TM, GRID_W)
    pos = row & (per - 1)

    def prev1(a):
        return jnp.where(pos == 0, 0.0, pltpu.roll(a, 1, 0))

    def next1(a):
        return jnp.where(pos == per - 1, 0.0, pltpu.roll(a, TM - 1, 0))

    h0, h1, h2, h3 = (h[:, j * q:(j + 1) * q] for j in range(4))
    up = jnp.concatenate([jnp.where(sub == 0, 0.0, hu[:, 2 * q:3 * q]), h2[0:TM - GRID_W]], axis=0)
    down = jnp.concatenate([h3[GRID_W:TM], jnp.where(sub == tiles_per_seq - 1, 0.0, hd[:, 3 * q:4 * q])], axis=0)
    s0 = prev1(h0)
    s1 = jnp.where(is_p, prev1(h1), next1(h1))
    s2 = jnp.where(is_p, next1(h2), up)
    s3 = jnp.where(is_p, next1(h3), down)
    xx = jnp.concatenate([s0, s1, s2, s3], axis=1) - h

    mu = mu_ref[...]
    mix = lambda j: h + xx * mu[j:j + 1]
    r = _bdot(mix(0), wrkv_ref[0])
    k = _bdot(mix(2), wrkv_ref[1])
    v = _bdot(mix(3), wrkv_ref[2])
    gate = _bdot(jax.nn.sigmoid(_bdot(mix(5), g1_ref[...])), g2_ref[...])
    w_all = w0_ref[...] + _bdot(jnp.tanh(_bdot(mix(1), w1_ref[...])), w2_ref[...])
    a_all = jax.nn.sigmoid(a0_ref[...] + _bdot(_bdot(mix(4), a1_ref[...]), a2_ref[...]))

    hsum, hexp = hsum_ref[...], hexp_ref[...]
    head_sum = lambda t: _dot_f32_rhs01(_dot_f32_rhs01(t, hsum), hexp)

    kkr = k * kk_ref[...]
    kk = kkr / jnp.maximum(jnp.sqrt(head_sum(kkr * kkr)), 1e-12)
    ka = ka_ref[...]
    kd_sum = jnp.zeros_like(k)
    for d in range(2):
        wd = w_all[:, d * D:(d + 1) * D]
        z = -wd
        softplus = jnp.maximum(z, 0.0) + jnp.log(1.0 + jnp.exp(-jnp.abs(z)))
        lw_o[d] = -jnp.exp(-softplus - 0.5)
        a = a_all[:, d * D:(d + 1) * D]
        kd = k * (1.0 + (a - 1.0) * ka)
        kd_o[d] = kd
        b_o[d] = kk * a
        kd_sum = kd_sum + kd
    r_o[...] = r
    v_o[...] = v
    kk_o[...] = kk
    gate_o[...] = gate
    bonus_o[...] = head_sum(r * kd_sum * rk_ref[...]) * v


def _rwkv_pre(x, mod, p_tiles, tiles_per_seq, norm_g, wts):
    n = x.shape[0]
    n_tiles = n // TM
    hb = TM // GRID_W
    n_hblk = n // GRID_W
    row = lambda a: a.reshape(1, -1)
    full = lambda a: pl.BlockSpec(a.shape, lambda i: (0,) * a.ndim)
    tok = pl.BlockSpec((TM, D), lambda i: (i, 0))
    tok2 = pl.BlockSpec((2, TM, D), lambda i: (0, i, 0))
    consts = [row(norm_g), wts["mu"], wts["wrkv"], wts["g1"], wts["g2"], wts["w1"], wts["w2"], wts["w0"],
              wts["a1"], wts["a2"], wts["a0"], row(wts["k_k"]), row(wts["k_a"]), row(wts["r_k"]),
              wts["hsum"], wts["hexp"]]
    return pl.pallas_call(
        functools.partial(_rwkv_pre_kernel, p_tiles=p_tiles, tiles_per_seq=tiles_per_seq),
        grid=(n_tiles,),
        in_specs=[
            tok,
            pl.BlockSpec((GRID_W, D), lambda i: (jnp.maximum(i * hb - 1, 0), 0)),
            pl.BlockSpec((GRID_W, D), lambda i: (jnp.minimum(i * hb + hb, n_hblk - 1), 0)),
            pl.BlockSpec((None, 1, N_MOD * D), lambda i: (_mod_row(i, p_tiles, tiles_per_seq), 0, 0)),
        ] + [full(a) for a in consts],
        out_specs=[tok, tok, tok, tok2, tok2, tok2, tok, tok],
        out_shape=[jax.ShapeDtypeStruct((n, D), F32)] * 3 + [jax.ShapeDtypeStruct((2, n, D), F32)] * 3
        + [jax.ShapeDtypeStruct((n, D), F32)] * 2,
        compiler_params=_cparams(1),
        name="rwkv_pre",
    )(x, x, x, mod, *consts)


class _ScanItems:
    def __init__(self, n_ctx, ctx_chunks, n_lat, lat_chunks):
        self.n_ctx, self.ctx_chunks, self.n_lat, self.lat_chunks = n_ctx, ctx_chunks, n_lat, lat_chunks
        self.ctx_items = n_ctx * ctx_chunks
        self.n_items = self.ctx_items + n_lat * lat_chunks

    def decode(self, j):
        is_ctx = j < self.ctx_items
        jl = jnp.maximum(j - self.ctx_items, 0)
        c = jnp.where(is_ctx, j % self.ctx_chunks, jl % self.lat_chunks)
        return is_ctx, c, jnp.where(is_ctx, self.ctx_chunks, self.lat_chunks)

    def block(self, d, j):
        _, c, nc = self.decode(j)
        return jnp.where(d == 0, j, j - c + (nc - 1 - c))

    def ctx_seq(self, j):
        return jnp.minimum(j // self.ctx_chunks, self.n_ctx - 1)

    def lat_seq(self, j):
        return jnp.maximum(j - self.ctx_items, 0) // self.lat_chunks


def _rwkv_scan_kernel(r_ref, v_ref, kk_ref, lw_ref, kd_ref, b_ref, s0_ref, y_ref, sf_ref, s_scr, *, items):
    d = pl.program_id(0)
    is_ctx, c, nc = items.decode(pl.program_id(1))
    rev = d == 1
    ch = RWKV_CHUNK

    @pl.when(c == 0)
    def _():
        s_scr[...] = jnp.where(is_ctx, 0.0, s0_ref[...])

    ti = lax.broadcasted_iota(I32, (ch, ch), 0)
    tj = lax.broadcasted_iota(I32, (ch, ch), 1)
    tri = jnp.where(jnp.where(rev, ti - tj, tj - ti) <= 0, 1.0, 0.0).astype(BF16)
    lw = lw_ref[...]
    cum = _dot_f32_lhs01(tri, lw)
    last = jnp.where(rev, cum[0:1], cum[ch - 1:ch])
    kk = kk_ref[...]
    kd = kd_ref[...]
    b = b_ref[...]
    e_neg = jnp.exp(-cum)
    e_rel = jnp.exp(last - cum)
    at = -kk * jnp.exp(cum - lw)
    rt = r_ref[...] * jnp.exp(cum)
    bt = b * e_neg
    kt = kd * e_neg
    bh = b * e_rel
    kh = kd * e_rel
    g_all = jnp.exp(last)
    v = v_ref[...]

    si = lax.broadcasted_iota(I32, (2 * ch, 2 * ch), 0)
    sj = lax.broadcasted_iota(I32, (2 * ch, 2 * ch), 1)
    same = (si < ch) == (sj < ch)
    ui, uj = si & (ch - 1), sj & (ch - 1)
    order = jnp.where(rev, ui - uj, uj - ui)
    strict = same & (order < 0)
    incl = same & (order <= 0)
    first = lax.broadcasted_iota(I32, (1, LANES), 1) < HEAD_A

    def stack(xp):
        return jnp.concatenate([jnp.where(first, xp, 0.0), jnp.where(first, 0.0, xp)], axis=0)

    pairs = range(N_PAIR)
    sls = [slice(p * LANES, (p + 1) * LANES) for p in pairs]
    s_old = [s_scr[p] for p in pairs]
    m1 = [_bdot_nt(jnp.concatenate([stack(at[:, sl]), stack(rt[:, sl])], axis=0),
                   jnp.concatenate([stack(bt[:, sl]), stack(kt[:, sl])], axis=0)) for sl in sls]
    m2 = [_bdot_nt(jnp.concatenate([at[:, sl], rt[:, sl]], axis=0), s_old[p])
          for p, sl in enumerate(sls)]
    vs = [stack(v[:, sl]) for sl in sls]
    x = [stack(m2[p][0:ch]) + _bdot(jnp.where(strict, m1[p][0:2 * ch, 2 * ch:4 * ch], 0.0), vs[p]) for p in pairs]
    nm = [jnp.where(strict, m1[p][0:2 * ch, 0:2 * ch], 0.0).astype(BF16) for p in pairs]
    for _ in range(5):
        sq_ap = [_bdot(nm[p], jnp.concatenate([nm[p], x[p].astype(BF16)], axis=1)) for p in pairs]
        nm = [sq_ap[p][:, 0:2 * ch].astype(BF16) for p in pairs]
        x = [x[p] + sq_ap[p][:, 2 * ch:4 * ch] for p in pairs]
    x = [x[p] + _bdot(nm[p], x[p]) for p in pairs]
    uv = [jnp.concatenate([x[p], vs[p]], axis=0).astype(BF16) for p in pairs]
    ys = [_bdot(jnp.concatenate([jnp.where(incl, m1[p][2 * ch:4 * ch, 0:2 * ch], 0.0),
                                 jnp.where(incl, m1[p][2 * ch:4 * ch, 2 * ch:4 * ch], 0.0)], axis=1), uv[p])
          for p in pairs]
    s_new = [s_old[p] * g_all[:, sl]
             + _bdot_tn(uv[p], jnp.concatenate([stack(bh[:, sl]), stack(kh[:, sl])], axis=0))
             for p, sl in enumerate(sls)]
    for p, sl in enumerate(sls):
        y_ref[:, sl] = ys[p][0:ch] + ys[p][ch:2 * ch] + m2[p][ch:2 * ch]
    for p in pairs:
        s_scr[p] = s_new[p]

    @pl.when(is_ctx & (c == nc - 1))
    def _():
        sf_ref[...] = s_scr[...]


def _rwkv_scan(r, v, kk, lw, kd, b, s0, items):
    n = r.shape[0]
    ch = RWKV_CHUNK
    tok = pl.BlockSpec((ch, D), lambda d, j: (items.block(d, j), 0))
    tokd = pl.BlockSpec((None, ch, D), lambda d, j: (d, items.block(d, j), 0))
    st = lambda seq: pl.BlockSpec((None, None, N_PAIR, LANES, LANES), lambda d, j: (d, seq(j), 0, 0, 0))
    return pl.pallas_call(
        functools.partial(_rwkv_scan_kernel, items=items),
        grid=(2, items.n_items),
        in_specs=[tok, tok, tok, tokd, tokd, tokd, st(items.lat_seq)],
        out_specs=[tokd, st(items.ctx_seq)],
        out_shape=[jax.ShapeDtypeStruct((2, n, D), F32),
                   jax.ShapeDtypeStruct((2, items.n_ctx, N_PAIR, LANES, LANES), F32)],
        scratch_shapes=[pltpu.VMEM((N_PAIR, LANES, LANES), F32)],
        compiler_params=_cparams(2),
        name="rwkv_scan",
    )(r, v, kk, lw, kd, b, s0)


def _rwkv_post_kernel(x_ref, y_ref, gate_ref, bonus_ref, mod_ref, lnw_ref, lnb_ref, wo_ref, hsum_ref, hexp_ref, o_ref):
    hsum, hexp = hsum_ref[...], hexp_ref[...]
    head_mean = lambda t: _dot_f32_rhs01(_dot_f32_rhs01(t, hsum), hexp) * (1.0 / HEAD_A)
    y = y_ref[0] + y_ref[1]
    yc = y - head_mean(y)
    yn = yc * lax.rsqrt(head_mean(yc * yc) + LNX_EPS)
    z = (yn * lnw_ref[...] + lnb_ref[...] + bonus_ref[...]) * gate_ref[...]
    o_ref[...] = x_ref[...] + mod_ref[:, 2 * D:3 * D] * _bdot(z, wo_ref[...])


def _rwkv_post(x, y, gate, bonus, mod, p_tiles, tiles_per_seq, ln_w, ln_b, wo, hsum, hexp):
    n = x.shape[0]
    full = lambda a: pl.BlockSpec(a.shape, lambda i: (0,) * a.ndim)
    tok = pl.BlockSpec((TM, D), lambda i: (i, 0))
    consts = [ln_w.reshape(1, D), ln_b.reshape(1, D), wo, hsum, hexp]
    return pl.pallas_call(
        _rwkv_post_kernel,
        grid=(n // TM,),
        in_specs=[tok, pl.BlockSpec((2, TM, D), lambda i: (0, i, 0)), tok, tok,
                  pl.BlockSpec((None, 1, N_MOD * D), lambda i: (_mod_row(i, p_tiles, tiles_per_seq), 0, 0))]
        + [full(a) for a in consts],
        out_specs=tok,
        out_shape=jax.ShapeDtypeStruct((n, D), F32),
        compiler_params=_cparams(1),
        name="rwkv_post",
    )(x, y, gate, bonus, mod, *consts)


def _ret_pre_kernel(x_ref, mod_ref, g_ref, cos_ref, sin_ref, win_ref, q_o, k_o, v_o, gate_o):
    mod = mod_ref[...]
    h = _norm_mod(x_ref[...], g_ref[...], mod[:, 0:D], mod[:, D:2 * D])
    proj = _bdot(h, win_ref[...])
    cos, sin = cos_ref[...], sin_ref[...]

    def rope(t):
        outs = []
        for j in range(D // LANES):
            tj = t[:, j * LANES:(j + 1) * LANES]
            cj = cos[:, (j % 2) * LANES:(j % 2 + 1) * LANES]
            sj = sin[:, (j % 2) * LANES:(j % 2 + 1) * LANES]
            outs.append(tj * cj + pltpu.roll(tj, LANES // 2, 1) * sj)
        return jnp.concatenate(outs, axis=1)

    q_o[...] = rope(proj[:, 0:D]).astype(BF16)
    k_o[...] = rope(proj[:, D:2 * D] * (DK_B ** -0.5)).astype(BF16)
    v_o[...] = proj[:, 2 * D:4 * D].astype(BF16)
    gate_o[...] = proj[:, 4 * D:6 * D]


def _ret_pre(x, mod, p_tiles, tiles_per_seq, norm_g, cos_t, sin_t, w_in):
    n = x.shape[0]
    full = lambda a: pl.BlockSpec(a.shape, lambda i: (0,) * a.ndim)
    tok = lambda w: pl.BlockSpec((TM, w), lambda i: (i, 0))
    tab = pl.BlockSpec((TM, DK_B), lambda i: (jnp.where(i < p_tiles, 0, 1 + (i - p_tiles) % tiles_per_seq), 0))
    return pl.pallas_call(
        _ret_pre_kernel,
        grid=(n // TM,),
        in_specs=[tok(D), pl.BlockSpec((None, 1, N_MOD * D), lambda i: (_mod_row(i, p_tiles, tiles_per_seq), 0, 0)),
                  full(norm_g.reshape(1, D)), tab, tab, full(w_in)],
        out_specs=[tok(D), tok(D), tok(2 * D), tok(2 * D)],
        out_shape=[jax.ShapeDtypeStruct((n, D), BF16), jax.ShapeDtypeStruct((n, D), BF16),
                   jax.ShapeDtypeStruct((n, 2 * D), BF16), jax.ShapeDtypeStruct((n, 2 * D), F32)],
        compiler_params=_cparams(1),
        name="ret_pre",
    )(x, mod, norm_g.reshape(1, D), cos_t, sin_t, w_in)


def _ret_scan_kernel(lg_ref, q_ref, k_ref, v_ref, s0_ref, o_ref, sf_ref, s_scr, *, items):
    d = pl.program_id(0)
    is_ctx, c, nc = items.decode(pl.program_id(1))
    rev = d == 1
    ch = RET_CHUNK

    @pl.when(c == 0)
    def _():
        s_scr[...] = jnp.where(is_ctx, 0.0, s0_ref[...])

    ti = lax.broadcasted_iota(I32, (ch, ch), 0)
    tj = lax.broadcasted_iota(I32, (ch, ch), 1)
    rel = jnp.where(rev, tj - ti, ti - tj).astype(F32)
    steps_q = jnp.where(rev, ch - ti, ti + 1).astype(F32)
    steps_k = jnp.where(rev, ti, ch - 1 - ti).astype(F32)
    for hd in range(H_B):
        lg = lg_ref[d, hd]
        mask = jnp.where(rel >= 0, jnp.exp(jnp.maximum(rel, 0.0) * lg), 0.0)
        q_dec = jnp.exp(steps_q * lg)
        k_dec = jnp.exp(steps_k * lg)
        chunk_dec = jnp.exp(jnp.full((1, DV_B), float(ch), F32) * lg)
        qh = q_ref[:, hd * DK_B:(hd + 1) * DK_B]
        kh = k_ref[:, hd * DK_B:(hd + 1) * DK_B]
        vh = v_ref[:, hd * DV_B:(hd + 1) * DV_B]
        s = s_scr[hd]
        scores = _bdot_nt(qh, kh) * mask
        inner = _bdot(scores, vh)
        cross = _bdot(qh, s) * jnp.concatenate([q_dec] * (DV_B // ch), axis=1)
        o_ref[:, hd * DV_B:(hd + 1) * DV_B] = inner + cross
        kdec = kh.astype(F32) * jnp.concatenate([k_dec] * (DK_B // ch), axis=1)
        s_scr[hd] = s * chunk_dec + _bdot_tn(kdec, vh)

    @pl.when(is_ctx & (c == nc - 1))
    def _():
        sf_ref[...] = s_scr[...]


def _ret_scan(log_gamma, q, k, v, s0, items):
    n = q.shape[0]
    ch = RET_CHUNK
    tok = lambda w: pl.BlockSpec((ch, w), lambda d, j: (items.block(d, j), 0))
    st = lambda seq: pl.BlockSpec((None, None, H_B, DK_B, DV_B), lambda d, j: (d, seq(j), 0, 0, 0))
    return pl.pallas_call(
        functools.partial(_ret_scan_kernel, items=items),
        grid=(2, items.n_items),
        in_specs=[pl.BlockSpec(memory_space=pltpu.SMEM), tok(D), tok(D), tok(2 * D), st(items.lat_seq)],
        out_specs=[pl.BlockSpec((None, ch, 2 * D), lambda d, j: (d, items.block(d, j), 0)), st(items.ctx_seq)],
        out_shape=[jax.ShapeDtypeStruct((2, n, 2 * D), F32),
                   jax.ShapeDtypeStruct((2, items.n_ctx, H_B, DK_B, DV_B), F32)],
        scratch_shapes=[pltpu.VMEM((H_B, DK_B, DV_B), F32)],
        compiler_params=_cparams(2),
        name="ret_scan",
    )(log_gamma, q, k, v, s0)


def _ret_post_kernel(x_ref, o_ref, gate_ref, mod_ref, wout_ref, out_ref):
    o = o_ref[0] + o_ref[1]
    parts = []
    for hd in range(H_B):
        oh = o[:, hd * DV_B:(hd + 1) * DV_B]
        parts.append(oh * lax.rsqrt(jnp.mean(oh * oh, axis=-1, keepdims=True) + NORM_EPS))
    y = _silu(gate_ref[...]) * jnp.concatenate(parts, axis=1)
    out_ref[...] = x_ref[...] + mod_ref[:, 2 * D:3 * D] * _bdot(y, wout_ref[...])


def _ret_post(x, o, gate, mod, p_tiles, tiles_per_seq, w_out):
    n = x.shape[0]
    tok = lambda w: pl.BlockSpec((TM, w), lambda i: (i, 0))
    return pl.pallas_call(
        _ret_post_kernel,
        grid=(n // TM,),
        in_specs=[tok(D), pl.BlockSpec((2, TM, 2 * D), lambda i: (0, i, 0)), tok(2 * D),
                  pl.BlockSpec((None, 1, N_MOD * D), lambda i: (_mod_row(i, p_tiles, tiles_per_seq), 0, 0)),
                  pl.BlockSpec(w_out.shape, lambda i: (0, 0))],
        out_specs=tok(D),
        out_shape=jax.ShapeDtypeStruct((n, D), F32),
        compiler_params=_cparams(1),
        name="ret_post",
    )(x, o, gate, mod, w_out)


def _moe_route_kernel(x_ref, mod_ref, g_ref, router_ref, bias_ref, h_o, e_o, w_o, p_o, cnt_o, carry):
    i = pl.program_id(0)

    @pl.when(i == 0)
    def _():
        carry[...] = jnp.zeros_like(carry)

    mod = mod_ref[...]
    h = _norm_mod(x_ref[...], g_ref[...], mod[:, 3 * D:4 * D], mod[:, 4 * D:5 * D])
    _rows_store(h_o, h)
    lane =lax.broadcasted_iota(I32, (TM, LANES), 1)
    valid = lane < N_EXPERTS
    neg = -jnp.inf
    scores = jax.nn.sigmoid(_dot_f32(h, router_ref[...]))
    biased = jnp.where(valid, scores + bias_ref[...], neg)

    def group_reduce(t, op):
        s = 1
        while s < PER_GROUP:
            partner = jnp.where((lane & s) == 0, pltpu.roll(t, LANES - s, 1), pltpu.roll(t, s, 1))
            t = op(t, partner)
            s *= 2
        return t

    lane_f = lane.astype(F32)
    group_f = jnp.floor(lane_f * (1.0 / PER_GROUP))

    def first_lane_of_max(t):
        m = jnp.max(t, axis=-1, keepdims=True)
        return jnp.min(jnp.where(t == m, lane_f, float(LANES)), axis=-1, keepdims=True)

    m1 = group_reduce(biased, jnp.maximum)
    first1 = group_reduce(jnp.where(biased == m1, lane_f, float(LANES)), jnp.minimum)
    m2 = group_reduce(jnp.where(lane_f == first1, neg, biased), jnp.maximum)
    gscore = jnp.where(valid, m1 + m2, neg)
    cand = jnp.full((TM, LANES), neg, F32)
    for _ in range(TOPK_GROUPS):
        gsel = group_f == jnp.floor(first_lane_of_max(gscore) * (1.0 / PER_GROUP))
        cand = jnp.where(gsel, biased, cand)
        gscore = jnp.where(gsel, neg, gscore)
    hits = []
    sel01 = jnp.zeros((TM, LANES), F32)
    e_cols = jnp.zeros((TM, LANES), F32)
    for j in range(TOP_K):
        fl = first_lane_of_max(cand)
        hit = lane_f == fl
        hits.append(hit)
        sel01 = jnp.where(hit, 1.0, sel01)
        cand = jnp.where(hit, neg, cand)
        e_cols = jnp.where(lane == j, fl, e_cols)
    wsum = jnp.sum(sel01 * scores, axis=-1, keepdims=True)

    ri = lax.broadcasted_iota(I32, (TM, TM), 0)
    rj = lax.broadcasted_iota(I32, (TM, TM), 1)
    below = jnp.where(rj < ri, 1.0, 0.0).astype(BF16)
    rank = jnp.dot(below, sel01.astype(BF16), preferred_element_type=F32) + carry[...]
    carry[...] = carry[...] + jnp.sum(sel01, axis=0, keepdims=True)
    w_cols = jnp.zeros((TM, LANES), F32)
    p_cols = jnp.zeros((TM, LANES), F32)
    for j in range(TOP_K):
        wj = jnp.sum(jnp.where(hits[j], scores, 0.0), axis=-1, keepdims=True)
        w_cols = jnp.where(lane == j, wj / wsum * ROUTED_SCALE, w_cols)
        p_cols = jnp.where(lane == j, jnp.sum(jnp.where(hits[j], rank, 0.0), axis=-1, keepdims=True), p_cols)
    e_o[...] = e_cols.astype(I32)
    w_o[...] = w_cols
    p_o[...] = p_cols.astype(I32)
    cnt_o[...] = jnp.broadcast_to(carry[...], cnt_o.shape)


def _moe_route(x, mod, p_tiles, tiles_per_seq, norm_g, router, bias):
    n = x.shape[0]
    full = lambda a: pl.BlockSpec(a.shape, lambda i: (0,) * a.ndim)
    tok = lambda w: pl.BlockSpec((TM, w), lambda i: (i, 0))
    router_p = jnp.pad(router, ((0, 0), (0, LANES - N_EXPERTS)))
    bias_p = jnp.pad(bias, (0, LANES - N_EXPERTS)).reshape(1, LANES)
    return pl.pallas_call(
        _moe_route_kernel,
        grid=(n // TM,),
        in_specs=[tok(D), pl.BlockSpec((None, 1, N_MOD * D), lambda i: (_mod_row(i, p_tiles, tiles_per_seq), 0, 0)),
                  full(norm_g.reshape(1, D)), full(router_p), full(bias_p)],
        out_specs=[pl.BlockSpec((TM * ROW_SUB, LANES), lambda i: (i, 0)), tok(LANES), tok(LANES), tok(LANES),
                   pl.BlockSpec((8, LANES), lambda i: (0, 0))],
        out_shape=[jax.ShapeDtypeStruct((n * ROW_SUB, LANES), F32), jax.ShapeDtypeStruct((n, LANES), I32),
                   jax.ShapeDtypeStruct((n, LANES), F32), jax.ShapeDtypeStruct((n, LANES), I32),
                   jax.ShapeDtypeStruct((8, LANES), F32)],
        scratch_shapes=[pltpu.VMEM((1, LANES), F32)],
        compiler_params=_cparams(1),
        name="moe_route",
    )(x, mod, norm_g.reshape(1, D), router_p, bias_p)


DISPATCH_TOK = 512


def _moe_dispatch_kernel(pad_lo_ref, pad_n_ref, dest_ref, h_ref, xs_hbm, zero_scr, sem):
    @pl.when(pl.program_id(0) == 0)
    def _():
        zero_scr[...] = jnp.zeros_like(zero_scr)

        def pad_copy(e, s):
            return pltpu.make_async_copy(zero_scr, _row_tile(xs_hbm, pad_lo_ref[e] + s), sem)

        def per_expert(e, carry):
            def issue(s, c):
                pad_copy(e, s).start()
                return c

            def drain(s, c):
                pad_copy(e, s).wait()
                return c

            lax.fori_loop(0, pad_n_ref[e], issue, 0)
            lax.fori_loop(0, pad_n_ref[e], drain, 0)
            return carry

        lax.fori_loop(0, N_EXPERTS, per_expert, 0)

    def copy(t, kq):
        return pltpu.make_async_copy(_row_tile(h_ref, t), _row_tile(xs_hbm, dest_ref[t * TOP_K + kq]), sem)

    def issue(t, carry):
        for kq in range(TOP_K):
            copy(t, kq).start(priority=kq % 2)
        return carry

    def drain(t, carry):
        for kq in range(TOP_K):
            copy(t, kq).wait()
        return carry

    lax.fori_loop(0, DISPATCH_TOK, issue, 0)
    lax.fori_loop(0, DISPATCH_TOK, drain, 0)


def _moe_dispatch(pad_lo, pad_n, dest_flat, h, n_slots):
    n = h.shape[0] // ROW_SUB
    grid_spec = pltpu.PrefetchScalarGridSpec(
        num_scalar_prefetch=2,
        grid=(n // DISPATCH_TOK,),
        in_specs=[pl.BlockSpec((DISPATCH_TOK * TOP_K,), lambda i, lo, cnt: (i,), memory_space=pltpu.SMEM),
                  pl.BlockSpec((DISPATCH_TOK * ROW_SUB, LANES), lambda i, lo, cnt: (i, 0))],
        out_specs=pl.BlockSpec(memory_space=pl.ANY),
        scratch_shapes=[pltpu.VMEM((ROW_SUB, LANES), F32), pltpu.SemaphoreType.DMA],
    )
    return pl.pallas_call(
        _moe_dispatch_kernel,
        grid_spec=grid_spec,
        out_shape=jax.ShapeDtypeStruct((n_slots * ROW_SUB, LANES), F32),
        compiler_params=_cparams(1, has_side_effects=True),
        name="moe_dispatch",
    )(pad_lo, pad_n, dest_flat, h)


def _moe_expert_kernel(be_ref, xs_ref, wgu_ref, wdn_ref, o_ref, wgu_bf, wdn_bf):
    i = pl.program_id(0)

    @pl.when((i == 0) | (be_ref[i] != be_ref[jnp.maximum(i - 1, 0)]))
    def _():
        wgu_bf[...] = wgu_ref[...].astype(BF16)
        wdn_bf[...] = wdn_ref[...].astype(BF16)

    gu = _bdot(_rows_load(xs_ref), wgu_bf[...])
    act = _silu(gu[:, 0:D_EXPERT]) * gu[:, D_EXPERT:2 * D_EXPERT]
    _rows_store(o_ref, _bdot(act, wdn_bf[...]))


def _moe_expert(block_e, xs, w_gu, w_down, layer):
    n_slots = xs.shape[0] // ROW_SUB
    n_blocks = n_slots // EXPERT_BLOCK
    rows = pl.BlockSpec((EXPERT_BLOCK * ROW_SUB, LANES), lambda i, be: (i, 0))
    grid_spec = pltpu.PrefetchScalarGridSpec(
        num_scalar_prefetch=1,
        grid=(n_blocks,),
        in_specs=[rows,
                  pl.BlockSpec((None, None, D, 2 * D_EXPERT), lambda i, be: (layer, be[i], 0, 0)),
                  pl.BlockSpec((None, None, D_EXPERT, D), lambda i, be: (layer, be[i], 0, 0))],
        out_specs=rows,
        scratch_shapes=[pltpu.VMEM((D, 2 * D_EXPERT), BF16), pltpu.VMEM((D_EXPERT, D), BF16)],
    )
    return pl.pallas_call(
        _moe_expert_kernel,
        grid_spec=grid_spec,
        out_shape=jax.ShapeDtypeStruct((n_slots * ROW_SUB, LANES), F32),
        compiler_params=_cparams(1),
        name="moe_expert",
    )(block_e, xs, w_gu, w_down)


COMBINE_TOK = 128


def _moe_combine_kernel(dest_ref, dest_next_ref, x_ref, h_ref, w_ref, mod_ref, shgu_ref, shdn_ref, gfin_ref, ys_hbm,
                        o_ref, buf, sems, *, final_norm):
    i = pl.program_id(0)
    slot = i % 2

    def copy(idx_ref, s, t, kq):
        return pltpu.make_async_copy(_row_tile(ys_hbm, idx_ref[t * TOP_K + kq]), _row_tile(buf.at[s, kq], t),
                                     sems.at[s])

    def fetch(idx_ref, s):
        def body(t, carry):
            for kq in range(TOP_K):
                copy(idx_ref, s, t, kq).start(priority=kq % 2)
            return carry
        lax.fori_loop(0, COMBINE_TOK, body, 0, unroll=2)

    @pl.when(i == 0)
    def _():
        fetch(dest_ref, slot)

    @pl.when(i + 1 < pl.num_programs(0))
    def _():
        fetch(dest_next_ref, 1 - slot)

    gu = _bdot(_rows_load(h_ref), shgu_ref[...])
    acc = _bdot(_silu(gu[:, 0:D_EXPERT]) * gu[:, D_EXPERT:2 * D_EXPERT], shdn_ref[...])

    def drain(t, carry):
        for kq in range(TOP_K):
            copy(dest_ref, slot, t, kq).wait()
        return carry

    lax.fori_loop(0, COMBINE_TOK, drain, 0)
    w = w_ref[...]
    for kq in range(TOP_K):
        acc = acc + _rows_load(buf.at[slot, kq]) * w[:, kq:kq + 1]
    out = x_ref[...] + mod_ref[:, 5 * D:6 * D] * acc
    if final_norm:
        out = out * lax.rsqrt(jnp.mean(out * out, axis=-1, keepdims=True) + NORM_EPS) * gfin_ref[...]
    o_ref[...] = out


def _moe_combine(dest_flat, x, h, w_cols, mod, p_tiles, tiles_per_seq, sh_gu, sh_down, g_final, ys, final_norm):
    n = x.shape[0]
    ratio = TM // COMBINE_TOK
    tok = lambda w: pl.BlockSpec((COMBINE_TOK, w), lambda i: (i, 0))
    full = lambda a: pl.BlockSpec(a.shape, lambda i: (0,) * a.ndim)
    n_tiles = n // COMBINE_TOK
    return pl.pallas_call(
        functools.partial(_moe_combine_kernel, final_norm=final_norm),
        grid=(n_tiles,),
        in_specs=[pl.BlockSpec((COMBINE_TOK * TOP_K,), lambda i: (i,), memory_space=pltpu.SMEM),
                  pl.BlockSpec((COMBINE_TOK * TOP_K,), lambda i: (jnp.minimum(i + 1, n_tiles - 1),),
                               memory_space=pltpu.SMEM),
                  tok(D), pl.BlockSpec((COMBINE_TOK * ROW_SUB, LANES), lambda i: (i, 0)), tok(LANES),
                  pl.BlockSpec((None, 1, N_MOD * D), lambda i: (_mod_row(i // ratio, p_tiles, tiles_per_seq), 0, 0)),
                  full(sh_gu), full(sh_down), full(g_final),
                  pl.BlockSpec(memory_space=pl.ANY)],
        out_specs=tok(D),
        out_shape=jax.ShapeDtypeStruct((n, D), F32),
        scratch_shapes=[pltpu.VMEM((2, TOP_K, COMBINE_TOK * ROW_SUB, LANES), F32), pltpu.SemaphoreType.DMA((2,))],
        compiler_params=_cparams(1),
        name="moe_combine",
    )(dest_flat, dest_flat, x, h, w_cols, mod, sh_gu, sh_down, g_final, ys)


def _moe_layer(x, mod, p_tiles, tiles_per_seq, norm_g, router, bias, w_gu, w_down, layer, sh_gu, sh_down, g_final,
               final_norm):
    n = x.shape[0]
    h, e_cols, w_cols, p_cols, counts = _moe_route(x, mod, p_tiles, tiles_per_seq, norm_g, router, bias)
    counts = counts[0, :N_EXPERTS].astype(I32)
    padded = (counts + EXPERT_BLOCK - 1) // EXPERT_BLOCK * EXPERT_BLOCK
    pad_end = jnp.cumsum(padded)
    pad_start = pad_end - padded
    dest = (pad_start[e_cols[:, :TOP_K]] + p_cols[:, :TOP_K]).reshape(-1)
    n_blocks = n * TOP_K // EXPERT_BLOCK + N_EXPERTS
    n_slots = n_blocks * EXPERT_BLOCK
    block_start = jnp.arange(n_blocks, dtype=I32) * EXPERT_BLOCK
    block_e = jnp.minimum(jnp.sum((pad_end[None, :] <= block_start[:, None]).astype(I32), axis=1), N_EXPERTS - 1)
    pad_n = (padded - counts).at[N_EXPERTS - 1].add(n_slots - pad_end[N_EXPERTS - 1])
    xs = _moe_dispatch(pad_start + counts, pad_n, dest, h, n_slots)
    ys = _moe_expert(block_e, xs, w_gu, w_down, layer)
    return _moe_combine(dest, x, h, w_cols, mod, p_tiles, tiles_per_seq, sh_gu, sh_down, g_final.reshape(1, D), ys,
                        final_norm)


def _pair_pack(s):
    lead = s.shape[:-3]
    s = s.reshape(lead + (N_PAIR, 2, HEAD_A, HEAD_A))
    z = jnp.zeros_like(s[..., 0, :, :])
    top = jnp.concatenate([s[..., 0, :, :], z], axis=-1)
    bot = jnp.concatenate([z, s[..., 1, :, :]], axis=-1)
    return jnp.concatenate([top, bot], axis=-2)


def _pair_unpack(s):
    a = s[..., :HEAD_A, :HEAD_A]
    b = s[..., HEAD_A:, HEAD_A:]
    return jnp.stack([a, b], axis=-3).reshape(s.shape[:-3] + (H_A, HEAD_A, HEAD_A))


def _rope_tables(rows):
    half = DK_B // 2
    n_freq = half // 2
    inv = ROPE_BASE ** (-jnp.arange(n_freq, dtype=F32) / n_freq)
    pos_r = jnp.repeat(jnp.arange(rows, dtype=F32), GRID_W)
    pos_c = jnp.tile(jnp.arange(GRID_W, dtype=F32), rows)

    def tab(pos):
        ang = pos[:, None] * inv[None, :]
        c, s = jnp.cos(ang), jnp.sin(ang)
        return jnp.concatenate([c, c], -1), jnp.concatenate([-s, s], -1)

    cr, sr = tab(pos_r)
    cc, sc = tab(pos_c)
    cos = jnp.concatenate([cr, cc], -1)
    sin = jnp.concatenate([sr, sc], -1)
    cos = jnp.concatenate([jnp.ones((TM, DK_B), F32), cos], 0)
    sin = jnp.concatenate([jnp.zeros((TM, DK_B), F32), sin], 0)
    return cos, sin


def kernel(x_prompt, x_sample, state_rwkv, state_ret, c, c_ctx, ada_w, ada_b, norm_mix, norm_ffn, norm_final, rwkv_mu, rwkv_w0, rwkv_w1, rwkv_w2, rwkv_a0, rwkv_a1, rwkv_a2, rwkv_wrkv, rwkv_wo, rwkv_g1, rwkv_g2, rwkv_k_k, rwkv_k_a, rwkv_r_k, rwkv_ln_w, rwkv_ln_b, ret_w_in, ret_w_out, ret_decay_logit, moe_router, moe_bias, moe_w_gu, moe_w_down, moe_sh_gu, moe_sh_down):
    bp, tp, _ = x_prompt.shape
    bs, ts, _ = x_sample.shape
    n_p, n_s = bp * tp, bs * ts
    assert tp == TM and ts % TM == 0 and TM % GRID_W == 0
    p_tiles = n_p // TM
    tiles_per_seq = ts // TM
    rows = ts // GRID_W
    x = jnp.concatenate([x_prompt.reshape(n_p, D), x_sample.reshape(n_s, D)], axis=0)
    n = n_p + n_s

    n_cond = 16
    cond = jnp.zeros((n_cond, D), F32).at[0].set(c_ctx).at[1:1 + bs].set(c)
    mod = _modulation(cond, ada_w, ada_b).reshape(ada_w.shape[0], n_cond, 1, N_MOD * D)

    head_of = jnp.arange(D, dtype=I32) // HEAD_A
    hsum = (head_of[:, None] == jnp.arange(LANES, dtype=I32)[None, :]).astype(BF16)
    hexp = hsum.T

    gl = rwkv_g1.shape[-1]
    glp = -(-gl // LANES) * LANES
    zb = lambda a: jnp.zeros_like(a)
    wts = {
        "mu": jnp.pad(rwkv_mu[0], ((0, 8 - N_MOD), (0, 0))),
        "wrkv": rwkv_wrkv[0].astype(BF16),
        "g1": jnp.pad(rwkv_g1[0], ((0, 0), (0, glp - gl))).astype(BF16),
        "g2": jnp.pad(rwkv_g2[0], ((0, glp - gl), (0, 0))).astype(BF16),
        "w1": jnp.concatenate([rwkv_w1[0, 0], rwkv_w1[0, 1]], axis=1).astype(BF16),
        "w2": jnp.concatenate([jnp.concatenate([rwkv_w2[0, 0], zb(rwkv_w2[0, 1])], 1),
                               jnp.concatenate([zb(rwkv_w2[0, 0]), rwkv_w2[0, 1]], 1)], 0).astype(BF16),
        "w0": rwkv_w0[0].reshape(1, 2 * D),
        "a1": jnp.concatenate([rwkv_a1[0, 0], rwkv_a1[0, 1]], axis=1).astype(BF16),
        "a2": jnp.concatenate([jnp.concatenate([rwkv_a2[0, 0], zb(rwkv_a2[0, 1])], 1),
                               jnp.concatenate([zb(rwkv_a2[0, 0]), rwkv_a2[0, 1]], 1)], 0).astype(BF16),
        "a0": rwkv_a0[0].reshape(1, 2 * D),
        "k_k": rwkv_k_k[0], "k_a": rwkv_k_a[0], "r_k": rwkv_r_k[0].reshape(D),
        "hsum": hsum, "hexp": hexp,
    }
    r, v, kk, lw, kd, b, gate, bonus = _rwkv_pre(x, mod[0], p_tiles, tiles_per_seq, norm_mix[0], wts)
    s0_lat = _pair_pack(jnp.moveaxis(state_rwkv[:, 0], 1, 0))
    y2, s_fin = _rwkv_scan(r, v, kk, lw, kd, b, s0_lat,
                           _ScanItems(bp, tp // RWKV_CHUNK, bs, ts // RWKV_CHUNK))
    new_state_rwkv = jnp.moveaxis(_pair_unpack(s_fin), 0, 1)[:, None]
    x = _rwkv_post(x, y2, gate, bonus, mod[0], p_tiles, tiles_per_seq, rwkv_ln_w[0], rwkv_ln_b[0],
                   rwkv_wo[0].astype(BF16), hsum, hexp)
    x = _moe_layer(x, mod[0], p_tiles, tiles_per_seq, norm_ffn[0], moe_router[0], moe_bias[0],
                   moe_w_gu, moe_w_down, 0, moe_sh_gu[0].astype(BF16),
                   moe_sh_down[0].astype(BF16), norm_final, False)

    cos_t, sin_t = _rope_tables(rows)
    q, k, vv, rgate = _ret_pre(x, mod[1], p_tiles, tiles_per_seq, norm_mix[1], cos_t, sin_t, ret_w_in[0].astype(BF16))
    log_gamma = jax.nn.log_sigmoid(ret_decay_logit[0].astype(F32))
    o2, r_fin = _ret_scan(log_gamma, q, k, vv, jnp.moveaxis(state_ret[:, 0], 1, 0),
                          _ScanItems(bp, tp // RET_CHUNK, bs, ts // RET_CHUNK))
    new_state_ret = jnp.moveaxis(r_fin, 0, 1)[:, None]
    x = _ret_post(x, o2, rgate, mod[1], p_tiles, tiles_per_seq, ret_w_out[0].astype(BF16))
    x = _moe_layer(x, mod[1], p_tiles, tiles_per_seq, norm_ffn[1], moe_router[1], moe_bias[1],
                   moe_w_gu, moe_w_down, 1, moe_sh_gu[1].astype(BF16),
                   moe_sh_down[1].astype(BF16), norm_final, True)

    return (x[:n_p].reshape(bp, tp, D), x[n_p:].reshape(bs, ts, D), new_state_rwkv, new_state_ret)
```

```python
import functools

import jax
import jax.numpy as jnp
from jax import lax
from jax.experimental import pallas as pl
from jax.experimental.pallas import tpu as pltpu

F32, BF16, I32 = jnp.float32, jnp.bfloat16, jnp.int32

D = 1024
N_MOD = 6
NORM_EPS = 1e-6
GRID_W = 64
HEAD_A = 64
H_A = D // HEAD_A
LNX_EPS = 64e-5
RWKV_CHUNK = 64
N_PAIR = H_A // 2
H_B = 4
DK_B = D // H_B
DV_B = 2 * DK_B
RET_CHUNK = 128
ROPE_BASE = 10000.0
N_EXPERTS = 64
TOP_K = 8
N_GROUPS = 8
TOPK_GROUPS = 4
PER_GROUP = N_EXPERTS // N_GROUPS
D_EXPERT = 256
ROUTED_SCALE = 2.5
EXPERT_BLOCK = 256

TM = 256
LANES = 128
ROW_SUB = D // LANES
VMEM_LIMIT = 56 * 1024 * 1024


def _cparams(n_grid_axes, **kw):
    return pltpu.CompilerParams(dimension_semantics=("arbitrary",) * n_grid_axes, vmem_limit_bytes=VMEM_LIMIT, **kw)


def _bdot(a, b):
    return jnp.dot(a.astype(BF16), b.astype(BF16), preferred_element_type=F32)


def _bdot_nt(a, b):
    return lax.dot_general(a.astype(BF16), b.astype(BF16), (((1,), (1,)), ((), ())), preferred_element_type=F32)


def _bdot_tn(a, b):
    return lax.dot_general(a.astype(BF16), b.astype(BF16), (((0,), (0,)), ((), ())), preferred_element_type=F32)


def _split3(x):
    hi = x.astype(BF16)
    r1 = x - hi.astype(F32)
    mid = r1.astype(BF16)
    lo = (r1 - mid.astype(F32)).astype(BF16)
    return hi, mid, lo


def _dot_f32(a, b):
    ah, am, al = _split3(a)
    bh, bm, bl = _split3(b)
    d = lambda x, y: jnp.dot(x, y, preferred_element_type=F32)
    return d(ah, bh) + (d(ah, bm) + d(am, bh)) + (d(ah, bl) + d(al, bh) + d(am, bm))


def _dot_f32_rhs01(a, m01):
    ah, am, al = _split3(a)
    d = lambda x: jnp.dot(x, m01, preferred_element_type=F32)
    return d(ah) + d(am) + d(al)


def _dot_f32_lhs01(m01, b):
    bh, bm, bl = _split3(b)
    d = lambda x: jnp.dot(m01, x, preferred_element_type=F32)
    return d(bh) + d(bm) + d(bl)


def _norm_mod(x, g, shift, scale):
    ms = jnp.mean(x * x, axis=-1, keepdims=True)
    y = x * lax.rsqrt(ms + NORM_EPS) * g
    return y * (1.0 + scale) + shift


def _silu(x):
    return x * jax.nn.sigmoid(x)


def _rows_load(ref):
    m = ref.shape[0] // ROW_SUB
    return jnp.concatenate([ref[pl.ds(j, m, stride=ROW_SUB), :] for j in range(ROW_SUB)], axis=1)


def _rows_store(ref, val):
    m = val.shape[0]
    for j in range(ROW_SUB):
        ref[pl.ds(j, m, stride=ROW_SUB), :] = val[:, j * LANES:(j + 1) * LANES]


def _row_tile(ref, idx):
    return ref.at[pl.ds(pl.multiple_of(idx * ROW_SUB, ROW_SUB), ROW_SUB)]


def _mod_row(i, p_tiles, tiles_per_seq):
    return jnp.where(i < p_tiles, 0, 1 + (i - p_tiles) // tiles_per_seq)


def _mod_kernel(cond_ref, w_ref, b_ref, o_ref):
    o_ref[...] = _dot_f32(_silu(cond_ref[...]), w_ref[...]) + b_ref[...]


def _modulation(cond, ada_w, ada_b):
    depth, _, n6 = ada_w.shape
    tn = 1536
    return pl.pallas_call(
        _mod_kernel,
        grid=(depth, n6 // tn),
        in_specs=[
            pl.BlockSpec(cond.shape, lambda l, j: (0, 0)),
            pl.BlockSpec((None, D, tn), lambda l, j: (l, 0, j)),
            pl.BlockSpec((None, 1, tn), lambda l, j: (l, 0, j)),
        ],
        out_specs=pl.BlockSpec((None, cond.shape[0], tn), lambda l, j: (l, 0, j)),
        out_shape=jax.ShapeDtypeStruct((depth, cond.shape[0], n6), F32),
        compiler_params=_cparams(2),
        name="adaln_mod",
    )(cond, ada_w, ada_b.reshape(depth, 1, n6))


def _rwkv_pre_kernel(x_ref, xu_ref, xd_ref, mod_ref, g_ref, mu_ref, wrkv_ref, g1_ref, g2_ref, w1_ref, w2_ref,
                     w0_ref, a1_ref, a2_ref, a0_ref, kk_ref, ka_ref, rk_ref, hsum_ref, hexp_ref,
                     r_o, v_o, kk_o, lw_o, kd_o, b_o, gate_o, bonus_o, *, p_tiles, tiles_per_seq):
    i = pl.program_id(0)
    is_p = i < p_tiles
    sub = (i - p_tiles) % tiles_per_seq
    mod = mod_ref[...]
    shift, scale = mod[:, 0:D], mod[:, D:2 * D]
    g = g_ref[...]
    h = _norm_mod(x_ref[...], g, shift, scale)
    hu = _norm_mod(xu_ref[...], g, shift, scale)
    hd = _norm_mod(xd_ref[...], g, shift, scale)

    q = D // 4
    row = lax.broadcasted_iota(I32, (TM, 1), 0)
    per = jnp.where(is_p, TM, GRID_W)
    pos = row & (per - 1)

    def prev1(a):
        return jnp.where(pos == 0, 0.0, pltpu.roll(a, 1, 0))

    def next1(a):
        return jnp.where(pos == per - 1, 0.0, pltpu.roll(a, TM - 1, 0))

    h0, h1, h2, h3 = (h[:, j * q:(j + 1) * q] for j in range(4))
    up = jnp.concatenate([jnp.where(sub == 0, 0.0, hu[:, 2 * q:3 * q]), h2[0:TM - GRID_W]], axis=0)
    down = jnp.concatenate([h3[GRID_W:TM], jnp.where(sub == tiles_per_seq - 1, 0.0, hd[:, 3 * q:4 * q])], axis=0)
    s0 = prev1(h0)
    s1 = jnp.where(is_p, prev1(h1), next1(h1))
    s2 = jnp.where(is_p, next1(h2), up)
    s3 = jnp.where(is_p, next1(h3), down)
    xx = jnp.concatenate([s0, s1, s2, s3], axis=1) - h

    mu = mu_ref[...]
    mix = lambda j: h + xx * mu[j:j + 1]
    r = _bdot(mix(0), wrkv_ref[0])
    k = _bdot(mix(2), wrkv_ref[1])
    v = _bdot(mix(3), wrkv_ref[2])
    gate = _bdot(jax.nn.sigmoid(_bdot(mix(5), g1_ref[...])), g2_ref[...])
    w_all = w0_ref[...] + _bdot(jnp.tanh(_bdot(mix(1), w1_ref[...])), w2_ref[...])
    a_all = jax.nn.sigmoid(a0_ref[...] + _bdot(_bdot(mix(4), a1_ref[...]), a2_ref[...]))

    hsum, hexp = hsum_ref[...], hexp_ref[...]
    head_sum = lambda t: _dot_f32_rhs01(_dot_f32_rhs01(t, hsum), hexp)

    kkr = k * kk_ref[...]
    kk = kkr / jnp.maximum(jnp.sqrt(head_sum(kkr * kkr)), 1e-12)
    ka = ka_ref[...]
    kd_sum = jnp.zeros_like(k)
    for d in range(2):
        wd = w_all[:, d * D:(d + 1) * D]
        z = -wd
        softplus = jnp.maximum(z, 0.0) + jnp.log(1.0 + jnp.exp(-jnp.abs(z)))
        lw_o[d] = -jnp.exp(-softplus - 0.5)
        a = a_all[:, d * D:(d + 1) * D]
        kd = k * (1.0 + (a - 1.0) * ka)
        kd_o[d] = kd
        b_o[d] = kk * a
        kd_sum = kd_sum + kd
    r_o[...] = r
    v_o[...] = v
    kk_o[...] = kk
    gate_o[...] = gate
    bonus_o[...] = head_sum(r * kd_sum * rk_ref[...]) * v


def _rwkv_pre(x, mod, p_tiles, tiles_per_seq, norm_g, wts):
    n = x.shape[0]
    n_tiles = n // TM
    hb = TM // GRID_W
    n_hblk = n // GRID_W
    row = lambda a: a.reshape(1, -1)
    full = lambda a: pl.BlockSpec(a.shape, lambda i: (0,) * a.ndim)
    tok = pl.BlockSpec((TM, D), lambda i: (i, 0))
    tok2 = pl.BlockSpec((2, TM, D), lambda i: (0, i, 0))
    consts = [row(norm_g), wts["mu"], wts["wrkv"], wts["g1"], wts["g2"], wts["w1"], wts["w2"], wts["w0"],
              wts["a1"], wts["a2"], wts["a0"], row(wts["k_k"]), row(wts["k_a"]), row(wts["r_k"]),
              wts["hsum"], wts["hexp"]]
    return pl.pallas_call(
        functools.partial(_rwkv_pre_kernel, p_tiles=p_tiles, tiles_per_seq=tiles_per_seq),
        grid=(n_tiles,),
        in_specs=[
            tok,
            pl.BlockSpec((GRID_W, D), lambda i: (jnp.maximum(i * hb - 1, 0), 0)),
            pl.BlockSpec((GRID_W, D), lambda i: (jnp.minimum(i * hb + hb, n_hblk - 1), 0)),
            pl.BlockSpec((None, 1, N_MOD * D), lambda i: (_mod_row(i, p_tiles, tiles_per_seq), 0, 0)),
        ] + [full(a) for a in consts],
        out_specs=[tok, tok, tok, tok2, tok2, tok2, tok, tok],
        out_shape=[jax.ShapeDtypeStruct((n, D), F32)] * 3 + [jax.ShapeDtypeStruct((2, n, D), F32)] * 3
        + [jax.ShapeDtypeStruct((n, D), F32)] * 2,
        compiler_params=_cparams(1),
        name="rwkv_pre",
    )(x, x, x, mod, *consts)


class _ScanItems:
    def __init__(self, n_ctx, ctx_chunks, n_lat, lat_chunks):
        self.n_ctx, self.ctx_chunks, self.n_lat, self.lat_chunks = n_ctx, ctx_chunks, n_lat, lat_chunks
        self.ctx_items = n_ctx * ctx_chunks
        self.n_items = self.ctx_items + n_lat * lat_chunks

    def decode(self, j):
        is_ctx = j < self.ctx_items
        jl = jnp.maximum(j - self.ctx_items, 0)
        c = jnp.where(is_ctx, j % self.ctx_chunks, jl % self.lat_chunks)
        return is_ctx, c, jnp.where(is_ctx, self.ctx_chunks, self.lat_chunks)

    def block(self, d, j):
        _, c, nc = self.decode(j)
        return jnp.where(d == 0, j, j - c + (nc - 1 - c))

    def ctx_seq(self, j):
        return jnp.minimum(j // self.ctx_chunks, self.n_ctx - 1)

    def lat_seq(self, j):
        return jnp.maximum(j - self.ctx_items, 0) // self.lat_chunks


def _rwkv_scan_kernel(r_ref, v_ref, kk_ref, lw_ref, kd_ref, b_ref, s0_ref, y_ref, sf_ref, s_scr, *, items):
    d = pl.program_id(0)
    is_ctx, c, nc = items.decode(pl.program_id(1))
    rev = d == 1
    ch = RWKV_CHUNK

    @pl.when(c == 0)
    def _():
        s_scr[...] = jnp.where(is_ctx, 0.0, s0_ref[...])

    ti = lax.broadcasted_iota(I32, (ch, ch), 0)
    tj = lax.broadcasted_iota(I32, (ch, ch), 1)
    tri = jnp.where(jnp.where(rev, ti - tj, tj - ti) <= 0, 1.0, 0.0).astype(BF16)
    lw = lw_ref[...]
    cum = _dot_f32_lhs01(tri, lw)
    last = jnp.where(rev, cum[0:1], cum[ch - 1:ch])
    kk = kk_ref[...]
    kd = kd_ref[...]
    b = b_ref[...]
    e_neg = jnp.exp(-cum)
    e_rel = jnp.exp(last - cum)
    at = -kk * jnp.exp(cum - lw)
    rt = r_ref[...] * jnp.exp(cum)
    bt = b * e_neg
    kt = kd * e_neg
    bh = b * e_rel
    kh = kd * e_rel
    g_all = jnp.exp(last)
    v = v_ref[...]

    si = lax.broadcasted_iota(I32, (2 * ch, 2 * ch), 0)
    sj = lax.broadcasted_iota(I32, (2 * ch, 2 * ch), 1)
    same = (si < ch) == (sj < ch)
    ui, uj = si & (ch - 1), sj & (ch - 1)
    order = jnp.where(rev, ui - uj, uj - ui)
    strict = same & (order < 0)
    incl = same & (order <= 0)
    first = lax.broadcasted_iota(I32, (1, LANES), 1) < HEAD_A

    def stack(xp):
        return jnp.concatenate([jnp.where(first, xp, 0.0), jnp.where(first, 0.0, xp)], axis=0)

    pairs = range(N_PAIR)
    sls = [slice(p * LANES, (p + 1) * LANES) for p in pairs]
    s_old = [s_scr[p] for p in pairs]
    m1 = [_bdot_nt(jnp.concatenate([stack(at[:, sl]), stack(rt[:, sl])], axis=0),
                   jnp.concatenate([stack(bt[:, sl]), stack(kt[:, sl])], axis=0)) for sl in sls]
    m2 = [_bdot_nt(jnp.concatenate([at[:, sl], rt[:, sl]], axis=0), s_old[p])
          for p, sl in enumerate(sls)]
    vs = [stack(v[:, sl]) for sl in sls]
    x = [stack(m2[p][0:ch]) + _bdot(jnp.where(strict, m1[p][0:2 * ch, 2 * ch:4 * ch], 0.0), vs[p]) for p in pairs]
    nm = [jnp.where(strict, m1[p][0:2 * ch, 0:2 * ch], 0.0).astype(BF16) for p in pairs]
    for _ in range(5):
        sq_ap = [_bdot(nm[p], jnp.concatenate([nm[p], x[p].astype(BF16)], axis=1)) for p in pairs]
        nm = [sq_ap[p][:, 0:2 * ch].astype(BF16) for p in pairs]
        x = [x[p] + sq_ap[p][:, 2 * ch:4 * ch] for p in pairs]
    x = [x[p] + _bdot(nm[p], x[p]) for p in pairs]
    uv = [jnp.concatenate([x[p], vs[p]], axis=0).astype(BF16) for p in pairs]
    ys = [_bdot(jnp.concatenate([jnp.where(incl, m1[p][2 * ch:4 * ch, 0:2 * ch], 0.0),
                                 jnp.where(incl, m1[p][2 * ch:4 * ch, 2 * ch:4 * ch], 0.0)], axis=1), uv[p])
          for p in pairs]
    s_new = [s_old[p] * g_all[:, sl]
             + _bdot_tn(uv[p], jnp.concatenate([stack(bh[:, sl]), stack(kh[:, sl])], axis=0))
             for p, sl in enumerate(sls)]
    for p, sl in enumerate(sls):
        y_ref[:, sl] = ys[p][0:ch] + ys[p][ch:2 * ch] + m2[p][ch:2 * ch]
    for p in pairs:
        s_scr[p] = s_new[p]

    @pl.when(is_ctx & (c == nc - 1))
    def _():
        sf_ref[...] = s_scr[...]


def _rwkv_scan(r, v, kk, lw, kd, b, s0, items):
    n = r.shape[0]
    ch = RWKV_CHUNK
    tok = pl.BlockSpec((ch, D), lambda d, j: (items.block(d, j), 0))
    tokd = pl.BlockSpec((None, ch, D), lambda d, j: (d, items.block(d, j), 0))
    st = lambda seq: pl.BlockSpec((None, None, N_PAIR, LANES, LANES), lambda d, j: (d, seq(j), 0, 0, 0))
    return pl.pallas_call(
        functools.partial(_rwkv_scan_kernel, items=items),
        grid=(2, items.n_items),
        in_specs=[tok, tok, tok, tokd, tokd, tokd, st(items.lat_seq)],
        out_specs=[tokd, st(items.ctx_seq)],
        out_shape=[jax.ShapeDtypeStruct((2, n, D), F32),
                   jax.ShapeDtypeStruct((2, items.n_ctx, N_PAIR, LANES, LANES), F32)],
        scratch_shapes=[pltpu.VMEM((N_PAIR, LANES, LANES), F32)],
        compiler_params=_cparams(2),
        name="rwkv_scan",
    )(r, v, kk, lw, kd, b, s0)


def _rwkv_post_kernel(x_ref, y_ref, gate_ref, bonus_ref, mod_ref, lnw_ref, lnb_ref, wo_ref, hsum_ref, hexp_ref, o_ref):
    hsum, hexp = hsum_ref[...], hexp_ref[...]
    head_mean = lambda t: _dot_f32_rhs01(_dot_f32_rhs01(t, hsum), hexp) * (1.0 / HEAD_A)
    y = y_ref[0] + y_ref[1]
    yc = y - head_mean(y)
    yn = yc * lax.rsqrt(head_mean(yc * yc) + LNX_EPS)
    z = (yn * lnw_ref[...] + lnb_ref[...] + bonus_ref[...]) * gate_ref[...]
    o_ref[...] = x_ref[...] + mod_ref[:, 2 * D:3 * D] * _bdot(z, wo_ref[...])


def _rwkv_post(x, y, gate, bonus, mod, p_tiles, tiles_per_seq, ln_w, ln_b, wo, hsum, hexp):
    n = x.shape[0]
    full = lambda a: pl.BlockSpec(a.shape, lambda i: (0,) * a.ndim)
    tok = pl.BlockSpec((TM, D), lambda i: (i, 0))
    consts = [ln_w.reshape(1, D), ln_b.reshape(1, D), wo, hsum, hexp]
    return pl.pallas_call(
        _rwkv_post_kernel,
        grid=(n // TM,),
        in_specs=[tok, pl.BlockSpec((2, TM, D), lambda i: (0, i, 0)), tok, tok,
                  pl.BlockSpec((None, 1, N_MOD * D), lambda i: (_mod_row(i, p_tiles, tiles_per_seq), 0, 0))]
        + [full(a) for a in consts],
        out_specs=tok,
        out_shape=jax.ShapeDtypeStruct((n, D), F32),
        compiler_params=_cparams(1),
        name="rwkv_post",
    )(x, y, gate, bonus, mod, *consts)


def _ret_pre_kernel(x_ref, mod_ref, g_ref, cos_ref, sin_ref, win_ref, q_o, k_o, v_o, gate_o):
    mod = mod_ref[...]
    h = _norm_mod(x_ref[...], g_ref[...], mod[:, 0:D], mod[:, D:2 * D])
    proj = _bdot(h, win_ref[...])
    cos, sin = cos_ref[...], sin_ref[...]

    def rope(t):
        outs = []
        for j in range(D // LANES):
            tj = t[:, j * LANES:(j + 1) * LANES]
            cj = cos[:, (j % 2) * LANES:(j % 2 + 1) * LANES]
            sj = sin[:, (j % 2) * LANES:(j % 2 + 1) * LANES]
            outs.append(tj * cj + pltpu.roll(tj, LANES // 2, 1) * sj)
        return jnp.concatenate(outs, axis=1)

    q_o[...] = rope(proj[:, 0:D]).astype(BF16)
    k_o[...] = rope(proj[:, D:2 * D] * (DK_B ** -0.5)).astype(BF16)
    v_o[...] = proj[:, 2 * D:4 * D].astype(BF16)
    gate_o[...] = proj[:, 4 * D:6 * D]


def _ret_pre(x, mod, p_tiles, tiles_per_seq, norm_g, cos_t, sin_t, w_in):
    n = x.shape[0]
    full = lambda a: pl.BlockSpec(a.shape, lambda i: (0,) * a.ndim)
    tok = lambda w: pl.BlockSpec((TM, w), lambda i: (i, 0))
    tab = pl.BlockSpec((TM, DK_B), lambda i: (jnp.where(i < p_tiles, 0, 1 + (i - p_tiles) % tiles_per_seq), 0))
    return pl.pallas_call(
        _ret_pre_kernel,
        grid=(n // TM,),
        in_specs=[tok(D), pl.BlockSpec((None, 1, N_MOD * D), lambda i: (_mod_row(i, p_tiles, tiles_per_seq), 0, 0)),
                  full(norm_g.reshape(1, D)), tab, tab, full(w_in)],
        out_specs=[tok(D), tok(D), tok(2 * D), tok(2 * D)],
        out_shape=[jax.ShapeDtypeStruct((n, D), BF16), jax.ShapeDtypeStruct((n, D), BF16),
                   jax.ShapeDtypeStruct((n, 2 * D), BF16), jax.ShapeDtypeStruct((n, 2 * D), F32)],
        compiler_params=_cparams(1),
        name="ret_pre",
    )(x, mod, norm_g.reshape(1, D), cos_t, sin_t, w_in)


def _ret_scan_kernel(lg_ref, q_ref, k_ref, v_ref, s0_ref, o_ref, sf_ref, s_scr, *, items):
    d = pl.program_id(0)
    is_ctx, c, nc = items.decode(pl.program_id(1))
    rev = d == 1
    ch = RET_CHUNK

    @pl.when(c == 0)
    def _():
        s_scr[...] = jnp.where(is_ctx, 0.0, s0_ref[...])

    ti = lax.broadcasted_iota(I32, (ch, ch), 0)
    tj = lax.broadcasted_iota(I32, (ch, ch), 1)
    rel = jnp.where(rev, tj - ti, ti - tj).astype(F32)
    steps_q = jnp.where(rev, ch - ti, ti + 1).astype(F32)
    steps_k = jnp.where(rev, ti, ch - 1 - ti).astype(F32)
    for hd in range(H_B):
        lg = lg_ref[d, hd]
        mask = jnp.where(rel >= 0, jnp.exp(jnp.maximum(rel, 0.0) * lg), 0.0)
        q_dec = jnp.exp(steps_q * lg)
        k_dec = jnp.exp(steps_k * lg)
        chunk_dec = jnp.exp(jnp.full((1, DV_B), float(ch), F32) * lg)
        qh = q_ref[:, hd * DK_B:(hd + 1) * DK_B]
        kh = k_ref[:, hd * DK_B:(hd + 1) * DK_B]
        vh = v_ref[:, hd * DV_B:(hd + 1) * DV_B]
        s = s_scr[hd]
        scores = _bdot_nt(qh, kh) * mask
        inner = _bdot(scores, vh)
        cross = _bdot(qh, s) * jnp.concatenate([q_dec] * (DV_B // ch), axis=1)
        o_ref[:, hd * DV_B:(hd + 1) * DV_B] = inner + cross
        kdec = kh.astype(F32) * jnp.concatenate([k_dec] * (DK_B // ch), axis=1)
        s_scr[hd] = s * chunk_dec + _bdot_tn(kdec, vh)

    @pl.when(is_ctx & (c == nc - 1))
    def _():
        sf_ref[...] = s_scr[...]


def _ret_scan(log_gamma, q, k, v, s0, items):
    n = q.shape[0]
    ch = RET_CHUNK
    tok = lambda w: pl.BlockSpec((ch, w), lambda d, j: (items.block(d, j), 0))
    st = lambda seq: pl.BlockSpec((None, None, H_B, DK_B, DV_B), lambda d, j: (d, seq(j), 0, 0, 0))
    return pl.pallas_call(
        functools.partial(_ret_scan_kernel, items=items),
        grid=(2, items.n_items),
        in_specs=[pl.BlockSpec(memory_space=pltpu.SMEM), tok(D), tok(D), tok(2 * D), st(items.lat_seq)],
        out_specs=[pl.BlockSpec((None, ch, 2 * D), lambda d, j: (d, items.block(d, j), 0)), st(items.ctx_seq)],
        out_shape=[jax.ShapeDtypeStruct((2, n, 2 * D), F32),
                   jax.ShapeDtypeStruct((2, items.n_ctx, H_B, DK_B, DV_B), F32)],
        scratch_shapes=[pltpu.VMEM((H_B, DK_B, DV_B), F32)],
        compiler_params=_cparams(2),
        name="ret_scan",
    )(log_gamma, q, k, v, s0)


def _ret_post_kernel(x_ref, o_ref, gate_ref, mod_ref, wout_ref, out_ref):
    o = o_ref[0] + o_ref[1]
    parts = []
    for hd in range(H_B):
        oh = o[:, hd * DV_B:(hd + 1) * DV_B]
        parts.append(oh * lax.rsqrt(jnp.mean(oh * oh, axis=-1, keepdims=True) + NORM_EPS))
    y = _silu(gate_ref[...]) * jnp.concatenate(parts, axis=1)
    out_ref[...] = x_ref[...] + mod_ref[:, 2 * D:3 * D] * _bdot(y, wout_ref[...])


def _ret_post(x, o, gate, mod, p_tiles, tiles_per_seq, w_out):
    n = x.shape[0]
    tok = lambda w: pl.BlockSpec((TM, w), lambda i: (i, 0))
    return pl.pallas_call(
        _ret_post_kernel,
        grid=(n // TM,),
        in_specs=[tok(D), pl.BlockSpec((2, TM, 2 * D), lambda i: (0, i, 0)), tok(2 * D),
                  pl.BlockSpec((None, 1, N_MOD * D), lambda i: (_mod_row(i, p_tiles, tiles_per_seq), 0, 0)),
                  pl.BlockSpec(w_out.shape, lambda i: (0, 0))],
        out_specs=tok(D),
        out_shape=jax.ShapeDtypeStruct((n, D), F32),
        compiler_params=_cparams(1),
        name="ret_post",
    )(x, o, gate, mod, w_out)


RANK_SPAN = 1 << 20


def _slot_of(pad_start_ref, code):
    return pad_start_ref[code // RANK_SPAN] + code % RANK_SPAN


def _moe_route_kernel(x_ref, mod_ref, g_ref, router_ref, bias_ref, h_o, w_o, code_o, cnt_o, carry):
    i = pl.program_id(0)

    @pl.when(i == 0)
    def _():
        carry[...] = jnp.zeros_like(carry)

    mod = mod_ref[...]
    h = _norm_mod(x_ref[...], g_ref[...], mod[:, 3 * D:4 * D], mod[:, 4 * D:5 * D])
    _rows_store(h_o, h)
    lane =lax.broadcasted_iota(I32, (TM, LANES), 1)
    valid = lane < N_EXPERTS
    neg = -jnp.inf
    scores = jax.nn.sigmoid(_dot_f32(h, router_ref[...]))
    biased = jnp.where(valid, scores + bias_ref[...], neg)

    def group_reduce(t, op):
        s = 1
        while s < PER_GROUP:
            partner = jnp.where((lane & s) == 0, pltpu.roll(t, LANES - s, 1), pltpu.roll(t, s, 1))
            t = op(t, partner)
            s *= 2
        return t

    lane_f = lane.astype(F32)
    group_f = jnp.floor(lane_f * (1.0 / PER_GROUP))

    def first_lane_of_max(t):
        m = jnp.max(t, axis=-1, keepdims=True)
        return jnp.min(jnp.where(t == m, lane_f, float(LANES)), axis=-1, keepdims=True)

    m1 = group_reduce(biased, jnp.maximum)
    first1 = group_reduce(jnp.where(biased == m1, lane_f, float(LANES)), jnp.minimum)
    m2 = group_reduce(jnp.where(lane_f == first1, neg, biased), jnp.maximum)
    gscore = jnp.where(valid, m1 + m2, neg)
    cand = jnp.full((TM, LANES), neg, F32)
    for _ in range(TOPK_GROUPS):
        gsel = group_f == jnp.floor(first_lane_of_max(gscore) * (1.0 / PER_GROUP))
        cand = jnp.where(gsel, biased, cand)
        gscore = jnp.where(gsel, neg, gscore)
    hits = []
    sel01 = jnp.zeros((TM, LANES), F32)
    e_cols = jnp.zeros((TM, LANES), F32)
    for j in range(TOP_K):
        fl = first_lane_of_max(cand)
        hit = lane_f == fl
        hits.append(hit)
        sel01 = jnp.where(hit, 1.0, sel01)
        cand = jnp.where(hit, neg, cand)
        e_cols = jnp.where(lane == j, fl, e_cols)
    wsum = jnp.sum(sel01 * scores, axis=-1, keepdims=True)

    ri = lax.broadcasted_iota(I32, (TM, TM), 0)
    rj = lax.broadcasted_iota(I32, (TM, TM), 1)
    below = jnp.where(rj < ri, 1.0, 0.0).astype(BF16)
    rank = jnp.dot(below, sel01.astype(BF16), preferred_element_type=F32) + carry[...]
    carry[...] = carry[...] + jnp.sum(sel01, axis=0, keepdims=True)
    w_cols = jnp.zeros((TM, LANES), F32)
    p_cols = jnp.zeros((TM, LANES), F32)
    for j in range(TOP_K):
        wj = jnp.sum(jnp.where(hits[j], scores, 0.0), axis=-1, keepdims=True)
        w_cols = jnp.where(lane == j, wj / wsum * ROUTED_SCALE, w_cols)
        p_cols = jnp.where(lane == j, jnp.sum(jnp.where(hits[j], rank, 0.0), axis=-1, keepdims=True), p_cols)
    w_o[...] = w_cols
    code_o[...] = e_cols.astype(I32) * RANK_SPAN + p_cols.astype(I32)
    cnt_o[...] = jnp.broadcast_to(carry[...], cnt_o.shape)


def _moe_route(x, mod, p_tiles, tiles_per_seq, norm_g, router, bias):
    n = x.shape[0]
    full = lambda a: pl.BlockSpec(a.shape, lambda i: (0,) * a.ndim)
    tok = lambda w: pl.BlockSpec((TM, w), lambda i: (i, 0))
    router_p = jnp.pad(router, ((0, 0), (0, LANES - N_EXPERTS)))
    bias_p = jnp.pad(bias, (0, LANES - N_EXPERTS)).reshape(1, LANES)
    return pl.pallas_call(
        _moe_route_kernel,
        grid=(n // TM,),
        in_specs=[tok(D), pl.BlockSpec((None, 1, N_MOD * D), lambda i: (_mod_row(i, p_tiles, tiles_per_seq), 0, 0)),
                  full(norm_g.reshape(1, D)), full(router_p), full(bias_p)],
        out_specs=[pl.BlockSpec((TM * ROW_SUB, LANES), lambda i: (i, 0)), tok(LANES), tok(LANES),
                   pl.BlockSpec((8, LANES), lambda i: (0, 0))],
        out_shape=[jax.ShapeDtypeStruct((n * ROW_SUB, LANES), F32), jax.ShapeDtypeStruct((n, LANES), F32),
                   jax.ShapeDtypeStruct((n, LANES), I32), jax.ShapeDtypeStruct((8, LANES), F32)],
        scratch_shapes=[pltpu.VMEM((1, LANES), F32)],
        compiler_params=_cparams(1),
        name="moe_route",
    )(x, mod, norm_g.reshape(1, D), router_p, bias_p)


DISPATCH_TOK = 512


def _moe_dispatch_kernel(pad_start_ref, pad_lo_ref, pad_n_ref, code_ref, h_ref, xs_hbm, zero_scr, sem):
    @pl.when(pl.program_id(0) == 0)
    def _():
        zero_scr[...] = jnp.zeros_like(zero_scr)

        def pad_copy(e, s):
            return pltpu.make_async_copy(zero_scr, _row_tile(xs_hbm, pad_lo_ref[e] + s), sem)

        def per_expert(e, carry):
            def issue(s, c):
                pad_copy(e, s).start()
                return c

            def drain(s, c):
                pad_copy(e, s).wait()
                return c

            lax.fori_loop(0, pad_n_ref[e], issue, 0)
            lax.fori_loop(0, pad_n_ref[e], drain, 0)
            return carry

        lax.fori_loop(0, N_EXPERTS, per_expert, 0)

    def copy(t, kq):
        slot = _slot_of(pad_start_ref, code_ref[t * TOP_K + kq])
        return pltpu.make_async_copy(_row_tile(h_ref, t), _row_tile(xs_hbm, slot), sem)

    def issue(t, carry):
        for kq in range(TOP_K):
            copy(t, kq).start(priority=kq % 2)
        return carry

    def drain(t, carry):
        for kq in range(TOP_K):
            copy(t, kq).wait()
        return carry

    lax.fori_loop(0, DISPATCH_TOK, issue, 0)
    lax.fori_loop(0, DISPATCH_TOK, drain, 0)


def _moe_dispatch(pad_start, pad_lo, pad_n, code_flat, h, n_slots):
    n = h.shape[0] // ROW_SUB
    grid_spec = pltpu.PrefetchScalarGridSpec(
        num_scalar_prefetch=3,
        grid=(n // DISPATCH_TOK,),
        in_specs=[pl.BlockSpec((DISPATCH_TOK * TOP_K,), lambda i, *_: (i,), memory_space=pltpu.SMEM),
                  pl.BlockSpec((DISPATCH_TOK * ROW_SUB, LANES), lambda i, *_: (i, 0))],
        out_specs=pl.BlockSpec(memory_space=pl.ANY),
        scratch_shapes=[pltpu.VMEM((ROW_SUB, LANES), F32), pltpu.SemaphoreType.DMA],
    )
    return pl.pallas_call(
        _moe_dispatch_kernel,
        grid_spec=grid_spec,
        out_shape=jax.ShapeDtypeStruct((n_slots * ROW_SUB, LANES), F32),
        compiler_params=_cparams(1, has_side_effects=True),
        name="moe_dispatch",
    )(pad_start, pad_lo, pad_n, code_flat, h)


def _moe_expert_kernel(be_ref, xs_ref, wgu_ref, wdn_ref, o_ref, wgu_bf, wdn_bf):
    i = pl.program_id(0)

    @pl.when((i == 0) | (be_ref[i] != be_ref[jnp.maximum(i - 1, 0)]))
    def _():
        wgu_bf[...] = wgu_ref[...].astype(BF16)
        wdn_bf[...] = wdn_ref[...].astype(BF16)

    gu = _bdot(_rows_load(xs_ref), wgu_bf[...])
    act = _silu(gu[:, 0:D_EXPERT]) * gu[:, D_EXPERT:2 * D_EXPERT]
    _rows_store(o_ref, _bdot(act, wdn_bf[...]))


def _moe_expert(block_e, xs, w_gu, w_down, layer):
    n_slots = xs.shape[0] // ROW_SUB
    n_blocks = n_slots // EXPERT_BLOCK
    rows = pl.BlockSpec((EXPERT_BLOCK * ROW_SUB, LANES), lambda i, be: (i, 0))
    grid_spec = pltpu.PrefetchScalarGridSpec(
        num_scalar_prefetch=1,
        grid=(n_blocks,),
        in_specs=[rows,
                  pl.BlockSpec((None, None, D, 2 * D_EXPERT), lambda i, be: (layer, be[i], 0, 0)),
                  pl.BlockSpec((None, None, D_EXPERT, D), lambda i, be: (layer, be[i], 0, 0))],
        out_specs=rows,
        scratch_shapes=[pltpu.VMEM((D, 2 * D_EXPERT), BF16), pltpu.VMEM((D_EXPERT, D), BF16)],
    )
    return pl.pallas_call(
        _moe_expert_kernel,
        grid_spec=grid_spec,
        out_shape=jax.ShapeDtypeStruct((n_slots * ROW_SUB, LANES), F32),
        compiler_params=_cparams(1),
        name="moe_expert",
    )(block_e, xs, w_gu, w_down)


COMBINE_TOK = 128


def _moe_combine_kernel(pad_start_ref, dest_ref, dest_next_ref, x_ref, h_ref, w_ref, mod_ref, shgu_ref, shdn_ref,
                        gfin_ref, ys_hbm, o_ref, buf, sems, *, final_norm):
    i = pl.program_id(0)
    slot = i % 2

    def copy(code_ref, s, t, kq):
        row = _row_tile(ys_hbm, _slot_of(pad_start_ref, code_ref[t * TOP_K + kq]))
        return pltpu.make_async_copy(row, _row_tile(buf.at[s, kq], t), sems.at[s])

    def fetch(idx_ref, s):
        def body(t, carry):
            for kq in range(TOP_K):
                copy(idx_ref, s, t, kq).start(priority=kq % 2)
            return carry
        lax.fori_loop(0, COMBINE_TOK, body, 0, unroll=2)

    @pl.when(i == 0)
    def _():
        fetch(dest_ref, slot)

    @pl.when(i + 1 < pl.num_programs(0))
    def _():
        fetch(dest_next_ref, 1 - slot)

    gu = _bdot(_rows_load(h_ref), shgu_ref[...])
    acc = _bdot(_silu(gu[:, 0:D_EXPERT]) * gu[:, D_EXPERT:2 * D_EXPERT], shdn_ref[...])

    def drain(t, carry):
        for kq in range(TOP_K):
            copy(dest_ref, slot, t, kq).wait()
        return carry

    lax.fori_loop(0, COMBINE_TOK, drain, 0)
    w = w_ref[...]
    for kq in range(TOP_K):
        acc = acc + _rows_load(buf.at[slot, kq]) * w[:, kq:kq + 1]
    out = x_ref[...] + mod_ref[:, 5 * D:6 * D] * acc
    if final_norm:
        out = out * lax.rsqrt(jnp.mean(out * out, axis=-1, keepdims=True) + NORM_EPS) * gfin_ref[...]
    o_ref[...] = out


def _moe_combine(pad_start, code_flat, x, h, w_cols, mod, p_tiles, tiles_per_seq, sh_gu, sh_down, g_final, ys,
                 final_norm):
    n = x.shape[0]
    ratio = TM // COMBINE_TOK
    tok = lambda w: pl.BlockSpec((COMBINE_TOK, w), lambda i, ps: (i, 0))
    full = lambda a: pl.BlockSpec(a.shape, lambda i, ps: (0,) * a.ndim)
    n_tiles = n // COMBINE_TOK
    grid_spec = pltpu.PrefetchScalarGridSpec(
        num_scalar_prefetch=1,
        grid=(n_tiles,),
        in_specs=[pl.BlockSpec((COMBINE_TOK * TOP_K,), lambda i, ps: (i,), memory_space=pltpu.SMEM),
                  pl.BlockSpec((COMBINE_TOK * TOP_K,), lambda i, ps: (jnp.minimum(i + 1, n_tiles - 1),),
                               memory_space=pltpu.SMEM),
                  tok(D), pl.BlockSpec((COMBINE_TOK * ROW_SUB, LANES), lambda i, ps: (i, 0)), tok(LANES),
                  pl.BlockSpec((None, 1, N_MOD * D),
                               lambda i, ps: (_mod_row(i // ratio, p_tiles, tiles_per_seq), 0, 0)),
                  full(sh_gu), full(sh_down), full(g_final),
                  pl.BlockSpec(memory_space=pl.ANY)],
        out_specs=tok(D),
        scratch_shapes=[pltpu.VMEM((2, TOP_K, COMBINE_TOK * ROW_SUB, LANES), F32), pltpu.SemaphoreType.DMA((2,))],
    )
    return pl.pallas_call(
        functools.partial(_moe_combine_kernel, final_norm=final_norm),
        grid_spec=grid_spec,
        out_shape=jax.ShapeDtypeStruct((n, D), F32),
        compiler_params=_cparams(1),
        name="moe_combine",
    )(pad_start, code_flat, code_flat, x, h, w_cols, mod, sh_gu, sh_down, g_final, ys)


def _moe_layer(x, mod, p_tiles, tiles_per_seq, norm_g, router, bias, w_gu, w_down, layer, sh_gu, sh_down, g_final,
               final_norm):
    n = x.shape[0]
    h, w_cols, codes, counts = _moe_route(x, mod, p_tiles, tiles_per_seq, norm_g, router, bias)
    code_flat = codes[:, :TOP_K].reshape(-1)
    counts = counts[0, :N_EXPERTS].astype(I32)
    padded = (counts + EXPERT_BLOCK - 1) // EXPERT_BLOCK * EXPERT_BLOCK
    pad_end = jnp.cumsum(padded)
    pad_start = pad_end - padded
    n_blocks = n * TOP_K // EXPERT_BLOCK + N_EXPERTS
    n_slots = n_blocks * EXPERT_BLOCK
    block_start = jnp.arange(n_blocks, dtype=I32) * EXPERT_BLOCK
    block_e = jnp.minimum(jnp.sum((pad_end[None, :] <= block_start[:, None]).astype(I32), axis=1), N_EXPERTS - 1)
    pad_n = (padded - counts).at[N_EXPERTS - 1].add(n_slots - pad_end[N_EXPERTS - 1])
    xs = _moe_dispatch(pad_start, pad_start + counts, pad_n, code_flat, h, n_slots)
    ys = _moe_expert(block_e, xs, w_gu, w_down, layer)
    return _moe_combine(pad_start, code_flat, x, h, w_cols, mod, p_tiles, tiles_per_seq, sh_gu, sh_down,
                        g_final.reshape(1, D), ys, final_norm)


def _pair_pack(s):
    lead = s.shape[:-3]
    s = s.reshape(lead + (N_PAIR, 2, HEAD_A, HEAD_A))
    z = jnp.zeros_like(s[..., 0, :, :])
    top = jnp.concatenate([s[..., 0, :, :], z], axis=-1)
    bot = jnp.concatenate([z, s[..., 1, :, :]], axis=-1)
    return jnp.concatenate([top, bot], axis=-2)


def _pair_unpack(s):
    a = s[..., :HEAD_A, :HEAD_A]
    b = s[..., HEAD_A:, HEAD_A:]
    return jnp.stack([a, b], axis=-3).reshape(s.shape[:-3] + (H_A, HEAD_A, HEAD_A))


def _rope_tables(rows):
    half = DK_B // 2
    n_freq = half // 2
    inv = ROPE_BASE ** (-jnp.arange(n_freq, dtype=F32) / n_freq)
    pos_r = jnp.repeat(jnp.arange(rows, dtype=F32), GRID_W)
    pos_c = jnp.tile(jnp.arange(GRID_W, dtype=F32), rows)

    def tab(pos):
        ang = pos[:, None] * inv[None, :]
        c, s = jnp.cos(ang), jnp.sin(ang)
        return jnp.concatenate([c, c], -1), jnp.concatenate([-s, s], -1)

    cr, sr = tab(pos_r)
    cc, sc = tab(pos_c)
    cos = jnp.concatenate([cr, cc], -1)
    sin = jnp.concatenate([sr, sc], -1)
    cos = jnp.concatenate([jnp.ones((TM, DK_B), F32), cos], 0)
    sin = jnp.concatenate([jnp.zeros((TM, DK_B), F32), sin], 0)
    return cos, sin


def kernel(x_prompt, x_sample, state_rwkv, state_ret, c, c_ctx, ada_w, ada_b, norm_mix, norm_ffn, norm_final, rwkv_mu, rwkv_w0, rwkv_w1, rwkv_w2, rwkv_a0, rwkv_a1, rwkv_a2, rwkv_wrkv, rwkv_wo, rwkv_g1, rwkv_g2, rwkv_k_k, rwkv_k_a, rwkv_r_k, rwkv_ln_w, rwkv_ln_b, ret_w_in, ret_w_out, ret_decay_logit, moe_router, moe_bias, moe_w_gu, moe_w_down, moe_sh_gu, moe_sh_down):
    bp, tp, _ = x_prompt.shape
    bs, ts, _ = x_sample.shape
    n_p, n_s = bp * tp, bs * ts
    assert tp == TM and ts % TM == 0 and TM % GRID_W == 0
    p_tiles = n_p // TM
    tiles_per_seq = ts // TM
    rows = ts // GRID_W
    x = jnp.concatenate([x_prompt.reshape(n_p, D), x_sample.reshape(n_s, D)], axis=0)
    n = n_p + n_s

    n_cond = 16
    cond = jnp.zeros((n_cond, D), F32).at[0].set(c_ctx).at[1:1 + bs].set(c)
    mod = _modulation(cond, ada_w, ada_b).reshape(ada_w.shape[0], n_cond, 1, N_MOD * D)

    head_of = jnp.arange(D, dtype=I32) // HEAD_A
    hsum = (head_of[:, None] == jnp.arange(LANES, dtype=I32)[None, :]).astype(BF16)
    hexp = hsum.T

    gl = rwkv_g1.shape[-1]
    glp = -(-gl // LANES) * LANES
    zb = lambda a: jnp.zeros_like(a)
    wts = {
        "mu": jnp.pad(rwkv_mu[0], ((0, 8 - N_MOD), (0, 0))),
        "wrkv": rwkv_wrkv[0].astype(BF16),
        "g1": jnp.pad(rwkv_g1[0], ((0, 0), (0, glp - gl))).astype(BF16),
        "g2": jnp.pad(rwkv_g2[0], ((0, glp - gl), (0, 0))).astype(BF16),
        "w1": jnp.concatenate([rwkv_w1[0, 0], rwkv_w1[0, 1]], axis=1).astype(BF16),
        "w2": jnp.concatenate([jnp.concatenate([rwkv_w2[0, 0], zb(rwkv_w2[0, 1])], 1),
                               jnp.concatenate([zb(rwkv_w2[0, 0]), rwkv_w2[0, 1]], 1)], 0).astype(BF16),
        "w0": rwkv_w0[0].reshape(1, 2 * D),
        "a1": jnp.concatenate([rwkv_a1[0, 0], rwkv_a1[0, 1]], axis=1).astype(BF16),
        "a2": jnp.concatenate([jnp.concatenate([rwkv_a2[0, 0], zb(rwkv_a2[0, 1])], 1),
                               jnp.concatenate([zb(rwkv_a2[0, 0]), rwkv_a2[0, 1]], 1)], 0).astype(BF16),
        "a0": rwkv_a0[0].reshape(1, 2 * D),
        "k_k": rwkv_k_k[0], "k_a": rwkv_k_a[0], "r_k": rwkv_r_k[0].reshape(D),
        "hsum": hsum, "hexp": hexp,
    }
    r, v, kk, lw, kd, b, gate, bonus = _rwkv_pre(x, mod[0], p_tiles, tiles_per_seq, norm_mix[0], wts)
    s0_lat = _pair_pack(jnp.moveaxis(state_rwkv[:, 0], 1, 0))
    y2, s_fin = _rwkv_scan(r, v, kk, lw, kd, b, s0_lat,
                           _ScanItems(bp, tp // RWKV_CHUNK, bs, ts // RWKV_CHUNK))
    new_state_rwkv = jnp.moveaxis(_pair_unpack(s_fin), 0, 1)[:, None]
    x = _rwkv_post(x, y2, gate, bonus, mod[0], p_tiles, tiles_per_seq, rwkv_ln_w[0], rwkv_ln_b[0],
                   rwkv_wo[0].astype(BF16), hsum, hexp)
    x = _moe_layer(x, mod[0], p_tiles, tiles_per_seq, norm_ffn[0], moe_router[0], moe_bias[0],
                   moe_w_gu, moe_w_down, 0, moe_sh_gu[0].astype(BF16),
                   moe_sh_down[0].astype(BF16), norm_final, False)

    cos_t, sin_t = _rope_tables(rows)
    q, k, vv, rgate = _ret_pre(x, mod[1], p_tiles, tiles_per_seq, norm_mix[1], cos_t, sin_t, ret_w_in[0].astype(BF16))
    log_gamma = jax.nn.log_sigmoid(ret_decay_logit[0].astype(F32))
    o2, r_fin = _ret_scan(log_gamma, q, k, vv, jnp.moveaxis(state_ret[:, 0], 1, 0),
                          _ScanItems(bp, tp // RET_CHUNK, bs, ts // RET_CHUNK))
    new_state_ret = jnp.moveaxis(r_fin, 0, 1)[:, None]
    x = _ret_post(x, o2, rgate, mod[1], p_tiles, tiles_per_seq, ret_w_out[0].astype(BF16))
    x = _moe_layer(x, mod[1], p_tiles, tiles_per_seq, norm_ffn[1], moe_router[1], moe_bias[1],
                   moe_w_gu, moe_w_down, 1, moe_sh_gu[1].astype(BF16),
                   moe_sh_down[1].astype(BF16), norm_final, True)

    return (x[:n_p].reshape(bp, tp, D), x[n_p:].reshape(bs, ts, D), new_state_rwkv, new_state_ret)
```

```python
import functools

import jax
import jax.numpy as jnp
from jax import lax
from jax.experimental import pallas as pl
from jax.experimental.pallas import tpu as pltpu

F32, BF16, I32 = jnp.float32, jnp.bfloat16, jnp.int32

D = 1024
N_MOD = 6
NORM_EPS = 1e-6
GRID_W = 64
HEAD_A = 64
H_A = D // HEAD_A
LNX_EPS = 64e-5
RWKV_CHUNK = 64
N_PAIR = H_A // 2
H_B = 4
DK_B = D // H_B
DV_B = 2 * DK_B
RET_CHUNK = 128
ROPE_BASE = 10000.0
N_EXPERTS = 64
TOP_K = 8
N_GROUPS = 8
TOPK_GROUPS = 4
PER_GROUP = N_EXPERTS // N_GROUPS
D_EXPERT = 256
ROUTED_SCALE = 2.5
EXPERT_BLOCK = 256

TM = 256
LANES = 128
ROW_SUB = D // LANES
VMEM_LIMIT = 56 * 1024 * 1024


def _cparams(n_grid_axes, **kw):
    return pltpu.CompilerParams(dimension_semantics=("arbitrary",) * n_grid_axes, vmem_limit_bytes=VMEM_LIMIT, **kw)


def _bdot(a, b):
    return jnp.dot(a.astype(BF16), b.astype(BF16), preferred_element_type=F32)


def _bdot_nt(a, b):
    return lax.dot_general(a.astype(BF16), b.astype(BF16), (((1,), (1,)), ((), ())), preferred_element_type=F32)


def _bdot_tn(a, b):
    return lax.dot_general(a.astype(BF16), b.astype(BF16), (((0,), (0,)), ((), ())), preferred_element_type=F32)


def _split3(x):
    hi = x.astype(BF16)
    r1 = x - hi.astype(F32)
    mid = r1.astype(BF16)
    lo = (r1 - mid.astype(F32)).astype(BF16)
    return hi, mid, lo


def _dot_f32(a, b):
    ah, am, al = _split3(a)
    bh, bm, bl = _split3(b)
    d = lambda x, y: jnp.dot(x, y, preferred_element_type=F32)
    return d(ah, bh) + (d(ah, bm) + d(am, bh)) + (d(ah, bl) + d(al, bh) + d(am, bm))


def _dot_f32_rhs01(a, m01):
    ah, am, al = _split3(a)
    d = lambda x: jnp.dot(x, m01, preferred_element_type=F32)
    return d(ah) + d(am) + d(al)


def _dot_f32_lhs01(m01, b):
    bh, bm, bl = _split3(b)
    d = lambda x: jnp.dot(m01, x, preferred_element_type=F32)
    return d(bh) + d(bm) + d(bl)


def _norm_mod(x, g, shift, scale):
    ms = jnp.mean(x * x, axis=-1, keepdims=True)
    y = x * lax.rsqrt(ms + NORM_EPS) * g
    return y * (1.0 + scale) + shift


def _silu(x):
    return x * jax.nn.sigmoid(x)


def _rows_load(ref):
    m = ref.shape[0] // ROW_SUB
    return jnp.concatenate([ref[pl.ds(j, m, stride=ROW_SUB), :] for j in range(ROW_SUB)], axis=1)


def _rows_store(ref, val):
    m = val.shape[0]
    for j in range(ROW_SUB):
        ref[pl.ds(j, m, stride=ROW_SUB), :] = val[:, j * LANES:(j + 1) * LANES]


def _row_tile(ref, idx):
    return ref.at[pl.ds(pl.multiple_of(idx * ROW_SUB, ROW_SUB), ROW_SUB)]


def _mod_row(i, p_tiles, tiles_per_seq):
    return jnp.where(i < p_tiles, 0, 1 + (i - p_tiles) // tiles_per_seq)


def _mod_kernel(cond_ref, w_ref, b_ref, o_ref):
    o_ref[...] = _dot_f32(_silu(cond_ref[...]), w_ref[...]) + b_ref[...]


def _modulation(cond, ada_w, ada_b):
    depth, _, n6 = ada_w.shape
    tn = 1536
    return pl.pallas_call(
        _mod_kernel,
        grid=(depth, n6 // tn),
        in_specs=[
            pl.BlockSpec(cond.shape, lambda l, j: (0, 0)),
            pl.BlockSpec((None, D, tn), lambda l, j: (l, 0, j)),
            pl.BlockSpec((None, 1, tn), lambda l, j: (l, 0, j)),
        ],
        out_specs=pl.BlockSpec((None, cond.shape[0], tn), lambda l, j: (l, 0, j)),
        out_shape=jax.ShapeDtypeStruct((depth, cond.shape[0], n6), F32),
        compiler_params=_cparams(2),
        name="adaln_mod",
    )(cond, ada_w, ada_b.reshape(depth, 1, n6))


def _rwkv_pre_kernel(x_ref, xu_ref, xd_ref, mod_ref, g_ref, mu_ref, wrkv_ref, g1_ref, g2_ref, w1_ref, w2_ref,
                     w0_ref, a1_ref, a2_ref, a0_ref, kk_ref, ka_ref, rk_ref, hsum_ref, hexp_ref,
                     r_o, v_o, kk_o, lw_o, kd_o, b_o, gate_o, bonus_o, *, p_tiles, tiles_per_seq):
    i = pl.program_id(0)
    is_p = i < p_tiles
    sub = (i - p_tiles) % tiles_per_seq
    mod = mod_ref[...]
    shift, scale = mod[:, 0:D], mod[:, D:2 * D]
    g = g_ref[...]
    h = _norm_mod(x_ref[...], g, shift, scale)
    hu = _norm_mod(xu_ref[...], g, shift, scale)
    hd = _norm_mod(xd_ref[...], g, shift, scale)

    q = D // 4
    row = lax.broadcasted_iota(I32, (TM, 1), 0)
    per = jnp.where(is_p, TM, GRID_W)
    pos = row & (per - 1)

    def prev1(a):
        return jnp.where(pos == 0, 0.0, pltpu.roll(a, 1, 0))

    def next1(a):
        return jnp.where(pos == per - 1, 0.0, pltpu.roll(a, TM - 1, 0))

    h0, h1, h2, h3 = (h[:, j * q:(j + 1) * q] for j in range(4))
    up = jnp.concatenate([jnp.where(sub == 0, 0.0, hu[:, 2 * q:3 * q]), h2[0:TM - GRID_W]], axis=0)
    down = jnp.concatenate([h3[GRID_W:TM], jnp.where(sub == tiles_per_seq - 1, 0.0, hd[:, 3 * q:4 * q])], axis=0)
    s0 = prev1(h0)
    s1 = jnp.where(is_p, prev1(h1), next1(h1))
    s2 = jnp.where(is_p, next1(h2), up)
    s3 = jnp.where(is_p, next1(h3), down)
    xx = jnp.concatenate([s0, s1, s2, s3], axis=1) - h

    mu = mu_ref[...]
    mix = lambda j: h + xx * mu[j:j + 1]
    r = _bdot(mix(0), wrkv_ref[0])
    k = _bdot(mix(2), wrkv_ref[1])
    v = _bdot(mix(3), wrkv_ref[2])
    gate = _bdot(jax.nn.sigmoid(_bdot(mix(5), g1_ref[...])), g2_ref[...])
    w_all = w0_ref[...] + _bdot(jnp.tanh(_bdot(mix(1), w1_ref[...])), w2_ref[...])
    a_all = jax.nn.sigmoid(a0_ref[...] + _bdot(_bdot(mix(4), a1_ref[...]), a2_ref[...]))

    hsum, hexp = hsum_ref[...], hexp_ref[...]
    head_sum = lambda t: _dot_f32_rhs01(_dot_f32_rhs01(t, hsum), hexp)

    kkr = k * kk_ref[...]
    kk = kkr / jnp.maximum(jnp.sqrt(head_sum(kkr * kkr)), 1e-12)
    ka = ka_ref[...]
    kd_sum = jnp.zeros_like(k)
    for d in range(2):
        wd = w_all[:, d * D:(d + 1) * D]
        z = -wd
        softplus = jnp.maximum(z, 0.0) + jnp.log(1.0 + jnp.exp(-jnp.abs(z)))
        lw_o[d] = -jnp.exp(-softplus - 0.5)
        a = a_all[:, d * D:(d + 1) * D]
        kd = k * (1.0 + (a - 1.0) * ka)
        kd_o[d] = kd
        b_o[d] = kk * a
        kd_sum = kd_sum + kd
    r_o[...] = r
    v_o[...] = v
    kk_o[...] = kk
    gate_o[...] = gate
    bonus_o[...] = head_sum(r * kd_sum * rk_ref[...]) * v


def _rwkv_pre(x, mod, p_tiles, tiles_per_seq, norm_g, wts):
    n = x.shape[0]
    n_tiles = n // TM
    hb = TM // GRID_W
    n_hblk = n // GRID_W
    row = lambda a: a.reshape(1, -1)
    full = lambda a: pl.BlockSpec(a.shape, lambda i: (0,) * a.ndim)
    tok = pl.BlockSpec((TM, D), lambda i: (i, 0))
    tok2 = pl.BlockSpec((2, TM, D), lambda i: (0, i, 0))
    consts = [row(norm_g), wts["mu"], wts["wrkv"], wts["g1"], wts["g2"], wts["w1"], wts["w2"], wts["w0"],
              wts["a1"], wts["a2"], wts["a0"], row(wts["k_k"]), row(wts["k_a"]), row(wts["r_k"]),
              wts["hsum"], wts["hexp"]]
    return pl.pallas_call(
        functools.partial(_rwkv_pre_kernel, p_tiles=p_tiles, tiles_per_seq=tiles_per_seq),
        grid=(n_tiles,),
        in_specs=[
            tok,
            pl.BlockSpec((GRID_W, D), lambda i: (jnp.maximum(i * hb - 1, 0), 0)),
            pl.BlockSpec((GRID_W, D), lambda i: (jnp.minimum(i * hb + hb, n_hblk - 1), 0)),
            pl.BlockSpec((None, 1, N_MOD * D), lambda i: (_mod_row(i, p_tiles, tiles_per_seq), 0, 0)),
        ] + [full(a) for a in consts],
        out_specs=[tok, tok, tok, tok2, tok2, tok2, tok, tok],
        out_shape=[jax.ShapeDtypeStruct((n, D), F32)] * 3 + [jax.ShapeDtypeStruct((2, n, D), F32)] * 3
        + [jax.ShapeDtypeStruct((n, D), F32)] * 2,
        compiler_params=_cparams(1),
        name="rwkv_pre",
    )(x, x, x, mod, *consts)


class _ScanItems:
    def __init__(self, n_ctx, ctx_chunks, n_lat, lat_chunks):
        self.n_ctx, self.ctx_chunks, self.n_lat, self.lat_chunks = n_ctx, ctx_chunks, n_lat, lat_chunks
        self.ctx_items = n_ctx * ctx_chunks
        self.n_items = self.ctx_items + n_lat * lat_chunks

    def decode(self, j):
        is_ctx = j < self.ctx_items
        jl = jnp.maximum(j - self.ctx_items, 0)
        c = jnp.where(is_ctx, j % self.ctx_chunks, jl % self.lat_chunks)
        return is_ctx, c, jnp.where(is_ctx, self.ctx_chunks, self.lat_chunks)

    def block(self, d, j):
        _, c, nc = self.decode(j)
        return jnp.where(d == 0, j, j - c + (nc - 1 - c))

    def ctx_seq(self, j):
        return jnp.minimum(j // self.ctx_chunks, self.n_ctx - 1)

    def lat_seq(self, j):
        return jnp.maximum(j - self.ctx_items, 0) // self.lat_chunks


def _rwkv_scan_kernel(r_ref, v_ref, kk_ref, lw_ref, kd_ref, b_ref, s0_ref, y_ref, sf_ref, s_scr, *, items):
    d = pl.program_id(0)
    is_ctx, c, nc = items.decode(pl.program_id(1))
    rev = d == 1
    ch = RWKV_CHUNK

    @pl.when(c == 0)
    def _():
        s_scr[...] = jnp.where(is_ctx, 0.0, s0_ref[...])

    ti = lax.broadcasted_iota(I32, (ch, ch), 0)
    tj = lax.broadcasted_iota(I32, (ch, ch), 1)
    tri = jnp.where(jnp.where(rev, ti - tj, tj - ti) <= 0, 1.0, 0.0).astype(BF16)
    lw = lw_ref[...]
    cum = _dot_f32_lhs01(tri, lw)
    last = jnp.where(rev, cum[0:1], cum[ch - 1:ch])
    kk = kk_ref[...]
    kd = kd_ref[...]
    b = b_ref[...]
    e_neg = jnp.exp(-cum)
    e_rel = jnp.exp(last - cum)
    at = -kk * jnp.exp(cum - lw)
    rt = r_ref[...] * jnp.exp(cum)
    bt = b * e_neg
    kt = kd * e_neg
    bh = b * e_rel
    kh = kd * e_rel
    g_all = jnp.exp(last)
    v = v_ref[...]

    si = lax.broadcasted_iota(I32, (2 * ch, 2 * ch), 0)
    sj = lax.broadcasted_iota(I32, (2 * ch, 2 * ch), 1)
    same = (si < ch) == (sj < ch)
    ui, uj = si & (ch - 1), sj & (ch - 1)
    order = jnp.where(rev, ui - uj, uj - ui)
    strict = same & (order < 0)
    incl = same & (order <= 0)
    first = lax.broadcasted_iota(I32, (1, LANES), 1) < HEAD_A

    def stack(xp):
        return jnp.concatenate([jnp.where(first, xp, 0.0), jnp.where(first, 0.0, xp)], axis=0)

    pairs = range(N_PAIR)
    sls = [slice(p * LANES, (p + 1) * LANES) for p in pairs]
    s_old = [s_scr[p] for p in pairs]
    m1 = [_bdot_nt(jnp.concatenate([stack(at[:, sl]), stack(rt[:, sl])], axis=0),
                   jnp.concatenate([stack(bt[:, sl]), stack(kt[:, sl])], axis=0)) for sl in sls]
    m2 = [_bdot_nt(jnp.concatenate([at[:, sl], rt[:, sl]], axis=0), s_old[p])
          for p, sl in enumerate(sls)]
    vs = [stack(v[:, sl]) for sl in sls]
    x = [stack(m2[p][0:ch]) + _bdot(jnp.where(strict, m1[p][0:2 * ch, 2 * ch:4 * ch], 0.0), vs[p]) for p in pairs]
    nm = [jnp.where(strict, m1[p][0:2 * ch, 0:2 * ch], 0.0).astype(BF16) for p in pairs]
    for _ in range(5):
        sq_ap = [_bdot(nm[p], jnp.concatenate([nm[p], x[p].astype(BF16)], axis=1)) for p in pairs]
        nm = [sq_ap[p][:, 0:2 * ch].astype(BF16) for p in pairs]
        x = [x[p] + sq_ap[p][:, 2 * ch:4 * ch] for p in pairs]
    x = [x[p] + _bdot(nm[p], x[p]) for p in pairs]
    uv = [jnp.concatenate([x[p], vs[p]], axis=0).astype(BF16) for p in pairs]
    ys = [_bdot(jnp.concatenate([jnp.where(incl, m1[p][2 * ch:4 * ch, 0:2 * ch], 0.0),
                                 jnp.where(incl, m1[p][2 * ch:4 * ch, 2 * ch:4 * ch], 0.0)], axis=1), uv[p])
          for p in pairs]
    s_new = [s_old[p] * g_all[:, sl]
             + _bdot_tn(uv[p], jnp.concatenate([stack(bh[:, sl]), stack(kh[:, sl])], axis=0))
             for p, sl in enumerate(sls)]
    for p, sl in enumerate(sls):
        y_ref[:, sl] = ys[p][0:ch] + ys[p][ch:2 * ch] + m2[p][ch:2 * ch]
    for p in pairs:
        s_scr[p] = s_new[p]

    @pl.when(is_ctx & (c == nc - 1))
    def _():
        sf_ref[...] = s_scr[...]


def _rwkv_scan(r, v, kk, lw, kd, b, s0, items):
    n = r.shape[0]
    ch = RWKV_CHUNK
    tok = pl.BlockSpec((ch, D), lambda d, j: (items.block(d, j), 0))
    tokd = pl.BlockSpec((None, ch, D), lambda d, j: (d, items.block(d, j), 0))
    st = lambda seq: pl.BlockSpec((None, None, N_PAIR, LANES, LANES), lambda d, j: (d, seq(j), 0, 0, 0))
    return pl.pallas_call(
        functools.partial(_rwkv_scan_kernel, items=items),
        grid=(2, items.n_items),
        in_specs=[tok, tok, tok, tokd, tokd, tokd, st(items.lat_seq)],
        out_specs=[tokd, st(items.ctx_seq)],
        out_shape=[jax.ShapeDtypeStruct((2, n, D), F32),
                   jax.ShapeDtypeStruct((2, items.n_ctx, N_PAIR, LANES, LANES), F32)],
        scratch_shapes=[pltpu.VMEM((N_PAIR, LANES, LANES), F32)],
        compiler_params=_cparams(2),
        name="rwkv_scan",
    )(r, v, kk, lw, kd, b, s0)


def _rwkv_post_kernel(x_ref, y_ref, gate_ref, bonus_ref, mod_ref, lnw_ref, lnb_ref, wo_ref, hsum_ref, hexp_ref, o_ref):
    hsum, hexp = hsum_ref[...], hexp_ref[...]
    head_mean = lambda t: _dot_f32_rhs01(_dot_f32_rhs01(t, hsum), hexp) * (1.0 / HEAD_A)
    y = y_ref[0] + y_ref[1]
    yc = y - head_mean(y)
    yn = yc * lax.rsqrt(head_mean(yc * yc) + LNX_EPS)
    z = (yn * lnw_ref[...] + lnb_ref[...] + bonus_ref[...]) * gate_ref[...]
    o_ref[...] = x_ref[...] + mod_ref[:, 2 * D:3 * D] * _bdot(z, wo_ref[...])


def _rwkv_post(x, y, gate, bonus, mod, p_tiles, tiles_per_seq, ln_w, ln_b, wo, hsum, hexp):
    n = x.shape[0]
    full = lambda a: pl.BlockSpec(a.shape, lambda i: (0,) * a.ndim)
    tok = pl.BlockSpec((TM, D), lambda i: (i, 0))
    consts = [ln_w.reshape(1, D), ln_b.reshape(1, D), wo, hsum, hexp]
    return pl.pallas_call(
        _rwkv_post_kernel,
        grid=(n // TM,),
        in_specs=[tok, pl.BlockSpec((2, TM, D), lambda i: (0, i, 0)), tok, tok,
                  pl.BlockSpec((None, 1, N_MOD * D), lambda i: (_mod_row(i, p_tiles, tiles_per_seq), 0, 0))]
        + [full(a) for a in consts],
        out_specs=tok,
        out_shape=jax.ShapeDtypeStruct((n, D), F32),
        compiler_params=_cparams(1),
        name="rwkv_post",
    )(x, y, gate, bonus, mod, *consts)


def _ret_pre_kernel(x_ref, mod_ref, g_ref, cos_ref, sin_ref, win_ref, q_o, k_o, v_o, gate_o):
    mod = mod_ref[...]
    h = _norm_mod(x_ref[...], g_ref[...], mod[:, 0:D], mod[:, D:2 * D])
    proj = _bdot(h, win_ref[...])
    cos, sin = cos_ref[...], sin_ref[...]

    def rope(t):
        outs = []
        for j in range(D // LANES):
            tj = t[:, j * LANES:(j + 1) * LANES]
            cj = cos[:, (j % 2) * LANES:(j % 2 + 1) * LANES]
            sj = sin[:, (j % 2) * LANES:(j % 2 + 1) * LANES]
            outs.append(tj * cj + pltpu.roll(tj, LANES // 2, 1) * sj)
        return jnp.concatenate(outs, axis=1)

    q_o[...] = rope(proj[:, 0:D]).astype(BF16)
    k_o[...] = rope(proj[:, D:2 * D] * (DK_B ** -0.5)).astype(BF16)
    v_o[...] = proj[:, 2 * D:4 * D].astype(BF16)
    gate_o[...] = proj[:, 4 * D:6 * D]


def _ret_pre(x, mod, p_tiles, tiles_per_seq, norm_g, cos_t, sin_t, w_in):
    n = x.shape[0]
    full = lambda a: pl.BlockSpec(a.shape, lambda i: (0,) * a.ndim)
    tok = lambda w: pl.BlockSpec((TM, w), lambda i: (i, 0))
    tab = pl.BlockSpec((TM, DK_B), lambda i: (jnp.where(i < p_tiles, 0, 1 + (i - p_tiles) % tiles_per_seq), 0))
    return pl.pallas_call(
        _ret_pre_kernel,
        grid=(n // TM,),
        in_specs=[tok(D), pl.BlockSpec((None, 1, N_MOD * D), lambda i: (_mod_row(i, p_tiles, tiles_per_seq), 0, 0)),
                  full(norm_g.reshape(1, D)), tab, tab, full(w_in)],
        out_specs=[tok(D), tok(D), tok(2 * D), tok(2 * D)],
        out_shape=[jax.ShapeDtypeStruct((n, D), BF16), jax.ShapeDtypeStruct((n, D), BF16),
                   jax.ShapeDtypeStruct((n, 2 * D), BF16), jax.ShapeDtypeStruct((n, 2 * D), F32)],
        compiler_params=_cparams(1),
        name="ret_pre",
    )(x, mod, norm_g.reshape(1, D), cos_t, sin_t, w_in)


def _ret_scan_kernel(lg_ref, q_ref, k_ref, v_ref, s0_ref, o_ref, sf_ref, s_scr, *, items):
    d = pl.program_id(0)
    is_ctx, c, nc = items.decode(pl.program_id(1))
    rev = d == 1
    ch = RET_CHUNK

    @pl.when(c == 0)
    def _():
        s_scr[...] = jnp.where(is_ctx, 0.0, s0_ref[...])

    ti = lax.broadcasted_iota(I32, (ch, ch), 0)
    tj = lax.broadcasted_iota(I32, (ch, ch), 1)
    rel = jnp.where(rev, tj - ti, ti - tj).astype(F32)
    steps_q = jnp.where(rev, ch - ti, ti + 1).astype(F32)
    steps_k = jnp.where(rev, ti, ch - 1 - ti).astype(F32)
    for hd in range(H_B):
        lg = lg_ref[d, hd]
        mask = jnp.where(rel >= 0, jnp.exp(jnp.maximum(rel, 0.0) * lg), 0.0)
        q_dec = jnp.exp(steps_q * lg)
        k_dec = jnp.exp(steps_k * lg)
        chunk_dec = jnp.exp(jnp.full((1, DV_B), float(ch), F32) * lg)
        qh = q_ref[:, hd * DK_B:(hd + 1) * DK_B]
        kh = k_ref[:, hd * DK_B:(hd + 1) * DK_B]
        vh = v_ref[:, hd * DV_B:(hd + 1) * DV_B]
        s = s_scr[hd]
        scores = _bdot_nt(qh, kh) * mask
        inner = _bdot(scores, vh)
        cross = _bdot(qh, s) * jnp.concatenate([q_dec] * (DV_B // ch), axis=1)
        o_ref[:, hd * DV_B:(hd + 1) * DV_B] = inner + cross
        kdec = kh.astype(F32) * jnp.concatenate([k_dec] * (DK_B // ch), axis=1)
        s_scr[hd] = s * chunk_dec + _bdot_tn(kdec, vh)

    @pl.when(is_ctx & (c == nc - 1))
    def _():
        sf_ref[...] = s_scr[...]


def _ret_scan(log_gamma, q, k, v, s0, items):
    n = q.shape[0]
    ch = RET_CHUNK
    tok = lambda w: pl.BlockSpec((ch, w), lambda d, j: (items.block(d, j), 0))
    st = lambda seq: pl.BlockSpec((None, None, H_B, DK_B, DV_B), lambda d, j: (d, seq(j), 0, 0, 0))
    return pl.pallas_call(
        functools.partial(_ret_scan_kernel, items=items),
        grid=(2, items.n_items),
        in_specs=[pl.BlockSpec(memory_space=pltpu.SMEM), tok(D), tok(D), tok(2 * D), st(items.lat_seq)],
        out_specs=[pl.BlockSpec((None, ch, 2 * D), lambda d, j: (d, items.block(d, j), 0)), st(items.ctx_seq)],
        out_shape=[jax.ShapeDtypeStruct((2, n, 2 * D), F32),
                   jax.ShapeDtypeStruct((2, items.n_ctx, H_B, DK_B, DV_B), F32)],
        scratch_shapes=[pltpu.VMEM((H_B, DK_B, DV_B), F32)],
        compiler_params=_cparams(2),
        name="ret_scan",
    )(log_gamma, q, k, v, s0)


def _ret_post_kernel(x_ref, o_ref, gate_ref, mod_ref, wout_ref, out_ref):
    o = o_ref[0] + o_ref[1]
    parts = []
    for hd in range(H_B):
        oh = o[:, hd * DV_B:(hd + 1) * DV_B]
        parts.append(oh * lax.rsqrt(jnp.mean(oh * oh, axis=-1, keepdims=True) + NORM_EPS))
    y = _silu(gate_ref[...]) * jnp.concatenate(parts, axis=1)
    out_ref[...] = x_ref[...] + mod_ref[:, 2 * D:3 * D] * _bdot(y, wout_ref[...])


def _ret_post(x, o, gate, mod, p_tiles, tiles_per_seq, w_out):
    n = x.shape[0]
    tok = lambda w: pl.BlockSpec((TM, w), lambda i: (i, 0))
    return pl.pallas_call(
        _ret_post_kernel,
        grid=(n // TM,),
        in_specs=[tok(D), pl.BlockSpec((2, TM, 2 * D), lambda i: (0, i, 0)), tok(2 * D),
                  pl.BlockSpec((None, 1, N_MOD * D), lambda i: (_mod_row(i, p_tiles, tiles_per_seq), 0, 0)),
                  pl.BlockSpec(w_out.shape, lambda i: (0, 0))],
        out_specs=tok(D),
        out_shape=jax.ShapeDtypeStruct((n, D), F32),
        compiler_params=_cparams(1),
        name="ret_post",
    )(x, o, gate, mod, w_out)


def _moe_route_kernel(x_ref, mod_ref, g_ref, router_ref, bias_ref, h_o, e_o, w_o, p_o, cnt_o, carry):
    i = pl.program_id(0)

    @pl.when(i == 0)
    def _():
        carry[...] = jnp.zeros_like(carry)

    mod = mod_ref[...]
    h = _norm_mod(x_ref[...], g_ref[...], mod[:, 3 * D:4 * D], mod[:, 4 * D:5 * D])
    _rows_store(h_o, h)
    lane =lax.broadcasted_iota(I32, (TM, LANES), 1)
    valid = lane < N_EXPERTS
    neg = -jnp.inf
    scores = jax.nn.sigmoid(_dot_f32(h, router_ref[...]))
    biased = jnp.where(valid, scores + bias_ref[...], neg)

    def group_reduce(t, op):
        s = 1
        while s < PER_GROUP:
            partner = jnp.where((lane & s) == 0, pltpu.roll(t, LANES - s, 1), pltpu.roll(t, s, 1))
            t = op(t, partner)
            s *= 2
        return t

    lane_f = lane.astype(F32)
    group_f = jnp.floor(lane_f * (1.0 / PER_GROUP))

    def first_lane_of_max(t):
        m = jnp.max(t, axis=-1, keepdims=True)
        return jnp.min(jnp.where(t == m, lane_f, float(LANES)), axis=-1, keepdims=True)

    m1 = group_reduce(biased, jnp.maximum)
    first1 = group_reduce(jnp.where(biased == m1, lane_f, float(LANES)), jnp.minimum)
    m2 = group_reduce(jnp.where(lane_f == first1, neg, biased), jnp.maximum)
    gscore = jnp.where(valid, m1 + m2, neg)
    cand = jnp.full((TM, LANES), neg, F32)
    for _ in range(TOPK_GROUPS):
        gsel = group_f == jnp.floor(first_lane_of_max(gscore) * (1.0 / PER_GROUP))
        cand = jnp.where(gsel, biased, cand)
        gscore = jnp.where(gsel, neg, gscore)
    hits = []
    sel01 = jnp.zeros((TM, LANES), F32)
    e_cols = jnp.zeros((TM, LANES), F32)
    for j in range(TOP_K):
        fl = first_lane_of_max(cand)
        hit = lane_f == fl
        hits.append(hit)
        sel01 = jnp.where(hit, 1.0, sel01)
        cand = jnp.where(hit, neg, cand)
        e_cols = jnp.where(lane == j, fl, e_cols)
    wsum = jnp.sum(sel01 * scores, axis=-1, keepdims=True)

    ri = lax.broadcasted_iota(I32, (TM, TM), 0)
    rj = lax.broadcasted_iota(I32, (TM, TM), 1)
    below = jnp.where(rj < ri, 1.0, 0.0).astype(BF16)
    rank = jnp.dot(below, sel01.astype(BF16), preferred_element_type=F32) + carry[...]
    carry[...] = carry[...] + jnp.sum(sel01, axis=0, keepdims=True)
    w_cols = jnp.zeros((TM, LANES), F32)
    p_cols = jnp.zeros((TM, LANES), F32)
    for j in range(TOP_K):
        wj = jnp.sum(jnp.where(hits[j], scores, 0.0), axis=-1, keepdims=True)
        w_cols = jnp.where(lane == j, wj / wsum * ROUTED_SCALE, w_cols)
        p_cols = jnp.where(lane == j, jnp.sum(jnp.where(hits[j], rank, 0.0), axis=-1, keepdims=True), p_cols)
    e_o[...] = e_cols.astype(I32)
    w_o[...] = w_cols
    p_o[...] = p_cols.astype(I32)
    cnt_o[...] = jnp.broadcast_to(carry[...], cnt_o.shape)


def _moe_route(x, mod, p_tiles, tiles_per_seq, norm_g, router, bias):
    n = x.shape[0]
    full = lambda a: pl.BlockSpec(a.shape, lambda i: (0,) * a.ndim)
    tok = lambda w: pl.BlockSpec((TM, w), lambda i: (i, 0))
    router_p = jnp.pad(router, ((0, 0), (0, LANES - N_EXPERTS)))
    bias_p = jnp.pad(bias, (0, LANES - N_EXPERTS)).reshape(1, LANES)
    return pl.pallas_call(
        _moe_route_kernel,
        grid=(n // TM,),
        in_specs=[tok(D), pl.BlockSpec((None, 1, N_MOD * D), lambda i: (_mod_row(i, p_tiles, tiles_per_seq), 0, 0)),
                  full(norm_g.reshape(1, D)), full(router_p), full(bias_p)],
        out_specs=[pl.BlockSpec((TM * ROW_SUB, LANES), lambda i: (i, 0)), tok(LANES), tok(LANES), tok(LANES),
                   pl.BlockSpec((8, LANES), lambda i: (0, 0))],
        out_shape=[jax.ShapeDtypeStruct((n * ROW_SUB, LANES), F32), jax.ShapeDtypeStruct((n, LANES), I32),
                   jax.ShapeDtypeStruct((n, LANES), F32), jax.ShapeDtypeStruct((n, LANES), I32),
                   jax.ShapeDtypeStruct((8, LANES), F32)],
        scratch_shapes=[pltpu.VMEM((1, LANES), F32)],
        compiler_params=_cparams(1),
        name="moe_route",
    )(x, mod, norm_g.reshape(1, D), router_p, bias_p)


DISPATCH_TOK = 512


def _moe_dispatch_kernel(pad_lo_ref, pad_n_ref, dest_ref, h_ref, xs_hbm, zero_scr, sem):
    @pl.when(pl.program_id(0) == 0)
    def _():
        zero_scr[...] = jnp.zeros_like(zero_scr)

        def pad_copy(e, s):
            return pltpu.make_async_copy(zero_scr, _row_tile(xs_hbm, pad_lo_ref[e] + s), sem)

        def per_expert(e, carry):
            def issue(s, c):
                pad_copy(e, s).start()
                return c

            def drain(s, c):
                pad_copy(e, s).wait()
                return c

            lax.fori_loop(0, pad_n_ref[e], issue, 0)
            lax.fori_loop(0, pad_n_ref[e], drain, 0)
            return carry

        lax.fori_loop(0, N_EXPERTS, per_expert, 0)

    def copy(t, kq):
        return pltpu.make_async_copy(_row_tile(h_ref, t), _row_tile(xs_hbm, dest_ref[t * TOP_K + kq]), sem)

    def issue(t, carry):
        for kq in range(TOP_K):
            copy(t, kq).start(priority=kq % 2)
        return carry

    def drain(t, carry):
        for kq in range(TOP_K):
            copy(t, kq).wait()
        return carry

    lax.fori_loop(0, DISPATCH_TOK, issue, 0)
    lax.fori_loop(0, DISPATCH_TOK, drain, 0)


def _moe_dispatch(pad_lo, pad_n, dest_flat, h, n_slots):
    n = h.shape[0] // ROW_SUB
    grid_spec = pltpu.PrefetchScalarGridSpec(
        num_scalar_prefetch=2,
        grid=(n // DISPATCH_TOK,),
        in_specs=[pl.BlockSpec((DISPATCH_TOK * TOP_K,), lambda i, *_: (i,), memory_space=pltpu.SMEM),
                  pl.BlockSpec((DISPATCH_TOK * ROW_SUB, LANES), lambda i, *_: (i, 0))],
        out_specs=pl.BlockSpec(memory_space=pl.ANY),
        scratch_shapes=[pltpu.VMEM((ROW_SUB, LANES), F32), pltpu.SemaphoreType.DMA],
    )
    return pl.pallas_call(
        _moe_dispatch_kernel,
        grid_spec=grid_spec,
        out_shape=jax.ShapeDtypeStruct((n_slots * ROW_SUB, LANES), F32),
        compiler_params=_cparams(1, has_side_effects=True),
        name="moe_dispatch",
    )(pad_lo, pad_n, dest_flat, h)


def _moe_expert_kernel(be_ref, xs_ref, wgu_ref, wdn_ref, o_ref, wgu_bf, wdn_bf):
    i = pl.program_id(0)

    @pl.when((i == 0) | (be_ref[i] != be_ref[jnp.maximum(i - 1, 0)]))
    def _():
        wgu_bf[...] = wgu_ref[...].astype(BF16)
        wdn_bf[...] = wdn_ref[...].astype(BF16)

    gu = _bdot(_rows_load(xs_ref), wgu_bf[...])
    act = _silu(gu[:, 0:D_EXPERT]) * gu[:, D_EXPERT:2 * D_EXPERT]
    _rows_store(o_ref, _bdot(act, wdn_bf[...]))


def _moe_expert(block_e, xs, w_gu, w_down, layer):
    n_slots = xs.shape[0] // ROW_SUB
    n_blocks = n_slots // EXPERT_BLOCK
    rows = pl.BlockSpec((EXPERT_BLOCK * ROW_SUB, LANES), lambda i, be: (i, 0))
    grid_spec = pltpu.PrefetchScalarGridSpec(
        num_scalar_prefetch=1,
        grid=(n_blocks,),
        in_specs=[rows,
                  pl.BlockSpec((None, None, D, 2 * D_EXPERT), lambda i, be: (layer, be[i], 0, 0)),
                  pl.BlockSpec((None, None, D_EXPERT, D), lambda i, be: (layer, be[i], 0, 0))],
        out_specs=rows,
        scratch_shapes=[pltpu.VMEM((D, 2 * D_EXPERT), BF16), pltpu.VMEM((D_EXPERT, D), BF16)],
    )
    return pl.pallas_call(
        _moe_expert_kernel,
        grid_spec=grid_spec,
        out_shape=jax.ShapeDtypeStruct((n_slots * ROW_SUB, LANES), F32),
        compiler_params=_cparams(1),
        name="moe_expert",
    )(block_e, xs, w_gu, w_down)


COMBINE_TOK = 128


def _moe_combine_kernel(dest_ref, dest_next_ref, x_ref, h_ref, w_ref, mod_ref, shgu_ref, shdn_ref, gfin_ref, ys_hbm,
                        *refs, final_norm, ctx_tiles):
    o_refs, (buf, sems) = refs[:-2], refs[-2:]
    i = pl.program_id(0)
    slot = i % 2

    def copy(idx_ref, s, t, kq):
        return pltpu.make_async_copy(_row_tile(ys_hbm, idx_ref[t * TOP_K + kq]), _row_tile(buf.at[s, kq], t),
                                     sems.at[s])

    def fetch(idx_ref, s):
        def body(t, carry):
            for kq in range(TOP_K):
                copy(idx_ref, s, t, kq).start(priority=kq % 2)
            return carry
        lax.fori_loop(0, COMBINE_TOK, body, 0, unroll=2)

    @pl.when(i == 0)
    def _():
        fetch(dest_ref, slot)

    @pl.when(i + 1 < pl.num_programs(0))
    def _():
        fetch(dest_next_ref, 1 - slot)

    gu = _bdot(_rows_load(h_ref), shgu_ref[...])
    acc = _bdot(_silu(gu[:, 0:D_EXPERT]) * gu[:, D_EXPERT:2 * D_EXPERT], shdn_ref[...])

    def drain(t, carry):
        for kq in range(TOP_K):
            copy(dest_ref, slot, t, kq).wait()
        return carry

    lax.fori_loop(0, COMBINE_TOK, drain, 0)
    w = w_ref[...]
    for kq in range(TOP_K):
        acc = acc + _rows_load(buf.at[slot, kq]) * w[:, kq:kq + 1]
    out = x_ref[...] + mod_ref[:, 5 * D:6 * D] * acc
    if final_norm:
        out = out * lax.rsqrt(jnp.mean(out * out, axis=-1, keepdims=True) + NORM_EPS) * gfin_ref[...]

        @pl.when(i < ctx_tiles)
        def _():
            o_refs[0][...] = out

        @pl.when(i >= ctx_tiles)
        def _():
            o_refs[1][...] = out
    else:
        o_refs[0][...] = out


def _moe_combine(dest_flat, x, h, w_cols, mod, p_tiles, tiles_per_seq, sh_gu, sh_down, g_final, ys, final_norm):
    n = x.shape[0]
    ratio = TM // COMBINE_TOK
    tok = lambda w: pl.BlockSpec((COMBINE_TOK, w), lambda i: (i, 0))
    full = lambda a: pl.BlockSpec(a.shape, lambda i: (0,) * a.ndim)
    n_tiles = n // COMBINE_TOK
    ctx_tiles = p_tiles * ratio
    if final_norm:
        out_specs = [pl.BlockSpec((COMBINE_TOK, D), lambda i: (jnp.minimum(i, ctx_tiles - 1), 0)),
                     pl.BlockSpec((COMBINE_TOK, D), lambda i: (jnp.maximum(i - ctx_tiles, 0), 0))]
        out_shape = [jax.ShapeDtypeStruct((ctx_tiles * COMBINE_TOK, D), F32),
                     jax.ShapeDtypeStruct((n - ctx_tiles * COMBINE_TOK, D), F32)]
    else:
        out_specs, out_shape = tok(D), jax.ShapeDtypeStruct((n, D), F32)
    return pl.pallas_call(
        functools.partial(_moe_combine_kernel, final_norm=final_norm, ctx_tiles=ctx_tiles),
        grid=(n_tiles,),
        in_specs=[pl.BlockSpec((COMBINE_TOK * TOP_K,), lambda i: (i,), memory_space=pltpu.SMEM),
                  pl.BlockSpec((COMBINE_TOK * TOP_K,), lambda i: (jnp.minimum(i + 1, n_tiles - 1),),
                               memory_space=pltpu.SMEM),
                  tok(D), pl.BlockSpec((COMBINE_TOK * ROW_SUB, LANES), lambda i: (i, 0)), tok(LANES),
                  pl.BlockSpec((None, 1, N_MOD * D), lambda i: (_mod_row(i // ratio, p_tiles, tiles_per_seq), 0, 0)),
                  full(sh_gu), full(sh_down), full(g_final),
                  pl.BlockSpec(memory_space=pl.ANY)],
        out_specs=out_specs,
        out_shape=out_shape,
        scratch_shapes=[pltpu.VMEM((2, TOP_K, COMBINE_TOK * ROW_SUB, LANES), F32), pltpu.SemaphoreType.DMA((2,))],
        compiler_params=_cparams(1),
        name="moe_combine",
    )(dest_flat, dest_flat, x, h, w_cols, mod, sh_gu, sh_down, g_final, ys)


def _moe_layer(x, mod, p_tiles, tiles_per_seq, norm_g, router, bias, w_gu, w_down, layer, sh_gu, sh_down, g_final,
               final_norm):
    n = x.shape[0]
    h, e_cols, w_cols, p_cols, counts = _moe_route(x, mod, p_tiles, tiles_per_seq, norm_g, router, bias)
    counts = counts[0, :N_EXPERTS].astype(I32)
    padded = (counts + EXPERT_BLOCK - 1) // EXPERT_BLOCK * EXPERT_BLOCK
    pad_end = jnp.cumsum(padded)
    pad_start = pad_end - padded
    chosen = e_cols[:, :TOP_K, None] == jnp.arange(N_EXPERTS, dtype=I32)
    dest = (jnp.sum(jnp.where(chosen, pad_start, 0), axis=-1) + p_cols[:, :TOP_K]).reshape(-1)
    n_blocks = n * TOP_K // EXPERT_BLOCK + N_EXPERTS
    n_slots = n_blocks * EXPERT_BLOCK
    block_start = jnp.arange(n_blocks, dtype=I32) * EXPERT_BLOCK
    block_e = jnp.minimum(jnp.sum((pad_end[None, :] <= block_start[:, None]).astype(I32), axis=1), N_EXPERTS - 1)
    pad_n = (padded - counts).at[N_EXPERTS - 1].add(n_slots - pad_end[N_EXPERTS - 1])
    xs = _moe_dispatch(pad_start + counts, pad_n, dest, h, n_slots)
    ys = _moe_expert(block_e, xs, w_gu, w_down, layer)
    return _moe_combine(dest, x, h, w_cols, mod, p_tiles, tiles_per_seq, sh_gu, sh_down, g_final.reshape(1, D), ys,
                        final_norm)


def _pair_pack(s):
    lead = s.shape[:-3]
    s = s.reshape(lead + (N_PAIR, 2, HEAD_A, HEAD_A))
    z = jnp.zeros_like(s[..., 0, :, :])
    top = jnp.concatenate([s[..., 0, :, :], z], axis=-1)
    bot = jnp.concatenate([z, s[..., 1, :, :]], axis=-1)
    return jnp.concatenate([top, bot], axis=-2)


def _pair_unpack(s):
    a = s[..., :HEAD_A, :HEAD_A]
    b = s[..., HEAD_A:, HEAD_A:]
    return jnp.stack([a, b], axis=-3).reshape(s.shape[:-3] + (H_A, HEAD_A, HEAD_A))


def _rope_tables(rows):
    half = DK_B // 2
    n_freq = half // 2
    inv = ROPE_BASE ** (-jnp.arange(n_freq, dtype=F32) / n_freq)
    pos_r = jnp.repeat(jnp.arange(rows, dtype=F32), GRID_W)
    pos_c = jnp.tile(jnp.arange(GRID_W, dtype=F32), rows)

    def tab(pos):
        ang = pos[:, None] * inv[None, :]
        c, s = jnp.cos(ang), jnp.sin(ang)
        return jnp.concatenate([c, c], -1), jnp.concatenate([-s, s], -1)

    cr, sr = tab(pos_r)
    cc, sc = tab(pos_c)
    cos = jnp.concatenate([cr, cc], -1)
    sin = jnp.concatenate([sr, sc], -1)
    cos = jnp.concatenate([jnp.ones((TM, DK_B), F32), cos], 0)
    sin = jnp.concatenate([jnp.zeros((TM, DK_B), F32), sin], 0)
    return cos, sin


def kernel(x_prompt, x_sample, state_rwkv, state_ret, c, c_ctx, ada_w, ada_b, norm_mix, norm_ffn, norm_final, rwkv_mu, rwkv_w0, rwkv_w1, rwkv_w2, rwkv_a0, rwkv_a1, rwkv_a2, rwkv_wrkv, rwkv_wo, rwkv_g1, rwkv_g2, rwkv_k_k, rwkv_k_a, rwkv_r_k, rwkv_ln_w, rwkv_ln_b, ret_w_in, ret_w_out, ret_decay_logit, moe_router, moe_bias, moe_w_gu, moe_w_down, moe_sh_gu, moe_sh_down):
    bp, tp, _ = x_prompt.shape
    bs, ts, _ = x_sample.shape
    n_p, n_s = bp * tp, bs * ts
    assert tp == TM and ts % TM == 0 and TM % GRID_W == 0
    p_tiles = n_p // TM
    tiles_per_seq = ts // TM
    rows = ts // GRID_W
    x = jnp.concatenate([x_prompt.reshape(n_p, D), x_sample.reshape(n_s, D)], axis=0)
    n = n_p + n_s

    n_cond = 16
    cond = jnp.zeros((n_cond, D), F32).at[0].set(c_ctx).at[1:1 + bs].set(c)
    mod = _modulation(cond, ada_w, ada_b).reshape(ada_w.shape[0], n_cond, 1, N_MOD * D)

    head_of = jnp.arange(D, dtype=I32) // HEAD_A
    hsum = (head_of[:, None] == jnp.arange(LANES, dtype=I32)[None, :]).astype(BF16)
    hexp = hsum.T

    gl = rwkv_g1.shape[-1]
    glp = -(-gl // LANES) * LANES
    zb = lambda a: jnp.zeros_like(a)
    wts = {
        "mu": jnp.pad(rwkv_mu[0], ((0, 8 - N_MOD), (0, 0))),
        "wrkv": rwkv_wrkv[0].astype(BF16),
        "g1": jnp.pad(rwkv_g1[0], ((0, 0), (0, glp - gl))).astype(BF16),
        "g2": jnp.pad(rwkv_g2[0], ((0, glp - gl), (0, 0))).astype(BF16),
        "w1": jnp.concatenate([rwkv_w1[0, 0], rwkv_w1[0, 1]], axis=1).astype(BF16),
        "w2": jnp.concatenate([jnp.concatenate([rwkv_w2[0, 0], zb(rwkv_w2[0, 1])], 1),
                               jnp.concatenate([zb(rwkv_w2[0, 0]), rwkv_w2[0, 1]], 1)], 0).astype(BF16),
        "w0": rwkv_w0[0].reshape(1, 2 * D),
        "a1": jnp.concatenate([rwkv_a1[0, 0], rwkv_a1[0, 1]], axis=1).astype(BF16),
        "a2": jnp.concatenate([jnp.concatenate([rwkv_a2[0, 0], zb(rwkv_a2[0, 1])], 1),
                               jnp.concatenate([zb(rwkv_a2[0, 0]), rwkv_a2[0, 1]], 1)], 0).astype(BF16),
        "a0": rwkv_a0[0].reshape(1, 2 * D),
        "k_k": rwkv_k_k[0], "k_a": rwkv_k_a[0], "r_k": rwkv_r_k[0].reshape(D),
        "hsum": hsum, "hexp": hexp,
    }
    r, v, kk, lw, kd, b, gate, bonus = _rwkv_pre(x, mod[0], p_tiles, tiles_per_seq, norm_mix[0], wts)
    s0_lat = _pair_pack(jnp.moveaxis(state_rwkv[:, 0], 1, 0))
    y2, s_fin = _rwkv_scan(r, v, kk, lw, kd, b, s0_lat,
                           _ScanItems(bp, tp // RWKV_CHUNK, bs, ts // RWKV_CHUNK))
    new_state_rwkv = jnp.moveaxis(_pair_unpack(s_fin), 0, 1)[:, None]
    x = _rwkv_post(x, y2, gate, bonus, mod[0], p_tiles, tiles_per_seq, rwkv_ln_w[0], rwkv_ln_b[0],
                   rwkv_wo[0].astype(BF16), hsum, hexp)
    x = _moe_layer(x, mod[0], p_tiles, tiles_per_seq, norm_ffn[0], moe_router[0], moe_bias[0],
                   moe_w_gu, moe_w_down, 0, moe_sh_gu[0].astype(BF16),
                   moe_sh_down[0].astype(BF16), norm_final, False)

    cos_t, sin_t = _rope_tables(rows)
    q, k, vv, rgate = _ret_pre(x, mod[1], p_tiles, tiles_per_seq, norm_mix[1], cos_t, sin_t, ret_w_in[0].astype(BF16))
    log_gamma = jax.nn.log_sigmoid(ret_decay_logit[0].astype(F32))
    o2, r_fin = _ret_scan(log_gamma, q, k, vv, jnp.moveaxis(state_ret[:, 0], 1, 0),
                          _ScanItems(bp, tp // RET_CHUNK, bs, ts // RET_CHUNK))
    new_state_ret = jnp.moveaxis(r_fin, 0, 1)[:, None]
    x = _ret_post(x, o2, rgate, mod[1], p_tiles, tiles_per_seq, ret_w_out[0].astype(BF16))
    y_ctx, y_lat = _moe_layer(x, mod[1], p_tiles, tiles_per_seq, norm_ffn[1], moe_router[1], moe_bias[1],
                              moe_w_gu, moe_w_down, 1, moe_sh_gu[1].astype(BF16),
                              moe_sh_down[1].astype(BF16), norm_final, True)

    return (y_ctx.reshape(bp, tp, D), y_lat.reshape(bs, ts, D), new_state_rwkv, new_state_ret)
```

```python
import functools

import jax
import jax.numpy as jnp
from jax import lax
from jax.experimental import pallas as pl
from jax.experimental.pallas import tpu as pltpu

F32, BF16, I32 = jnp.float32, jnp.bfloat16, jnp.int32

D = 1024
N_MOD = 6
NORM_EPS = 1e-6
GRID_W = 64
HEAD_A = 64
H_A = D // HEAD_A
LNX_EPS = 64e-5
RWKV_CHUNK = 64
N_PAIR = H_A // 2
H_B = 4
DK_B = D // H_B
DV_B = 2 * DK_B
RET_CHUNK = 128
ROPE_BASE = 10000.0
N_EXPERTS = 64
TOP_K = 8
N_GROUPS = 8
TOPK_GROUPS = 4
PER_GROUP = N_EXPERTS // N_GROUPS
D_EXPERT = 256
ROUTED_SCALE = 2.5
EXPERT_BLOCK = 256

TM = 256
LANES = 128
ROW_SUB = D // LANES
VMEM_LIMIT = 56 * 1024 * 1024


def _cparams(n_grid_axes, **kw):
    return pltpu.CompilerParams(dimension_semantics=("arbitrary",) * n_grid_axes, vmem_limit_bytes=VMEM_LIMIT, **kw)


def _bdot(a, b):
    return jnp.dot(a.astype(BF16), b.astype(BF16), preferred_element_type=F32)


def _bdot_nt(a, b):
    return lax.dot_general(a.astype(BF16), b.astype(BF16), (((1,), (1,)), ((), ())), preferred_element_type=F32)


def _bdot_tn(a, b):
    return lax.dot_general(a.astype(BF16), b.astype(BF16), (((0,), (0,)), ((), ())), preferred_element_type=F32)


def _split3(x):
    hi = x.astype(BF16)
    r1 = x - hi.astype(F32)
    mid = r1.astype(BF16)
    lo = (r1 - mid.astype(F32)).astype(BF16)
    return hi, mid, lo


def _dot_f32(a, b):
    ah, am, al = _split3(a)
    bh, bm, bl = _split3(b)
    d = lambda x, y: jnp.dot(x, y, preferred_element_type=F32)
    return d(ah, bh) + (d(ah, bm) + d(am, bh)) + (d(ah, bl) + d(al, bh) + d(am, bm))


def _dot_f32_rhs01(a, m01):
    ah, am, al = _split3(a)
    d = lambda x: jnp.dot(x, m01, preferred_element_type=F32)
    return d(ah) + d(am) + d(al)


def _dot_f32_lhs01(m01, b):
    bh, bm, bl = _split3(b)
    d = lambda x: jnp.dot(m01, x, preferred_element_type=F32)
    return d(bh) + d(bm) + d(bl)


def _norm_mod(x, g, shift, scale):
    ms = jnp.mean(x * x, axis=-1, keepdims=True)
    y = x * lax.rsqrt(ms + NORM_EPS) * g
    return y * (1.0 + scale) + shift


def _silu(x):
    return x * jax.nn.sigmoid(x)


def _rows_load(ref):
    m = ref.shape[0] // ROW_SUB
    return jnp.concatenate([ref[pl.ds(j, m, stride=ROW_SUB), :] for j in range(ROW_SUB)], axis=1)


def _rows_store(ref, val):
    m = val.shape[0]
    for j in range(ROW_SUB):
        ref[pl.ds(j, m, stride=ROW_SUB), :] = val[:, j * LANES:(j + 1) * LANES]


def _row_tile(ref, idx):
    return ref.at[pl.ds(pl.multiple_of(idx * ROW_SUB, ROW_SUB), ROW_SUB)]


def _mod_row(i, p_tiles, tiles_per_seq):
    return jnp.where(i < p_tiles, 0, 1 + (i - p_tiles) // tiles_per_seq)


def _mod_kernel(cond_ref, w_ref, b_ref, o_ref):
    o_ref[...] = _dot_f32(_silu(cond_ref[...]), w_ref[...]) + b_ref[...]


def _modulation(cond, ada_w, ada_b):
    depth, _, n6 = ada_w.shape
    tn = 1536
    return pl.pallas_call(
        _mod_kernel,
        grid=(depth, n6 // tn),
        in_specs=[
            pl.BlockSpec(cond.shape, lambda l, j: (0, 0)),
            pl.BlockSpec((None, D, tn), lambda l, j: (l, 0, j)),
            pl.BlockSpec((None, 1, tn), lambda l, j: (l, 0, j)),
        ],
        out_specs=pl.BlockSpec((None, cond.shape[0], tn), lambda l, j: (l, 0, j)),
        out_shape=jax.ShapeDtypeStruct((depth, cond.shape[0], n6), F32),
        compiler_params=_cparams(2),
        name="adaln_mod",
    )(cond, ada_w, ada_b.reshape(depth, 1, n6))


def _rwkv_pre_kernel(x_ref, xu_ref, xd_ref, mod_ref, g_ref, mu_ref, wrkv_ref, g1_ref, g2_ref, w1_ref, w2_ref,
                     w0_ref, a1_ref, a2_ref, a0_ref, kk_ref, ka_ref, rk_ref, hsum_ref, hexp_ref,
                     r_o, v_o, kk_o, lw_o, kd_o, b_o, gate_o, bonus_o, *, p_tiles, tiles_per_seq):
    i = pl.program_id(0)
    is_p = i < p_tiles
    sub = (i - p_tiles) % tiles_per_seq
    mod = mod_ref[...]
    shift, scale = mod[:, 0:D], mod[:, D:2 * D]
    g = g_ref[...]
    h = _norm_mod(x_ref[...], g, shift, scale)
    hu = _norm_mod(xu_ref[...], g, shift, scale)
    hd = _norm_mod(xd_ref[...], g, shift, scale)

    q = D // 4
    row = lax.broadcasted_iota(I32, (TM, 1), 0)
    per = jnp.where(is_p, TM, GRID_W)
    pos = row & (per - 1)

    def prev1(a):
        return jnp.where(pos == 0, 0.0, pltpu.roll(a, 1, 0))

    def next1(a):
        return jnp.where(pos == per - 1, 0.0, pltpu.roll(a, TM - 1, 0))

    h0, h1, h2, h3 = (h[:, j * q:(j + 1) * q] for j in range(4))
    up = jnp.concatenate([jnp.where(sub == 0, 0.0, hu[:, 2 * q:3 * q]), h2[0:TM - GRID_W]], axis=0)
    down = jnp.concatenate([h3[GRID_W:TM], jnp.where(sub == tiles_per_seq - 1, 0.0, hd[:, 3 * q:4 * q])], axis=0)
    s0 = prev1(h0)
    s1 = jnp.where(is_p, prev1(h1), next1(h1))
    s2 = jnp.where(is_p, next1(h2), up)
    s3 = jnp.where(is_p, next1(h3), down)
    xx = jnp.concatenate([s0, s1, s2, s3], axis=1) - h

    mu = mu_ref[...]
    mix = lambda j: h + xx * mu[j:j + 1]
    r = _bdot(mix(0), wrkv_ref[0])
    k = _bdot(mix(2), wrkv_ref[1])
    v = _bdot(mix(3), wrkv_ref[2])
    gate = _bdot(jax.nn.sigmoid(_bdot(mix(5), g1_ref[...])), g2_ref[...])
    w_all = w0_ref[...] + _bdot(jnp.tanh(_bdot(mix(1), w1_ref[...])), w2_ref[...])
    a_all = jax.nn.sigmoid(a0_ref[...] + _bdot(_bdot(mix(4), a1_ref[...]), a2_ref[...]))

    hsum, hexp = hsum_ref[...], hexp_ref[...]
    head_sum = lambda t: _dot_f32_rhs01(_dot_f32_rhs01(t, hsum), hexp)

    kkr = k * kk_ref[...]
    kk = kkr / jnp.maximum(jnp.sqrt(head_sum(kkr * kkr)), 1e-12)
    ka = ka_ref[...]
    kd_sum = jnp.zeros_like(k)
    for d in range(2):
        wd = w_all[:, d * D:(d + 1) * D]
        z = -wd
        softplus = jnp.maximum(z, 0.0) + jnp.log(1.0 + jnp.exp(-jnp.abs(z)))
        lw_o[d] = -jnp.exp(-softplus - 0.5)
        a = a_all[:, d * D:(d + 1) * D]
        kd = k * (1.0 + (a - 1.0) * ka)
        kd_o[d] = kd
        b_o[d] = kk * a
        kd_sum = kd_sum + kd
    r_o[...] = r
    v_o[...] = v
    kk_o[...] = kk
    gate_o[...] = gate
    bonus_o[...] = head_sum(r * kd_sum * rk_ref[...]) * v


def _rwkv_pre(x, mod, p_tiles, tiles_per_seq, norm_g, wts):
    n = x.shape[0]
    n_tiles = n // TM
    hb = TM // GRID_W
    n_hblk = n // GRID_W
    row = lambda a: a.reshape(1, -1)
    full = lambda a: pl.BlockSpec(a.shape, lambda i: (0,) * a.ndim)
    tok = pl.BlockSpec((TM, D), lambda i: (i, 0))
    tok2 = pl.BlockSpec((2, TM, D), lambda i: (0, i, 0))
    consts = [row(norm_g), wts["mu"], wts["wrkv"], wts["g1"], wts["g2"], wts["w1"], wts["w2"], wts["w0"],
              wts["a1"], wts["a2"], wts["a0"], row(wts["k_k"]), row(wts["k_a"]), row(wts["r_k"]),
              wts["hsum"], wts["hexp"]]
    return pl.pallas_call(
        functools.partial(_rwkv_pre_kernel, p_tiles=p_tiles, tiles_per_seq=tiles_per_seq),
        grid=(n_tiles,),
        in_specs=[
            tok,
            pl.BlockSpec((GRID_W, D), lambda i: (jnp.maximum(i * hb - 1, 0), 0)),
            pl.BlockSpec((GRID_W, D), lambda i: (jnp.minimum(i * hb + hb, n_hblk - 1), 0)),
            pl.BlockSpec((None, 1, N_MOD * D), lambda i: (_mod_row(i, p_tiles, tiles_per_seq), 0, 0)),
        ] + [full(a) for a in consts],
        out_specs=[tok, tok, tok, tok2, tok2, tok2, tok, tok],
        out_shape=[jax.ShapeDtypeStruct((n, D), F32)] * 3 + [jax.ShapeDtypeStruct((2, n, D), F32)] * 3
        + [jax.ShapeDtypeStruct((n, D), F32)] * 2,
        compiler_params=_cparams(1),
        name="rwkv_pre",
    )(x, x, x, mod, *consts)


class _ScanItems:
    def __init__(self, n_ctx, ctx_chunks, n_lat, lat_chunks):
        self.n_ctx, self.ctx_chunks, self.n_lat, self.lat_chunks = n_ctx, ctx_chunks, n_lat, lat_chunks
        self.ctx_items = n_ctx * ctx_chunks
        self.n_items = self.ctx_items + n_lat * lat_chunks

    def decode(self, j):
        is_ctx = j < self.ctx_items
        jl = jnp.maximum(j - self.ctx_items, 0)
        c = jnp.where(is_ctx, j % self.ctx_chunks, jl % self.lat_chunks)
        return is_ctx, c, jnp.where(is_ctx, self.ctx_chunks, self.lat_chunks)

    def block(self, d, j):
        _, c, nc = self.decode(j)
        return jnp.where(d == 0, j, j - c + (nc - 1 - c))

    def ctx_seq(self, j):
        return jnp.minimum(j // self.ctx_chunks, self.n_ctx - 1)

    def lat_seq(self, j):
        return jnp.maximum(j - self.ctx_items, 0) // self.lat_chunks


def _rwkv_scan_kernel(rf_ref, vf_ref, kkf_ref, lwf_ref, kdf_ref, bf_ref, rb_ref, vb_ref, kkb_ref, lwb_ref, kdb_ref,
                      bb_ref, s0_ref, yf_ref, yb_ref, sf_ref, s_scr, *, items):
    is_ctx, c, nc = items.decode(pl.program_id(0))
    ch = RWKV_CHUNK

    @pl.when(c == 0)
    def _():
        s_scr[...] = jnp.where(is_ctx, 0.0, s0_ref[...])

    ti = lax.broadcasted_iota(I32, (ch, ch), 0)
    tj = lax.broadcasted_iota(I32, (ch, ch), 1)
    si = lax.broadcasted_iota(I32, (2 * ch, 2 * ch), 0)
    sj = lax.broadcasted_iota(I32, (2 * ch, 2 * ch), 1)
    same = (si < ch) == (sj < ch)
    ui, uj = si & (ch - 1), sj & (ch - 1)
    first = lax.broadcasted_iota(I32, (1, LANES), 1) < HEAD_A

    def stack(xp):
        return jnp.concatenate([jnp.where(first, xp, 0.0), jnp.where(first, 0.0, xp)], axis=0)

    def prologue(rev, r_ref, v_ref, kk_ref, lw_ref, kd_ref, b_ref):
        tri = jnp.where((tj >= ti) if rev else (tj <= ti), 1.0, 0.0).astype(BF16)
        lw = lw_ref[...]
        cum = _dot_f32_lhs01(tri, lw)
        last = cum[0:1] if rev else cum[ch - 1:ch]
        kk, kd, b = kk_ref[...], kd_ref[...], b_ref[...]
        e_neg = jnp.exp(-cum)
        e_rel = jnp.exp(last - cum)
        return dict(at=-kk * jnp.exp(cum - lw), rt=r_ref[...] * jnp.exp(cum), bt=b * e_neg, kt=kd * e_neg,
                    bh=b * e_rel, kh=kd * e_rel, g_all=jnp.exp(last), v=v_ref[...],
                    strict=same & ((uj > ui) if rev else (uj < ui)),
                    incl=same & ((uj >= ui) if rev else (uj <= ui)))

    dirs = [prologue(False, rf_ref, vf_ref, kkf_ref, lwf_ref, kdf_ref, bf_ref),
            prologue(True, rb_ref, vb_ref, kkb_ref, lwb_ref, kdb_ref, bb_ref)]

    units = [(d, p, slice(p * LANES, (p + 1) * LANES)) for d in range(2) for p in range(N_PAIR)]
    pairs = range(len(units))
    s_old = [s_scr[d, p] for d, p, _ in units]
    m1 = [_bdot_nt(jnp.concatenate([stack(dirs[d]["at"][:, sl]), stack(dirs[d]["rt"][:, sl])], axis=0),
                   jnp.concatenate([stack(dirs[d]["bt"][:, sl]), stack(dirs[d]["kt"][:, sl])], axis=0))
          for d, _, sl in units]
    m2 = [_bdot_nt(jnp.concatenate([dirs[d]["at"][:, sl], dirs[d]["rt"][:, sl]], axis=0), s_old[u])
          for u, (d, _, sl) in enumerate(units)]
    vs = [stack(dirs[d]["v"][:, sl]) for d, _, sl in units]
    strict = [dirs[d]["strict"] for d, _, _ in units]
    incl = [dirs[d]["incl"] for d, _, _ in units]
    x = [stack(m2[p][0:ch]) + _bdot(jnp.where(strict[p], m1[p][0:2 * ch, 2 * ch:4 * ch], 0.0), vs[p]) for p in pairs]
    nm = [jnp.where(strict[p], m1[p][0:2 * ch, 0:2 * ch], 0.0).astype(BF16) for p in pairs]
    for _ in range(5):
        sq_ap = [_bdot(nm[p], jnp.concatenate([nm[p], x[p].astype(BF16)], axis=1)) for p in pairs]
        nm = [sq_ap[p][:, 0:2 * ch].astype(BF16) for p in pairs]
        x = [x[p] + sq_ap[p][:, 2 * ch:4 * ch] for p in pairs]
    x = [x[p] + _bdot(nm[p], x[p]) for p in pairs]
    uv = [jnp.concatenate([x[p], vs[p]], axis=0).astype(BF16) for p in pairs]
    ys = [_bdot(jnp.concatenate([jnp.where(incl[p], m1[p][2 * ch:4 * ch, 0:2 * ch], 0.0),
                                 jnp.where(incl[p], m1[p][2 * ch:4 * ch, 2 * ch:4 * ch], 0.0)], axis=1), uv[p])
          for p in pairs]
    s_new = [s_old[u] * dirs[d]["g_all"][:, sl]
             + _bdot_tn(uv[u], jnp.concatenate([stack(dirs[d]["bh"][:, sl]), stack(dirs[d]["kh"][:, sl])], axis=0))
             for u, (d, _, sl) in enumerate(units)]
    for u, (d, _, sl) in enumerate(units):
        (yf_ref, yb_ref)[d][:, sl] = ys[u][0:ch] + ys[u][ch:2 * ch] + m2[u][ch:2 * ch]
    for u, (d, p, _) in enumerate(units):
        s_scr[d, p] = s_new[u]

    @pl.when(is_ctx & (c == nc - 1))
    def _():
        sf_ref[...] = s_scr[...]


def _rwkv_scan(r, v, kk, lw, kd, b, s0, items):
    n = r.shape[0]
    ch = RWKV_CHUNK
    tok = lambda d: pl.BlockSpec((ch, D), lambda j: (items.block(d, j), 0))
    tokd = lambda d: pl.BlockSpec((None, ch, D), lambda j: (d, items.block(d, j), 0))
    st = lambda seq: pl.BlockSpec((2, None, N_PAIR, LANES, LANES), lambda j: (0, seq(j), 0, 0, 0))
    per_dir = lambda d: [tok(d), tok(d), tok(d), tokd(d), tokd(d), tokd(d)]
    return pl.pallas_call(
        functools.partial(_rwkv_scan_kernel, items=items),
        grid=(items.n_items,),
        in_specs=per_dir(0) + per_dir(1) + [st(items.lat_seq)],
        out_specs=[tok(0), tok(1), st(items.ctx_seq)],
        out_shape=[jax.ShapeDtypeStruct((n, D), F32), jax.ShapeDtypeStruct((n, D), F32),
                   jax.ShapeDtypeStruct((2, items.n_ctx, N_PAIR, LANES, LANES), F32)],
        scratch_shapes=[pltpu.VMEM((2, N_PAIR, LANES, LANES), F32)],
        compiler_params=_cparams(1),
        name="rwkv_scan",
    )(r, v, kk, lw, kd, b, r, v, kk, lw, kd, b, s0)


def _rwkv_post_kernel(x_ref, yf_ref, yb_ref, gate_ref, bonus_ref, mod_ref, lnw_ref, lnb_ref, wo_ref, hsum_ref, hexp_ref,
                      o_ref):
    hsum, hexp = hsum_ref[...], hexp_ref[...]
    head_mean = lambda t: _dot_f32_rhs01(_dot_f32_rhs01(t, hsum), hexp) * (1.0 / HEAD_A)
    y = yf_ref[...] + yb_ref[...]
    yc = y - head_mean(y)
    yn = yc * lax.rsqrt(head_mean(yc * yc) + LNX_EPS)
    z = (yn * lnw_ref[...] + lnb_ref[...] + bonus_ref[...]) * gate_ref[...]
    o_ref[...] = x_ref[...] + mod_ref[:, 2 * D:3 * D] * _bdot(z, wo_ref[...])


def _rwkv_post(x, y_fwd, y_bwd, gate, bonus, mod, p_tiles, tiles_per_seq, ln_w, ln_b, wo, hsum, hexp):
    n = x.shape[0]
    full = lambda a: pl.BlockSpec(a.shape, lambda i: (0,) * a.ndim)
    tok = pl.BlockSpec((TM, D), lambda i: (i, 0))
    consts = [ln_w.reshape(1, D), ln_b.reshape(1, D), wo, hsum, hexp]
    return pl.pallas_call(
        _rwkv_post_kernel,
        grid=(n // TM,),
        in_specs=[tok, tok, tok, tok, tok,
                  pl.BlockSpec((None, 1, N_MOD * D), lambda i: (_mod_row(i, p_tiles, tiles_per_seq), 0, 0))]
        + [full(a) for a in consts],
        out_specs=tok,
        out_shape=jax.ShapeDtypeStruct((n, D), F32),
        compiler_params=_cparams(1),
        name="rwkv_post",
    )(x, y_fwd, y_bwd, gate, bonus, mod, *consts)


def _ret_pre_kernel(x_ref, mod_ref, g_ref, cos_ref, sin_ref, win_ref, q_o, k_o, v_o, gate_o):
    mod = mod_ref[...]
    h = _norm_mod(x_ref[...], g_ref[...], mod[:, 0:D], mod[:, D:2 * D])
    proj = _bdot(h, win_ref[...])
    cos, sin = cos_ref[...], sin_ref[...]

    def rope(t):
        outs = []
        for j in range(D // LANES):
            tj = t[:, j * LANES:(j + 1) * LANES]
            cj = cos[:, (j % 2) * LANES:(j % 2 + 1) * LANES]
            sj = sin[:, (j % 2) * LANES:(j % 2 + 1) * LANES]
            outs.append(tj * cj + pltpu.roll(tj, LANES // 2, 1) * sj)
        return jnp.concatenate(outs, axis=1)

    q_o[...] = rope(proj[:, 0:D]).astype(BF16)
    k_o[...] = rope(proj[:, D:2 * D] * (DK_B ** -0.5)).astype(BF16)
    v_o[...] = proj[:, 2 * D:4 * D].astype(BF16)
    gate_o[...] = proj[:, 4 * D:6 * D]


def _ret_pre(x, mod, p_tiles, tiles_per_seq, norm_g, cos_t, sin_t, w_in):
    n = x.shape[0]
    full = lambda a: pl.BlockSpec(a.shape, lambda i: (0,) * a.ndim)
    tok = lambda w: pl.BlockSpec((TM, w), lambda i: (i, 0))
    tab = pl.BlockSpec((TM, DK_B), lambda i: (jnp.where(i < p_tiles, 0, 1 + (i - p_tiles) % tiles_per_seq), 0))
    return pl.pallas_call(
        _ret_pre_kernel,
        grid=(n // TM,),
        in_specs=[tok(D), pl.BlockSpec((None, 1, N_MOD * D), lambda i: (_mod_row(i, p_tiles, tiles_per_seq), 0, 0)),
                  full(norm_g.reshape(1, D)), tab, tab, full(w_in)],
        out_specs=[tok(D), tok(D), tok(2 * D), tok(2 * D)],
        out_shape=[jax.ShapeDtypeStruct((n, D), BF16), jax.ShapeDtypeStruct((n, D), BF16),
                   jax.ShapeDtypeStruct((n, 2 * D), BF16), jax.ShapeDtypeStruct((n, 2 * D), F32)],
        compiler_params=_cparams(1),
        name="ret_pre",
    )(x, mod, norm_g.reshape(1, D), cos_t, sin_t, w_in)


def _ret_scan_kernel(lg_ref, q_ref, k_ref, v_ref, s0_ref, o_ref, sf_ref, s_scr, *, items):
    d = pl.program_id(0)
    is_ctx, c, nc = items.decode(pl.program_id(1))
    rev = d == 1
    ch = RET_CHUNK

    @pl.when(c == 0)
    def _():
        s_scr[...] = jnp.where(is_ctx, 0.0, s0_ref[...])

    ti = lax.broadcasted_iota(I32, (ch, ch), 0)
    tj = lax.broadcasted_iota(I32, (ch, ch), 1)
    rel = jnp.where(rev, tj - ti, ti - tj).astype(F32)
    steps_q = jnp.where(rev, ch - ti, ti + 1).astype(F32)
    steps_k = jnp.where(rev, ti, ch - 1 - ti).astype(F32)
    for hd in range(H_B):
        lg = lg_ref[d, hd]
        mask = jnp.where(rel >= 0, jnp.exp(jnp.maximum(rel, 0.0) * lg), 0.0)
        q_dec = jnp.exp(steps_q * lg)
        k_dec = jnp.exp(steps_k * lg)
        chunk_dec = jnp.exp(jnp.full((1, DV_B), float(ch), F32) * lg)
        qh = q_ref[:, hd * DK_B:(hd + 1) * DK_B]
        kh = k_ref[:, hd * DK_B:(hd + 1) * DK_B]
        vh = v_ref[:, hd * DV_B:(hd + 1) * DV_B]
        s = s_scr[hd]
        scores = _bdot_nt(qh, kh) * mask
        inner = _bdot(scores, vh)
        cross = _bdot(qh, s) * jnp.concatenate([q_dec] * (DV_B // ch), axis=1)
        o_ref[:, hd * DV_B:(hd + 1) * DV_B] = inner + cross
        kdec = kh.astype(F32) * jnp.concatenate([k_dec] * (DK_B // ch), axis=1)
        s_scr[hd] = s * chunk_dec + _bdot_tn(kdec, vh)

    @pl.when(is_ctx & (c == nc - 1))
    def _():
        sf_ref[...] = s_scr[...]


def _ret_scan(log_gamma, q, k, v, s0, items):
    n = q.shape[0]
    ch = RET_CHUNK
    tok = lambda w: pl.BlockSpec((ch, w), lambda d, j: (items.block(d, j), 0))
    st = lambda seq: pl.BlockSpec((None, None, H_B, DK_B, DV_B), lambda d, j: (d, seq(j), 0, 0, 0))
    return pl.pallas_call(
        functools.partial(_ret_scan_kernel, items=items),
        grid=(2, items.n_items),
        in_specs=[pl.BlockSpec(memory_space=pltpu.SMEM), tok(D), tok(D), tok(2 * D), st(items.lat_seq)],
        out_specs=[pl.BlockSpec((None, ch, 2 * D), lambda d, j: (d, items.block(d, j), 0)), st(items.ctx_seq)],
        out_shape=[jax.ShapeDtypeStruct((2, n, 2 * D), F32),
                   jax.ShapeDtypeStruct((2, items.n_ctx, H_B, DK_B, DV_B), F32)],
        scratch_shapes=[pltpu.VMEM((H_B, DK_B, DV_B), F32)],
        compiler_params=_cparams(2),
        name="ret_scan",
    )(log_gamma, q, k, v, s0)


def _ret_post_kernel(x_ref, o_ref, gate_ref, mod_ref, wout_ref, out_ref):
    o = o_ref[0] + o_ref[1]
    parts = []
    for hd in range(H_B):
        oh = o[:, hd * DV_B:(hd + 1) * DV_B]
        parts.append(oh * lax.rsqrt(jnp.mean(oh * oh, axis=-1, keepdims=True) + NORM_EPS))
    y = _silu(gate_ref[...]) * jnp.concatenate(parts, axis=1)
    out_ref[...] = x_ref[...] + mod_ref[:, 2 * D:3 * D] * _bdot(y, wout_ref[...])


def _ret_post(x, o, gate, mod, p_tiles, tiles_per_seq, w_out):
    n = x.shape[0]
    tok = lambda w: pl.BlockSpec((TM, w), lambda i: (i, 0))
    return pl.pallas_call(
        _ret_post_kernel,
        grid=(n // TM,),
        in_specs=[tok(D), pl.BlockSpec((2, TM, 2 * D), lambda i: (0, i, 0)), tok(2 * D),
                  pl.BlockSpec((None, 1, N_MOD * D), lambda i: (_mod_row(i, p_tiles, tiles_per_seq), 0, 0)),
                  pl.BlockSpec(w_out.shape, lambda i: (0, 0))],
        out_specs=tok(D),
        out_shape=jax.ShapeDtypeStruct((n, D), F32),
        compiler_params=_cparams(1),
        name="ret_post",
    )(x, o, gate, mod, w_out)


def _moe_route_kernel(x_ref, mod_ref, g_ref, router_ref, bias_ref, h_o, e_o, w_o, p_o, cnt_o, carry):
    i = pl.program_id(0)

    @pl.when(i == 0)
    def _():
        carry[...] = jnp.zeros_like(carry)

    mod = mod_ref[...]
    h = _norm_mod(x_ref[...], g_ref[...], mod[:, 3 * D:4 * D], mod[:, 4 * D:5 * D])
    _rows_store(h_o, h)
    lane =lax.broadcasted_iota(I32, (TM, LANES), 1)
    valid = lane < N_EXPERTS
    neg = -jnp.inf
    scores = jax.nn.sigmoid(_dot_f32(h, router_ref[...]))
    biased = jnp.where(valid, scores + bias_ref[...], neg)

    def group_reduce(t, op):
        s = 1
        while s < PER_GROUP:
            partner = jnp.where((lane & s) == 0, pltpu.roll(t, LANES - s, 1), pltpu.roll(t, s, 1))
            t = op(t, partner)
            s *= 2
        return t

    lane_f = lane.astype(F32)
    group_f = jnp.floor(lane_f * (1.0 / PER_GROUP))

    def first_lane_of_max(t):
        m = jnp.max(t, axis=-1, keepdims=True)
        return jnp.min(jnp.where(t == m, lane_f, float(LANES)), axis=-1, keepdims=True)

    m1 = group_reduce(biased, jnp.maximum)
    first1 = group_reduce(jnp.where(biased == m1, lane_f, float(LANES)), jnp.minimum)
    m2 = group_reduce(jnp.where(lane_f == first1, neg, biased), jnp.maximum)
    gscore = jnp.where(valid, m1 + m2, neg)
    cand = jnp.full((TM, LANES), neg, F32)
    for _ in range(TOPK_GROUPS):
        gsel = group_f == jnp.floor(first_lane_of_max(gscore) * (1.0 / PER_GROUP))
        cand = jnp.where(gsel, biased, cand)
        gscore = jnp.where(gsel, neg, gscore)
    hits = []
    sel01 = jnp.zeros((TM, LANES), F32)
    e_cols = jnp.zeros((TM, LANES), F32)
    for j in range(TOP_K):
        fl = first_lane_of_max(cand)
        hit = lane_f == fl
        hits.append(hit)
        sel01 = jnp.where(hit, 1.0, sel01)
        cand = jnp.where(hit, neg, cand)
        e_cols = jnp.where(lane == j, fl, e_cols)
    wsum = jnp.sum(sel01 * scores, axis=-1, keepdims=True)

    ri = lax.broadcasted_iota(I32, (TM, TM), 0)
    rj = lax.broadcasted_iota(I32, (TM, TM), 1)
    below = jnp.where(rj < ri, 1.0, 0.0).astype(BF16)
    rank = jnp.dot(below, sel01.astype(BF16), preferred_element_type=F32) + carry[...]
    carry[...] = carry[...] + jnp.sum(sel01, axis=0, keepdims=True)
    w_cols = jnp.zeros((TM, LANES), F32)
    p_cols = jnp.zeros((TM, LANES), F32)
    for j in range(TOP_K):
        wj = jnp.sum(jnp.where(hits[j], scores, 0.0), axis=-1, keepdims=True)
        w_cols = jnp.where(lane == j, wj / wsum * ROUTED_SCALE, w_cols)
        p_cols = jnp.where(lane == j, jnp.sum(jnp.where(hits[j], rank, 0.0), axis=-1, keepdims=True), p_cols)
    e_o[...] = e_cols.astype(I32)
    w_o[...] = w_cols
    p_o[...] = p_cols.astype(I32)
    cnt_o[...] = jnp.broadcast_to(carry[...], cnt_o.shape)


def _moe_route(x, mod, p_tiles, tiles_per_seq, norm_g, router, bias):
    n = x.shape[0]
    full = lambda a: pl.BlockSpec(a.shape, lambda i: (0,) * a.ndim)
    tok = lambda w: pl.BlockSpec((TM, w), lambda i: (i, 0))
    router_p = jnp.pad(router, ((0, 0), (0, LANES - N_EXPERTS)))
    bias_p = jnp.pad(bias, (0, LANES - N_EXPERTS)).reshape(1, LANES)
    return pl.pallas_call(
        _moe_route_kernel,
        grid=(n // TM,),
        in_specs=[tok(D), pl.BlockSpec((None, 1, N_MOD * D), lambda i: (_mod_row(i, p_tiles, tiles_per_seq), 0, 0)),
                  full(norm_g.reshape(1, D)), full(router_p), full(bias_p)],
        out_specs=[pl.BlockSpec((TM * ROW_SUB, LANES), lambda i: (i, 0)), tok(LANES), tok(LANES), tok(LANES),
                   pl.BlockSpec((8, LANES), lambda i: (0, 0))],
        out_shape=[jax.ShapeDtypeStruct((n * ROW_SUB, LANES), F32), jax.ShapeDtypeStruct((n, LANES), I32),
                   jax.ShapeDtypeStruct((n, LANES), F32), jax.ShapeDtypeStruct((n, LANES), I32),
                   jax.ShapeDtypeStruct((8, LANES), F32)],
        scratch_shapes=[pltpu.VMEM((1, LANES), F32)],
        compiler_params=_cparams(1),
        name="moe_route",
    )(x, mod, norm_g.reshape(1, D), router_p, bias_p)


DISPATCH_TOK = 512


def _moe_dispatch_kernel(pad_lo_ref, pad_n_ref, dest_ref, h_ref, xs_hbm, zero_scr, sem):
    @pl.when(pl.program_id(0) == 0)
    def _():
        zero_scr[...] = jnp.zeros_like(zero_scr)

        def pad_copy(e, s):
            return pltpu.make_async_copy(zero_scr, _row_tile(xs_hbm, pad_lo_ref[e] + s), sem)

        def per_expert(e, carry):
            def issue(s, c):
                pad_copy(e, s).start()
                return c

            def drain(s, c):
                pad_copy(e, s).wait()
                return c

            lax.fori_loop(0, pad_n_ref[e], issue, 0)
            lax.fori_loop(0, pad_n_ref[e], drain, 0)
            return carry

        lax.fori_loop(0, N_EXPERTS, per_expert, 0)

    def copy(t, kq):
        return pltpu.make_async_copy(_row_tile(h_ref, t), _row_tile(xs_hbm, dest_ref[t * TOP_K + kq]), sem)

    def issue(t, carry):
        for kq in range(TOP_K):
            copy(t, kq).start(priority=kq % 2)
        return carry

    def drain(t, carry):
        for kq in range(TOP_K):
            copy(t, kq).wait()
        return carry

    lax.fori_loop(0, DISPATCH_TOK, issue, 0)
    lax.fori_loop(0, DISPATCH_TOK, drain, 0)


def _moe_dispatch(pad_lo, pad_n, dest_flat, h, n_slots):
    n = h.shape[0] // ROW_SUB
    grid_spec = pltpu.PrefetchScalarGridSpec(
        num_scalar_prefetch=2,
        grid=(n // DISPATCH_TOK,),
        in_specs=[pl.BlockSpec((DISPATCH_TOK * TOP_K,), lambda i, *_: (i,), memory_space=pltpu.SMEM),
                  pl.BlockSpec((DISPATCH_TOK * ROW_SUB, LANES), lambda i, *_: (i, 0))],
        out_specs=pl.BlockSpec(memory_space=pl.ANY),
        scratch_shapes=[pltpu.VMEM((ROW_SUB, LANES), F32), pltpu.SemaphoreType.DMA],
    )
    return pl.pallas_call(
        _moe_dispatch_kernel,
        grid_spec=grid_spec,
        out_shape=jax.ShapeDtypeStruct((n_slots * ROW_SUB, LANES), F32),
        compiler_params=_cparams(1, has_side_effects=True),
        name="moe_dispatch",
    )(pad_lo, pad_n, dest_flat, h)


def _moe_expert_kernel(be_ref, xs_ref, wgu_ref, wdn_ref, o_ref, wgu_bf, wdn_bf):
    i = pl.program_id(0)

    @pl.when((i == 0) | (be_ref[i] != be_ref[jnp.maximum(i - 1, 0)]))
    def _():
        wgu_bf[...] = wgu_ref[...].astype(BF16)
        wdn_bf[...] = wdn_ref[...].astype(BF16)

    gu = _bdot(_rows_load(xs_ref), wgu_bf[...])
    act = _silu(gu[:, 0:D_EXPERT]) * gu[:, D_EXPERT:2 * D_EXPERT]
    _rows_store(o_ref, _bdot(act, wdn_bf[...]))


def _moe_expert(block_e, xs, w_gu, w_down, layer):
    n_slots = xs.shape[0] // ROW_SUB
    n_blocks = n_slots // EXPERT_BLOCK
    rows = pl.BlockSpec((EXPERT_BLOCK * ROW_SUB, LANES), lambda i, be: (i, 0))
    grid_spec = pltpu.PrefetchScalarGridSpec(
        num_scalar_prefetch=1,
        grid=(n_blocks,),
        in_specs=[rows,
                  pl.BlockSpec((None, None, D, 2 * D_EXPERT), lambda i, be: (layer, be[i], 0, 0)),
                  pl.BlockSpec((None, None, D_EXPERT, D), lambda i, be: (layer, be[i], 0, 0))],
        out_specs=rows,
        scratch_shapes=[pltpu.VMEM((D, 2 * D_EXPERT), BF16), pltpu.VMEM((D_EXPERT, D), BF16)],
    )
    return pl.pallas_call(
        _moe_expert_kernel,
        grid_spec=grid_spec,
        out_shape=jax.ShapeDtypeStruct((n_slots * ROW_SUB, LANES), F32),
        compiler_params=_cparams(1),
        name="moe_expert",
    )(block_e, xs, w_gu, w_down)


COMBINE_TOK = 128


def _moe_combine_kernel(dest_ref, dest_next_ref, x_ref, h_ref, w_ref, mod_ref, shgu_ref, shdn_ref, gfin_ref, ys_hbm,
                        *refs, final_norm, ctx_tiles):
    o_refs, (buf, sems) = refs[:-2], refs[-2:]
    i = pl.program_id(0)
    slot = i % 2

    def copy(idx_ref, s, t, kq):
        return pltpu.make_async_copy(_row_tile(ys_hbm, idx_ref[t * TOP_K + kq]), _row_tile(buf.at[s, kq], t),
                                     sems.at[s])

    def fetch(idx_ref, s):
        def body(t, carry):
            for kq in range(TOP_K):
                copy(idx_ref, s, t, kq).start(priority=kq % 2)
            return carry
        lax.fori_loop(0, COMBINE_TOK, body, 0, unroll=2)

    @pl.when(i == 0)
    def _():
        fetch(dest_ref, slot)

    @pl.when(i + 1 < pl.num_programs(0))
    def _():
        fetch(dest_next_ref, 1 - slot)

    gu = _bdot(_rows_load(h_ref), shgu_ref[...])
    acc = _bdot(_silu(gu[:, 0:D_EXPERT]) * gu[:, D_EXPERT:2 * D_EXPERT], shdn_ref[...])

    def drain(t, carry):
        for kq in range(TOP_K):
            copy(dest_ref, slot, t, kq).wait()
        return carry

    lax.fori_loop(0, COMBINE_TOK, drain, 0)
    w = w_ref[...]
    for kq in range(TOP_K):
        acc = acc + _rows_load(buf.at[slot, kq]) * w[:, kq:kq + 1]
    out = x_ref[...] + mod_ref[:, 5 * D:6 * D] * acc
    if final_norm:
        out = out * lax.rsqrt(jnp.mean(out * out, axis=-1, keepdims=True) + NORM_EPS) * gfin_ref[...]

        @pl.when(i < ctx_tiles)
        def _():
            o_refs[0][...] = out

        @pl.when(i >= ctx_tiles)
        def _():
            o_refs[1][...] = out
    else:
        o_refs[0][...] = out


def _moe_combine(dest_flat, x, h, w_cols, mod, p_tiles, tiles_per_seq, sh_gu, sh_down, g_final, ys, final_norm):
    n = x.shape[0]
    ratio = TM // COMBINE_TOK
    tok = lambda w: pl.BlockSpec((COMBINE_TOK, w), lambda i: (i, 0))
    full = lambda a: pl.BlockSpec(a.shape, lambda i: (0,) * a.ndim)
    n_tiles = n // COMBINE_TOK
    ctx_tiles = p_tiles * ratio
    if final_norm:
        out_specs = [pl.BlockSpec((COMBINE_TOK, D), lambda i: (jnp.minimum(i, ctx_tiles - 1), 0)),
                     pl.BlockSpec((COMBINE_TOK, D), lambda i: (jnp.maximum(i - ctx_tiles, 0), 0))]
        out_shape = [jax.ShapeDtypeStruct((ctx_tiles * COMBINE_TOK, D), F32),
                     jax.ShapeDtypeStruct((n - ctx_tiles * COMBINE_TOK, D), F32)]
    else:
        out_specs, out_shape = tok(D), jax.ShapeDtypeStruct((n, D), F32)
    return pl.pallas_call(
        functools.partial(_moe_combine_kernel, final_norm=final_norm, ctx_tiles=ctx_tiles),
        grid=(n_tiles,),
        in_specs=[pl.BlockSpec((COMBINE_TOK * TOP_K,), lambda i: (i,), memory_space=pltpu.SMEM),
                  pl.BlockSpec((COMBINE_TOK * TOP_K,), lambda i: (jnp.minimum(i + 1, n_tiles - 1),),
                               memory_space=pltpu.SMEM),
                  tok(D), pl.BlockSpec((COMBINE_TOK * ROW_SUB, LANES), lambda i: (i, 0)), tok(LANES),
                  pl.BlockSpec((None, 1, N_MOD * D), lambda i: (_mod_row(i // ratio, p_tiles, tiles_per_seq), 0, 0)),
                  full(sh_gu), full(sh_down), full(g_final),
                  pl.BlockSpec(memory_space=pl.ANY)],
        out_specs=out_specs,
        out_shape=out_shape,
        scratch_shapes=[pltpu.VMEM((2, TOP_K, COMBINE_TOK * ROW_SUB, LANES), F32), pltpu.SemaphoreType.DMA((2,))],
        compiler_params=_cparams(1),
        name="moe_combine",
    )(dest_flat, dest_flat, x, h, w_cols, mod, sh_gu, sh_down, g_final, ys)


def _moe_layer(x, mod, p_tiles, tiles_per_seq, norm_g, router, bias, w_gu, w_down, layer, sh_gu, sh_down, g_final,
               final_norm):
    n = x.shape[0]
    h, e_cols, w_cols, p_cols, counts = _moe_route(x, mod, p_tiles, tiles_per_seq, norm_g, router, bias)
    counts = counts[0, :N_EXPERTS].astype(I32)
    padded = (counts + EXPERT_BLOCK - 1) // EXPERT_BLOCK * EXPERT_BLOCK
    pad_end = jnp.cumsum(padded)
    pad_start = pad_end - padded
    chosen = e_cols[:, :TOP_K, None] == jnp.arange(N_EXPERTS, dtype=I32)
    dest = (jnp.sum(jnp.where(chosen, pad_start, 0), axis=-1) + p_cols[:, :TOP_K]).reshape(-1)
    n_blocks = n * TOP_K // EXPERT_BLOCK + N_EXPERTS
    n_slots = n_blocks * EXPERT_BLOCK
    block_start = jnp.arange(n_blocks, dtype=I32) * EXPERT_BLOCK
    block_e = jnp.minimum(jnp.sum((pad_end[None, :] <= block_start[:, None]).astype(I32), axis=1), N_EXPERTS - 1)
    pad_n = (padded - counts).at[N_EXPERTS - 1].add(n_slots - pad_end[N_EXPERTS - 1])
    xs = _moe_dispatch(pad_start + counts, pad_n, dest, h, n_slots)
    ys = _moe_expert(block_e, xs, w_gu, w_down, layer)
    return _moe_combine(dest, x, h, w_cols, mod, p_tiles, tiles_per_seq, sh_gu, sh_down, g_final.reshape(1, D), ys,
                        final_norm)


def _pair_pack(s):
    lead = s.shape[:-3]
    s = s.reshape(lead + (N_PAIR, 2, HEAD_A, HEAD_A))
    z = jnp.zeros_like(s[..., 0, :, :])
    top = jnp.concatenate([s[..., 0, :, :], z], axis=-1)
    bot = jnp.concatenate([z, s[..., 1, :, :]], axis=-1)
    return jnp.concatenate([top, bot], axis=-2)


def _pair_unpack(s):
    a = s[..., :HEAD_A, :HEAD_A]
    b = s[..., HEAD_A:, HEAD_A:]
    return jnp.stack([a, b], axis=-3).reshape(s.shape[:-3] + (H_A, HEAD_A, HEAD_A))


def _rope_tables(rows):
    half = DK_B // 2
    n_freq = half // 2
    inv = ROPE_BASE ** (-jnp.arange(n_freq, dtype=F32) / n_freq)
    pos_r = jnp.repeat(jnp.arange(rows, dtype=F32), GRID_W)
    pos_c = jnp.tile(jnp.arange(GRID_W, dtype=F32), rows)

    def tab(pos):
        ang = pos[:, None] * inv[None, :]
        c, s = jnp.cos(ang), jnp.sin(ang)
        return jnp.concatenate([c, c], -1), jnp.concatenate([-s, s], -1)

    cr, sr = tab(pos_r)
    cc, sc = tab(pos_c)
    cos = jnp.concatenate([cr, cc], -1)
    sin = jnp.concatenate([sr, sc], -1)
    cos = jnp.concatenate([jnp.ones((TM, DK_B), F32), cos], 0)
    sin = jnp.concatenate([jnp.zeros((TM, DK_B), F32), sin], 0)
    return cos, sin


def kernel(x_prompt, x_sample, state_rwkv, state_ret, c, c_ctx, ada_w, ada_b, norm_mix, norm_ffn, norm_final, rwkv_mu, rwkv_w0, rwkv_w1, rwkv_w2, rwkv_a0, rwkv_a1, rwkv_a2, rwkv_wrkv, rwkv_wo, rwkv_g1, rwkv_g2, rwkv_k_k, rwkv_k_a, rwkv_r_k, rwkv_ln_w, rwkv_ln_b, ret_w_in, ret_w_out, ret_decay_logit, moe_router, moe_bias, moe_w_gu, moe_w_down, moe_sh_gu, moe_sh_down):
    bp, tp, _ = x_prompt.shape
    bs, ts, _ = x_sample.shape
    n_p, n_s = bp * tp, bs * ts
    assert tp == TM and ts % TM == 0 and TM % GRID_W == 0
    p_tiles = n_p // TM
    tiles_per_seq = ts // TM
    rows = ts // GRID_W
    x = jnp.concatenate([x_prompt.reshape(n_p, D), x_sample.reshape(n_s, D)], axis=0)
    n = n_p + n_s

    n_cond = 16
    cond = jnp.zeros((n_cond, D), F32).at[0].set(c_ctx).at[1:1 + bs].set(c)
    mod = _modulation(cond, ada_w, ada_b).reshape(ada_w.shape[0], n_cond, 1, N_MOD * D)

    head_of = jnp.arange(D, dtype=I32) // HEAD_A
    hsum = (head_of[:, None] == jnp.arange(LANES, dtype=I32)[None, :]).astype(BF16)
    hexp = hsum.T

    gl = rwkv_g1.shape[-1]
    glp = -(-gl // LANES) * LANES
    zb = lambda a: jnp.zeros_like(a)
    wts = {
        "mu": jnp.pad(rwkv_mu[0], ((0, 8 - N_MOD), (0, 0))),
        "wrkv": rwkv_wrkv[0].astype(BF16),
        "g1": jnp.pad(rwkv_g1[0], ((0, 0), (0, glp - gl))).astype(BF16),
        "g2": jnp.pad(rwkv_g2[0], ((0, glp - gl), (0, 0))).astype(BF16),
        "w1": jnp.concatenate([rwkv_w1[0, 0], rwkv_w1[0, 1]], axis=1).astype(BF16),
        "w2": jnp.concatenate([jnp.concatenate([rwkv_w2[0, 0], zb(rwkv_w2[0, 1])], 1),
                               jnp.concatenate([zb(rwkv_w2[0, 0]), rwkv_w2[0, 1]], 1)], 0).astype(BF16),
        "w0": rwkv_w0[0].reshape(1, 2 * D),
        "a1": jnp.concatenate([rwkv_a1[0, 0], rwkv_a1[0, 1]], axis=1).astype(BF16),
        "a2": jnp.concatenate([jnp.concatenate([rwkv_a2[0, 0], zb(rwkv_a2[0, 1])], 1),
                               jnp.concatenate([zb(rwkv_a2[0, 0]), rwkv_a2[0, 1]], 1)], 0).astype(BF16),
        "a0": rwkv_a0[0].reshape(1, 2 * D),
        "k_k": rwkv_k_k[0], "k_a": rwkv_k_a[0], "r_k": rwkv_r_k[0].reshape(D),
        "hsum": hsum, "hexp": hexp,
    }
    r, v, kk, lw, kd, b, gate, bonus = _rwkv_pre(x, mod[0], p_tiles, tiles_per_seq, norm_mix[0], wts)
    s0_lat = _pair_pack(jnp.moveaxis(state_rwkv[:, 0], 1, 0))
    y_fwd, y_bwd, s_fin = _rwkv_scan(r, v, kk, lw, kd, b, s0_lat,
                                     _ScanItems(bp, tp // RWKV_CHUNK, bs, ts // RWKV_CHUNK))
    new_state_rwkv = jnp.moveaxis(_pair_unpack(s_fin), 0, 1)[:, None]
    x = _rwkv_post(x, y_fwd, y_bwd, gate, bonus, mod[0], p_tiles, tiles_per_seq, rwkv_ln_w[0], rwkv_ln_b[0],
                   rwkv_wo[0].astype(BF16), hsum, hexp)
    x = _moe_layer(x, mod[0], p_tiles, tiles_per_seq, norm_ffn[0], moe_router[0], moe_bias[0],
                   moe_w_gu, moe_w_down, 0, moe_sh_gu[0].astype(BF16),
                   moe_sh_down[0].astype(BF16), norm_final, False)

    cos_t, sin_t = _rope_tables(rows)
    q, k, vv, rgate = _ret_pre(x, mod[1], p_tiles, tiles_per_seq, norm_mix[1], cos_t, sin_t, ret_w_in[0].astype(BF16))
    log_gamma = jax.nn.log_sigmoid(ret_decay_logit[0].astype(F32))
    o2, r_fin = _ret_scan(log_gamma, q, k, vv, jnp.moveaxis(state_ret[:, 0], 1, 0),
                          _ScanItems(bp, tp // RET_CHUNK, bs, ts // RET_CHUNK))
    new_state_ret = jnp.moveaxis(r_fin, 0, 1)[:, None]
    x = _ret_post(x, o2, rgate, mod[1], p_tiles, tiles_per_seq, ret_w_out[0].astype(BF16))
    y_ctx, y_lat = _moe_layer(x, mod[1], p_tiles, tiles_per_seq, norm_ffn[1], moe_router[1], moe_bias[1],
                              moe_w_gu, moe_w_down, 1, moe_sh_gu[1].astype(BF16),
                              moe_sh_down[1].astype(BF16), norm_final, True)

    return (y_ctx.reshape(bp, tp, D), y_lat.reshape(bs, ts, D), new_state_rwkv, new_state_ret)
```

```python
import functools

import jax
import jax.numpy as jnp
from jax import lax
from jax.experimental import pallas as pl
from jax.experimental.pallas import tpu as pltpu

F32, BF16, I32 = jnp.float32, jnp.bfloat16, jnp.int32

D = 1024
N_MOD = 6
NORM_EPS = 1e-6
GRID_W = 64
HEAD_A = 64
H_A = D // HEAD_A
LNX_EPS = 64e-5
RWKV_CHUNK = 64
N_PAIR = H_A // 2
H_B = 4
DK_B = D // H_B
DV_B = 2 * DK_B
RET_CHUNK = 128
ROPE_BASE = 10000.0
N_EXPERTS = 64
TOP_K = 8
N_GROUPS = 8
TOPK_GROUPS = 4
PER_GROUP = N_EXPERTS // N_GROUPS
D_EXPERT = 256
ROUTED_SCALE = 2.5
EXPERT_BLOCK = 256

TM = 256
LANES = 128
ROW_SUB = D // LANES
VMEM_LIMIT = 56 * 1024 * 1024


def _cparams(n_grid_axes, **kw):
    return pltpu.CompilerParams(dimension_semantics=("arbitrary",) * n_grid_axes, vmem_limit_bytes=VMEM_LIMIT, **kw)


def _bdot(a, b):
    return jnp.dot(a.astype(BF16), b.astype(BF16), preferred_element_type=F32)


def _bdot_nt(a, b):
    return lax.dot_general(a.astype(BF16), b.astype(BF16), (((1,), (1,)), ((), ())), preferred_element_type=F32)


def _bdot_tn(a, b):
    return lax.dot_general(a.astype(BF16), b.astype(BF16), (((0,), (0,)), ((), ())), preferred_element_type=F32)


def _split3(x):
    hi = x.astype(BF16)
    r1 = x - hi.astype(F32)
    mid = r1.astype(BF16)
    lo = (r1 - mid.astype(F32)).astype(BF16)
    return hi, mid, lo


def _dot_f32(a, b):
    ah, am, al = _split3(a)
    bh, bm, bl = _split3(b)
    d = lambda x, y: jnp.dot(x, y, preferred_element_type=F32)
    return d(ah, bh) + (d(ah, bm) + d(am, bh)) + (d(ah, bl) + d(al, bh) + d(am, bm))


def _dot_f32_rhs01(a, m01):
    ah, am, al = _split3(a)
    d = lambda x: jnp.dot(x, m01, preferred_element_type=F32)
    return d(ah) + d(am) + d(al)


def _dot_f32_lhs01(m01, b):
    bh, bm, bl = _split3(b)
    d = lambda x: jnp.dot(m01, x, preferred_element_type=F32)
    return d(bh) + d(bm) + d(bl)


def _norm_mod(x, g, shift, scale):
    ms = jnp.mean(x * x, axis=-1, keepdims=True)
    y = x * lax.rsqrt(ms + NORM_EPS) * g
    return y * (1.0 + scale) + shift


def _silu(x):
    return x * jax.nn.sigmoid(x)


def _rows_load(ref):
    m = ref.shape[0] // ROW_SUB
    return jnp.concatenate([ref[pl.ds(j, m, stride=ROW_SUB), :] for j in range(ROW_SUB)], axis=1)


def _rows_store(ref, val):
    m = val.shape[0]
    for j in range(ROW_SUB):
        ref[pl.ds(j, m, stride=ROW_SUB), :] = val[:, j * LANES:(j + 1) * LANES]


def _row_tile(ref, idx):
    return ref.at[pl.ds(pl.multiple_of(idx * ROW_SUB, ROW_SUB), ROW_SUB)]


def _mod_row(i, p_tiles, tiles_per_seq):
    return jnp.where(i < p_tiles, 0, 1 + (i - p_tiles) // tiles_per_seq)


def _mod_kernel(cond_ref, w_ref, b_ref, o_ref):
    o_ref[...] = _dot_f32(_silu(cond_ref[...]), w_ref[...]) + b_ref[...]


def _modulation(cond, ada_w, ada_b):
    depth, _, n6 = ada_w.shape
    tn = 1536
    return pl.pallas_call(
        _mod_kernel,
        grid=(depth, n6 // tn),
        in_specs=[
            pl.BlockSpec(cond.shape, lambda l, j: (0, 0)),
            pl.BlockSpec((None, D, tn), lambda l, j: (l, 0, j)),
            pl.BlockSpec((None, 1, tn), lambda l, j: (l, 0, j)),
        ],
        out_specs=pl.BlockSpec((None, cond.shape[0], tn), lambda l, j: (l, 0, j)),
        out_shape=jax.ShapeDtypeStruct((depth, cond.shape[0], n6), F32),
        compiler_params=_cparams(2),
        name="adaln_mod",
    )(cond, ada_w, ada_b.reshape(depth, 1, n6))


def _rwkv_pre_kernel(x_ref, xu_ref, xd_ref, mod_ref, g_ref, mu_ref, wrkv_ref, g1_ref, g2_ref, w1_ref, w2_ref,
                     w0_ref, a1_ref, a2_ref, a0_ref, kk_ref, ka_ref, rk_ref, hsum_ref, hexp_ref,
                     r_o, v_o, kk_o, lw_o, kd_o, b_o, gate_o, bonus_o, *, p_tiles, tiles_per_seq):
    i = pl.program_id(0)
    is_p = i < p_tiles
    sub = (i - p_tiles) % tiles_per_seq
    mod = mod_ref[...]
    shift, scale = mod[:, 0:D], mod[:, D:2 * D]
    g = g_ref[...]
    h = _norm_mod(x_ref[...], g, shift, scale)
    hu = _norm_mod(xu_ref[...], g, shift, scale)
    hd = _norm_mod(xd_ref[...], g, shift, scale)

    q = D // 4
    row = lax.broadcasted_iota(I32, (TM, 1), 0)
    per = jnp.where(is_p, TM, GRID_W)
    pos = row & (per - 1)

    def prev1(a):
        return jnp.where(pos == 0, 0.0, pltpu.roll(a, 1, 0))

    def next1(a):
        return jnp.where(pos == per - 1, 0.0, pltpu.roll(a, TM - 1, 0))

    h0, h1, h2, h3 = (h[:, j * q:(j + 1) * q] for j in range(4))
    up = jnp.concatenate([jnp.where(sub == 0, 0.0, hu[:, 2 * q:3 * q]), h2[0:TM - GRID_W]], axis=0)
    down = jnp.concatenate([h3[GRID_W:TM], jnp.where(sub == tiles_per_seq - 1, 0.0, hd[:, 3 * q:4 * q])], axis=0)
    s0 = prev1(h0)
    s1 = jnp.where(is_p, prev1(h1), next1(h1))
    s2 = jnp.where(is_p, next1(h2), up)
    s3 = jnp.where(is_p, next1(h3), down)
    xx = jnp.concatenate([s0, s1, s2, s3], axis=1) - h

    mu = mu_ref[...]
    mix = lambda j: h + xx * mu[j:j + 1]
    r = _bdot(mix(0), wrkv_ref[0])
    k = _bdot(mix(2), wrkv_ref[1])
    v = _bdot(mix(3), wrkv_ref[2])
    gate = _bdot(jax.nn.sigmoid(_bdot(mix(5), g1_ref[...])), g2_ref[...])
    w_all = w0_ref[...] + _bdot(jnp.tanh(_bdot(mix(1), w1_ref[...])), w2_ref[...])
    a_all = jax.nn.sigmoid(a0_ref[...] + _bdot(_bdot(mix(4), a1_ref[...]), a2_ref[...]))

    hsum, hexp = hsum_ref[...], hexp_ref[...]
    head_sum = lambda t: _dot_f32_rhs01(_dot_f32_rhs01(t, hsum), hexp)

    kkr = k * kk_ref[...]
    kk = kkr / jnp.maximum(jnp.sqrt(head_sum(kkr * kkr)), 1e-12)
    ka = ka_ref[...]
    kd_sum = jnp.zeros_like(k)
    for d in range(2):
        wd = w_all[:, d * D:(d + 1) * D]
        z = -wd
        softplus = jnp.maximum(z, 0.0) + jnp.log(1.0 + jnp.exp(-jnp.abs(z)))
        lw_o[d] = -jnp.exp(-softplus - 0.5)
        a = a_all[:, d * D:(d + 1) * D]
        kd = k * (1.0 + (a - 1.0) * ka)
        kd_o[d] = kd
        b_o[d] = kk * a
        kd_sum = kd_sum + kd
    r_o[...] = r
    v_o[...] = v
    kk_o[...] = kk
    gate_o[...] = gate
    bonus_o[...] = head_sum(r * kd_sum * rk_ref[...]) * v


def _rwkv_pre(x, mod, p_tiles, tiles_per_seq, norm_g, wts):
    n = x.shape[0]
    n_tiles = n // TM
    hb = TM // GRID_W
    n_hblk = n // GRID_W
    row = lambda a: a.reshape(1, -1)
    full = lambda a: pl.BlockSpec(a.shape, lambda i: (0,) * a.ndim)
    tok = pl.BlockSpec((TM, D), lambda i: (i, 0))
    tok2 = pl.BlockSpec((2, TM, D), lambda i: (0, i, 0))
    consts = [row(norm_g), wts["mu"], wts["wrkv"], wts["g1"], wts["g2"], wts["w1"], wts["w2"], wts["w0"],
              wts["a1"], wts["a2"], wts["a0"], row(wts["k_k"]), row(wts["k_a"]), row(wts["r_k"]),
              wts["hsum"], wts["hexp"]]
    return pl.pallas_call(
        functools.partial(_rwkv_pre_kernel, p_tiles=p_tiles, tiles_per_seq=tiles_per_seq),
        grid=(n_tiles,),
        in_specs=[
            tok,
            pl.BlockSpec((GRID_W, D), lambda i: (jnp.maximum(i * hb - 1, 0), 0)),
            pl.BlockSpec((GRID_W, D), lambda i: (jnp.minimum(i * hb + hb, n_hblk - 1), 0)),
            pl.BlockSpec((None, 1, N_MOD * D), lambda i: (_mod_row(i, p_tiles, tiles_per_seq), 0, 0)),
        ] + [full(a) for a in consts],
        out_specs=[tok, tok, tok, tok2, tok2, tok2, tok, tok],
        out_shape=[jax.ShapeDtypeStruct((n, D), F32)] * 3 + [jax.ShapeDtypeStruct((2, n, D), F32)] * 3
        + [jax.ShapeDtypeStruct((n, D), F32)] * 2,
        compiler_params=_cparams(1),
        name="rwkv_pre",
    )(x, x, x, mod, *consts)


class _ScanItems:
    def __init__(self, n_ctx, ctx_chunks, n_lat, lat_chunks):
        self.n_ctx, self.ctx_chunks, self.n_lat, self.lat_chunks = n_ctx, ctx_chunks, n_lat, lat_chunks
        self.ctx_items = n_ctx * ctx_chunks
        self.n_items = self.ctx_items + n_lat * lat_chunks

    def decode(self, j):
        is_ctx = j < self.ctx_items
        jl = jnp.maximum(j - self.ctx_items, 0)
        c = jnp.where(is_ctx, j % self.ctx_chunks, jl % self.lat_chunks)
        return is_ctx, c, jnp.where(is_ctx, self.ctx_chunks, self.lat_chunks)

    def block(self, d, j):
        _, c, nc = self.decode(j)
        return jnp.where(d == 0, j, j - c + (nc - 1 - c))

    def ctx_seq(self, j):
        return jnp.minimum(j // self.ctx_chunks, self.n_ctx - 1)

    def lat_seq(self, j):
        return jnp.maximum(j - self.ctx_items, 0) // self.lat_chunks


def _rwkv_scan_kernel(rf_ref, vf_ref, kkf_ref, lwf_ref, kdf_ref, bf_ref, rb_ref, vb_ref, kkb_ref, lwb_ref, kdb_ref,
                      bb_ref, s0_ref, yf_ref, yb_ref, sf_ref, s_scr, *, items):
    is_ctx, c, nc = items.decode(pl.program_id(0))
    ch = RWKV_CHUNK

    @pl.when(c == 0)
    def _():
        s_scr[...] = jnp.where(is_ctx, 0.0, s0_ref[...])

    ti = lax.broadcasted_iota(I32, (ch, ch), 0)
    tj = lax.broadcasted_iota(I32, (ch, ch), 1)
    si = lax.broadcasted_iota(I32, (2 * ch, 2 * ch), 0)
    sj = lax.broadcasted_iota(I32, (2 * ch, 2 * ch), 1)
    same = (si < ch) == (sj < ch)
    ui, uj = si & (ch - 1), sj & (ch - 1)
    first = lax.broadcasted_iota(I32, (1, LANES), 1) < HEAD_A

    def stack(xp):
        return jnp.concatenate([jnp.where(first, xp, 0.0), jnp.where(first, 0.0, xp)], axis=0)

    def prologue(rev, r_ref, v_ref, kk_ref, lw_ref, kd_ref, b_ref):
        tri = jnp.where((tj >= ti) if rev else (tj <= ti), 1.0, 0.0).astype(BF16)
        lw = lw_ref[...]
        cum = _dot_f32_lhs01(tri, lw)
        last = cum[0:1] if rev else cum[ch - 1:ch]
        kk, kd, b = kk_ref[...], kd_ref[...], b_ref[...]
        e_neg = jnp.exp(-cum)
        e_rel = jnp.exp(last - cum)
        return dict(at=-kk * jnp.exp(cum - lw), rt=r_ref[...] * jnp.exp(cum), bt=b * e_neg, kt=kd * e_neg,
                    bh=b * e_rel, kh=kd * e_rel, g_all=jnp.exp(last), v=v_ref[...],
                    strict=same & ((uj > ui) if rev else (uj < ui)),
                    incl=same & ((uj >= ui) if rev else (uj <= ui)))

    dirs = [prologue(False, rf_ref, vf_ref, kkf_ref, lwf_ref, kdf_ref, bf_ref),
            prologue(True, rb_ref, vb_ref, kkb_ref, lwb_ref, kdb_ref, bb_ref)]

    units = [(d, p, slice(p * LANES, (p + 1) * LANES)) for d in range(2) for p in range(N_PAIR)]
    pairs = range(len(units))
    s_old = [s_scr[d, p] for d, p, _ in units]
    m1 = [_bdot_nt(jnp.concatenate([stack(dirs[d]["at"][:, sl]), stack(dirs[d]["rt"][:, sl])], axis=0),
                   jnp.concatenate([stack(dirs[d]["bt"][:, sl]), stack(dirs[d]["kt"][:, sl])], axis=0))
          for d, _, sl in units]
    m2 = [_bdot_nt(jnp.concatenate([dirs[d]["at"][:, sl], dirs[d]["rt"][:, sl]], axis=0), s_old[u])
          for u, (d, _, sl) in enumerate(units)]
    vs = [stack(dirs[d]["v"][:, sl]) for d, _, sl in units]
    strict = [dirs[d]["strict"] for d, _, _ in units]
    incl = [dirs[d]["incl"] for d, _, _ in units]
    x = [stack(m2[p][0:ch]) + _bdot(jnp.where(strict[p], m1[p][0:2 * ch, 2 * ch:4 * ch], 0.0), vs[p]) for p in pairs]
    nm = [jnp.where(strict[p], m1[p][0:2 * ch, 0:2 * ch], 0.0).astype(BF16) for p in pairs]
    for _ in range(5):
        sq_ap = [_bdot(nm[p], jnp.concatenate([nm[p], x[p].astype(BF16)], axis=1)) for p in pairs]
        nm = [sq_ap[p][:, 0:2 * ch].astype(BF16) for p in pairs]
        x = [x[p] + sq_ap[p][:, 2 * ch:4 * ch] for p in pairs]
    x = [x[p] + _bdot(nm[p], x[p]) for p in pairs]
    uv = [jnp.concatenate([x[p], vs[p]], axis=0).astype(BF16) for p in pairs]
    ys = [_bdot(jnp.concatenate([jnp.where(incl[p], m1[p][2 * ch:4 * ch, 0:2 * ch], 0.0),
                                 jnp.where(incl[p], m1[p][2 * ch:4 * ch, 2 * ch:4 * ch], 0.0)], axis=1), uv[p])
          for p in pairs]
    s_new = [s_old[u] * dirs[d]["g_all"][:, sl]
             + _bdot_tn(uv[u], jnp.concatenate([stack(dirs[d]["bh"][:, sl]), stack(dirs[d]["kh"][:, sl])], axis=0))
             for u, (d, _, sl) in enumerate(units)]
    for u, (d, _, sl) in enumerate(units):
        (yf_ref, yb_ref)[d][:, sl] = ys[u][0:ch] + ys[u][ch:2 * ch] + m2[u][ch:2 * ch]
    for u, (d, p, _) in enumerate(units):
        s_scr[d, p] = s_new[u]

    @pl.when(is_ctx & (c == nc - 1))
    def _():
        sf_ref[...] = s_scr[...]


def _rwkv_scan(r, v, kk, lw, kd, b, s0, items):
    n = r.shape[0]
    ch = RWKV_CHUNK
    tok = lambda d: pl.BlockSpec((ch, D), lambda j: (items.block(d, j), 0))
    tokd = lambda d: pl.BlockSpec((None, ch, D), lambda j: (d, items.block(d, j), 0))
    st = lambda seq: pl.BlockSpec((2, None, N_PAIR, LANES, LANES), lambda j: (0, seq(j), 0, 0, 0))
    per_dir = lambda d: [tok(d), tok(d), tok(d), tokd(d), tokd(d), tokd(d)]
    return pl.pallas_call(
        functools.partial(_rwkv_scan_kernel, items=items),
        grid=(items.n_items,),
        in_specs=per_dir(0) + per_dir(1) + [st(items.lat_seq)],
        out_specs=[tok(0), tok(1), st(items.ctx_seq)],
        out_shape=[jax.ShapeDtypeStruct((n, D), F32), jax.ShapeDtypeStruct((n, D), F32),
                   jax.ShapeDtypeStruct((2, items.n_ctx, N_PAIR, LANES, LANES), F32)],
        scratch_shapes=[pltpu.VMEM((2, N_PAIR, LANES, LANES), F32)],
        compiler_params=_cparams(1),
        name="rwkv_scan",
    )(r, v, kk, lw, kd, b, r, v, kk, lw, kd, b, s0)


def _rwkv_post_kernel(x_ref, yf_ref, yb_ref, gate_ref, bonus_ref, mod_ref, lnw_ref, lnb_ref, wo_ref, hsum_ref, hexp_ref,
                      o_ref):
    hsum, hexp = hsum_ref[...], hexp_ref[...]
    head_mean = lambda t: _dot_f32_rhs01(_dot_f32_rhs01(t, hsum), hexp) * (1.0 / HEAD_A)
    y = yf_ref[...] + yb_ref[...]
    yc = y - head_mean(y)
    yn = yc * lax.rsqrt(head_mean(yc * yc) + LNX_EPS)
    z = (yn * lnw_ref[...] + lnb_ref[...] + bonus_ref[...]) * gate_ref[...]
    o_ref[...] = x_ref[...] + mod_ref[:, 2 * D:3 * D] * _bdot(z, wo_ref[...])


def _rwkv_post(x, y_fwd, y_bwd, gate, bonus, mod, p_tiles, tiles_per_seq, ln_w, ln_b, wo, hsum, hexp):
    n = x.shape[0]
    full = lambda a: pl.BlockSpec(a.shape, lambda i: (0,) * a.ndim)
    tok = pl.BlockSpec((TM, D), lambda i: (i, 0))
    consts = [ln_w.reshape(1, D), ln_b.reshape(1, D), wo, hsum, hexp]
    return pl.pallas_call(
        _rwkv_post_kernel,
        grid=(n // TM,),
        in_specs=[tok, tok, tok, tok, tok,
                  pl.BlockSpec((None, 1, N_MOD * D), lambda i: (_mod_row(i, p_tiles, tiles_per_seq), 0, 0))]
        + [full(a) for a in consts],
        out_specs=tok,
        out_shape=jax.ShapeDtypeStruct((n, D), F32),
        compiler_params=_cparams(1),
        name="rwkv_post",
    )(x, y_fwd, y_bwd, gate, bonus, mod, *consts)


def _ret_pre_kernel(x_ref, mod_ref, g_ref, cos_ref, sin_ref, win_ref, q_o, k_o, v_o, gate_o):
    mod = mod_ref[...]
    h = _norm_mod(x_ref[...], g_ref[...], mod[:, 0:D], mod[:, D:2 * D])
    proj = _bdot(h, win_ref[...])
    cos, sin = cos_ref[...], sin_ref[...]

    def rope(t):
        outs = []
        for j in range(D // LANES):
            tj = t[:, j * LANES:(j + 1) * LANES]
            cj = cos[:, (j % 2) * LANES:(j % 2 + 1) * LANES]
            sj = sin[:, (j % 2) * LANES:(j % 2 + 1) * LANES]
            outs.append(tj * cj + pltpu.roll(tj, LANES // 2, 1) * sj)
        return jnp.concatenate(outs, axis=1)

    q_o[...] = rope(proj[:, 0:D]).astype(BF16)
    k_o[...] = rope(proj[:, D:2 * D] * (DK_B ** -0.5)).astype(BF16)
    v_o[...] = proj[:, 2 * D:4 * D].astype(BF16)
    gate_o[...] = proj[:, 4 * D:6 * D]


def _ret_pre(x, mod, p_tiles, tiles_per_seq, norm_g, cos_t, sin_t, w_in):
    n = x.shape[0]
    full = lambda a: pl.BlockSpec(a.shape, lambda i: (0,) * a.ndim)
    tok = lambda w: pl.BlockSpec((TM, w), lambda i: (i, 0))
    tab = pl.BlockSpec((TM, DK_B), lambda i: (jnp.where(i < p_tiles, 0, 1 + (i - p_tiles) % tiles_per_seq), 0))
    return pl.pallas_call(
        _ret_pre_kernel,
        grid=(n // TM,),
        in_specs=[tok(D), pl.BlockSpec((None, 1, N_MOD * D), lambda i: (_mod_row(i, p_tiles, tiles_per_seq), 0, 0)),
                  full(norm_g.reshape(1, D)), tab, tab, full(w_in)],
        out_specs=[tok(D), tok(D), tok(2 * D), tok(2 * D)],
        out_shape=[jax.ShapeDtypeStruct((n, D), BF16), jax.ShapeDtypeStruct((n, D), BF16),
                   jax.ShapeDtypeStruct((n, 2 * D), BF16), jax.ShapeDtypeStruct((n, 2 * D), F32)],
        compiler_params=_cparams(1),
        name="ret_pre",
    )(x, mod, norm_g.reshape(1, D), cos_t, sin_t, w_in)


def _ret_scan_kernel(lg_ref, qf_ref, kf_ref, vf_ref, qb_ref, kb_ref, vb_ref, s0_ref, of_ref, ob_ref, sf_ref, s_scr, *,
                     items):
    is_ctx, c, nc = items.decode(pl.program_id(0))
    ch = RET_CHUNK

    @pl.when(c == 0)
    def _():
        s_scr[...] = jnp.where(is_ctx, 0.0, s0_ref[...])

    ti = lax.broadcasted_iota(I32, (ch, ch), 0)
    tj = lax.broadcasted_iota(I32, (ch, ch), 1)
    io = ((qf_ref, kf_ref, vf_ref, of_ref), (qb_ref, kb_ref, vb_ref, ob_ref))
    units = [(d, hd) for d in range(2) for hd in range(H_B)]

    def decays(d, hd):
        rev = d == 1
        lg = lg_ref[d, hd]
        rel = ((tj - ti) if rev else (ti - tj)).astype(F32)
        steps_q = ((ch - ti) if rev else (ti + 1)).astype(F32)
        steps_k = (ti if rev else (ch - 1 - ti)).astype(F32)
        return dict(mask=jnp.where(rel >= 0, jnp.exp(jnp.maximum(rel, 0.0) * lg), 0.0),
                    q_dec=jnp.exp(steps_q * lg), k_dec=jnp.exp(steps_k * lg),
                    chunk_dec=jnp.exp(jnp.full((1, DV_B), float(ch), F32) * lg))

    dec = [decays(d, hd) for d, hd in units]
    qh = [io[d][0][:, hd * DK_B:(hd + 1) * DK_B] for d, hd in units]
    kh = [io[d][1][:, hd * DK_B:(hd + 1) * DK_B] for d, hd in units]
    vh = [io[d][2][:, hd * DV_B:(hd + 1) * DV_B] for d, hd in units]
    s_old = [s_scr[d, hd] for d, hd in units]
    rng = range(len(units))
    scores = [_bdot_nt(qh[u], kh[u]) * dec[u]["mask"] for u in rng]
    cross = [_bdot(qh[u], s_old[u]) * jnp.concatenate([dec[u]["q_dec"]] * (DV_B // ch), axis=1) for u in rng]
    out = [_bdot(scores[u], vh[u]) + cross[u] for u in rng]
    s_new = [s_old[u] * dec[u]["chunk_dec"]
             + _bdot_tn(kh[u].astype(F32) * jnp.concatenate([dec[u]["k_dec"]] * (DK_B // ch), axis=1), vh[u])
             for u in rng]
    for u, (d, hd) in enumerate(units):
        io[d][3][:, hd * DV_B:(hd + 1) * DV_B] = out[u]
    for u, (d, hd) in enumerate(units):
        s_scr[d, hd] = s_new[u]

    @pl.when(is_ctx & (c == nc - 1))
    def _():
        sf_ref[...] = s_scr[...]


def _ret_scan(log_gamma, q, k, v, s0, items):
    n = q.shape[0]
    ch = RET_CHUNK
    tok = lambda d, w: pl.BlockSpec((ch, w), lambda j: (items.block(d, j), 0))
    st = lambda seq: pl.BlockSpec((2, None, H_B, DK_B, DV_B), lambda j: (0, seq(j), 0, 0, 0))
    per_dir = lambda d: [tok(d, D), tok(d, D), tok(d, 2 * D)]
    return pl.pallas_call(
        functools.partial(_ret_scan_kernel, items=items),
        grid=(items.n_items,),
        in_specs=[pl.BlockSpec(memory_space=pltpu.SMEM)] + per_dir(0) + per_dir(1) + [st(items.lat_seq)],
        out_specs=[tok(0, 2 * D), tok(1, 2 * D), st(items.ctx_seq)],
        out_shape=[jax.ShapeDtypeStruct((n, 2 * D), F32), jax.ShapeDtypeStruct((n, 2 * D), F32),
                   jax.ShapeDtypeStruct((2, items.n_ctx, H_B, DK_B, DV_B), F32)],
        scratch_shapes=[pltpu.VMEM((2, H_B, DK_B, DV_B), F32)],
        compiler_params=_cparams(1),
        name="ret_scan",
    )(log_gamma, q, k, v, q, k, v, s0)


def _ret_post_kernel(x_ref, of_ref, ob_ref, gate_ref, mod_ref, wout_ref, out_ref):
    o = of_ref[...] + ob_ref[...]
    parts = []
    for hd in range(H_B):
        oh = o[:, hd * DV_B:(hd + 1) * DV_B]
        parts.append(oh * lax.rsqrt(jnp.mean(oh * oh, axis=-1, keepdims=True) + NORM_EPS))
    y = _silu(gate_ref[...]) * jnp.concatenate(parts, axis=1)
    out_ref[...] = x_ref[...] + mod_ref[:, 2 * D:3 * D] * _bdot(y, wout_ref[...])


def _ret_post(x, o_fwd, o_bwd, gate, mod, p_tiles, tiles_per_seq, w_out):
    n = x.shape[0]
    tok = lambda w: pl.BlockSpec((TM, w), lambda i: (i, 0))
    return pl.pallas_call(
        _ret_post_kernel,
        grid=(n // TM,),
        in_specs=[tok(D), tok(2 * D), tok(2 * D), tok(2 * D),
                  pl.BlockSpec((None, 1, N_MOD * D), lambda i: (_mod_row(i, p_tiles, tiles_per_seq), 0, 0)),
                  pl.BlockSpec(w_out.shape, lambda i: (0, 0))],
        out_specs=tok(D),
        out_shape=jax.ShapeDtypeStruct((n, D), F32),
        compiler_params=_cparams(1),
        name="ret_post",
    )(x, o_fwd, o_bwd, gate, mod, w_out)


def _moe_route_kernel(x_ref, mod_ref, g_ref, router_ref, bias_ref, h_o, e_o, w_o, p_o, cnt_o, carry):
    i = pl.program_id(0)

    @pl.when(i == 0)
    def _():
        carry[...] = jnp.zeros_like(carry)

    mod = mod_ref[...]
    h = _norm_mod(x_ref[...], g_ref[...], mod[:, 3 * D:4 * D], mod[:, 4 * D:5 * D])
    _rows_store(h_o, h)
    lane =lax.broadcasted_iota(I32, (TM, LANES), 1)
    valid = lane < N_EXPERTS
    neg = -jnp.inf
    scores = jax.nn.sigmoid(_dot_f32(h, router_ref[...]))
    biased = jnp.where(valid, scores + bias_ref[...], neg)

    def group_reduce(t, op):
        s = 1
        while s < PER_GROUP:
            partner = jnp.where((lane & s) == 0, pltpu.roll(t, LANES - s, 1), pltpu.roll(t, s, 1))
            t = op(t, partner)
            s *= 2
        return t

    lane_f = lane.astype(F32)
    group_f = jnp.floor(lane_f * (1.0 / PER_GROUP))

    def first_lane_of_max(t):
        m = jnp.max(t, axis=-1, keepdims=True)
        return jnp.min(jnp.where(t == m, lane_f, float(LANES)), axis=-1, keepdims=True)

    m1 = group_reduce(biased, jnp.maximum)
    first1 = group_reduce(jnp.where(biased == m1, lane_f, float(LANES)), jnp.minimum)
    m2 = group_reduce(jnp.where(lane_f == first1, neg, biased), jnp.maximum)
    gscore = jnp.where(valid, m1 + m2, neg)
    cand = jnp.full((TM, LANES), neg, F32)
    for _ in range(TOPK_GROUPS):
        gsel = group_f == jnp.floor(first_lane_of_max(gscore) * (1.0 / PER_GROUP))
        cand = jnp.where(gsel, biased, cand)
        gscore = jnp.where(gsel, neg, gscore)
    hits = []
    sel01 = jnp.zeros((TM, LANES), F32)
    e_cols = jnp.zeros((TM, LANES), F32)
    for j in range(TOP_K):
        fl = first_lane_of_max(cand)
        hit = lane_f == fl
        hits.append(hit)
        sel01 = jnp.where(hit, 1.0, sel01)
        cand = jnp.where(hit, neg, cand)
        e_cols = jnp.where(lane == j, fl, e_cols)
    wsum = jnp.sum(sel01 * scores, axis=-1, keepdims=True)

    ri = lax.broadcasted_iota(I32, (TM, TM), 0)
    rj = lax.broadcasted_iota(I32, (TM, TM), 1)
    below = jnp.where(rj < ri, 1.0, 0.0).astype(BF16)
    rank = jnp.dot(below, sel01.astype(BF16), preferred_element_type=F32) + carry[...]
    carry[...] = carry[...] + jnp.sum(sel01, axis=0, keepdims=True)
    w_cols = jnp.zeros((TM, LANES), F32)
    p_cols = jnp.zeros((TM, LANES), F32)
    for j in range(TOP_K):
        wj = jnp.sum(jnp.where(hits[j], scores, 0.0), axis=-1, keepdims=True)
        w_cols = jnp.where(lane == j, wj / wsum * ROUTED_SCALE, w_cols)
        p_cols = jnp.where(lane == j, jnp.sum(jnp.where(hits[j], rank, 0.0), axis=-1, keepdims=True), p_cols)
    e_o[...] = e_cols.astype(I32)
    w_o[...] = w_cols
    p_o[...] = p_cols.astype(I32)
    cnt_o[...] = jnp.broadcast_to(carry[...], cnt_o.shape)


def _moe_route(x, mod, p_tiles, tiles_per_seq, norm_g, router, bias):
    n = x.shape[0]
    full = lambda a: pl.BlockSpec(a.shape, lambda i: (0,) * a.ndim)
    tok = lambda w: pl.BlockSpec((TM, w), lambda i: (i, 0))
    router_p = jnp.pad(router, ((0, 0), (0, LANES - N_EXPERTS)))
    bias_p = jnp.pad(bias, (0, LANES - N_EXPERTS)).reshape(1, LANES)
    return pl.pallas_call(
        _moe_route_kernel,
        grid=(n // TM,),
        in_specs=[tok(D), pl.BlockSpec((None, 1, N_MOD * D), lambda i: (_mod_row(i, p_tiles, tiles_per_seq), 0, 0)),
                  full(norm_g.reshape(1, D)), full(router_p), full(bias_p)],
        out_specs=[pl.BlockSpec((TM * ROW_SUB, LANES), lambda i: (i, 0)), tok(LANES), tok(LANES), tok(LANES),
                   pl.BlockSpec((8, LANES), lambda i: (0, 0))],
        out_shape=[jax.ShapeDtypeStruct((n * ROW_SUB, LANES), F32), jax.ShapeDtypeStruct((n, LANES), I32),
                   jax.ShapeDtypeStruct((n, LANES), F32), jax.ShapeDtypeStruct((n, LANES), I32),
                   jax.ShapeDtypeStruct((8, LANES), F32)],
        scratch_shapes=[pltpu.VMEM((1, LANES), F32)],
        compiler_params=_cparams(1),
        name="moe_route",
    )(x, mod, norm_g.reshape(1, D), router_p, bias_p)


DISPATCH_TOK = 512


def _moe_dispatch_kernel(pad_lo_ref, pad_n_ref, dest_ref, h_ref, xs_hbm, zero_scr, sem):
    @pl.when(pl.program_id(0) == 0)
    def _():
        zero_scr[...] = jnp.zeros_like(zero_scr)

        def pad_copy(e, s):
            return pltpu.make_async_copy(zero_scr, _row_tile(xs_hbm, pad_lo_ref[e] + s), sem)

        def per_expert(e, carry):
            def issue(s, c):
                pad_copy(e, s).start()
                return c

            def drain(s, c):
                pad_copy(e, s).wait()
                return c

            lax.fori_loop(0, pad_n_ref[e], issue, 0)
            lax.fori_loop(0, pad_n_ref[e], drain, 0)
            return carry

        lax.fori_loop(0, N_EXPERTS, per_expert, 0)

    def copy(t, kq):
        return pltpu.make_async_copy(_row_tile(h_ref, t), _row_tile(xs_hbm, dest_ref[t * TOP_K + kq]), sem)

    def issue(t, carry):
        for kq in range(TOP_K):
            copy(t, kq).start(priority=kq % 2)
        return carry

    def drain(t, carry):
        for kq in range(TOP_K):
            copy(t, kq).wait()
        return carry

    lax.fori_loop(0, DISPATCH_TOK, issue, 0)
    lax.fori_loop(0, DISPATCH_TOK, drain, 0)


def _moe_dispatch(pad_lo, pad_n, dest_flat, h, n_slots):
    n = h.shape[0] // ROW_SUB
    grid_spec = pltpu.PrefetchScalarGridSpec(
        num_scalar_prefetch=2,
        grid=(n // DISPATCH_TOK,),
        in_specs=[pl.BlockSpec((DISPATCH_TOK * TOP_K,), lambda i, *_: (i,), memory_space=pltpu.SMEM),
                  pl.BlockSpec((DISPATCH_TOK * ROW_SUB, LANES), lambda i, *_: (i, 0))],
        out_specs=pl.BlockSpec(memory_space=pl.ANY),
        scratch_shapes=[pltpu.VMEM((ROW_SUB, LANES), F32), pltpu.SemaphoreType.DMA],
    )
    return pl.pallas_call(
        _moe_dispatch_kernel,
        grid_spec=grid_spec,
        out_shape=jax.ShapeDtypeStruct((n_slots * ROW_SUB, LANES), F32),
        compiler_params=_cparams(1, has_side_effects=True),
        name="moe_dispatch",
    )(pad_lo, pad_n, dest_flat, h)


def _moe_expert_kernel(be_ref, xs_ref, wgu_ref, wdn_ref, o_ref, wgu_bf, wdn_bf):
    i = pl.program_id(0)

    @pl.when((i == 0) | (be_ref[i] != be_ref[jnp.maximum(i - 1, 0)]))
    def _():
        wgu_bf[...] = wgu_ref[...].astype(BF16)
        wdn_bf[...] = wdn_ref[...].astype(BF16)

    gu = _bdot(_rows_load(xs_ref), wgu_bf[...])
    act = _silu(gu[:, 0:D_EXPERT]) * gu[:, D_EXPERT:2 * D_EXPERT]
    _rows_store(o_ref, _bdot(act, wdn_bf[...]))


def _moe_expert(block_e, xs, w_gu, w_down, layer):
    n_slots = xs.shape[0] // ROW_SUB
    n_blocks = n_slots // EXPERT_BLOCK
    rows = pl.BlockSpec((EXPERT_BLOCK * ROW_SUB, LANES), lambda i, be: (i, 0))
    grid_spec = pltpu.PrefetchScalarGridSpec(
        num_scalar_prefetch=1,
        grid=(n_blocks,),
        in_specs=[rows,
                  pl.BlockSpec((None, None, D, 2 * D_EXPERT), lambda i, be: (layer, be[i], 0, 0)),
                  pl.BlockSpec((None, None, D_EXPERT, D), lambda i, be: (layer, be[i], 0, 0))],
        out_specs=rows,
        scratch_shapes=[pltpu.VMEM((D, 2 * D_EXPERT), BF16), pltpu.VMEM((D_EXPERT, D), BF16)],
    )
    return pl.pallas_call(
        _moe_expert_kernel,
        grid_spec=grid_spec,
        out_shape=jax.ShapeDtypeStruct((n_slots * ROW_SUB, LANES), F32),
        compiler_params=_cparams(1),
        name="moe_expert",
    )(block_e, xs, w_gu, w_down)


COMBINE_TOK = 128


def _moe_combine_kernel(dest_ref, dest_next_ref, x_ref, h_ref, w_ref, mod_ref, shgu_ref, shdn_ref, gfin_ref, ys_hbm,
                        *refs, final_norm, ctx_tiles):
    o_refs, (buf, sems) = refs[:-2], refs[-2:]
    i = pl.program_id(0)
    slot = i % 2

    def copy(idx_ref, s, t, kq):
        return pltpu.make_async_copy(_row_tile(ys_hbm, idx_ref[t * TOP_K + kq]), _row_tile(buf.at[s, kq], t),
                                     sems.at[s])

    def fetch(idx_ref, s):
        def body(t, carry):
            for kq in range(TOP_K):
                copy(idx_ref, s, t, kq).start(priority=kq % 2)
            return carry
        lax.fori_loop(0, COMBINE_TOK, body, 0, unroll=2)

    @pl.when(i == 0)
    def _():
        fetch(dest_ref, slot)

    @pl.when(i + 1 < pl.num_programs(0))
    def _():
        fetch(dest_next_ref, 1 - slot)

    gu = _bdot(_rows_load(h_ref), shgu_ref[...])
    acc = _bdot(_silu(gu[:, 0:D_EXPERT]) * gu[:, D_EXPERT:2 * D_EXPERT], shdn_ref[...])

    def drain(t, carry):
        for kq in range(TOP_K):
            copy(dest_ref, slot, t, kq).wait()
        return carry

    lax.fori_loop(0, COMBINE_TOK, drain, 0)
    w = w_ref[...]
    for kq in range(TOP_K):
        acc = acc + _rows_load(buf.at[slot, kq]) * w[:, kq:kq + 1]
    out = x_ref[...] + mod_ref[:, 5 * D:6 * D] * acc
    if final_norm:
        out = out * lax.rsqrt(jnp.mean(out * out, axis=-1, keepdims=True) + NORM_EPS) * gfin_ref[...]

        @pl.when(i < ctx_tiles)
        def _():
            o_refs[0][...] = out

        @pl.when(i >= ctx_tiles)
        def _():
            o_refs[1][...] = out
    else:
        o_refs[0][...] = out


def _moe_combine(dest_flat, x, h, w_cols, mod, p_tiles, tiles_per_seq, sh_gu, sh_down, g_final, ys, final_norm):
    n = x.shape[0]
    ratio = TM // COMBINE_TOK
    tok = lambda w: pl.BlockSpec((COMBINE_TOK, w), lambda i: (i, 0))
    full = lambda a: pl.BlockSpec(a.shape, lambda i: (0,) * a.ndim)
    n_tiles = n // COMBINE_TOK
    ctx_tiles = p_tiles * ratio
    if final_norm:
        out_specs = [pl.BlockSpec((COMBINE_TOK, D), lambda i: (jnp.minimum(i, ctx_tiles - 1), 0)),
                     pl.BlockSpec((COMBINE_TOK, D), lambda i: (jnp.maximum(i - ctx_tiles, 0), 0))]
        out_shape = [jax.ShapeDtypeStruct((ctx_tiles * COMBINE_TOK, D), F32),
                     jax.ShapeDtypeStruct((n - ctx_tiles * COMBINE_TOK, D), F32)]
    else:
        out_specs, out_shape = tok(D), jax.ShapeDtypeStruct((n, D), F32)
    return pl.pallas_call(
        functools.partial(_moe_combine_kernel, final_norm=final_norm, ctx_tiles=ctx_tiles),
        grid=(n_tiles,),
        in_specs=[pl.BlockSpec((COMBINE_TOK * TOP_K,), lambda i: (i,), memory_space=pltpu.SMEM),
                  pl.BlockSpec((COMBINE_TOK * TOP_K,), lambda i: (jnp.minimum(i + 1, n_tiles - 1),),
                               memory_space=pltpu.SMEM),
                  tok(D), pl.BlockSpec((COMBINE_TOK * ROW_SUB, LANES), lambda i: (i, 0)), tok(LANES),
                  pl.BlockSpec((None, 1, N_MOD * D), lambda i: (_mod_row(i // ratio, p_tiles, tiles_per_seq), 0, 0)),
                  full(sh_gu), full(sh_down), full(g_final),
                  pl.BlockSpec(memory_space=pl.ANY)],
        out_specs=out_specs,
        out_shape=out_shape,
        scratch_shapes=[pltpu.VMEM((2, TOP_K, COMBINE_TOK * ROW_SUB, LANES), F32), pltpu.SemaphoreType.DMA((2,))],
        compiler_params=_cparams(1),
        name="moe_combine",
    )(dest_flat, dest_flat, x, h, w_cols, mod, sh_gu, sh_down, g_final, ys)


def _moe_layer(x, mod, p_tiles, tiles_per_seq, norm_g, router, bias, w_gu, w_down, layer, sh_gu, sh_down, g_final,
               final_norm):
    n = x.shape[0]
    h, e_cols, w_cols, p_cols, counts = _moe_route(x, mod, p_tiles, tiles_per_seq, norm_g, router, bias)
    counts = counts[0, :N_EXPERTS].astype(I32)
    padded = (counts + EXPERT_BLOCK - 1) // EXPERT_BLOCK * EXPERT_BLOCK
    pad_end = jnp.cumsum(padded)
    pad_start = pad_end - padded
    chosen = e_cols[:, :TOP_K, None] == jnp.arange(N_EXPERTS, dtype=I32)
    dest = (jnp.sum(jnp.where(chosen, pad_start, 0), axis=-1) + p_cols[:, :TOP_K]).reshape(-1)
    n_blocks = n * TOP_K // EXPERT_BLOCK + N_EXPERTS
    n_slots = n_blocks * EXPERT_BLOCK
    block_start = jnp.arange(n_blocks, dtype=I32) * EXPERT_BLOCK
    block_e = jnp.minimum(jnp.sum((pad_end[None, :] <= block_start[:, None]).astype(I32), axis=1), N_EXPERTS - 1)
    pad_n = (padded - counts).at[N_EXPERTS - 1].add(n_slots - pad_end[N_EXPERTS - 1])
    xs = _moe_dispatch(pad_start + counts, pad_n, dest, h, n_slots)
    ys = _moe_expert(block_e, xs, w_gu, w_down, layer)
    return _moe_combine(dest, x, h, w_cols, mod, p_tiles, tiles_per_seq, sh_gu, sh_down, g_final.reshape(1, D), ys,
                        final_norm)


def _pair_pack(s):
    lead = s.shape[:-3]
    s = s.reshape(lead + (N_PAIR, 2, HEAD_A, HEAD_A))
    z = jnp.zeros_like(s[..., 0, :, :])
    top = jnp.concatenate([s[..., 0, :, :], z], axis=-1)
    bot = jnp.concatenate([z, s[..., 1, :, :]], axis=-1)
    return jnp.concatenate([top, bot], axis=-2)


def _pair_unpack(s):
    a = s[..., :HEAD_A, :HEAD_A]
    b = s[..., HEAD_A:, HEAD_A:]
    return jnp.stack([a, b], axis=-3).reshape(s.shape[:-3] + (H_A, HEAD_A, HEAD_A))


def _rope_tables(rows):
    half = DK_B // 2
    n_freq = half // 2
    inv = ROPE_BASE ** (-jnp.arange(n_freq, dtype=F32) / n_freq)
    pos_r = jnp.repeat(jnp.arange(rows, dtype=F32), GRID_W)
    pos_c = jnp.tile(jnp.arange(GRID_W, dtype=F32), rows)

    def tab(pos):
        ang = pos[:, None] * inv[None, :]
        c, s = jnp.cos(ang), jnp.sin(ang)
        return jnp.concatenate([c, c], -1), jnp.concatenate([-s, s], -1)

    cr, sr = tab(pos_r)
    cc, sc = tab(pos_c)
    cos = jnp.concatenate([cr, cc], -1)
    sin = jnp.concatenate([sr, sc], -1)
    cos = jnp.concatenate([jnp.ones((TM, DK_B), F32), cos], 0)
    sin = jnp.concatenate([jnp.zeros((TM, DK_B), F32), sin], 0)
    return cos, sin


def kernel(x_prompt, x_sample, state_rwkv, state_ret, c, c_ctx, ada_w, ada_b, norm_mix, norm_ffn, norm_final, rwkv_mu, rwkv_w0, rwkv_w1, rwkv_w2, rwkv_a0, rwkv_a1, rwkv_a2, rwkv_wrkv, rwkv_wo, rwkv_g1, rwkv_g2, rwkv_k_k, rwkv_k_a, rwkv_r_k, rwkv_ln_w, rwkv_ln_b, ret_w_in, ret_w_out, ret_decay_logit, moe_router, moe_bias, moe_w_gu, moe_w_down, moe_sh_gu, moe_sh_down):
    bp, tp, _ = x_prompt.shape
    bs, ts, _ = x_sample.shape
    n_p, n_s = bp * tp, bs * ts
    assert tp == TM and ts % TM == 0 and TM % GRID_W == 0
    p_tiles = n_p // TM
    tiles_per_seq = ts // TM
    rows = ts // GRID_W
    x = jnp.concatenate([x_prompt.reshape(n_p, D), x_sample.reshape(n_s, D)], axis=0)
    n = n_p + n_s

    n_cond = 16
    cond = jnp.zeros((n_cond, D), F32).at[0].set(c_ctx).at[1:1 + bs].set(c)
    mod = _modulation(cond, ada_w, ada_b).reshape(ada_w.shape[0], n_cond, 1, N_MOD * D)

    head_of = jnp.arange(D, dtype=I32) // HEAD_A
    hsum = (head_of[:, None] == jnp.arange(LANES, dtype=I32)[None, :]).astype(BF16)
    hexp = hsum.T

    gl = rwkv_g1.shape[-1]
    glp = -(-gl // LANES) * LANES
    zb = lambda a: jnp.zeros_like(a)
    wts = {
        "mu": jnp.pad(rwkv_mu[0], ((0, 8 - N_MOD), (0, 0))),
        "wrkv": rwkv_wrkv[0].astype(BF16),
        "g1": jnp.pad(rwkv_g1[0], ((0, 0), (0, glp - gl))).astype(BF16),
        "g2": jnp.pad(rwkv_g2[0], ((0, glp - gl), (0, 0))).astype(BF16),
        "w1": jnp.concatenate([rwkv_w1[0, 0], rwkv_w1[0, 1]], axis=1).astype(BF16),
        "w2": jnp.concatenate([jnp.concatenate([rwkv_w2[0, 0], zb(rwkv_w2[0, 1])], 1),
                               jnp.concatenate([zb(rwkv_w2[0, 0]), rwkv_w2[0, 1]], 1)], 0).astype(BF16),
        "w0": rwkv_w0[0].reshape(1, 2 * D),
        "a1": jnp.concatenate([rwkv_a1[0, 0], rwkv_a1[0, 1]], axis=1).astype(BF16),
        "a2": jnp.concatenate([jnp.concatenate([rwkv_a2[0, 0], zb(rwkv_a2[0, 1])], 1),
                               jnp.concatenate([zb(rwkv_a2[0, 0]), rwkv_a2[0, 1]], 1)], 0).astype(BF16),
        "a0": rwkv_a0[0].reshape(1, 2 * D),
        "k_k": rwkv_k_k[0], "k_a": rwkv_k_a[0], "r_k": rwkv_r_k[0].reshape(D),
        "hsum": hsum, "hexp": hexp,
    }
    r, v, kk, lw, kd, b, gate, bonus = _rwkv_pre(x, mod[0], p_tiles, tiles_per_seq, norm_mix[0], wts)
    s0_lat = _pair_pack(jnp.moveaxis(state_rwkv[:, 0], 1, 0))
    y_fwd, y_bwd, s_fin = _rwkv_scan(r, v, kk, lw, kd, b, s0_lat,
                                     _ScanItems(bp, tp // RWKV_CHUNK, bs, ts // RWKV_CHUNK))
    new_state_rwkv = jnp.moveaxis(_pair_unpack(s_fin), 0, 1)[:, None]
    x = _rwkv_post(x, y_fwd, y_bwd, gate, bonus, mod[0], p_tiles, tiles_per_seq, rwkv_ln_w[0], rwkv_ln_b[0],
                   rwkv_wo[0].astype(BF16), hsum, hexp)
    x = _moe_layer(x, mod[0], p_tiles, tiles_per_seq, norm_ffn[0], moe_router[0], moe_bias[0],
                   moe_w_gu, moe_w_down, 0, moe_sh_gu[0].astype(BF16),
                   moe_sh_down[0].astype(BF16), norm_final, False)

    cos_t, sin_t = _rope_tables(rows)
    q, k, vv, rgate = _ret_pre(x, mod[1], p_tiles, tiles_per_seq, norm_mix[1], cos_t, sin_t, ret_w_in[0].astype(BF16))
    log_gamma = jax.nn.log_sigmoid(ret_decay_logit[0].astype(F32))
    o_fwd, o_bwd, r_fin = _ret_scan(log_gamma, q, k, vv, jnp.moveaxis(state_ret[:, 0], 1, 0),
                                    _ScanItems(bp, tp // RET_CHUNK, bs, ts // RET_CHUNK))
    new_state_ret = jnp.moveaxis(r_fin, 0, 1)[:, None]
    x = _ret_post(x, o_fwd, o_bwd, rgate, mod[1], p_tiles, tiles_per_seq, ret_w_out[0].astype(BF16))
    y_ctx, y_lat = _moe_layer(x, mod[1], p_tiles, tiles_per_seq, norm_ffn[1], moe_router[1], moe_bias[1],
                              moe_w_gu, moe_w_down, 1, moe_sh_gu[1].astype(BF16),
                              moe_sh_down[1].astype(BF16), norm_final, True)

    return (y_ctx.reshape(bp, tp, D), y_lat.reshape(bs, ts, D), new_state_rwkv, new_state_ret)
```

```python
import functools

import jax
import jax.numpy as jnp
from jax import lax
from jax.experimental import pallas as pl
from jax.experimental.pallas import tpu as pltpu

F32, BF16, I32 = jnp.float32, jnp.bfloat16, jnp.int32

D = 1024
N_MOD = 6
NORM_EPS = 1e-6
GRID_W = 64
HEAD_A = 64
H_A = D // HEAD_A
LNX_EPS = 64e-5
RWKV_CHUNK = 64
N_PAIR = H_A // 2
H_B = 4
DK_B = D // H_B
DV_B = 2 * DK_B
RET_CHUNK = 128
ROPE_BASE = 10000.0
N_EXPERTS = 64
TOP_K = 8
N_GROUPS = 8
TOPK_GROUPS = 4
PER_GROUP = N_EXPERTS // N_GROUPS
D_EXPERT = 256
ROUTED_SCALE = 2.5
EXPERT_BLOCK = 256

TM = 256
LANES = 128
ROW_SUB = D // LANES
VMEM_LIMIT = 56 * 1024 * 1024


def _cparams(n_grid_axes, **kw):
    return pltpu.CompilerParams(dimension_semantics=("arbitrary",) * n_grid_axes, vmem_limit_bytes=VMEM_LIMIT, **kw)


def _bdot(a, b):
    return jnp.dot(a.astype(BF16), b.astype(BF16), preferred_element_type=F32)


def _bdot_nt(a, b):
    return lax.dot_general(a.astype(BF16), b.astype(BF16), (((1,), (1,)), ((), ())), preferred_element_type=F32)


def _bdot_tn(a, b):
    return lax.dot_general(a.astype(BF16), b.astype(BF16), (((0,), (0,)), ((), ())), preferred_element_type=F32)


def _split3(x):
    hi = x.astype(BF16)
    r1 = x - hi.astype(F32)
    mid = r1.astype(BF16)
    lo = (r1 - mid.astype(F32)).astype(BF16)
    return hi, mid, lo


def _dot_f32(a, b):
    ah, am, al = _split3(a)
    bh, bm, bl = _split3(b)
    d = lambda x, y: jnp.dot(x, y, preferred_element_type=F32)
    return d(ah, bh) + (d(ah, bm) + d(am, bh)) + (d(ah, bl) + d(al, bh) + d(am, bm))


def _dot_f32_rhs01(a, m01):
    ah, am, al = _split3(a)
    d = lambda x: jnp.dot(x, m01, preferred_element_type=F32)
    return d(ah) + d(am) + d(al)


def _dot_f32_lhs01(m01, b):
    bh, bm, bl = _split3(b)
    d = lambda x: jnp.dot(m01, x, preferred_element_type=F32)
    return d(bh) + d(bm) + d(bl)


def _norm_mod(x, g, shift, scale):
    ms = jnp.mean(x * x, axis=-1, keepdims=True)
    y = x * lax.rsqrt(ms + NORM_EPS) * g
    return y * (1.0 + scale) + shift


def _silu(x):
    return x * jax.nn.sigmoid(x)


def _rows_load(ref):
    m = ref.shape[0] // ROW_SUB
    return jnp.concatenate([ref[pl.ds(j, m, stride=ROW_SUB), :] for j in range(ROW_SUB)], axis=1)


def _rows_store(ref, val):
    m = val.shape[0]
    for j in range(ROW_SUB):
        ref[pl.ds(j, m, stride=ROW_SUB), :] = val[:, j * LANES:(j + 1) * LANES]


def _row_tile(ref, idx):
    return ref.at[pl.ds(pl.multiple_of(idx * ROW_SUB, ROW_SUB), ROW_SUB)]


def _mod_row(i, p_tiles, tiles_per_seq):
    return jnp.where(i < p_tiles, 0, 1 + (i - p_tiles) // tiles_per_seq)


def _mod_kernel(cond_ref, w_ref, b_ref, o_ref):
    o_ref[...] = _dot_f32(_silu(cond_ref[...]), w_ref[...]) + b_ref[...]


def _modulation(cond, ada_w, ada_b):
    depth, _, n6 = ada_w.shape
    tn = 1536
    return pl.pallas_call(
        _mod_kernel,
        grid=(depth, n6 // tn),
        in_specs=[
            pl.BlockSpec(cond.shape, lambda l, j: (0, 0)),
            pl.BlockSpec((None, D, tn), lambda l, j: (l, 0, j)),
            pl.BlockSpec((None, 1, tn), lambda l, j: (l, 0, j)),
        ],
        out_specs=pl.BlockSpec((None, cond.shape[0], tn), lambda l, j: (l, 0, j)),
        out_shape=jax.ShapeDtypeStruct((depth, cond.shape[0], n6), F32),
        compiler_params=_cparams(2),
        name="adaln_mod",
    )(cond, ada_w, ada_b.reshape(depth, 1, n6))


def _rwkv_pre_kernel(x_ref, xu_ref, xd_ref, mod_ref, g_ref, mu_ref, wrkv_ref, g1_ref, g2_ref, w1_ref, w2_ref,
                     w0_ref, a1_ref, a2_ref, a0_ref, kk_ref, ka_ref, rk_ref, hsum_ref, hexp_ref,
                     r_o, v_o, kk_o, lw_o, kd_o, b_o, gate_o, bonus_o, *, p_tiles, tiles_per_seq):
    i = pl.program_id(0)
    is_p = i < p_tiles
    sub = (i - p_tiles) % tiles_per_seq
    mod = mod_ref[...]
    shift, scale = mod[:, 0:D], mod[:, D:2 * D]
    g = g_ref[...]
    h = _norm_mod(x_ref[...], g, shift, scale)
    hu = _norm_mod(xu_ref[...], g, shift, scale)
    hd = _norm_mod(xd_ref[...], g, shift, scale)

    q = D // 4
    row = lax.broadcasted_iota(I32, (TM, 1), 0)
    per = jnp.where(is_p, TM, GRID_W)
    pos = row & (per - 1)

    def prev1(a):
        return jnp.where(pos == 0, 0.0, pltpu.roll(a, 1, 0))

    def next1(a):
        return jnp.where(pos == per - 1, 0.0, pltpu.roll(a, TM - 1, 0))

    h0, h1, h2, h3 = (h[:, j * q:(j + 1) * q] for j in range(4))
    up = jnp.concatenate([jnp.where(sub == 0, 0.0, hu[:, 2 * q:3 * q]), h2[0:TM - GRID_W]], axis=0)
    down = jnp.concatenate([h3[GRID_W:TM], jnp.where(sub == tiles_per_seq - 1, 0.0, hd[:, 3 * q:4 * q])], axis=0)
    s0 = prev1(h0)
    s1 = jnp.where(is_p, prev1(h1), next1(h1))
    s2 = jnp.where(is_p, next1(h2), up)
    s3 = jnp.where(is_p, next1(h3), down)
    xx = jnp.concatenate([s0, s1, s2, s3], axis=1) - h

    mu = mu_ref[...]
    mix = lambda j: h + xx * mu[j:j + 1]
    r = _bdot(mix(0), wrkv_ref[0])
    k = _bdot(mix(2), wrkv_ref[1])
    v = _bdot(mix(3), wrkv_ref[2])
    gate = _bdot(jax.nn.sigmoid(_bdot(mix(5), g1_ref[...])), g2_ref[...])
    w_all = w0_ref[...] + _bdot(jnp.tanh(_bdot(mix(1), w1_ref[...])), w2_ref[...])
    a_all = jax.nn.sigmoid(a0_ref[...] + _bdot(_bdot(mix(4), a1_ref[...]), a2_ref[...]))

    hsum, hexp = hsum_ref[...], hexp_ref[...]
    head_sum = lambda t: _dot_f32_rhs01(_dot_f32_rhs01(t, hsum), hexp)

    kkr = k * kk_ref[...]
    kk = kkr / jnp.maximum(jnp.sqrt(head_sum(kkr * kkr)), 1e-12)
    ka = ka_ref[...]
    kd_sum = jnp.zeros_like(k)
    for d in range(2):
        wd = w_all[:, d * D:(d + 1) * D]
        z = -wd
        softplus = jnp.maximum(z, 0.0) + jnp.log(1.0 + jnp.exp(-jnp.abs(z)))
        lw_o[d] = -jnp.exp(-softplus - 0.5)
        a = a_all[:, d * D:(d + 1) * D]
        kd = k * (1.0 + (a - 1.0) * ka)
        kd_o[d] = kd
        b_o[d] = kk * a
        kd_sum = kd_sum + kd
    r_o[...] = r
    v_o[...] = v
    kk_o[...] = kk
    gate_o[...] = gate.astype(BF16)
    bonus_o[...] = (head_sum(r * kd_sum * rk_ref[...]) * v).astype(BF16)


def _rwkv_pre(x, mod, p_tiles, tiles_per_seq, norm_g, wts):
    n = x.shape[0]
    n_tiles = n // TM
    hb = TM // GRID_W
    n_hblk = n // GRID_W
    row = lambda a: a.reshape(1, -1)
    full = lambda a: pl.BlockSpec(a.shape, lambda i: (0,) * a.ndim)
    tok = pl.BlockSpec((TM, D), lambda i: (i, 0))
    tok2 = pl.BlockSpec((2, TM, D), lambda i: (0, i, 0))
    consts = [row(norm_g), wts["mu"], wts["wrkv"], wts["g1"], wts["g2"], wts["w1"], wts["w2"], wts["w0"],
              wts["a1"], wts["a2"], wts["a0"], row(wts["k_k"]), row(wts["k_a"]), row(wts["r_k"]),
              wts["hsum"], wts["hexp"]]
    return pl.pallas_call(
        functools.partial(_rwkv_pre_kernel, p_tiles=p_tiles, tiles_per_seq=tiles_per_seq),
        grid=(n_tiles,),
        in_specs=[
            tok,
            pl.BlockSpec((GRID_W, D), lambda i: (jnp.maximum(i * hb - 1, 0), 0)),
            pl.BlockSpec((GRID_W, D), lambda i: (jnp.minimum(i * hb + hb, n_hblk - 1), 0)),
            pl.BlockSpec((None, 1, N_MOD * D), lambda i: (_mod_row(i, p_tiles, tiles_per_seq), 0, 0)),
        ] + [full(a) for a in consts],
        out_specs=[tok, tok, tok, tok2, tok2, tok2, tok, tok],
        out_shape=[jax.ShapeDtypeStruct((n, D), F32)] * 3 + [jax.ShapeDtypeStruct((2, n, D), F32)] * 3
        + [jax.ShapeDtypeStruct((n, D), BF16)] * 2,
        compiler_params=_cparams(1),
        name="rwkv_pre",
    )(x, x, x, mod, *consts)


class _ScanItems:
    def __init__(self, n_ctx, ctx_chunks, n_lat, lat_chunks):
        self.n_ctx, self.ctx_chunks, self.n_lat, self.lat_chunks = n_ctx, ctx_chunks, n_lat, lat_chunks
        self.ctx_items = n_ctx * ctx_chunks
        self.n_items = self.ctx_items + n_lat * lat_chunks

    def decode(self, j):
        is_ctx = j < self.ctx_items
        jl = jnp.maximum(j - self.ctx_items, 0)
        c = jnp.where(is_ctx, j % self.ctx_chunks, jl % self.lat_chunks)
        return is_ctx, c, jnp.where(is_ctx, self.ctx_chunks, self.lat_chunks)

    def block(self, d, j):
        _, c, nc = self.decode(j)
        return jnp.where(d == 0, j, j - c + (nc - 1 - c))

    def ctx_seq(self, j):
        return jnp.minimum(j // self.ctx_chunks, self.n_ctx - 1)

    def lat_seq(self, j):
        return jnp.maximum(j - self.ctx_items, 0) // self.lat_chunks


def _rwkv_scan_kernel(rf_ref, vf_ref, kkf_ref, lwf_ref, kdf_ref, bf_ref, rb_ref, vb_ref, kkb_ref, lwb_ref, kdb_ref,
                      bb_ref, s0_ref, yf_ref, yb_ref, sf_ref, s_scr, *, items):
    is_ctx, c, nc = items.decode(pl.program_id(0))
    ch = RWKV_CHUNK

    @pl.when(c == 0)
    def _():
        s_scr[...] = jnp.where(is_ctx, 0.0, s0_ref[...])

    ti = lax.broadcasted_iota(I32, (ch, ch), 0)
    tj = lax.broadcasted_iota(I32, (ch, ch), 1)
    si = lax.broadcasted_iota(I32, (2 * ch, 2 * ch), 0)
    sj = lax.broadcasted_iota(I32, (2 * ch, 2 * ch), 1)
    same = (si < ch) == (sj < ch)
    ui, uj = si & (ch - 1), sj & (ch - 1)
    first = lax.broadcasted_iota(I32, (1, LANES), 1) < HEAD_A

    def stack(xp):
        return jnp.concatenate([jnp.where(first, xp, 0.0), jnp.where(first, 0.0, xp)], axis=0)

    def prologue(rev, r_ref, v_ref, kk_ref, lw_ref, kd_ref, b_ref):
        tri = jnp.where((tj >= ti) if rev else (tj <= ti), 1.0, 0.0).astype(BF16)
        lw = lw_ref[...]
        cum = _dot_f32_lhs01(tri, lw)
        last = cum[0:1] if rev else cum[ch - 1:ch]
        kk, kd, b = kk_ref[...], kd_ref[...], b_ref[...]
        e_neg = jnp.exp(-cum)
        e_rel = jnp.exp(last - cum)
        return dict(at=-kk * jnp.exp(cum - lw), rt=r_ref[...] * jnp.exp(cum), bt=b * e_neg, kt=kd * e_neg,
                    bh=b * e_rel, kh=kd * e_rel, g_all=jnp.exp(last), v=v_ref[...],
                    strict=same & ((uj > ui) if rev else (uj < ui)),
                    incl=same & ((uj >= ui) if rev else (uj <= ui)))

    dirs = [prologue(False, rf_ref, vf_ref, kkf_ref, lwf_ref, kdf_ref, bf_ref),
            prologue(True, rb_ref, vb_ref, kkb_ref, lwb_ref, kdb_ref, bb_ref)]

    units = [(d, p, slice(p * LANES, (p + 1) * LANES)) for d in range(2) for p in range(N_PAIR)]
    pairs = range(len(units))
    s_old = [s_scr[d, p] for d, p, _ in units]
    m1 = [_bdot_nt(jnp.concatenate([stack(dirs[d]["at"][:, sl]), stack(dirs[d]["rt"][:, sl])], axis=0),
                   jnp.concatenate([stack(dirs[d]["bt"][:, sl]), stack(dirs[d]["kt"][:, sl])], axis=0))
          for d, _, sl in units]
    m2 = [_bdot_nt(jnp.concatenate([dirs[d]["at"][:, sl], dirs[d]["rt"][:, sl]], axis=0), s_old[u])
          for u, (d, _, sl) in enumerate(units)]
    vs = [stack(dirs[d]["v"][:, sl]) for d, _, sl in units]
    strict = [dirs[d]["strict"] for d, _, _ in units]
    incl = [dirs[d]["incl"] for d, _, _ in units]
    x = [stack(m2[p][0:ch]) + _bdot(jnp.where(strict[p], m1[p][0:2 * ch, 2 * ch:4 * ch], 0.0), vs[p]) for p in pairs]
    nm = [jnp.where(strict[p], m1[p][0:2 * ch, 0:2 * ch], 0.0).astype(BF16) for p in pairs]
    for _ in range(5):
        sq_ap = [_bdot(nm[p], jnp.concatenate([nm[p], x[p].astype(BF16)], axis=1)) for p in pairs]
        nm = [sq_ap[p][:, 0:2 * ch].astype(BF16) for p in pairs]
        x = [x[p] + sq_ap[p][:, 2 * ch:4 * ch] for p in pairs]
    x = [x[p] + _bdot(nm[p], x[p]) for p in pairs]
    uv = [jnp.concatenate([x[p], vs[p]], axis=0).astype(BF16) for p in pairs]
    ys = [_bdot(jnp.concatenate([jnp.where(incl[p], m1[p][2 * ch:4 * ch, 0:2 * ch], 0.0),
                                 jnp.where(incl[p], m1[p][2 * ch:4 * ch, 2 * ch:4 * ch], 0.0)], axis=1), uv[p])
          for p in pairs]
    s_new = [s_old[u] * dirs[d]["g_all"][:, sl]
             + _bdot_tn(uv[u], jnp.concatenate([stack(dirs[d]["bh"][:, sl]), stack(dirs[d]["kh"][:, sl])], axis=0))
             for u, (d, _, sl) in enumerate(units)]
    for u, (d, _, sl) in enumerate(units):
        (yf_ref, yb_ref)[d][:, sl] = (ys[u][0:ch] + ys[u][ch:2 * ch] + m2[u][ch:2 * ch]).astype(BF16)
    for u, (d, p, _) in enumerate(units):
        s_scr[d, p] = s_new[u]

    @pl.when(is_ctx & (c == nc - 1))
    def _():
        sf_ref[...] = s_scr[...]


def _rwkv_scan(r, v, kk, lw, kd, b, s0, items):
    n = r.shape[0]
    ch = RWKV_CHUNK
    tok = lambda d: pl.BlockSpec((ch, D), lambda j: (items.block(d, j), 0))
    tokd = lambda d: pl.BlockSpec((None, ch, D), lambda j: (d, items.block(d, j), 0))
    st = lambda seq: pl.BlockSpec((2, None, N_PAIR, LANES, LANES), lambda j: (0, seq(j), 0, 0, 0))
    per_dir = lambda d: [tok(d), tok(d), tok(d), tokd(d), tokd(d), tokd(d)]
    return pl.pallas_call(
        functools.partial(_rwkv_scan_kernel, items=items),
        grid=(items.n_items,),
        in_specs=per_dir(0) + per_dir(1) + [st(items.lat_seq)],
        out_specs=[tok(0), tok(1), st(items.ctx_seq)],
        out_shape=[jax.ShapeDtypeStruct((n, D), BF16), jax.ShapeDtypeStruct((n, D), BF16),
                   jax.ShapeDtypeStruct((2, items.n_ctx, N_PAIR, LANES, LANES), F32)],
        scratch_shapes=[pltpu.VMEM((2, N_PAIR, LANES, LANES), F32)],
        compiler_params=_cparams(1),
        name="rwkv_scan",
    )(r, v, kk, lw, kd, b, r, v, kk, lw, kd, b, s0)


def _rwkv_post_kernel(x_ref, yf_ref, yb_ref, gate_ref, bonus_ref, mod_ref, lnw_ref, lnb_ref, wo_ref, hsum_ref, hexp_ref,
                      o_ref):
    hsum, hexp = hsum_ref[...], hexp_ref[...]
    head_mean = lambda t: _dot_f32_rhs01(_dot_f32_rhs01(t, hsum), hexp) * (1.0 / HEAD_A)
    y = yf_ref[...].astype(F32) + yb_ref[...].astype(F32)
    yc = y - head_mean(y)
    yn = yc * lax.rsqrt(head_mean(yc * yc) + LNX_EPS)
    z = (yn * lnw_ref[...] + lnb_ref[...] + bonus_ref[...].astype(F32)) * gate_ref[...].astype(F32)
    o_ref[...] = x_ref[...] + mod_ref[:, 2 * D:3 * D] * _bdot(z, wo_ref[...])


def _rwkv_post(x, y_fwd, y_bwd, gate, bonus, mod, p_tiles, tiles_per_seq, ln_w, ln_b, wo, hsum, hexp):
    n = x.shape[0]
    full = lambda a: pl.BlockSpec(a.shape, lambda i: (0,) * a.ndim)
    tok = pl.BlockSpec((TM, D), lambda i: (i, 0))
    consts = [ln_w.reshape(1, D), ln_b.reshape(1, D), wo, hsum, hexp]
    return pl.pallas_call(
        _rwkv_post_kernel,
        grid=(n // TM,),
        in_specs=[tok, tok, tok, tok, tok,
                  pl.BlockSpec((None, 1, N_MOD * D), lambda i: (_mod_row(i, p_tiles, tiles_per_seq), 0, 0))]
        + [full(a) for a in consts],
        out_specs=tok,
        out_shape=jax.ShapeDtypeStruct((n, D), F32),
        compiler_params=_cparams(1),
        name="rwkv_post",
    )(x, y_fwd, y_bwd, gate, bonus, mod, *consts)


def _ret_pre_kernel(x_ref, mod_ref, g_ref, cos_ref, sin_ref, win_ref, q_o, k_o, v_o, gate_o):
    mod = mod_ref[...]
    h = _norm_mod(x_ref[...], g_ref[...], mod[:, 0:D], mod[:, D:2 * D])
    proj = _bdot(h, win_ref[...])
    cos, sin = cos_ref[...], sin_ref[...]

    def rope(t):
        outs = []
        for j in range(D // LANES):
            tj = t[:, j * LANES:(j + 1) * LANES]
            cj = cos[:, (j % 2) * LANES:(j % 2 + 1) * LANES]
            sj = sin[:, (j % 2) * LANES:(j % 2 + 1) * LANES]
            outs.append(tj * cj + pltpu.roll(tj, LANES // 2, 1) * sj)
        return jnp.concatenate(outs, axis=1)

    q_o[...] = rope(proj[:, 0:D]).astype(BF16)
    k_o[...] = rope(proj[:, D:2 * D] * (DK_B ** -0.5)).astype(BF16)
    v_o[...] = proj[:, 2 * D:4 * D].astype(BF16)
    gate_o[...] = proj[:, 4 * D:6 * D].astype(BF16)


def _ret_pre(x, mod, p_tiles, tiles_per_seq, norm_g, cos_t, sin_t, w_in):
    n = x.shape[0]
    full = lambda a: pl.BlockSpec(a.shape, lambda i: (0,) * a.ndim)
    tok = lambda w: pl.BlockSpec((TM, w), lambda i: (i, 0))
    tab = pl.BlockSpec((TM, DK_B), lambda i: (jnp.where(i < p_tiles, 0, 1 + (i - p_tiles) % tiles_per_seq), 0))
    return pl.pallas_call(
        _ret_pre_kernel,
        grid=(n // TM,),
        in_specs=[tok(D), pl.BlockSpec((None, 1, N_MOD * D), lambda i: (_mod_row(i, p_tiles, tiles_per_seq), 0, 0)),
                  full(norm_g.reshape(1, D)), tab, tab, full(w_in)],
        out_specs=[tok(D), tok(D), tok(2 * D), tok(2 * D)],
        out_shape=[jax.ShapeDtypeStruct((n, D), BF16), jax.ShapeDtypeStruct((n, D), BF16),
                   jax.ShapeDtypeStruct((n, 2 * D), BF16), jax.ShapeDtypeStruct((n, 2 * D), BF16)],
        compiler_params=_cparams(1),
        name="ret_pre",
    )(x, mod, norm_g.reshape(1, D), cos_t, sin_t, w_in)


def _ret_scan_kernel(lg_ref, qf_ref, kf_ref, vf_ref, qb_ref, kb_ref, vb_ref, s0_ref, of_ref, ob_ref, sf_ref, s_scr, *,
                     items):
    is_ctx, c, nc = items.decode(pl.program_id(0))
    ch = RET_CHUNK

    @pl.when(c == 0)
    def _():
        s_scr[...] = jnp.where(is_ctx, 0.0, s0_ref[...])

    ti = lax.broadcasted_iota(I32, (ch, ch), 0)
    tj = lax.broadcasted_iota(I32, (ch, ch), 1)
    io = ((qf_ref, kf_ref, vf_ref, of_ref), (qb_ref, kb_ref, vb_ref, ob_ref))
    units = [(d, hd) for d in range(2) for hd in range(H_B)]

    def decays(d, hd):
        rev = d == 1
        lg = lg_ref[d, hd]
        rel = ((tj - ti) if rev else (ti - tj)).astype(F32)
        steps_q = ((ch - ti) if rev else (ti + 1)).astype(F32)
        steps_k = (ti if rev else (ch - 1 - ti)).astype(F32)
        return dict(mask=jnp.where(rel >= 0, jnp.exp(jnp.maximum(rel, 0.0) * lg), 0.0),
                    q_dec=jnp.exp(steps_q * lg), k_dec=jnp.exp(steps_k * lg),
                    chunk_dec=jnp.exp(jnp.full((1, DV_B), float(ch), F32) * lg))

    dec = [decays(d, hd) for d, hd in units]
    qh = [io[d][0][:, hd * DK_B:(hd + 1) * DK_B] for d, hd in units]
    kh = [io[d][1][:, hd * DK_B:(hd + 1) * DK_B] for d, hd in units]
    vh = [io[d][2][:, hd * DV_B:(hd + 1) * DV_B] for d, hd in units]
    s_old = [s_scr[d, hd] for d, hd in units]
    rng = range(len(units))
    scores = [_bdot_nt(qh[u], kh[u]) * dec[u]["mask"] for u in rng]
    cross = [_bdot(qh[u], s_old[u]) * jnp.concatenate([dec[u]["q_dec"]] * (DV_B // ch), axis=1) for u in rng]
    out = [_bdot(scores[u], vh[u]) + cross[u] for u in rng]
    s_new = [s_old[u] * dec[u]["chunk_dec"]
             + _bdot_tn(kh[u].astype(F32) * jnp.concatenate([dec[u]["k_dec"]] * (DK_B // ch), axis=1), vh[u])
             for u in rng]
    for u, (d, hd) in enumerate(units):
        io[d][3][:, hd * DV_B:(hd + 1) * DV_B] = out[u].astype(BF16)
    for u, (d, hd) in enumerate(units):
        s_scr[d, hd] = s_new[u]

    @pl.when(is_ctx & (c == nc - 1))
    def _():
        sf_ref[...] = s_scr[...]


def _ret_scan(log_gamma, q, k, v, s0, items):
    n = q.shape[0]
    ch = RET_CHUNK
    tok = lambda d, w: pl.BlockSpec((ch, w), lambda j: (items.block(d, j), 0))
    st = lambda seq: pl.BlockSpec((2, None, H_B, DK_B, DV_B), lambda j: (0, seq(j), 0, 0, 0))
    per_dir = lambda d: [tok(d, D), tok(d, D), tok(d, 2 * D)]
    return pl.pallas_call(
        functools.partial(_ret_scan_kernel, items=items),
        grid=(items.n_items,),
        in_specs=[pl.BlockSpec(memory_space=pltpu.SMEM)] + per_dir(0) + per_dir(1) + [st(items.lat_seq)],
        out_specs=[tok(0, 2 * D), tok(1, 2 * D), st(items.ctx_seq)],
        out_shape=[jax.ShapeDtypeStruct((n, 2 * D), BF16), jax.ShapeDtypeStruct((n, 2 * D), BF16),
                   jax.ShapeDtypeStruct((2, items.n_ctx, H_B, DK_B, DV_B), F32)],
        scratch_shapes=[pltpu.VMEM((2, H_B, DK_B, DV_B), F32)],
        compiler_params=_cparams(1),
        name="ret_scan",
    )(log_gamma, q, k, v, q, k, v, s0)


def _ret_post_kernel(x_ref, of_ref, ob_ref, gate_ref, mod_ref, wout_ref, out_ref):
    o = of_ref[...].astype(F32) + ob_ref[...].astype(F32)
    parts = []
    for hd in range(H_B):
        oh = o[:, hd * DV_B:(hd + 1) * DV_B]
        parts.append(oh * lax.rsqrt(jnp.mean(oh * oh, axis=-1, keepdims=True) + NORM_EPS))
    y = _silu(gate_ref[...].astype(F32)) * jnp.concatenate(parts, axis=1)
    out_ref[...] = x_ref[...] + mod_ref[:, 2 * D:3 * D] * _bdot(y, wout_ref[...])


def _ret_post(x, o_fwd, o_bwd, gate, mod, p_tiles, tiles_per_seq, w_out):
    n = x.shape[0]
    tok = lambda w: pl.BlockSpec((TM, w), lambda i: (i, 0))
    return pl.pallas_call(
        _ret_post_kernel,
        grid=(n // TM,),
        in_specs=[tok(D), tok(2 * D), tok(2 * D), tok(2 * D),
                  pl.BlockSpec((None, 1, N_MOD * D), lambda i: (_mod_row(i, p_tiles, tiles_per_seq), 0, 0)),
                  pl.BlockSpec(w_out.shape, lambda i: (0, 0))],
        out_specs=tok(D),
        out_shape=jax.ShapeDtypeStruct((n, D), F32),
        compiler_params=_cparams(1),
        name="ret_post",
    )(x, o_fwd, o_bwd, gate, mod, w_out)


def _moe_route_kernel(x_ref, mod_ref, g_ref, router_ref, bias_ref, h_o, e_o, w_o, p_o, cnt_o, carry):
    i = pl.program_id(0)

    @pl.when(i == 0)
    def _():
        carry[...] = jnp.zeros_like(carry)

    mod = mod_ref[...]
    h = _norm_mod(x_ref[...], g_ref[...], mod[:, 3 * D:4 * D], mod[:, 4 * D:5 * D])
    _rows_store(h_o, h)
    lane =lax.broadcasted_iota(I32, (TM, LANES), 1)
    valid = lane < N_EXPERTS
    neg = -jnp.inf
    scores = jax.nn.sigmoid(_dot_f32(h, router_ref[...]))
    biased = jnp.where(valid, scores + bias_ref[...], neg)

    def group_reduce(t, op):
        s = 1
        while s < PER_GROUP:
            partner = jnp.where((lane & s) == 0, pltpu.roll(t, LANES - s, 1), pltpu.roll(t, s, 1))
            t = op(t, partner)
            s *= 2
        return t

    lane_f = lane.astype(F32)
    group_f = jnp.floor(lane_f * (1.0 / PER_GROUP))

    def first_lane_of_max(t):
        m = jnp.max(t, axis=-1, keepdims=True)
        return jnp.min(jnp.where(t == m, lane_f, float(LANES)), axis=-1, keepdims=True)

    m1 = group_reduce(biased, jnp.maximum)
    first1 = group_reduce(jnp.where(biased == m1, lane_f, float(LANES)), jnp.minimum)
    m2 = group_reduce(jnp.where(lane_f == first1, neg, biased), jnp.maximum)
    gscore = jnp.where(valid, m1 + m2, neg)
    cand = jnp.full((TM, LANES), neg, F32)
    for _ in range(TOPK_GROUPS):
        gsel = group_f == jnp.floor(first_lane_of_max(gscore) * (1.0 / PER_GROUP))
        cand = jnp.where(gsel, biased, cand)
        gscore = jnp.where(gsel, neg, gscore)
    hits = []
    sel01 = jnp.zeros((TM, LANES), F32)
    e_cols = jnp.zeros((TM, LANES), F32)
    for j in range(TOP_K):
        fl = first_lane_of_max(cand)
        hit = lane_f == fl
        hits.append(hit)
        sel01 = jnp.where(hit, 1.0, sel01)
        cand = jnp.where(hit, neg, cand)
        e_cols = jnp.where(lane == j, fl, e_cols)
    wsum = jnp.sum(sel01 * scores, axis=-1, keepdims=True)

    ri = lax.broadcasted_iota(I32, (TM, TM), 0)
    rj = lax.broadcasted_iota(I32, (TM, TM), 1)
    below = jnp.where(rj < ri, 1.0, 0.0).astype(BF16)
    rank = jnp.dot(below, sel01.astype(BF16), preferred_element_type=F32) + carry[...]
    carry[...] = carry[...] + jnp.sum(sel01, axis=0, keepdims=True)
    w_cols = jnp.zeros((TM, LANES), F32)
    p_cols = jnp.zeros((TM, LANES), F32)
    for j in range(TOP_K):
        wj = jnp.sum(jnp.where(hits[j], scores, 0.0), axis=-1, keepdims=True)
        w_cols = jnp.where(lane == j, wj / wsum * ROUTED_SCALE, w_cols)
        p_cols = jnp.where(lane == j, jnp.sum(jnp.where(hits[j], rank, 0.0), axis=-1, keepdims=True), p_cols)
    e_o[...] = e_cols.astype(I32)
    w_o[...] = w_cols
    p_o[...] = p_cols.astype(I32)
    cnt_o[...] = jnp.broadcast_to(carry[...], cnt_o.shape)


def _moe_route(x, mod, p_tiles, tiles_per_seq, norm_g, router, bias):
    n = x.shape[0]
    full = lambda a: pl.BlockSpec(a.shape, lambda i: (0,) * a.ndim)
    tok = lambda w: pl.BlockSpec((TM, w), lambda i: (i, 0))
    router_p = jnp.pad(router, ((0, 0), (0, LANES - N_EXPERTS)))
    bias_p = jnp.pad(bias, (0, LANES - N_EXPERTS)).reshape(1, LANES)
    return pl.pallas_call(
        _moe_route_kernel,
        grid=(n // TM,),
        in_specs=[tok(D), pl.BlockSpec((None, 1, N_MOD * D), lambda i: (_mod_row(i, p_tiles, tiles_per_seq), 0, 0)),
                  full(norm_g.reshape(1, D)), full(router_p), full(bias_p)],
        out_specs=[pl.BlockSpec((TM * ROW_SUB, LANES), lambda i: (i, 0)), tok(LANES), tok(LANES), tok(LANES),
                   pl.BlockSpec((8, LANES), lambda i: (0, 0))],
        out_shape=[jax.ShapeDtypeStruct((n * ROW_SUB, LANES), F32), jax.ShapeDtypeStruct((n, LANES), I32),
                   jax.ShapeDtypeStruct((n, LANES), F32), jax.ShapeDtypeStruct((n, LANES), I32),
                   jax.ShapeDtypeStruct((8, LANES), F32)],
        scratch_shapes=[pltpu.VMEM((1, LANES), F32)],
        compiler_params=_cparams(1),
        name="moe_route",
    )(x, mod, norm_g.reshape(1, D), router_p, bias_p)


DISPATCH_TOK = 512


def _moe_dispatch_kernel(pad_lo_ref, pad_n_ref, dest_ref, h_ref, xs_hbm, zero_scr, sem):
    @pl.when(pl.program_id(0) == 0)
    def _():
        zero_scr[...] = jnp.zeros_like(zero_scr)

        def pad_copy(e, s):
            return pltpu.make_async_copy(zero_scr, _row_tile(xs_hbm, pad_lo_ref[e] + s), sem)

        def per_expert(e, carry):
            def issue(s, c):
                pad_copy(e, s).start()
                return c

            def drain(s, c):
                pad_copy(e, s).wait()
                return c

            lax.fori_loop(0, pad_n_ref[e], issue, 0)
            lax.fori_loop(0, pad_n_ref[e], drain, 0)
            return carry

        lax.fori_loop(0, N_EXPERTS, per_expert, 0)

    def copy(t, kq):
        return pltpu.make_async_copy(_row_tile(h_ref, t), _row_tile(xs_hbm, dest_ref[t * TOP_K + kq]), sem)

    def issue(t, carry):
        for kq in range(TOP_K):
            copy(t, kq).start(priority=kq % 2)
        return carry

    def drain(t, carry):
        for kq in range(TOP_K):
            copy(t, kq).wait()
        return carry

    lax.fori_loop(0, DISPATCH_TOK, issue, 0)
    lax.fori_loop(0, DISPATCH_TOK, drain, 0)


def _moe_dispatch(pad_lo, pad_n, dest_flat, h, n_slots):
    n = h.shape[0] // ROW_SUB
    grid_spec = pltpu.PrefetchScalarGridSpec(
        num_scalar_prefetch=2,
        grid=(n // DISPATCH_TOK,),
        in_specs=[pl.BlockSpec((DISPATCH_TOK * TOP_K,), lambda i, *_: (i,), memory_space=pltpu.SMEM),
                  pl.BlockSpec((DISPATCH_TOK * ROW_SUB, LANES), lambda i, *_: (i, 0))],
        out_specs=pl.BlockSpec(memory_space=pl.ANY),
        scratch_shapes=[pltpu.VMEM((ROW_SUB, LANES), F32), pltpu.SemaphoreType.DMA],
    )
    return pl.pallas_call(
        _moe_dispatch_kernel,
        grid_spec=grid_spec,
        out_shape=jax.ShapeDtypeStruct((n_slots * ROW_SUB, LANES), F32),
        compiler_params=_cparams(1, has_side_effects=True),
        name="moe_dispatch",
    )(pad_lo, pad_n, dest_flat, h)


def _moe_expert_kernel(be_ref, xs_ref, wgu_ref, wdn_ref, o_ref, wgu_bf, wdn_bf):
    i = pl.program_id(0)

    @pl.when((i == 0) | (be_ref[i] != be_ref[jnp.maximum(i - 1, 0)]))
    def _():
        wgu_bf[...] = wgu_ref[...].astype(BF16)
        wdn_bf[...] = wdn_ref[...].astype(BF16)

    gu = _bdot(_rows_load(xs_ref), wgu_bf[...])
    act = _silu(gu[:, 0:D_EXPERT]) * gu[:, D_EXPERT:2 * D_EXPERT]
    _rows_store(o_ref, _bdot(act, wdn_bf[...]))


def _moe_expert(block_e, xs, w_gu, w_down, layer):
    n_slots = xs.shape[0] // ROW_SUB
    n_blocks = n_slots // EXPERT_BLOCK
    rows = pl.BlockSpec((EXPERT_BLOCK * ROW_SUB, LANES), lambda i, be: (i, 0))
    grid_spec = pltpu.PrefetchScalarGridSpec(
        num_scalar_prefetch=1,
        grid=(n_blocks,),
        in_specs=[rows,
                  pl.BlockSpec((None, None, D, 2 * D_EXPERT), lambda i, be: (layer, be[i], 0, 0)),
                  pl.BlockSpec((None, None, D_EXPERT, D), lambda i, be: (layer, be[i], 0, 0))],
        out_specs=rows,
        scratch_shapes=[pltpu.VMEM((D, 2 * D_EXPERT), BF16), pltpu.VMEM((D_EXPERT, D), BF16)],
    )
    return pl.pallas_call(
        _moe_expert_kernel,
        grid_spec=grid_spec,
        out_shape=jax.ShapeDtypeStruct((n_slots * ROW_SUB, LANES), F32),
        compiler_params=_cparams(1),
        name="moe_expert",
    )(block_e, xs, w_gu, w_down)


COMBINE_TOK = 128


def _moe_combine_kernel(dest_ref, dest_next_ref, x_ref, h_ref, w_ref, mod_ref, shgu_ref, shdn_ref, gfin_ref, ys_hbm,
                        *refs, final_norm, ctx_tiles):
    o_refs, (buf, sems) = refs[:-2], refs[-2:]
    i = pl.program_id(0)
    slot = i % 2

    def copy(idx_ref, s, t, kq):
        return pltpu.make_async_copy(_row_tile(ys_hbm, idx_ref[t * TOP_K + kq]), _row_tile(buf.at[s, kq], t),
                                     sems.at[s])

    def fetch(idx_ref, s):
        def body(t, carry):
            for kq in range(TOP_K):
                copy(idx_ref, s, t, kq).start(priority=kq % 2)
            return carry
        lax.fori_loop(0, COMBINE_TOK, body, 0, unroll=2)

    @pl.when(i == 0)
    def _():
        fetch(dest_ref, slot)

    @pl.when(i + 1 < pl.num_programs(0))
    def _():
        fetch(dest_next_ref, 1 - slot)

    gu = _bdot(_rows_load(h_ref), shgu_ref[...])
    acc = _bdot(_silu(gu[:, 0:D_EXPERT]) * gu[:, D_EXPERT:2 * D_EXPERT], shdn_ref[...])

    def drain(t, carry):
        for kq in range(TOP_K):
            copy(dest_ref, slot, t, kq).wait()
        return carry

    lax.fori_loop(0, COMBINE_TOK, drain, 0)
    w = w_ref[...]
    for kq in range(TOP_K):
        acc = acc + _rows_load(buf.at[slot, kq]) * w[:, kq:kq + 1]
    out = x_ref[...] + mod_ref[:, 5 * D:6 * D] * acc
    if final_norm:
        out = out * lax.rsqrt(jnp.mean(out * out, axis=-1, keepdims=True) + NORM_EPS) * gfin_ref[...]

        @pl.when(i < ctx_tiles)
        def _():
            o_refs[0][...] = out

        @pl.when(i >= ctx_tiles)
        def _():
            o_refs[1][...] = out
    else:
        o_refs[0][...] = out


def _moe_combine(dest_flat, x, h, w_cols, mod, p_tiles, tiles_per_seq, sh_gu, sh_down, g_final, ys, final_norm):
    n = x.shape[0]
    ratio = TM // COMBINE_TOK
    tok = lambda w: pl.BlockSpec((COMBINE_TOK, w), lambda i: (i, 0))
    full = lambda a: pl.BlockSpec(a.shape, lambda i: (0,) * a.ndim)
    n_tiles = n // COMBINE_TOK
    ctx_tiles = p_tiles * ratio
    if final_norm:
        out_specs = [pl.BlockSpec((COMBINE_TOK, D), lambda i: (jnp.minimum(i, ctx_tiles - 1), 0)),
                     pl.BlockSpec((COMBINE_TOK, D), lambda i: (jnp.maximum(i - ctx_tiles, 0), 0))]
        out_shape = [jax.ShapeDtypeStruct((ctx_tiles * COMBINE_TOK, D), F32),
                     jax.ShapeDtypeStruct((n - ctx_tiles * COMBINE_TOK, D), F32)]
    else:
        out_specs, out_shape = tok(D), jax.ShapeDtypeStruct((n, D), F32)
    return pl.pallas_call(
        functools.partial(_moe_combine_kernel, final_norm=final_norm, ctx_tiles=ctx_tiles),
        grid=(n_tiles,),
        in_specs=[pl.BlockSpec((COMBINE_TOK * TOP_K,), lambda i: (i,), memory_space=pltpu.SMEM),
                  pl.BlockSpec((COMBINE_TOK * TOP_K,), lambda i: (jnp.minimum(i + 1, n_tiles - 1),),
                               memory_space=pltpu.SMEM),
                  tok(D), pl.BlockSpec((COMBINE_TOK * ROW_SUB, LANES), lambda i: (i, 0)), tok(LANES),
                  pl.BlockSpec((None, 1, N_MOD * D), lambda i: (_mod_row(i // ratio, p_tiles, tiles_per_seq), 0, 0)),
                  full(sh_gu), full(sh_down), full(g_final),
                  pl.BlockSpec(memory_space=pl.ANY)],
        out_specs=out_specs,
        out_shape=out_shape,
        scratch_shapes=[pltpu.VMEM((2, TOP_K, COMBINE_TOK * ROW_SUB, LANES), F32), pltpu.SemaphoreType.DMA((2,))],
        compiler_params=_cparams(1),
        name="moe_combine",
    )(dest_flat, dest_flat, x, h, w_cols, mod, sh_gu, sh_down, g_final, ys)


def _moe_layer(x, mod, p_tiles, tiles_per_seq, norm_g, router, bias, w_gu, w_down, layer, sh_gu, sh_down, g_final,
               final_norm):
    n = x.shape[0]
    h, e_cols, w_cols, p_cols, counts = _moe_route(x, mod, p_tiles, tiles_per_seq, norm_g, router, bias)
    counts = counts[0, :N_EXPERTS].astype(I32)
    padded = (counts + EXPERT_BLOCK - 1) // EXPERT_BLOCK * EXPERT_BLOCK
    pad_end = jnp.cumsum(padded)
    pad_start = pad_end - padded
    chosen = e_cols[:, :TOP_K, None] == jnp.arange(N_EXPERTS, dtype=I32)
    dest = (jnp.sum(jnp.where(chosen, pad_start, 0), axis=-1) + p_cols[:, :TOP_K]).reshape(-1)
    n_blocks = n * TOP_K // EXPERT_BLOCK + N_EXPERTS
    n_slots = n_blocks * EXPERT_BLOCK
    block_start = jnp.arange(n_blocks, dtype=I32) * EXPERT_BLOCK
    block_e = jnp.minimum(jnp.sum((pad_end[None, :] <= block_start[:, None]).astype(I32), axis=1), N_EXPERTS - 1)
    pad_n = (padded - counts).at[N_EXPERTS - 1].add(n_slots - pad_end[N_EXPERTS - 1])
    xs = _moe_dispatch(pad_start + counts, pad_n, dest, h, n_slots)
    ys = _moe_expert(block_e, xs, w_gu, w_down, layer)
    return _moe_combine(dest, x, h, w_cols, mod, p_tiles, tiles_per_seq, sh_gu, sh_down, g_final.reshape(1, D), ys,
                        final_norm)


def _pair_pack(s):
    lead = s.shape[:-3]
    s = s.reshape(lead + (N_PAIR, 2, HEAD_A, HEAD_A))
    z = jnp.zeros_like(s[..., 0, :, :])
    top = jnp.concatenate([s[..., 0, :, :], z], axis=-1)
    bot = jnp.concatenate([z, s[..., 1, :, :]], axis=-1)
    return jnp.concatenate([top, bot], axis=-2)


def _pair_unpack(s):
    a = s[..., :HEAD_A, :HEAD_A]
    b = s[..., HEAD_A:, HEAD_A:]
    return jnp.stack([a, b], axis=-3).reshape(s.shape[:-3] + (H_A, HEAD_A, HEAD_A))


def _rope_tables(rows):
    half = DK_B // 2
    n_freq = half // 2
    inv = ROPE_BASE ** (-jnp.arange(n_freq, dtype=F32) / n_freq)
    pos_r = jnp.repeat(jnp.arange(rows, dtype=F32), GRID_W)
    pos_c = jnp.tile(jnp.arange(GRID_W, dtype=F32), rows)

    def tab(pos):
        ang = pos[:, None] * inv[None, :]
        c, s = jnp.cos(ang), jnp.sin(ang)
        return jnp.concatenate([c, c], -1), jnp.concatenate([-s, s], -1)

    cr, sr = tab(pos_r)
    cc, sc = tab(pos_c)
    cos = jnp.concatenate([cr, cc], -1)
    sin = jnp.concatenate([sr, sc], -1)
    cos = jnp.concatenate([jnp.ones((TM, DK_B), F32), cos], 0)
    sin = jnp.concatenate([jnp.zeros((TM, DK_B), F32), sin], 0)
    return cos, sin


def kernel(x_prompt, x_sample, state_rwkv, state_ret, c, c_ctx, ada_w, ada_b, norm_mix, norm_ffn, norm_final, rwkv_mu, rwkv_w0, rwkv_w1, rwkv_w2, rwkv_a0, rwkv_a1, rwkv_a2, rwkv_wrkv, rwkv_wo, rwkv_g1, rwkv_g2, rwkv_k_k, rwkv_k_a, rwkv_r_k, rwkv_ln_w, rwkv_ln_b, ret_w_in, ret_w_out, ret_decay_logit, moe_router, moe_bias, moe_w_gu, moe_w_down, moe_sh_gu, moe_sh_down):
    bp, tp, _ = x_prompt.shape
    bs, ts, _ = x_sample.shape
    n_p, n_s = bp * tp, bs * ts
    assert tp == TM and ts % TM == 0 and TM % GRID_W == 0
    p_tiles = n_p // TM
    tiles_per_seq = ts // TM
    rows = ts // GRID_W
    x = jnp.concatenate([x_prompt.reshape(n_p, D), x_sample.reshape(n_s, D)], axis=0)
    n = n_p + n_s

    n_cond = 16
    cond = jnp.zeros((n_cond, D), F32).at[0].set(c_ctx).at[1:1 + bs].set(c)
    mod = _modulation(cond, ada_w, ada_b).reshape(ada_w.shape[0], n_cond, 1, N_MOD * D)

    head_of = jnp.arange(D, dtype=I32) // HEAD_A
    hsum = (head_of[:, None] == jnp.arange(LANES, dtype=I32)[None, :]).astype(BF16)
    hexp = hsum.T

    gl = rwkv_g1.shape[-1]
    glp = -(-gl // LANES) * LANES
    zb = lambda a: jnp.zeros_like(a)
    wts = {
        "mu": jnp.pad(rwkv_mu[0], ((0, 8 - N_MOD), (0, 0))),
        "wrkv": rwkv_wrkv[0].astype(BF16),
        "g1": jnp.pad(rwkv_g1[0], ((0, 0), (0, glp - gl))).astype(BF16),
        "g2": jnp.pad(rwkv_g2[0], ((0, glp - gl), (0, 0))).astype(BF16),
        "w1": jnp.concatenate([rwkv_w1[0, 0], rwkv_w1[0, 1]], axis=1).astype(BF16),
        "w2": jnp.concatenate([jnp.concatenate([rwkv_w2[0, 0], zb(rwkv_w2[0, 1])], 1),
                               jnp.concatenate([zb(rwkv_w2[0, 0]), rwkv_w2[0, 1]], 1)], 0).astype(BF16),
        "w0": rwkv_w0[0].reshape(1, 2 * D),
        "a1": jnp.concatenate([rwkv_a1[0, 0], rwkv_a1[0, 1]], axis=1).astype(BF16),
        "a2": jnp.concatenate([jnp.concatenate([rwkv_a2[0, 0], zb(rwkv_a2[0, 1])], 1),
                               jnp.concatenate([zb(rwkv_a2[0, 0]), rwkv_a2[0, 1]], 1)], 0).astype(BF16),
        "a0": rwkv_a0[0].reshape(1, 2 * D),
        "k_k": rwkv_k_k[0], "k_a": rwkv_k_a[0], "r_k": rwkv_r_k[0].reshape(D),
        "hsum": hsum, "hexp": hexp,
    }
    r, v, kk, lw, kd, b, gate, bonus = _rwkv_pre(x, mod[0], p_tiles, tiles_per_seq, norm_mix[0], wts)
    s0_lat = _pair_pack(jnp.moveaxis(state_rwkv[:, 0], 1, 0))
    y_fwd, y_bwd, s_fin = _rwkv_scan(r, v, kk, lw, kd, b, s0_lat,
                                     _ScanItems(bp, tp // RWKV_CHUNK, bs, ts // RWKV_CHUNK))
    new_state_rwkv = jnp.moveaxis(_pair_unpack(s_fin), 0, 1)[:, None]
    x = _rwkv_post(x, y_fwd, y_bwd, gate, bonus, mod[0], p_tiles, tiles_per_seq, rwkv_ln_w[0], rwkv_ln_b[0],
                   rwkv_wo[0].astype(BF16), hsum, hexp)
    x = _moe_layer(x, mod[0], p_tiles, tiles_per_seq, norm_ffn[0], moe_router[0], moe_bias[0],
                   moe_w_gu, moe_w_down, 0, moe_sh_gu[0].astype(BF16),
                   moe_sh_down[0].astype(BF16), norm_final, False)

    cos_t, sin_t = _rope_tables(rows)
    q, k, vv, rgate = _ret_pre(x, mod[1], p_tiles, tiles_per_seq, norm_mix[1], cos_t, sin_t, ret_w_in[0].astype(BF16))
    log_gamma = jax.nn.log_sigmoid(ret_decay_logit[0].astype(F32))
    o_fwd, o_bwd, r_fin = _ret_scan(log_gamma, q, k, vv, jnp.moveaxis(state_ret[:, 0], 1, 0),
                                    _ScanItems(bp, tp // RET_CHUNK, bs, ts // RET_CHUNK))
    new_state_ret = jnp.moveaxis(r_fin, 0, 1)[:, None]
    x = _ret_post(x, o_fwd, o_bwd, rgate, mod[1], p_tiles, tiles_per_seq, ret_w_out[0].astype(BF16))
    y_ctx, y_lat = _moe_layer(x, mod[1], p_tiles, tiles_per_seq, norm_ffn[1], moe_router[1], moe_bias[1],
                              moe_w_gu, moe_w_down, 1, moe_sh_gu[1].astype(BF16),
                              moe_sh_down[1].astype(BF16), norm_final, True)

    return (y_ctx.reshape(bp, tp, D), y_lat.reshape(bs, ts, D), new_state_rwkv, new_state_ret)
```

```python
import functools

import jax
import jax.numpy as jnp
from jax import lax
from jax.experimental import pallas as pl
from jax.experimental.pallas import tpu as pltpu

F32, BF16, I32 = jnp.float32, jnp.bfloat16, jnp.int32

D = 1024
N_MOD = 6
NORM_EPS = 1e-6
GRID_W = 64
HEAD_A = 64
H_A = D // HEAD_A
LNX_EPS = 64e-5
RWKV_CHUNK = 64
N_PAIR = H_A // 2
H_B = 4
DK_B = D // H_B
DV_B = 2 * DK_B
RET_CHUNK = 128
ROPE_BASE = 10000.0
N_EXPERTS = 64
TOP_K = 8
N_GROUPS = 8
TOPK_GROUPS = 4
PER_GROUP = N_EXPERTS // N_GROUPS
D_EXPERT = 256
ROUTED_SCALE = 2.5
EXPERT_BLOCK = 256

TM = 256
LANES = 128
ROW_SUB = D // LANES
VMEM_LIMIT = 56 * 1024 * 1024


def _cparams(n_grid_axes, **kw):
    return pltpu.CompilerParams(dimension_semantics=("arbitrary",) * n_grid_axes, vmem_limit_bytes=VMEM_LIMIT, **kw)


def _bdot(a, b):
    return jnp.dot(a.astype(BF16), b.astype(BF16), preferred_element_type=F32)


def _bdot_nt(a, b):
    return lax.dot_general(a.astype(BF16), b.astype(BF16), (((1,), (1,)), ((), ())), preferred_element_type=F32)


def _bdot_tn(a, b):
    return lax.dot_general(a.astype(BF16), b.astype(BF16), (((0,), (0,)), ((), ())), preferred_element_type=F32)


def _split3(x):
    hi = x.astype(BF16)
    r1 = x - hi.astype(F32)
    mid = r1.astype(BF16)
    lo = (r1 - mid.astype(F32)).astype(BF16)
    return hi, mid, lo


def _dot_f32(a, b):
    ah, am, al = _split3(a)
    bh, bm, bl = _split3(b)
    d = lambda x, y: jnp.dot(x, y, preferred_element_type=F32)
    return d(ah, bh) + (d(ah, bm) + d(am, bh)) + (d(ah, bl) + d(al, bh) + d(am, bm))


def _dot_f32_rhs01(a, m01):
    ah, am, al = _split3(a)
    d = lambda x: jnp.dot(x, m01, preferred_element_type=F32)
    return d(ah) + d(am) + d(al)


def _dot_f32_lhs01(m01, b):
    bh, bm, bl = _split3(b)
    d = lambda x: jnp.dot(m01, x, preferred_element_type=F32)
    return d(bh) + d(bm) + d(bl)


def _norm_mod(x, g, shift, scale):
    ms = jnp.mean(x * x, axis=-1, keepdims=True)
    y = x * lax.rsqrt(ms + NORM_EPS) * g
    return y * (1.0 + scale) + shift


def _silu(x):
    return x * jax.nn.sigmoid(x)


def _rows_load(ref):
    m = ref.shape[0] // ROW_SUB
    return jnp.concatenate([ref[pl.ds(j, m, stride=ROW_SUB), :] for j in range(ROW_SUB)], axis=1)


def _rows_store(ref, val):
    m = val.shape[0]
    for j in range(ROW_SUB):
        ref[pl.ds(j, m, stride=ROW_SUB), :] = val[:, j * LANES:(j + 1) * LANES]


def _row_tile(ref, idx):
    return ref.at[pl.ds(pl.multiple_of(idx * ROW_SUB, ROW_SUB), ROW_SUB)]


def _mod_row(i, p_tiles, tiles_per_seq):
    return jnp.where(i < p_tiles, 0, 1 + (i - p_tiles) // tiles_per_seq)


def _mod_kernel(cond_ref, w_ref, b_ref, o_ref):
    o_ref[...] = _dot_f32(_silu(cond_ref[...]), w_ref[...]) + b_ref[...]


def _modulation(cond, ada_w, ada_b):
    depth, _, n6 = ada_w.shape
    tn = 1536
    return pl.pallas_call(
        _mod_kernel,
        grid=(depth, n6 // tn),
        in_specs=[
            pl.BlockSpec(cond.shape, lambda l, j: (0, 0)),
            pl.BlockSpec((None, D, tn), lambda l, j: (l, 0, j)),
            pl.BlockSpec((None, 1, tn), lambda l, j: (l, 0, j)),
        ],
        out_specs=pl.BlockSpec((None, cond.shape[0], tn), lambda l, j: (l, 0, j)),
        out_shape=jax.ShapeDtypeStruct((depth, cond.shape[0], n6), F32),
        compiler_params=_cparams(2),
        name="adaln_mod",
    )(cond, ada_w, ada_b.reshape(depth, 1, n6))


def _rwkv_pre_kernel(x_ref, xu_ref, xd_ref, mod_ref, g_ref, mu_ref, wrkv_ref, g1_ref, g2_ref, w1_ref, w2_ref,
                     w0_ref, a1_ref, a2_ref, a0_ref, kk_ref, ka_ref, rk_ref, hsum_ref, hexp_ref,
                     r_o, v_o, kk_o, lw_o, kd_o, b_o, gate_o, bonus_o, *, p_tiles, tiles_per_seq):
    i = pl.program_id(0)
    is_p = i < p_tiles
    sub = (i - p_tiles) % tiles_per_seq
    mod = mod_ref[...]
    shift, scale = mod[:, 0:D], mod[:, D:2 * D]
    g = g_ref[...]
    h = _norm_mod(x_ref[...], g, shift, scale)
    hu = _norm_mod(xu_ref[...], g, shift, scale)
    hd = _norm_mod(xd_ref[...], g, shift, scale)

    q = D // 4
    row = lax.broadcasted_iota(I32, (TM, 1), 0)
    per = jnp.where(is_p, TM, GRID_W)
    pos = row & (per - 1)

    def prev1(a):
        return jnp.where(pos == 0, 0.0, pltpu.roll(a, 1, 0))

    def next1(a):
        return jnp.where(pos == per - 1, 0.0, pltpu.roll(a, TM - 1, 0))

    h0, h1, h2, h3 = (h[:, j * q:(j + 1) * q] for j in range(4))
    up = jnp.concatenate([jnp.where(sub == 0, 0.0, hu[:, 2 * q:3 * q]), h2[0:TM - GRID_W]], axis=0)
    down = jnp.concatenate([h3[GRID_W:TM], jnp.where(sub == tiles_per_seq - 1, 0.0, hd[:, 3 * q:4 * q])], axis=0)
    s0 = prev1(h0)
    s1 = jnp.where(is_p, prev1(h1), next1(h1))
    s2 = jnp.where(is_p, next1(h2), up)
    s3 = jnp.where(is_p, next1(h3), down)
    xx = jnp.concatenate([s0, s1, s2, s3], axis=1) - h

    mu = mu_ref[...]
    mix = lambda j: h + xx * mu[j:j + 1]
    r = _bdot(mix(0), wrkv_ref[0])
    k = _bdot(mix(2), wrkv_ref[1])
    v = _bdot(mix(3), wrkv_ref[2])
    gate = _bdot(jax.nn.sigmoid(_bdot(mix(5), g1_ref[...])), g2_ref[...])
    w_all = w0_ref[...] + _bdot(jnp.tanh(_bdot(mix(1), w1_ref[...])), w2_ref[...])
    a_all = jax.nn.sigmoid(a0_ref[...] + _bdot(_bdot(mix(4), a1_ref[...]), a2_ref[...]))

    hsum, hexp = hsum_ref[...], hexp_ref[...]
    head_sum = lambda t: _dot_f32_rhs01(_dot_f32_rhs01(t, hsum), hexp)

    kkr = k * kk_ref[...]
    kk = kkr / jnp.maximum(jnp.sqrt(head_sum(kkr * kkr)), 1e-12)
    ka = ka_ref[...]
    kd_sum = jnp.zeros_like(k)
    for d in range(2):
        wd = w_all[:, d * D:(d + 1) * D]
        z = -wd
        softplus = jnp.maximum(z, 0.0) + jnp.log(1.0 + jnp.exp(-jnp.abs(z)))
        lw_o[d] = -jnp.exp(-softplus - 0.5)
        a = a_all[:, d * D:(d + 1) * D]
        kd = k * (1.0 + (a - 1.0) * ka)
        kd_o[d] = kd
        b_o[d] = kk * a
        kd_sum = kd_sum + kd
    r_o[...] = r
    v_o[...] = v
    kk_o[...] = kk
    gate_o[...] = gate.astype(BF16)
    bonus_o[...] = (head_sum(r * kd_sum * rk_ref[...]) * v).astype(BF16)


def _rwkv_pre(x, mod, p_tiles, tiles_per_seq, norm_g, wts):
    n = x.shape[0]
    n_tiles = n // TM
    hb = TM // GRID_W
    n_hblk = n // GRID_W
    row = lambda a: a.reshape(1, -1)
    full = lambda a: pl.BlockSpec(a.shape, lambda i: (0,) * a.ndim)
    tok = pl.BlockSpec((TM, D), lambda i: (i, 0))
    tok2 = pl.BlockSpec((2, TM, D), lambda i: (0, i, 0))
    consts = [row(norm_g), wts["mu"], wts["wrkv"], wts["g1"], wts["g2"], wts["w1"], wts["w2"], wts["w0"],
              wts["a1"], wts["a2"], wts["a0"], row(wts["k_k"]), row(wts["k_a"]), row(wts["r_k"]),
              wts["hsum"], wts["hexp"]]
    return pl.pallas_call(
        functools.partial(_rwkv_pre_kernel, p_tiles=p_tiles, tiles_per_seq=tiles_per_seq),
        grid=(n_tiles,),
        in_specs=[
            tok,
            pl.BlockSpec((GRID_W, D), lambda i: (jnp.maximum(i * hb - 1, 0), 0)),
            pl.BlockSpec((GRID_W, D), lambda i: (jnp.minimum(i * hb + hb, n_hblk - 1), 0)),
            pl.BlockSpec((None, 1, N_MOD * D), lambda i: (_mod_row(i, p_tiles, tiles_per_seq), 0, 0)),
        ] + [full(a) for a in consts],
        out_specs=[tok, tok, tok, tok2, tok2, tok2, tok, tok],
        out_shape=[jax.ShapeDtypeStruct((n, D), F32)] * 3 + [jax.ShapeDtypeStruct((2, n, D), F32)] * 3
        + [jax.ShapeDtypeStruct((n, D), BF16)] * 2,
        compiler_params=_cparams(1),
        name="rwkv_pre",
    )(x, x, x, mod, *consts)


class _ScanItems:
    def __init__(self, n_ctx, ctx_chunks, n_lat, lat_chunks):
        self.n_ctx, self.ctx_chunks, self.n_lat, self.lat_chunks = n_ctx, ctx_chunks, n_lat, lat_chunks
        self.ctx_items = n_ctx * ctx_chunks
        self.n_items = self.ctx_items + n_lat * lat_chunks

    def decode(self, j):
        is_ctx = j < self.ctx_items
        jl = jnp.maximum(j - self.ctx_items, 0)
        c = jnp.where(is_ctx, j % self.ctx_chunks, jl % self.lat_chunks)
        return is_ctx, c, jnp.where(is_ctx, self.ctx_chunks, self.lat_chunks)

    def block(self, d, j):
        _, c, nc = self.decode(j)
        return jnp.where(d == 0, j, j - c + (nc - 1 - c))

    def ctx_seq(self, j):
        return jnp.minimum(j // self.ctx_chunks, self.n_ctx - 1)

    def lat_seq(self, j):
        return jnp.maximum(j - self.ctx_items, 0) // self.lat_chunks


def _rwkv_scan_kernel(rf_ref, vf_ref, kkf_ref, lwf_ref, kdf_ref, bf_ref, rb_ref, vb_ref, kkb_ref, lwb_ref, kdb_ref,
                      bb_ref, s0_ref, yf_ref, yb_ref, sf_ref, s_scr, *, items):
    is_ctx, c, nc = items.decode(pl.program_id(0))
    ch = RWKV_CHUNK

    @pl.when(c == 0)
    def _():
        s_scr[...] = jnp.where(is_ctx, 0.0, s0_ref[...])

    ti = lax.broadcasted_iota(I32, (ch, ch), 0)
    tj = lax.broadcasted_iota(I32, (ch, ch), 1)
    si = lax.broadcasted_iota(I32, (2 * ch, 2 * ch), 0)
    sj = lax.broadcasted_iota(I32, (2 * ch, 2 * ch), 1)
    same = (si < ch) == (sj < ch)
    ui, uj = si & (ch - 1), sj & (ch - 1)
    first = lax.broadcasted_iota(I32, (1, LANES), 1) < HEAD_A

    def stack(xp):
        return jnp.concatenate([jnp.where(first, xp, 0.0), jnp.where(first, 0.0, xp)], axis=0)

    def prologue(rev, r_ref, v_ref, kk_ref, lw_ref, kd_ref, b_ref):
        tri = jnp.where((tj >= ti) if rev else (tj <= ti), 1.0, 0.0).astype(BF16)
        lw = lw_ref[...]
        cum = _dot_f32_lhs01(tri, lw)
        last = cum[0:1] if rev else cum[ch - 1:ch]
        kk, kd, b = kk_ref[...], kd_ref[...], b_ref[...]
        e_neg = jnp.exp(-cum)
        e_rel = jnp.exp(last - cum)
        return dict(at=-kk * jnp.exp(cum - lw), rt=r_ref[...] * jnp.exp(cum), bt=b * e_neg, kt=kd * e_neg,
                    bh=b * e_rel, kh=kd * e_rel, g_all=jnp.exp(last), v=v_ref[...],
                    strict=same & ((uj > ui) if rev else (uj < ui)),
                    incl=same & ((uj >= ui) if rev else (uj <= ui)))

    dirs = [prologue(False, rf_ref, vf_ref, kkf_ref, lwf_ref, kdf_ref, bf_ref),
            prologue(True, rb_ref, vb_ref, kkb_ref, lwb_ref, kdb_ref, bb_ref)]

    units = [(d, p, slice(p * LANES, (p + 1) * LANES)) for d in range(2) for p in range(N_PAIR)]
    pairs = range(len(units))
    s_old = [s_scr[d, p] for d, p, _ in units]
    m1 = [_bdot_nt(jnp.concatenate([stack(dirs[d]["at"][:, sl]), stack(dirs[d]["rt"][:, sl])], axis=0),
                   jnp.concatenate([stack(dirs[d]["bt"][:, sl]), stack(dirs[d]["kt"][:, sl])], axis=0))
          for d, _, sl in units]
    m2 = [_bdot_nt(jnp.concatenate([dirs[d]["at"][:, sl], dirs[d]["rt"][:, sl]], axis=0), s_old[u])
          for u, (d, _, sl) in enumerate(units)]
    vs = [stack(dirs[d]["v"][:, sl]) for d, _, sl in units]
    strict = [dirs[d]["strict"] for d, _, _ in units]
    incl = [dirs[d]["incl"] for d, _, _ in units]
    x = [stack(m2[p][0:ch]) + _bdot(jnp.where(strict[p], m1[p][0:2 * ch, 2 * ch:4 * ch], 0.0), vs[p]) for p in pairs]
    nm = [jnp.where(strict[p], m1[p][0:2 * ch, 0:2 * ch], 0.0).astype(BF16) for p in pairs]
    for _ in range(5):
        sq_ap = [_bdot(nm[p], jnp.concatenate([nm[p], x[p].astype(BF16)], axis=1)) for p in pairs]
        nm = [sq_ap[p][:, 0:2 * ch].astype(BF16) for p in pairs]
        x = [x[p] + sq_ap[p][:, 2 * ch:4 * ch] for p in pairs]
    x = [x[p] + _bdot(nm[p], x[p]) for p in pairs]
    uv = [jnp.concatenate([x[p], vs[p]], axis=0).astype(BF16) for p in pairs]
    ys = [_bdot(jnp.concatenate([jnp.where(incl[p], m1[p][2 * ch:4 * ch, 0:2 * ch], 0.0),
                                 jnp.where(incl[p], m1[p][2 * ch:4 * ch, 2 * ch:4 * ch], 0.0)], axis=1), uv[p])
          for p in pairs]
    s_new = [s_old[u] * dirs[d]["g_all"][:, sl]
             + _bdot_tn(uv[u], jnp.concatenate([stack(dirs[d]["bh"][:, sl]), stack(dirs[d]["kh"][:, sl])], axis=0))
             for u, (d, _, sl) in enumerate(units)]
    for u, (d, _, sl) in enumerate(units):
        (yf_ref, yb_ref)[d][:, sl] = (ys[u][0:ch] + ys[u][ch:2 * ch] + m2[u][ch:2 * ch]).astype(BF16)
    for u, (d, p, _) in enumerate(units):
        s_scr[d, p] = s_new[u]

    @pl.when(is_ctx & (c == nc - 1))
    def _():
        sf_ref[...] = s_scr[...]


def _rwkv_scan(r, v, kk, lw, kd, b, s0, items):
    n = r.shape[0]
    ch = RWKV_CHUNK
    tok = lambda d: pl.BlockSpec((ch, D), lambda j: (items.block(d, j), 0))
    tokd = lambda d: pl.BlockSpec((None, ch, D), lambda j: (d, items.block(d, j), 0))
    st = lambda seq: pl.BlockSpec((2, None, N_PAIR, LANES, LANES), lambda j: (0, seq(j), 0, 0, 0))
    per_dir = lambda d: [tok(d), tok(d), tok(d), tokd(d), tokd(d), tokd(d)]
    return pl.pallas_call(
        functools.partial(_rwkv_scan_kernel, items=items),
        grid=(items.n_items,),
        in_specs=per_dir(0) + per_dir(1) + [st(items.lat_seq)],
        out_specs=[tok(0), tok(1), st(items.ctx_seq)],
        out_shape=[jax.ShapeDtypeStruct((n, D), BF16), jax.ShapeDtypeStruct((n, D), BF16),
                   jax.ShapeDtypeStruct((2, items.n_ctx, N_PAIR, LANES, LANES), F32)],
        scratch_shapes=[pltpu.VMEM((2, N_PAIR, LANES, LANES), F32)],
        compiler_params=_cparams(1),
        name="rwkv_scan",
    )(r, v, kk, lw, kd, b, r, v, kk, lw, kd, b, s0)


def _rwkv_post_kernel(x_ref, yf_ref, yb_ref, gate_ref, bonus_ref, mod_ref, lnw_ref, lnb_ref, wo_ref, hsum_ref, hexp_ref,
                      o_ref):
    hsum, hexp = hsum_ref[...], hexp_ref[...]
    head_mean = lambda t: _dot_f32_rhs01(_dot_f32_rhs01(t, hsum), hexp) * (1.0 / HEAD_A)
    y = yf_ref[...].astype(F32) + yb_ref[...].astype(F32)
    yc = y - head_mean(y)
    yn = yc * lax.rsqrt(head_mean(yc * yc) + LNX_EPS)
    z = (yn * lnw_ref[...] + lnb_ref[...] + bonus_ref[...].astype(F32)) * gate_ref[...].astype(F32)
    o_ref[...] = x_ref[...] + mod_ref[:, 2 * D:3 * D] * _bdot(z, wo_ref[...])


def _rwkv_post(x, y_fwd, y_bwd, gate, bonus, mod, p_tiles, tiles_per_seq, ln_w, ln_b, wo, hsum, hexp):
    n = x.shape[0]
    full = lambda a: pl.BlockSpec(a.shape, lambda i: (0,) * a.ndim)
    tok = pl.BlockSpec((TM, D), lambda i: (i, 0))
    consts = [ln_w.reshape(1, D), ln_b.reshape(1, D), wo, hsum, hexp]
    return pl.pallas_call(
        _rwkv_post_kernel,
        grid=(n // TM,),
        in_specs=[tok, tok, tok, tok, tok,
                  pl.BlockSpec((None, 1, N_MOD * D), lambda i: (_mod_row(i, p_tiles, tiles_per_seq), 0, 0))]
        + [full(a) for a in consts],
        out_specs=tok,
        out_shape=jax.ShapeDtypeStruct((n, D), F32),
        compiler_params=_cparams(1),
        name="rwkv_post",
    )(x, y_fwd, y_bwd, gate, bonus, mod, *consts)


def _ret_pre_kernel(x_ref, mod_ref, g_ref, cos_ref, sin_ref, win_ref, q_o, k_o, v_o, gate_o):
    mod = mod_ref[...]
    h = _norm_mod(x_ref[...], g_ref[...], mod[:, 0:D], mod[:, D:2 * D])
    proj = _bdot(h, win_ref[...])
    cos, sin = cos_ref[...], sin_ref[...]

    def rope(t):
        outs = []
        for j in range(D // LANES):
            tj = t[:, j * LANES:(j + 1) * LANES]
            cj = cos[:, (j % 2) * LANES:(j % 2 + 1) * LANES]
            sj = sin[:, (j % 2) * LANES:(j % 2 + 1) * LANES]
            outs.append(tj * cj + pltpu.roll(tj, LANES // 2, 1) * sj)
        return jnp.concatenate(outs, axis=1)

    q_o[...] = rope(proj[:, 0:D]).astype(BF16)
    k_o[...] = rope(proj[:, D:2 * D] * (DK_B ** -0.5)).astype(BF16)
    v_o[...] = proj[:, 2 * D:4 * D].astype(BF16)
    gate_o[...] = proj[:, 4 * D:6 * D].astype(BF16)


def _ret_pre(x, mod, p_tiles, tiles_per_seq, norm_g, cos_t, sin_t, w_in):
    n = x.shape[0]
    full = lambda a: pl.BlockSpec(a.shape, lambda i: (0,) * a.ndim)
    tok = lambda w: pl.BlockSpec((TM, w), lambda i: (i, 0))
    tab = pl.BlockSpec((TM, DK_B), lambda i: (jnp.where(i < p_tiles, 0, 1 + (i - p_tiles) % tiles_per_seq), 0))
    return pl.pallas_call(
        _ret_pre_kernel,
        grid=(n // TM,),
        in_specs=[tok(D), pl.BlockSpec((None, 1, N_MOD * D), lambda i: (_mod_row(i, p_tiles, tiles_per_seq), 0, 0)),
                  full(norm_g.reshape(1, D)), tab, tab, full(w_in)],
        out_specs=[tok(D), tok(D), tok(2 * D), tok(2 * D)],
        out_shape=[jax.ShapeDtypeStruct((n, D), BF16), jax.ShapeDtypeStruct((n, D), BF16),
                   jax.ShapeDtypeStruct((n, 2 * D), BF16), jax.ShapeDtypeStruct((n, 2 * D), BF16)],
        compiler_params=_cparams(1),
        name="ret_pre",
    )(x, mod, norm_g.reshape(1, D), cos_t, sin_t, w_in)


def _ret_scan_kernel(lg_ref, qf_ref, kf_ref, vf_ref, qb_ref, kb_ref, vb_ref, s0_ref, of_ref, ob_ref, sf_ref, s_scr, *,
                     items):
    is_ctx, c, nc = items.decode(pl.program_id(0))
    ch = RET_CHUNK

    @pl.when(c == 0)
    def _():
        s_scr[...] = jnp.where(is_ctx, 0.0, s0_ref[...])

    ti = lax.broadcasted_iota(I32, (ch, ch), 0)
    tj = lax.broadcasted_iota(I32, (ch, ch), 1)
    io = ((qf_ref, kf_ref, vf_ref, of_ref), (qb_ref, kb_ref, vb_ref, ob_ref))
    units = [(d, hd) for d in range(2) for hd in range(H_B)]

    def decays(d, hd):
        rev = d == 1
        lg = lg_ref[d, hd]
        rel = ((tj - ti) if rev else (ti - tj)).astype(F32)
        steps_q = ((ch - ti) if rev else (ti + 1)).astype(F32)
        steps_k = (ti if rev else (ch - 1 - ti)).astype(F32)
        return dict(mask=jnp.where(rel >= 0, jnp.exp(jnp.maximum(rel, 0.0) * lg), 0.0),
                    q_dec=jnp.exp(steps_q * lg), k_dec=jnp.exp(steps_k * lg),
                    chunk_dec=jnp.exp(jnp.full((1, DV_B), float(ch), F32) * lg))

    dec = [decays(d, hd) for d, hd in units]
    qh = [io[d][0][:, hd * DK_B:(hd + 1) * DK_B] for d, hd in units]
    kh = [io[d][1][:, hd * DK_B:(hd + 1) * DK_B] for d, hd in units]
    vh = [io[d][2][:, hd * DV_B:(hd + 1) * DV_B] for d, hd in units]
    s_old = [s_scr[d, hd] for d, hd in units]
    rng = range(len(units))
    scores = [_bdot_nt(qh[u], kh[u]) * dec[u]["mask"] for u in rng]
    cross = [_bdot(qh[u], s_old[u]) * jnp.concatenate([dec[u]["q_dec"]] * (DV_B // ch), axis=1) for u in rng]
    out = [_bdot(scores[u], vh[u]) + cross[u] for u in rng]
    s_new = [s_old[u] * dec[u]["chunk_dec"]
             + _bdot_tn(kh[u].astype(F32) * jnp.concatenate([dec[u]["k_dec"]] * (DK_B // ch), axis=1), vh[u])
             for u in rng]
    for u, (d, hd) in enumerate(units):
        io[d][3][:, hd * DV_B:(hd + 1) * DV_B] = out[u].astype(BF16)
    for u, (d, hd) in enumerate(units):
        s_scr[d, hd] = s_new[u]

    @pl.when(is_ctx & (c == nc - 1))
    def _():
        sf_ref[...] = s_scr[...]


def _ret_scan(log_gamma, q, k, v, s0, items):
    n = q.shape[0]
    ch = RET_CHUNK
    tok = lambda d, w: pl.BlockSpec((ch, w), lambda j: (items.block(d, j), 0))
    st = lambda seq: pl.BlockSpec((2, None, H_B, DK_B, DV_B), lambda j: (0, seq(j), 0, 0, 0))
    per_dir = lambda d: [tok(d, D), tok(d, D), tok(d, 2 * D)]
    return pl.pallas_call(
        functools.partial(_ret_scan_kernel, items=items),
        grid=(items.n_items,),
        in_specs=[pl.BlockSpec(memory_space=pltpu.SMEM)] + per_dir(0) + per_dir(1) + [st(items.lat_seq)],
        out_specs=[tok(0, 2 * D), tok(1, 2 * D), st(items.ctx_seq)],
        out_shape=[jax.ShapeDtypeStruct((n, 2 * D), BF16), jax.ShapeDtypeStruct((n, 2 * D), BF16),
                   jax.ShapeDtypeStruct((2, items.n_ctx, H_B, DK_B, DV_B), F32)],
        scratch_shapes=[pltpu.VMEM((2, H_B, DK_B, DV_B), F32)],
        compiler_params=_cparams(1),
        name="ret_scan",
    )(log_gamma, q, k, v, q, k, v, s0)


def _ret_post_kernel(x_ref, of_ref, ob_ref, gate_ref, mod_ref, wout_ref, out_ref):
    o = of_ref[...].astype(F32) + ob_ref[...].astype(F32)
    parts = []
    for hd in range(H_B):
        oh = o[:, hd * DV_B:(hd + 1) * DV_B]
        parts.append(oh * lax.rsqrt(jnp.mean(oh * oh, axis=-1, keepdims=True) + NORM_EPS))
    y = _silu(gate_ref[...].astype(F32)) * jnp.concatenate(parts, axis=1)
    out_ref[...] = x_ref[...] + mod_ref[:, 2 * D:3 * D] * _bdot(y, wout_ref[...])


def _ret_post(x, o_fwd, o_bwd, gate, mod, p_tiles, tiles_per_seq, w_out):
    n = x.shape[0]
    tok = lambda w: pl.BlockSpec((TM, w), lambda i: (i, 0))
    return pl.pallas_call(
        _ret_post_kernel,
        grid=(n // TM,),
        in_specs=[tok(D), tok(2 * D), tok(2 * D), tok(2 * D),
                  pl.BlockSpec((None, 1, N_MOD * D), lambda i: (_mod_row(i, p_tiles, tiles_per_seq), 0, 0)),
                  pl.BlockSpec(w_out.shape, lambda i: (0, 0))],
        out_specs=tok(D),
        out_shape=jax.ShapeDtypeStruct((n, D), F32),
        compiler_params=_cparams(1),
        name="ret_post",
    )(x, o_fwd, o_bwd, gate, mod, w_out)


def _moe_route_kernel(x_ref, mod_ref, g_ref, router_ref, bias_ref, h_o, e_o, w_o, p_o, cnt_o, carry):
    i = pl.program_id(0)

    @pl.when(i == 0)
    def _():
        carry[...] = jnp.zeros_like(carry)

    mod = mod_ref[...]
    h = _norm_mod(x_ref[...], g_ref[...], mod[:, 3 * D:4 * D], mod[:, 4 * D:5 * D])
    _rows_store(h_o, h)
    lane =lax.broadcasted_iota(I32, (TM, LANES), 1)
    valid = lane < N_EXPERTS
    neg = -jnp.inf
    scores = jax.nn.sigmoid(_dot_f32(h, router_ref[...]))
    biased = jnp.where(valid, scores + bias_ref[...], neg)

    def group_reduce(t, op):
        s = 1
        while s < PER_GROUP:
            partner = jnp.where((lane & s) == 0, pltpu.roll(t, LANES - s, 1), pltpu.roll(t, s, 1))
            t = op(t, partner)
            s *= 2
        return t

    lane_f = lane.astype(F32)
    group_f = jnp.floor(lane_f * (1.0 / PER_GROUP))

    def first_lane_of_max(t):
        m = jnp.max(t, axis=-1, keepdims=True)
        return jnp.min(jnp.where(t == m, lane_f, float(LANES)), axis=-1, keepdims=True)

    m1 = group_reduce(biased, jnp.maximum)
    first1 = group_reduce(jnp.where(biased == m1, lane_f, float(LANES)), jnp.minimum)
    m2 = group_reduce(jnp.where(lane_f == first1, neg, biased), jnp.maximum)
    gscore = jnp.where(valid, m1 + m2, neg)
    cand = jnp.full((TM, LANES), neg, F32)
    for _ in range(TOPK_GROUPS):
        gsel = group_f == jnp.floor(first_lane_of_max(gscore) * (1.0 / PER_GROUP))
        cand = jnp.where(gsel, biased, cand)
        gscore = jnp.where(gsel, neg, gscore)
    hits = []
    sel01 = jnp.zeros((TM, LANES), F32)
    e_cols = jnp.zeros((TM, LANES), F32)
    for j in range(TOP_K):
        fl = first_lane_of_max(cand)
        hit = lane_f == fl
        hits.append(hit)
        sel01 = jnp.where(hit, 1.0, sel01)
        cand = jnp.where(hit, neg, cand)
        e_cols = jnp.where(lane == j, fl, e_cols)
    wsum = jnp.sum(sel01 * scores, axis=-1, keepdims=True)

    ri = lax.broadcasted_iota(I32, (TM, TM), 0)
    rj = lax.broadcasted_iota(I32, (TM, TM), 1)
    below = jnp.where(rj < ri, 1.0, 0.0).astype(BF16)
    rank = jnp.dot(below, sel01.astype(BF16), preferred_element_type=F32) + carry[...]
    carry[...] = carry[...] + jnp.sum(sel01, axis=0, keepdims=True)
    w_cols = jnp.zeros((TM, LANES), F32)
    p_cols = jnp.zeros((TM, LANES), F32)
    for j in range(TOP_K):
        wj = jnp.sum(jnp.where(hits[j], scores, 0.0), axis=-1, keepdims=True)
        w_cols = jnp.where(lane == j, wj / wsum * ROUTED_SCALE, w_cols)
        p_cols = jnp.where(lane == j, jnp.sum(jnp.where(hits[j], rank, 0.0), axis=-1, keepdims=True), p_cols)
    e_o[...] = e_cols.astype(I32)
    w_o[...] = w_cols
    p_o[...] = p_cols.astype(I32)
    cnt_o[...] = jnp.broadcast_to(carry[...], cnt_o.shape)


def _moe_route(x, mod, p_tiles, tiles_per_seq, norm_g, router, bias):
    n = x.shape[0]
    full = lambda a: pl.BlockSpec(a.shape, lambda i: (0,) * a.ndim)
    tok = lambda w: pl.BlockSpec((TM, w), lambda i: (i, 0))
    router_p = jnp.pad(router, ((0, 0), (0, LANES - N_EXPERTS)))
    bias_p = jnp.pad(bias, (0, LANES - N_EXPERTS)).reshape(1, LANES)
    return pl.pallas_call(
        _moe_route_kernel,
        grid=(n // TM,),
        in_specs=[tok(D), pl.BlockSpec((None, 1, N_MOD * D), lambda i: (_mod_row(i, p_tiles, tiles_per_seq), 0, 0)),
                  full(norm_g.reshape(1, D)), full(router_p), full(bias_p)],
        out_specs=[pl.BlockSpec((TM * ROW_SUB, LANES), lambda i: (i, 0)), tok(LANES), tok(LANES), tok(LANES),
                   pl.BlockSpec((8, LANES), lambda i: (0, 0))],
        out_shape=[jax.ShapeDtypeStruct((n * ROW_SUB, LANES), F32), jax.ShapeDtypeStruct((n, LANES), I32),
                   jax.ShapeDtypeStruct((n, LANES), F32), jax.ShapeDtypeStruct((n, LANES), I32),
                   jax.ShapeDtypeStruct((8, LANES), F32)],
        scratch_shapes=[pltpu.VMEM((1, LANES), F32)],
        compiler_params=_cparams(1),
        name="moe_route",
    )(x, mod, norm_g.reshape(1, D), router_p, bias_p)


DISPATCH_TOK = 512


def _moe_dispatch_kernel(pad_lo_ref, pad_n_ref, dest_ref, h_ref, h_hbm, xs_hbm, zero_scr, sem):
    @pl.when(pl.program_id(0) == 0)
    def _():
        zero_scr[...] = jnp.zeros_like(zero_scr)

        def pad_copy(e, s):
            return pltpu.make_async_copy(zero_scr, _row_tile(xs_hbm, pad_lo_ref[e] + s), sem)

        def per_expert(e, carry):
            def issue(s, c):
                pad_copy(e, s).start()
                return c

            def drain(s, c):
                pad_copy(e, s).wait()
                return c

            lax.fori_loop(0, pad_n_ref[e], issue, 0)
            lax.fori_loop(0, pad_n_ref[e], drain, 0)
            return carry

        lax.fori_loop(0, N_EXPERTS, per_expert, 0)

    tile0 = pl.program_id(0) * DISPATCH_TOK

    def copy(t, kq):
        src = _row_tile(h_ref, t) if kq % HBM_SRC_EVERY else _row_tile(h_hbm, tile0 + t)
        return pltpu.make_async_copy(src, _row_tile(xs_hbm, dest_ref[t * TOP_K + kq]), sem)

    def issue(t, carry):
        for kq in range(TOP_K):
            copy(t, kq).start(priority=kq % 2)
        return carry

    def drain(t, carry):
        for kq in range(TOP_K):
            copy(t, kq).wait()
        return carry

    lax.fori_loop(0, DISPATCH_TOK, issue, 0)
    lax.fori_loop(0, DISPATCH_TOK, drain, 0)


HBM_SRC_EVERY = 4


def _moe_dispatch(pad_lo, pad_n, dest_flat, h, n_slots):
    n = h.shape[0] // ROW_SUB
    grid_spec = pltpu.PrefetchScalarGridSpec(
        num_scalar_prefetch=2,
        grid=(n // DISPATCH_TOK,),
        in_specs=[pl.BlockSpec((DISPATCH_TOK * TOP_K,), lambda i, *_: (i,), memory_space=pltpu.SMEM),
                  pl.BlockSpec((DISPATCH_TOK * ROW_SUB, LANES), lambda i, *_: (i, 0)),
                  pl.BlockSpec(memory_space=pl.ANY)],
        out_specs=pl.BlockSpec(memory_space=pl.ANY),
        scratch_shapes=[pltpu.VMEM((ROW_SUB, LANES), F32), pltpu.SemaphoreType.DMA],
    )
    return pl.pallas_call(
        _moe_dispatch_kernel,
        grid_spec=grid_spec,
        out_shape=jax.ShapeDtypeStruct((n_slots * ROW_SUB, LANES), F32),
        compiler_params=_cparams(1, has_side_effects=True),
        name="moe_dispatch",
    )(pad_lo, pad_n, dest_flat, h, h)


def _moe_expert_kernel(be_ref, xs_ref, wgu_ref, wdn_ref, o_ref, wgu_bf, wdn_bf):
    i = pl.program_id(0)

    @pl.when((i == 0) | (be_ref[i] != be_ref[jnp.maximum(i - 1, 0)]))
    def _():
        wgu_bf[...] = wgu_ref[...].astype(BF16)
        wdn_bf[...] = wdn_ref[...].astype(BF16)

    gu = _bdot(_rows_load(xs_ref), wgu_bf[...])
    act = _silu(gu[:, 0:D_EXPERT]) * gu[:, D_EXPERT:2 * D_EXPERT]
    _rows_store(o_ref, _bdot(act, wdn_bf[...]))


def _moe_expert(block_e, xs, w_gu, w_down, layer):
    n_slots = xs.shape[0] // ROW_SUB
    n_blocks = n_slots // EXPERT_BLOCK
    rows = pl.BlockSpec((EXPERT_BLOCK * ROW_SUB, LANES), lambda i, be: (i, 0))
    grid_spec = pltpu.PrefetchScalarGridSpec(
        num_scalar_prefetch=1,
        grid=(n_blocks,),
        in_specs=[rows,
                  pl.BlockSpec((None, None, D, 2 * D_EXPERT), lambda i, be: (layer, be[i], 0, 0)),
                  pl.BlockSpec((None, None, D_EXPERT, D), lambda i, be: (layer, be[i], 0, 0))],
        out_specs=rows,
        scratch_shapes=[pltpu.VMEM((D, 2 * D_EXPERT), BF16), pltpu.VMEM((D_EXPERT, D), BF16)],
    )
    return pl.pallas_call(
        _moe_expert_kernel,
        grid_spec=grid_spec,
        out_shape=jax.ShapeDtypeStruct((n_slots * ROW_SUB, LANES), F32),
        compiler_params=_cparams(1),
        name="moe_expert",
    )(block_e, xs, w_gu, w_down)


COMBINE_TOK = 128


def _moe_combine_kernel(dest_ref, dest_next_ref, x_ref, h_ref, w_ref, mod_ref, shgu_ref, shdn_ref, gfin_ref, ys_hbm,
                        *refs, final_norm, ctx_tiles):
    o_refs, (buf, sems) = refs[:-2], refs[-2:]
    i = pl.program_id(0)
    slot = i % 2

    def copy(idx_ref, s, t, kq):
        return pltpu.make_async_copy(_row_tile(ys_hbm, idx_ref[t * TOP_K + kq]), _row_tile(buf.at[s, kq], t),
                                     sems.at[s])

    def fetch(idx_ref, s):
        def body(t, carry):
            for kq in range(TOP_K):
                copy(idx_ref, s, t, kq).start(priority=kq % 2)
            return carry
        lax.fori_loop(0, COMBINE_TOK, body, 0, unroll=2)

    @pl.when(i == 0)
    def _():
        fetch(dest_ref, slot)

    @pl.when(i + 1 < pl.num_programs(0))
    def _():
        fetch(dest_next_ref, 1 - slot)

    gu = _bdot(_rows_load(h_ref), shgu_ref[...])
    acc = _bdot(_silu(gu[:, 0:D_EXPERT]) * gu[:, D_EXPERT:2 * D_EXPERT], shdn_ref[...])

    def drain(t, carry):
        for kq in range(TOP_K):
            copy(dest_ref, slot, t, kq).wait()
        return carry

    lax.fori_loop(0, COMBINE_TOK, drain, 0)
    w = w_ref[...]
    for kq in range(TOP_K):
        acc = acc + _rows_load(buf.at[slot, kq]) * w[:, kq:kq + 1]
    out = x_ref[...] + mod_ref[:, 5 * D:6 * D] * acc
    if final_norm:
        out = out * lax.rsqrt(jnp.mean(out * out, axis=-1, keepdims=True) + NORM_EPS) * gfin_ref[...]

        @pl.when(i < ctx_tiles)
        def _():
            o_refs[0][...] = out

        @pl.when(i >= ctx_tiles)
        def _():
            o_refs[1][...] = out
    else:
        o_refs[0][...] = out


def _moe_combine(dest_flat, x, h, w_cols, mod, p_tiles, tiles_per_seq, sh_gu, sh_down, g_final, ys, final_norm):
    n = x.shape[0]
    ratio = TM // COMBINE_TOK
    tok = lambda w: pl.BlockSpec((COMBINE_TOK, w), lambda i: (i, 0))
    full = lambda a: pl.BlockSpec(a.shape, lambda i: (0,) * a.ndim)
    n_tiles = n // COMBINE_TOK
    ctx_tiles = p_tiles * ratio
    if final_norm:
        out_specs = [pl.BlockSpec((COMBINE_TOK, D), lambda i: (jnp.minimum(i, ctx_tiles - 1), 0)),
                     pl.BlockSpec((COMBINE_TOK, D), lambda i: (jnp.maximum(i - ctx_tiles, 0), 0))]
        out_shape = [jax.ShapeDtypeStruct((ctx_tiles * COMBINE_TOK, D), F32),
                     jax.ShapeDtypeStruct((n - ctx_tiles * COMBINE_TOK, D), F32)]
    else:
        out_specs, out_shape = tok(D), jax.ShapeDtypeStruct((n, D), F32)
    return pl.pallas_call(
        functools.partial(_moe_combine_kernel, final_norm=final_norm, ctx_tiles=ctx_tiles),
        grid=(n_tiles,),
        in_specs=[pl.BlockSpec((COMBINE_TOK * TOP_K,), lambda i: (i,), memory_space=pltpu.SMEM),
                  pl.BlockSpec((COMBINE_TOK * TOP_K,), lambda i: (jnp.minimum(i + 1, n_tiles - 1),),
                               memory_space=pltpu.SMEM),
                  tok(D), pl.BlockSpec((COMBINE_TOK * ROW_SUB, LANES), lambda i: (i, 0)), tok(LANES),
                  pl.BlockSpec((None, 1, N_MOD * D), lambda i: (_mod_row(i // ratio, p_tiles, tiles_per_seq), 0, 0)),
                  full(sh_gu), full(sh_down), full(g_final),
                  pl.BlockSpec(memory_space=pl.ANY)],
        out_specs=out_specs,
        out_shape=out_shape,
        scratch_shapes=[pltpu.VMEM((2, TOP_K, COMBINE_TOK * ROW_SUB, LANES), F32), pltpu.SemaphoreType.DMA((2,))],
        compiler_params=_cparams(1),
        name="moe_combine",
    )(dest_flat, dest_flat, x, h, w_cols, mod, sh_gu, sh_down, g_final, ys)


def _moe_layer(x, mod, p_tiles, tiles_per_seq, norm_g, router, bias, w_gu, w_down, layer, sh_gu, sh_down, g_final,
               final_norm):
    n = x.shape[0]
    h, e_cols, w_cols, p_cols, counts = _moe_route(x, mod, p_tiles, tiles_per_seq, norm_g, router, bias)
    counts = counts[0, :N_EXPERTS].astype(I32)
    padded = (counts + EXPERT_BLOCK - 1) // EXPERT_BLOCK * EXPERT_BLOCK
    pad_end = jnp.cumsum(padded)
    pad_start = pad_end - padded
    chosen = e_cols[:, :TOP_K, None] == jnp.arange(N_EXPERTS, dtype=I32)
    dest = (jnp.sum(jnp.where(chosen, pad_start, 0), axis=-1) + p_cols[:, :TOP_K]).reshape(-1)
    n_blocks = n * TOP_K // EXPERT_BLOCK + N_EXPERTS
    n_slots = n_blocks * EXPERT_BLOCK
    block_start = jnp.arange(n_blocks, dtype=I32) * EXPERT_BLOCK
    block_e = jnp.minimum(jnp.sum((pad_end[None, :] <= block_start[:, None]).astype(I32), axis=1), N_EXPERTS - 1)
    pad_n = (padded - counts).at[N_EXPERTS - 1].add(n_slots - pad_end[N_EXPERTS - 1])
    xs = _moe_dispatch(pad_start + counts, pad_n, dest, h, n_slots)
    ys = _moe_expert(block_e, xs, w_gu, w_down, layer)
    return _moe_combine(dest, x, h, w_cols, mod, p_tiles, tiles_per_seq, sh_gu, sh_down, g_final.reshape(1, D), ys,
                        final_norm)


def _pair_pack(s):
    lead = s.shape[:-3]
    s = s.reshape(lead + (N_PAIR, 2, HEAD_A, HEAD_A))
    z = jnp.zeros_like(s[..., 0, :, :])
    top = jnp.concatenate([s[..., 0, :, :], z], axis=-1)
    bot = jnp.concatenate([z, s[..., 1, :, :]], axis=-1)
    return jnp.concatenate([top, bot], axis=-2)


def _pair_unpack(s):
    a = s[..., :HEAD_A, :HEAD_A]
    b = s[..., HEAD_A:, HEAD_A:]
    return jnp.stack([a, b], axis=-3).reshape(s.shape[:-3] + (H_A, HEAD_A, HEAD_A))


def _rope_tables(rows):
    half = DK_B // 2
    n_freq = half // 2
    inv = ROPE_BASE ** (-jnp.arange(n_freq, dtype=F32) / n_freq)
    pos_r = jnp.repeat(jnp.arange(rows, dtype=F32), GRID_W)
    pos_c = jnp.tile(jnp.arange(GRID_W, dtype=F32), rows)

    def tab(pos):
        ang = pos[:, None] * inv[None, :]
        c, s = jnp.cos(ang), jnp.sin(ang)
        return jnp.concatenate([c, c], -1), jnp.concatenate([-s, s], -1)

    cr, sr = tab(pos_r)
    cc, sc = tab(pos_c)
    cos = jnp.concatenate([cr, cc], -1)
    sin = jnp.concatenate([sr, sc], -1)
    cos = jnp.concatenate([jnp.ones((TM, DK_B), F32), cos], 0)
    sin = jnp.concatenate([jnp.zeros((TM, DK_B), F32), sin], 0)
    return cos, sin


def kernel(x_prompt, x_sample, state_rwkv, state_ret, c, c_ctx, ada_w, ada_b, norm_mix, norm_ffn, norm_final, rwkv_mu, rwkv_w0, rwkv_w1, rwkv_w2, rwkv_a0, rwkv_a1, rwkv_a2, rwkv_wrkv, rwkv_wo, rwkv_g1, rwkv_g2, rwkv_k_k, rwkv_k_a, rwkv_r_k, rwkv_ln_w, rwkv_ln_b, ret_w_in, ret_w_out, ret_decay_logit, moe_router, moe_bias, moe_w_gu, moe_w_down, moe_sh_gu, moe_sh_down):
    bp, tp, _ = x_prompt.shape
    bs, ts, _ = x_sample.shape
    n_p, n_s = bp * tp, bs * ts
    assert tp == TM and ts % TM == 0 and TM % GRID_W == 0
    p_tiles = n_p // TM
    tiles_per_seq = ts // TM
    rows = ts // GRID_W
    x = jnp.concatenate([x_prompt.reshape(n_p, D), x_sample.reshape(n_s, D)], axis=0)
    n = n_p + n_s

    n_cond = 16
    cond = jnp.zeros((n_cond, D), F32).at[0].set(c_ctx).at[1:1 + bs].set(c)
    mod = _modulation(cond, ada_w, ada_b).reshape(ada_w.shape[0], n_cond, 1, N_MOD * D)

    head_of = jnp.arange(D, dtype=I32) // HEAD_A
    hsum = (head_of[:, None] == jnp.arange(LANES, dtype=I32)[None, :]).astype(BF16)
    hexp = hsum.T

    gl = rwkv_g1.shape[-1]
    glp = -(-gl // LANES) * LANES
    zb = lambda a: jnp.zeros_like(a)
    wts = {
        "mu": jnp.pad(rwkv_mu[0], ((0, 8 - N_MOD), (0, 0))),
        "wrkv": rwkv_wrkv[0].astype(BF16),
        "g1": jnp.pad(rwkv_g1[0], ((0, 0), (0, glp - gl))).astype(BF16),
        "g2": jnp.pad(rwkv_g2[0], ((0, glp - gl), (0, 0))).astype(BF16),
        "w1": jnp.concatenate([rwkv_w1[0, 0], rwkv_w1[0, 1]], axis=1).astype(BF16),
        "w2": jnp.concatenate([jnp.concatenate([rwkv_w2[0, 0], zb(rwkv_w2[0, 1])], 1),
                               jnp.concatenate([zb(rwkv_w2[0, 0]), rwkv_w2[0, 1]], 1)], 0).astype(BF16),
        "w0": rwkv_w0[0].reshape(1, 2 * D),
        "a1": jnp.concatenate([rwkv_a1[0, 0], rwkv_a1[0, 1]], axis=1).astype(BF16),
        "a2": jnp.concatenate([jnp.concatenate([rwkv_a2[0, 0], zb(rwkv_a2[0, 1])], 1),
                               jnp.concatenate([zb(rwkv_a2[0, 0]), rwkv_a2[0, 1]], 1)], 0).astype(BF16),
        "a0": rwkv_a0[0].reshape(1, 2 * D),
        "k_k": rwkv_k_k[0], "k_a": rwkv_k_a[0], "r_k": rwkv_r_k[0].reshape(D),
        "hsum": hsum, "hexp": hexp,
    }
    r, v, kk, lw, kd, b, gate, bonus = _rwkv_pre(x, mod[0], p_tiles, tiles_per_seq, norm_mix[0], wts)
    s0_lat = _pair_pack(jnp.moveaxis(state_rwkv[:, 0], 1, 0))
    y_fwd, y_bwd, s_fin = _rwkv_scan(r, v, kk, lw, kd, b, s0_lat,
                                     _ScanItems(bp, tp // RWKV_CHUNK, bs, ts // RWKV_CHUNK))
    new_state_rwkv = jnp.moveaxis(_pair_unpack(s_fin), 0, 1)[:, None]
    x = _rwkv_post(x, y_fwd, y_bwd, gate, bonus, mod[0], p_tiles, tiles_per_seq, rwkv_ln_w[0], rwkv_ln_b[0],
                   rwkv_wo[0].astype(BF16), hsum, hexp)
    x = _moe_layer(x, mod[0], p_tiles, tiles_per_seq, norm_ffn[0], moe_router[0], moe_bias[0],
                   moe_w_gu, moe_w_down, 0, moe_sh_gu[0].astype(BF16),
                   moe_sh_down[0].astype(BF16), norm_final, False)

    cos_t, sin_t = _rope_tables(rows)
    q, k, vv, rgate = _ret_pre(x, mod[1], p_tiles, tiles_per_seq, norm_mix[1], cos_t, sin_t, ret_w_in[0].astype(BF16))
    log_gamma = jax.nn.log_sigmoid(ret_decay_logit[0].astype(F32))
    o_fwd, o_bwd, r_fin = _ret_scan(log_gamma, q, k, vv, jnp.moveaxis(state_ret[:, 0], 1, 0),
                                    _ScanItems(bp, tp // RET_CHUNK, bs, ts // RET_CHUNK))
    new_state_ret = jnp.moveaxis(r_fin, 0, 1)[:, None]
    x = _ret_post(x, o_fwd, o_bwd, rgate, mod[1], p_tiles, tiles_per_seq, ret_w_out[0].astype(BF16))
    y_ctx, y_lat = _moe_layer(x, mod[1], p_tiles, tiles_per_seq, norm_ffn[1], moe_router[1], moe_bias[1],
                              moe_w_gu, moe_w_down, 1, moe_sh_gu[1].astype(BF16),
                              moe_sh_down[1].astype(BF16), norm_final, True)

    return (y_ctx.reshape(bp, tp, D), y_lat.reshape(bs, ts, D), new_state_rwkv, new_state_ret)
```

```python
import functools

import jax
import jax.numpy as jnp
from jax import lax
from jax.experimental import pallas as pl
from jax.experimental.pallas import tpu as pltpu

F32, BF16, I32 = jnp.float32, jnp.bfloat16, jnp.int32

D = 1024
N_MOD = 6
NORM_EPS = 1e-6
GRID_W = 64
HEAD_A = 64
H_A = D // HEAD_A
LNX_EPS = 64e-5
RWKV_CHUNK = 64
N_PAIR = H_A // 2
H_B = 4
DK_B = D // H_B
DV_B = 2 * DK_B
RET_CHUNK = 128
ROPE_BASE = 10000.0
N_EXPERTS = 64
TOP_K = 8
N_GROUPS = 8
TOPK_GROUPS = 4
PER_GROUP = N_EXPERTS // N_GROUPS
D_EXPERT = 256
ROUTED_SCALE = 2.5
EXPERT_BLOCK = 256

TM = 256
LANES = 128
ROW_SUB = D // LANES
VMEM_LIMIT = 56 * 1024 * 1024


def _cparams(n_grid_axes, **kw):
    return pltpu.CompilerParams(dimension_semantics=("arbitrary",) * n_grid_axes, vmem_limit_bytes=VMEM_LIMIT, **kw)


def _bdot(a, b):
    return jnp.dot(a.astype(BF16), b.astype(BF16), preferred_element_type=F32)


def _bdot_nt(a, b):
    return lax.dot_general(a.astype(BF16), b.astype(BF16), (((1,), (1,)), ((), ())), preferred_element_type=F32)


def _bdot_tn(a, b):
    return lax.dot_general(a.astype(BF16), b.astype(BF16), (((0,), (0,)), ((), ())), preferred_element_type=F32)


def _split3(x):
    hi = x.astype(BF16)
    r1 = x - hi.astype(F32)
    mid = r1.astype(BF16)
    lo = (r1 - mid.astype(F32)).astype(BF16)
    return hi, mid, lo


def _dot_f32(a, b):
    ah, am, al = _split3(a)
    bh, bm, bl = _split3(b)
    d = lambda x, y: jnp.dot(x, y, preferred_element_type=F32)
    return d(ah, bh) + (d(ah, bm) + d(am, bh)) + (d(ah, bl) + d(al, bh) + d(am, bm))


def _dot_f32_rhs01(a, m01):
    ah, am, al = _split3(a)
    d = lambda x: jnp.dot(x, m01, preferred_element_type=F32)
    return d(ah) + d(am) + d(al)


def _dot_f32_lhs01(m01, b):
    bh, bm, bl = _split3(b)
    d = lambda x: jnp.dot(m01, x, preferred_element_type=F32)
    return d(bh) + d(bm) + d(bl)


def _norm_mod(x, g, shift, scale):
    ms = jnp.mean(x * x, axis=-1, keepdims=True)
    y = x * lax.rsqrt(ms + NORM_EPS) * g
    return y * (1.0 + scale) + shift


def _silu(x):
    return x * jax.nn.sigmoid(x)


def _rows_load(ref):
    m = ref.shape[0] // ROW_SUB
    return jnp.concatenate([ref[pl.ds(j, m, stride=ROW_SUB), :] for j in range(ROW_SUB)], axis=1)


def _rows_store(ref, val):
    m = val.shape[0]
    for j in range(ROW_SUB):
        ref[pl.ds(j, m, stride=ROW_SUB), :] = val[:, j * LANES:(j + 1) * LANES]


def _row_tile(ref, idx):
    return ref.at[pl.ds(pl.multiple_of(idx * ROW_SUB, ROW_SUB), ROW_SUB)]


def _mod_row(i, p_tiles, tiles_per_seq):
    return jnp.where(i < p_tiles, 0, 1 + (i - p_tiles) // tiles_per_seq)


def _mod_kernel(cond_ref, w_ref, b_ref, o_ref):
    o_ref[...] = _dot_f32(_silu(cond_ref[...]), w_ref[...]) + b_ref[...]


def _modulation(cond, ada_w, ada_b):
    depth, _, n6 = ada_w.shape
    tn = 1536
    return pl.pallas_call(
        _mod_kernel,
        grid=(depth, n6 // tn),
        in_specs=[
            pl.BlockSpec(cond.shape, lambda l, j: (0, 0)),
            pl.BlockSpec((None, D, tn), lambda l, j: (l, 0, j)),
            pl.BlockSpec((None, 1, tn), lambda l, j: (l, 0, j)),
        ],
        out_specs=pl.BlockSpec((None, cond.shape[0], tn), lambda l, j: (l, 0, j)),
        out_shape=jax.ShapeDtypeStruct((depth, cond.shape[0], n6), F32),
        compiler_params=_cparams(2),
        name="adaln_mod",
    )(cond, ada_w, ada_b.reshape(depth, 1, n6))


def _rwkv_pre_kernel(xc_ref, xl_ref, xu_ref, xd_ref, mod_ref, g_ref, mu_ref, wrkv_ref, g1_ref, g2_ref, w1_ref, w2_ref,
                     w0_ref, a1_ref, a2_ref, a0_ref, kk_ref, ka_ref, rk_ref, hsum_ref, hexp_ref,
                     r_o, v_o, kk_o, lw_o, kd_o, b_o, gate_o, bonus_o, *, p_tiles, tiles_per_seq):
    i = pl.program_id(0)
    is_p = i < p_tiles
    sub = (i - p_tiles) % tiles_per_seq
    mod = mod_ref[...]
    shift, scale = mod[:, 0:D], mod[:, D:2 * D]
    g = g_ref[...]
    h = _norm_mod(jnp.where(is_p, xc_ref[...], xl_ref[...]), g, shift, scale)
    hu = _norm_mod(xu_ref[...], g, shift, scale)
    hd = _norm_mod(xd_ref[...], g, shift, scale)

    q = D // 4
    row = lax.broadcasted_iota(I32, (TM, 1), 0)
    per = jnp.where(is_p, TM, GRID_W)
    pos = row & (per - 1)

    def prev1(a):
        return jnp.where(pos == 0, 0.0, pltpu.roll(a, 1, 0))

    def next1(a):
        return jnp.where(pos == per - 1, 0.0, pltpu.roll(a, TM - 1, 0))

    h0, h1, h2, h3 = (h[:, j * q:(j + 1) * q] for j in range(4))
    up = jnp.concatenate([jnp.where(sub == 0, 0.0, hu[:, 2 * q:3 * q]), h2[0:TM - GRID_W]], axis=0)
    down = jnp.concatenate([h3[GRID_W:TM], jnp.where(sub == tiles_per_seq - 1, 0.0, hd[:, 3 * q:4 * q])], axis=0)
    s0 = prev1(h0)
    s1 = jnp.where(is_p, prev1(h1), next1(h1))
    s2 = jnp.where(is_p, next1(h2), up)
    s3 = jnp.where(is_p, next1(h3), down)
    xx = jnp.concatenate([s0, s1, s2, s3], axis=1) - h

    mu = mu_ref[...]
    mix = lambda j: h + xx * mu[j:j + 1]
    r = _bdot(mix(0), wrkv_ref[0])
    k = _bdot(mix(2), wrkv_ref[1])
    v = _bdot(mix(3), wrkv_ref[2])
    gate = _bdot(jax.nn.sigmoid(_bdot(mix(5), g1_ref[...])), g2_ref[...])
    w_all = w0_ref[...] + _bdot(jnp.tanh(_bdot(mix(1), w1_ref[...])), w2_ref[...])
    a_all = jax.nn.sigmoid(a0_ref[...] + _bdot(_bdot(mix(4), a1_ref[...]), a2_ref[...]))

    hsum, hexp = hsum_ref[...], hexp_ref[...]
    head_sum = lambda t: _dot_f32_rhs01(_dot_f32_rhs01(t, hsum), hexp)

    kkr = k * kk_ref[...]
    kk = kkr / jnp.maximum(jnp.sqrt(head_sum(kkr * kkr)), 1e-12)
    ka = ka_ref[...]
    kd_sum = jnp.zeros_like(k)
    for d in range(2):
        wd = w_all[:, d * D:(d + 1) * D]
        z = -wd
        softplus = jnp.maximum(z, 0.0) + jnp.log(1.0 + jnp.exp(-jnp.abs(z)))
        lw_o[d] = -jnp.exp(-softplus - 0.5)
        a = a_all[:, d * D:(d + 1) * D]
        kd = k * (1.0 + (a - 1.0) * ka)
        kd_o[d] = kd
        b_o[d] = kk * a
        kd_sum = kd_sum + kd
    r_o[...] = r
    v_o[...] = v
    kk_o[...] = kk
    gate_o[...] = gate.astype(BF16)
    bonus_o[...] = (head_sum(r * kd_sum * rk_ref[...]) * v).astype(BF16)


def _rwkv_pre(x_ctx, x_lat, mod, p_tiles, tiles_per_seq, norm_g, wts):
    n = x_ctx.shape[0] + x_lat.shape[0]
    n_tiles = n // TM
    hb = TM // GRID_W
    n_hblk = x_lat.shape[0] // GRID_W
    lat = lambda i: jnp.maximum(i - p_tiles, 0)
    row = lambda a: a.reshape(1, -1)
    full = lambda a: pl.BlockSpec(a.shape, lambda i: (0,) * a.ndim)
    tok = pl.BlockSpec((TM, D), lambda i: (i, 0))
    tok2 = pl.BlockSpec((2, TM, D), lambda i: (0, i, 0))
    consts = [row(norm_g), wts["mu"], wts["wrkv"], wts["g1"], wts["g2"], wts["w1"], wts["w2"], wts["w0"],
              wts["a1"], wts["a2"], wts["a0"], row(wts["k_k"]), row(wts["k_a"]), row(wts["r_k"]),
              wts["hsum"], wts["hexp"]]
    return pl.pallas_call(
        functools.partial(_rwkv_pre_kernel, p_tiles=p_tiles, tiles_per_seq=tiles_per_seq),
        grid=(n_tiles,),
        in_specs=[
            pl.BlockSpec((TM, D), lambda i: (jnp.minimum(i, p_tiles - 1), 0)),
            pl.BlockSpec((TM, D), lambda i: (lat(i), 0)),
            pl.BlockSpec((GRID_W, D), lambda i: (jnp.maximum(lat(i) * hb - 1, 0), 0)),
            pl.BlockSpec((GRID_W, D), lambda i: (jnp.minimum(lat(i) * hb + hb, n_hblk - 1), 0)),
            pl.BlockSpec((None, 1, N_MOD * D), lambda i: (_mod_row(i, p_tiles, tiles_per_seq), 0, 0)),
        ] + [full(a) for a in consts],
        out_specs=[tok, tok, tok, tok2, tok2, tok2, tok, tok],
        out_shape=[jax.ShapeDtypeStruct((n, D), F32)] * 3 + [jax.ShapeDtypeStruct((2, n, D), F32)] * 3
        + [jax.ShapeDtypeStruct((n, D), BF16)] * 2,
        compiler_params=_cparams(1),
        name="rwkv_pre",
    )(x_ctx, x_lat, x_lat, x_lat, mod, *consts)


class _ScanItems:
    def __init__(self, n_ctx, ctx_chunks, n_lat, lat_chunks):
        self.n_ctx, self.ctx_chunks, self.n_lat, self.lat_chunks = n_ctx, ctx_chunks, n_lat, lat_chunks
        self.ctx_items = n_ctx * ctx_chunks
        self.n_items = self.ctx_items + n_lat * lat_chunks

    def decode(self, j):
        is_ctx = j < self.ctx_items
        jl = jnp.maximum(j - self.ctx_items, 0)
        c = jnp.where(is_ctx, j % self.ctx_chunks, jl % self.lat_chunks)
        return is_ctx, c, jnp.where(is_ctx, self.ctx_chunks, self.lat_chunks)

    def block(self, d, j):
        _, c, nc = self.decode(j)
        return jnp.where(d == 0, j, j - c + (nc - 1 - c))

    def ctx_seq(self, j):
        return jnp.minimum(j // self.ctx_chunks, self.n_ctx - 1)

    def lat_seq(self, j):
        return jnp.maximum(j - self.ctx_items, 0) // self.lat_chunks


def _rwkv_scan_kernel(rf_ref, vf_ref, kkf_ref, lwf_ref, kdf_ref, bf_ref, rb_ref, vb_ref, kkb_ref, lwb_ref, kdb_ref,
                      bb_ref, s0_ref, yf_ref, yb_ref, sf_ref, s_scr, *, items):
    is_ctx, c, nc = items.decode(pl.program_id(0))
    ch = RWKV_CHUNK

    @pl.when(c == 0)
    def _():
        s_scr[...] = jnp.where(is_ctx, 0.0, s0_ref[...])

    ti = lax.broadcasted_iota(I32, (ch, ch), 0)
    tj = lax.broadcasted_iota(I32, (ch, ch), 1)
    si = lax.broadcasted_iota(I32, (2 * ch, 2 * ch), 0)
    sj = lax.broadcasted_iota(I32, (2 * ch, 2 * ch), 1)
    same = (si < ch) == (sj < ch)
    ui, uj = si & (ch - 1), sj & (ch - 1)
    first = lax.broadcasted_iota(I32, (1, LANES), 1) < HEAD_A

    def stack(xp):
        return jnp.concatenate([jnp.where(first, xp, 0.0), jnp.where(first, 0.0, xp)], axis=0)

    def prologue(rev, r_ref, v_ref, kk_ref, lw_ref, kd_ref, b_ref):
        tri = jnp.where((tj >= ti) if rev else (tj <= ti), 1.0, 0.0).astype(BF16)
        lw = lw_ref[...]
        cum = _dot_f32_lhs01(tri, lw)
        last = cum[0:1] if rev else cum[ch - 1:ch]
        kk, kd, b = kk_ref[...], kd_ref[...], b_ref[...]
        e_neg = jnp.exp(-cum)
        e_rel = jnp.exp(last - cum)
        return dict(at=-kk * jnp.exp(cum - lw), rt=r_ref[...] * jnp.exp(cum), bt=b * e_neg, kt=kd * e_neg,
                    bh=b * e_rel, kh=kd * e_rel, g_all=jnp.exp(last), v=v_ref[...],
                    strict=same & ((uj > ui) if rev else (uj < ui)),
                    incl=same & ((uj >= ui) if rev else (uj <= ui)))

    dirs = [prologue(False, rf_ref, vf_ref, kkf_ref, lwf_ref, kdf_ref, bf_ref),
            prologue(True, rb_ref, vb_ref, kkb_ref, lwb_ref, kdb_ref, bb_ref)]

    units = [(d, p, slice(p * LANES, (p + 1) * LANES)) for d in range(2) for p in range(N_PAIR)]
    pairs = range(len(units))
    s_old = [s_scr[d, p] for d, p, _ in units]
    m1 = [_bdot_nt(jnp.concatenate([stack(dirs[d]["at"][:, sl]), stack(dirs[d]["rt"][:, sl])], axis=0),
                   jnp.concatenate([stack(dirs[d]["bt"][:, sl]), stack(dirs[d]["kt"][:, sl])], axis=0))
          for d, _, sl in units]
    m2 = [_bdot_nt(jnp.concatenate([dirs[d]["at"][:, sl], dirs[d]["rt"][:, sl]], axis=0), s_old[u])
          for u, (d, _, sl) in enumerate(units)]
    vs = [stack(dirs[d]["v"][:, sl]) for d, _, sl in units]
    strict = [dirs[d]["strict"] for d, _, _ in units]
    incl = [dirs[d]["incl"] for d, _, _ in units]
    x = [stack(m2[p][0:ch]) + _bdot(jnp.where(strict[p], m1[p][0:2 * ch, 2 * ch:4 * ch], 0.0), vs[p]) for p in pairs]
    nm = [jnp.where(strict[p], m1[p][0:2 * ch, 0:2 * ch], 0.0).astype(BF16) for p in pairs]
    for _ in range(5):
        sq_ap = [_bdot(nm[p], jnp.concatenate([nm[p], x[p].astype(BF16)], axis=1)) for p in pairs]
        nm = [sq_ap[p][:, 0:2 * ch].astype(BF16) for p in pairs]
        x = [x[p] + sq_ap[p][:, 2 * ch:4 * ch] for p in pairs]
    x = [x[p] + _bdot(nm[p], x[p]) for p in pairs]
    uv = [jnp.concatenate([x[p], vs[p]], axis=0).astype(BF16) for p in pairs]
    ys = [_bdot(jnp.concatenate([jnp.where(incl[p], m1[p][2 * ch:4 * ch, 0:2 * ch], 0.0),
                                 jnp.where(incl[p], m1[p][2 * ch:4 * ch, 2 * ch:4 * ch], 0.0)], axis=1), uv[p])
          for p in pairs]
    s_new = [s_old[u] * dirs[d]["g_all"][:, sl]
             + _bdot_tn(uv[u], jnp.concatenate([stack(dirs[d]["bh"][:, sl]), stack(dirs[d]["kh"][:, sl])], axis=0))
             for u, (d, _, sl) in enumerate(units)]
    for u, (d, _, sl) in enumerate(units):
        (yf_ref, yb_ref)[d][:, sl] = (ys[u][0:ch] + ys[u][ch:2 * ch] + m2[u][ch:2 * ch]).astype(BF16)
    for u, (d, p, _) in enumerate(units):
        s_scr[d, p] = s_new[u]

    @pl.when(is_ctx & (c == nc - 1))
    def _():
        sf_ref[...] = s_scr[...]


def _rwkv_scan(r, v, kk, lw, kd, b, s0, items):
    n = r.shape[0]
    ch = RWKV_CHUNK
    tok = lambda d: pl.BlockSpec((ch, D), lambda j: (items.block(d, j), 0))
    tokd = lambda d: pl.BlockSpec((None, ch, D), lambda j: (d, items.block(d, j), 0))
    st = lambda seq: pl.BlockSpec((2, None, N_PAIR, LANES, LANES), lambda j: (0, seq(j), 0, 0, 0))
    per_dir = lambda d: [tok(d), tok(d), tok(d), tokd(d), tokd(d), tokd(d)]
    return pl.pallas_call(
        functools.partial(_rwkv_scan_kernel, items=items),
        grid=(items.n_items,),
        in_specs=per_dir(0) + per_dir(1) + [st(items.lat_seq)],
        out_specs=[tok(0), tok(1), st(items.ctx_seq)],
        out_shape=[jax.ShapeDtypeStruct((n, D), BF16), jax.ShapeDtypeStruct((n, D), BF16),
                   jax.ShapeDtypeStruct((2, items.n_ctx, N_PAIR, LANES, LANES), F32)],
        scratch_shapes=[pltpu.VMEM((2, N_PAIR, LANES, LANES), F32)],
        compiler_params=_cparams(1),
        name="rwkv_scan",
    )(r, v, kk, lw, kd, b, r, v, kk, lw, kd, b, s0)


def _rwkv_post_kernel(xc_ref, xl_ref, yf_ref, yb_ref, gate_ref, bonus_ref, mod_ref, lnw_ref, lnb_ref, wo_ref, hsum_ref,
                      hexp_ref, o_ref, *, p_tiles):
    x = jnp.where(pl.program_id(0) < p_tiles, xc_ref[...], xl_ref[...])
    hsum, hexp = hsum_ref[...], hexp_ref[...]
    head_mean = lambda t: _dot_f32_rhs01(_dot_f32_rhs01(t, hsum), hexp) * (1.0 / HEAD_A)
    y = yf_ref[...].astype(F32) + yb_ref[...].astype(F32)
    yc = y - head_mean(y)
    yn = yc * lax.rsqrt(head_mean(yc * yc) + LNX_EPS)
    z = (yn * lnw_ref[...] + lnb_ref[...] + bonus_ref[...].astype(F32)) * gate_ref[...].astype(F32)
    o_ref[...] = x + mod_ref[:, 2 * D:3 * D] * _bdot(z, wo_ref[...])


def _rwkv_post(x_ctx, x_lat, y_fwd, y_bwd, gate, bonus, mod, p_tiles, tiles_per_seq, ln_w, ln_b, wo, hsum, hexp):
    n = x_ctx.shape[0] + x_lat.shape[0]
    full = lambda a: pl.BlockSpec(a.shape, lambda i: (0,) * a.ndim)
    tok = pl.BlockSpec((TM, D), lambda i: (i, 0))
    consts = [ln_w.reshape(1, D), ln_b.reshape(1, D), wo, hsum, hexp]
    return pl.pallas_call(
        functools.partial(_rwkv_post_kernel, p_tiles=p_tiles),
        grid=(n // TM,),
        in_specs=[pl.BlockSpec((TM, D), lambda i: (jnp.minimum(i, p_tiles - 1), 0)),
                  pl.BlockSpec((TM, D), lambda i: (jnp.maximum(i - p_tiles, 0), 0)),
                  tok, tok, tok, tok,
                  pl.BlockSpec((None, 1, N_MOD * D), lambda i: (_mod_row(i, p_tiles, tiles_per_seq), 0, 0))]
        + [full(a) for a in consts],
        out_specs=tok,
        out_shape=jax.ShapeDtypeStruct((n, D), F32),
        compiler_params=_cparams(1),
        name="rwkv_post",
    )(x_ctx, x_lat, y_fwd, y_bwd, gate, bonus, mod, *consts)


def _ret_pre_kernel(x_ref, mod_ref, g_ref, cos_ref, sin_ref, win_ref, q_o, k_o, v_o, gate_o):
    mod = mod_ref[...]
    h = _norm_mod(x_ref[...], g_ref[...], mod[:, 0:D], mod[:, D:2 * D])
    proj = _bdot(h, win_ref[...])
    cos, sin = cos_ref[...], sin_ref[...]

    def rope(t):
        outs = []
        for j in range(D // LANES):
            tj = t[:, j * LANES:(j + 1) * LANES]
            cj = cos[:, (j % 2) * LANES:(j % 2 + 1) * LANES]
            sj = sin[:, (j % 2) * LANES:(j % 2 + 1) * LANES]
            outs.append(tj * cj + pltpu.roll(tj, LANES // 2, 1) * sj)
        return jnp.concatenate(outs, axis=1)

    q_o[...] = rope(proj[:, 0:D]).astype(BF16)
    k_o[...] = rope(proj[:, D:2 * D] * (DK_B ** -0.5)).astype(BF16)
    v_o[...] = proj[:, 2 * D:4 * D].astype(BF16)
    gate_o[...] = proj[:, 4 * D:6 * D].astype(BF16)


def _ret_pre(x, mod, p_tiles, tiles_per_seq, norm_g, cos_t, sin_t, w_in):
    n = x.shape[0]
    full = lambda a: pl.BlockSpec(a.shape, lambda i: (0,) * a.ndim)
    tok = lambda w: pl.BlockSpec((TM, w), lambda i: (i, 0))
    tab = pl.BlockSpec((TM, DK_B), lambda i: (jnp.where(i < p_tiles, 0, 1 + (i - p_tiles) % tiles_per_seq), 0))
    return pl.pallas_call(
        _ret_pre_kernel,
        grid=(n // TM,),
        in_specs=[tok(D), pl.BlockSpec((None, 1, N_MOD * D), lambda i: (_mod_row(i, p_tiles, tiles_per_seq), 0, 0)),
                  full(norm_g.reshape(1, D)), tab, tab, full(w_in)],
        out_specs=[tok(D), tok(D), tok(2 * D), tok(2 * D)],
        out_shape=[jax.ShapeDtypeStruct((n, D), BF16), jax.ShapeDtypeStruct((n, D), BF16),
                   jax.ShapeDtypeStruct((n, 2 * D), BF16), jax.ShapeDtypeStruct((n, 2 * D), BF16)],
        compiler_params=_cparams(1),
        name="ret_pre",
    )(x, mod, norm_g.reshape(1, D), cos_t, sin_t, w_in)


def _ret_scan_kernel(lg_ref, qf_ref, kf_ref, vf_ref, qb_ref, kb_ref, vb_ref, s0_ref, of_ref, ob_ref, sf_ref, s_scr, *,
                     items):
    is_ctx, c, nc = items.decode(pl.program_id(0))
    ch = RET_CHUNK

    @pl.when(c == 0)
    def _():
        s_scr[...] = jnp.where(is_ctx, 0.0, s0_ref[...])

    ti = lax.broadcasted_iota(I32, (ch, ch), 0)
    tj = lax.broadcasted_iota(I32, (ch, ch), 1)
    io = ((qf_ref, kf_ref, vf_ref, of_ref), (qb_ref, kb_ref, vb_ref, ob_ref))
    units = [(d, hd) for d in range(2) for hd in range(H_B)]

    def decays(d, hd):
        rev = d == 1
        lg = lg_ref[d, hd]
        rel = ((tj - ti) if rev else (ti - tj)).astype(F32)
        steps_q = ((ch - ti) if rev else (ti + 1)).astype(F32)
        steps_k = (ti if rev else (ch - 1 - ti)).astype(F32)
        return dict(mask=jnp.where(rel >= 0, jnp.exp(jnp.maximum(rel, 0.0) * lg), 0.0),
                    q_dec=jnp.exp(steps_q * lg), k_dec=jnp.exp(steps_k * lg),
                    chunk_dec=jnp.exp(jnp.full((1, DV_B), float(ch), F32) * lg))

    dec = [decays(d, hd) for d, hd in units]
    qh = [io[d][0][:, hd * DK_B:(hd + 1) * DK_B] for d, hd in units]
    kh = [io[d][1][:, hd * DK_B:(hd + 1) * DK_B] for d, hd in units]
    vh = [io[d][2][:, hd * DV_B:(hd + 1) * DV_B] for d, hd in units]
    s_old = [s_scr[d, hd] for d, hd in units]
    rng = range(len(units))
    scores = [_bdot_nt(qh[u], kh[u]) * dec[u]["mask"] for u in rng]
    cross = [_bdot(qh[u], s_old[u]) * jnp.concatenate([dec[u]["q_dec"]] * (DV_B // ch), axis=1) for u in rng]
    out = [_bdot(scores[u], vh[u]) + cross[u] for u in rng]
    s_new = [s_old[u] * dec[u]["chunk_dec"]
             + _bdot_tn(kh[u].astype(F32) * jnp.concatenate([dec[u]["k_dec"]] * (DK_B // ch), axis=1), vh[u])
             for u in rng]
    for u, (d, hd) in enumerate(units):
        io[d][3][:, hd * DV_B:(hd + 1) * DV_B] = out[u].astype(BF16)
    for u, (d, hd) in enumerate(units):
        s_scr[d, hd] = s_new[u]

    @pl.when(is_ctx & (c == nc - 1))
    def _():
        sf_ref[...] = s_scr[...]


def _ret_scan(log_gamma, q, k, v, s0, items):
    n = q.shape[0]
    ch = RET_CHUNK
    tok = lambda d, w: pl.BlockSpec((ch, w), lambda j: (items.block(d, j), 0))
    st = lambda seq: pl.BlockSpec((2, None, H_B, DK_B, DV_B), lambda j: (0, seq(j), 0, 0, 0))
    per_dir = lambda d: [tok(d, D), tok(d, D), tok(d, 2 * D)]
    return pl.pallas_call(
        functools.partial(_ret_scan_kernel, items=items),
        grid=(items.n_items,),
        in_specs=[pl.BlockSpec(memory_space=pltpu.SMEM)] + per_dir(0) + per_dir(1) + [st(items.lat_seq)],
        out_specs=[tok(0, 2 * D), tok(1, 2 * D), st(items.ctx_seq)],
        out_shape=[jax.ShapeDtypeStruct((n, 2 * D), BF16), jax.ShapeDtypeStruct((n, 2 * D), BF16),
                   jax.ShapeDtypeStruct((2, items.n_ctx, H_B, DK_B, DV_B), F32)],
        scratch_shapes=[pltpu.VMEM((2, H_B, DK_B, DV_B), F32)],
        compiler_params=_cparams(1),
        name="ret_scan",
    )(log_gamma, q, k, v, q, k, v, s0)


def _ret_post_kernel(x_ref, of_ref, ob_ref, gate_ref, mod_ref, wout_ref, out_ref):
    o = of_ref[...].astype(F32) + ob_ref[...].astype(F32)
    parts = []
    for hd in range(H_B):
        oh = o[:, hd * DV_B:(hd + 1) * DV_B]
        parts.append(oh * lax.rsqrt(jnp.mean(oh * oh, axis=-1, keepdims=True) + NORM_EPS))
    y = _silu(gate_ref[...].astype(F32)) * jnp.concatenate(parts, axis=1)
    out_ref[...] = x_ref[...] + mod_ref[:, 2 * D:3 * D] * _bdot(y, wout_ref[...])


def _ret_post(x, o_fwd, o_bwd, gate, mod, p_tiles, tiles_per_seq, w_out):
    n = x.shape[0]
    tok = lambda w: pl.BlockSpec((TM, w), lambda i: (i, 0))
    return pl.pallas_call(
        _ret_post_kernel,
        grid=(n // TM,),
        in_specs=[tok(D), tok(2 * D), tok(2 * D), tok(2 * D),
                  pl.BlockSpec((None, 1, N_MOD * D), lambda i: (_mod_row(i, p_tiles, tiles_per_seq), 0, 0)),
                  pl.BlockSpec(w_out.shape, lambda i: (0, 0))],
        out_specs=tok(D),
        out_shape=jax.ShapeDtypeStruct((n, D), F32),
        compiler_params=_cparams(1),
        name="ret_post",
    )(x, o_fwd, o_bwd, gate, mod, w_out)


def _moe_route_kernel(x_ref, mod_ref, g_ref, router_ref, bias_ref, h_o, e_o, w_o, p_o, cnt_o, carry):
    i = pl.program_id(0)

    @pl.when(i == 0)
    def _():
        carry[...] = jnp.zeros_like(carry)

    mod = mod_ref[...]
    h = _norm_mod(x_ref[...], g_ref[...], mod[:, 3 * D:4 * D], mod[:, 4 * D:5 * D])
    _rows_store(h_o, h)
    lane =lax.broadcasted_iota(I32, (TM, LANES), 1)
    valid = lane < N_EXPERTS
    neg = -jnp.inf
    scores = jax.nn.sigmoid(_dot_f32(h, router_ref[...]))
    biased = jnp.where(valid, scores + bias_ref[...], neg)

    def group_reduce(t, op):
        s = 1
        while s < PER_GROUP:
            partner = jnp.where((lane & s) == 0, pltpu.roll(t, LANES - s, 1), pltpu.roll(t, s, 1))
            t = op(t, partner)
            s *= 2
        return t

    lane_f = lane.astype(F32)
    group_f = jnp.floor(lane_f * (1.0 / PER_GROUP))

    def first_lane_of_max(t):
        m = jnp.max(t, axis=-1, keepdims=True)
        return jnp.min(jnp.where(t == m, lane_f, float(LANES)), axis=-1, keepdims=True)

    m1 = group_reduce(biased, jnp.maximum)
    first1 = group_reduce(jnp.where(biased == m1, lane_f, float(LANES)), jnp.minimum)
    m2 = group_reduce(jnp.where(lane_f == first1, neg, biased), jnp.maximum)
    gscore = jnp.where(valid, m1 + m2, neg)
    cand = jnp.full((TM, LANES), neg, F32)
    for _ in range(TOPK_GROUPS):
        gsel = group_f == jnp.floor(first_lane_of_max(gscore) * (1.0 / PER_GROUP))
        cand = jnp.where(gsel, biased, cand)
        gscore = jnp.where(gsel, neg, gscore)
    hits = []
    sel01 = jnp.zeros((TM, LANES), F32)
    e_cols = jnp.zeros((TM, LANES), F32)
    for j in range(TOP_K):
        fl = first_lane_of_max(cand)
        hit = lane_f == fl
        hits.append(hit)
        sel01 = jnp.where(hit, 1.0, sel01)
        cand = jnp.where(hit, neg, cand)
        e_cols = jnp.where(lane == j, fl, e_cols)
    wsum = jnp.sum(sel01 * scores, axis=-1, keepdims=True)

    ri = lax.broadcasted_iota(I32, (TM, TM), 0)
    rj = lax.broadcasted_iota(I32, (TM, TM), 1)
    below = jnp.where(rj < ri, 1.0, 0.0).astype(BF16)
    rank = jnp.dot(below, sel01.astype(BF16), preferred_element_type=F32) + carry[...]
    carry[...] = carry[...] + jnp.sum(sel01, axis=0, keepdims=True)
    w_cols = jnp.zeros((TM, LANES), F32)
    p_cols = jnp.zeros((TM, LANES), F32)
    for j in range(TOP_K):
        wj = jnp.sum(jnp.where(hits[j], scores, 0.0), axis=-1, keepdims=True)
        w_cols = jnp.where(lane == j, wj / wsum * ROUTED_SCALE, w_cols)
        p_cols = jnp.where(lane == j, jnp.sum(jnp.where(hits[j], rank, 0.0), axis=-1, keepdims=True), p_cols)
    e_o[...] = e_cols.astype(I32)
    w_o[...] = w_cols
    p_o[...] = p_cols.astype(I32)
    cnt_o[...] = jnp.broadcast_to(carry[...], cnt_o.shape)


def _moe_route(x, mod, p_tiles, tiles_per_seq, norm_g, router, bias):
    n = x.shape[0]
    full = lambda a: pl.BlockSpec(a.shape, lambda i: (0,) * a.ndim)
    tok = lambda w: pl.BlockSpec((TM, w), lambda i: (i, 0))
    router_p = jnp.pad(router, ((0, 0), (0, LANES - N_EXPERTS)))
    bias_p = jnp.pad(bias, (0, LANES - N_EXPERTS)).reshape(1, LANES)
    return pl.pallas_call(
        _moe_route_kernel,
        grid=(n // TM,),
        in_specs=[tok(D), pl.BlockSpec((None, 1, N_MOD * D), lambda i: (_mod_row(i, p_tiles, tiles_per_seq), 0, 0)),
                  full(norm_g.reshape(1, D)), full(router_p), full(bias_p)],
        out_specs=[pl.BlockSpec((TM * ROW_SUB, LANES), lambda i: (i, 0)), tok(LANES), tok(LANES), tok(LANES),
                   pl.BlockSpec((8, LANES), lambda i: (0, 0))],
        out_shape=[jax.ShapeDtypeStruct((n * ROW_SUB, LANES), F32), jax.ShapeDtypeStruct((n, LANES), I32),
                   jax.ShapeDtypeStruct((n, LANES), F32), jax.ShapeDtypeStruct((n, LANES), I32),
                   jax.ShapeDtypeStruct((8, LANES), F32)],
        scratch_shapes=[pltpu.VMEM((1, LANES), F32)],
        compiler_params=_cparams(1),
        name="moe_route",
    )(x, mod, norm_g.reshape(1, D), router_p, bias_p)


DISPATCH_TOK = 512


def _moe_dispatch_kernel(pad_lo_ref, pad_n_ref, dest_ref, h_ref, xs_hbm, zero_scr, sem):
    @pl.when(pl.program_id(0) == 0)
    def _():
        zero_scr[...] = jnp.zeros_like(zero_scr)

        def pad_copy(e, s):
            return pltpu.make_async_copy(zero_scr, _row_tile(xs_hbm, pad_lo_ref[e] + s), sem)

        def per_expert(e, carry):
            def issue(s, c):
                pad_copy(e, s).start()
                return c

            def drain(s, c):
                pad_copy(e, s).wait()
                return c

            lax.fori_loop(0, pad_n_ref[e], issue, 0)
            lax.fori_loop(0, pad_n_ref[e], drain, 0)
            return carry

        lax.fori_loop(0, N_EXPERTS, per_expert, 0)

    def copy(t, kq):
        return pltpu.make_async_copy(_row_tile(h_ref, t), _row_tile(xs_hbm, dest_ref[t * TOP_K + kq]), sem)

    def issue(t, carry):
        for kq in range(TOP_K):
            copy(t, kq).start(priority=kq % 2)
        return carry

    def drain(t, carry):
        for kq in range(TOP_K):
            copy(t, kq).wait()
        return carry

    lax.fori_loop(0, DISPATCH_TOK, issue, 0)
    lax.fori_loop(0, DISPATCH_TOK, drain, 0)


def _moe_dispatch(pad_lo, pad_n, dest_flat, h, n_slots):
    n = h.shape[0] // ROW_SUB
    grid_spec = pltpu.PrefetchScalarGridSpec(
        num_scalar_prefetch=2,
        grid=(n // DISPATCH_TOK,),
        in_specs=[pl.BlockSpec((DISPATCH_TOK * TOP_K,), lambda i, *_: (i,), memory_space=pltpu.SMEM),
                  pl.BlockSpec((DISPATCH_TOK * ROW_SUB, LANES), lambda i, *_: (i, 0))],
        out_specs=pl.BlockSpec(memory_space=pl.ANY),
        scratch_shapes=[pltpu.VMEM((ROW_SUB, LANES), F32), pltpu.SemaphoreType.DMA],
    )
    return pl.pallas_call(
        _moe_dispatch_kernel,
        grid_spec=grid_spec,
        out_shape=jax.ShapeDtypeStruct((n_slots * ROW_SUB, LANES), F32),
        compiler_params=_cparams(1, has_side_effects=True),
        name="moe_dispatch",
    )(pad_lo, pad_n, dest_flat, h)


def _moe_expert_kernel(be_ref, xs_ref, wgu_ref, wdn_ref, o_ref, wgu_bf, wdn_bf):
    i = pl.program_id(0)

    @pl.when((i == 0) | (be_ref[i] != be_ref[jnp.maximum(i - 1, 0)]))
    def _():
        wgu_bf[...] = wgu_ref[...].astype(BF16)
        wdn_bf[...] = wdn_ref[...].astype(BF16)

    gu = _bdot(_rows_load(xs_ref), wgu_bf[...])
    act = _silu(gu[:, 0:D_EXPERT]) * gu[:, D_EXPERT:2 * D_EXPERT]
    _rows_store(o_ref, _bdot(act, wdn_bf[...]))


def _moe_expert(block_e, xs, w_gu, w_down, layer):
    n_slots = xs.shape[0] // ROW_SUB
    n_blocks = n_slots // EXPERT_BLOCK
    rows = pl.BlockSpec((EXPERT_BLOCK * ROW_SUB, LANES), lambda i, be: (i, 0))
    grid_spec = pltpu.PrefetchScalarGridSpec(
        num_scalar_prefetch=1,
        grid=(n_blocks,),
        in_specs=[rows,
                  pl.BlockSpec((None, None, D, 2 * D_EXPERT), lambda i, be: (layer, be[i], 0, 0)),
                  pl.BlockSpec((None, None, D_EXPERT, D), lambda i, be: (layer, be[i], 0, 0))],
        out_specs=rows,
        scratch_shapes=[pltpu.VMEM((D, 2 * D_EXPERT), BF16), pltpu.VMEM((D_EXPERT, D), BF16)],
    )
    return pl.pallas_call(
        _moe_expert_kernel,
        grid_spec=grid_spec,
        out_shape=jax.ShapeDtypeStruct((n_slots * ROW_SUB, LANES), F32),
        compiler_params=_cparams(1),
        name="moe_expert",
    )(block_e, xs, w_gu, w_down)


COMBINE_TOK = 128


def _moe_combine_kernel(dest_ref, dest_next_ref, x_ref, h_ref, w_ref, mod_ref, shgu_ref, shdn_ref, gfin_ref, ys_hbm,
                        *refs, final_norm, ctx_tiles):
    o_refs, (buf, sems) = refs[:-2], refs[-2:]
    i = pl.program_id(0)
    slot = i % 2

    def copy(idx_ref, s, t, kq):
        return pltpu.make_async_copy(_row_tile(ys_hbm, idx_ref[t * TOP_K + kq]), _row_tile(buf.at[s, kq], t),
                                     sems.at[s])

    def fetch(idx_ref, s):
        def body(t, carry):
            for kq in range(TOP_K):
                copy(idx_ref, s, t, kq).start(priority=kq % 2)
            return carry
        lax.fori_loop(0, COMBINE_TOK, body, 0, unroll=2)

    @pl.when(i == 0)
    def _():
        fetch(dest_ref, slot)

    @pl.when(i + 1 < pl.num_programs(0))
    def _():
        fetch(dest_next_ref, 1 - slot)

    gu = _bdot(_rows_load(h_ref), shgu_ref[...])
    acc = _bdot(_silu(gu[:, 0:D_EXPERT]) * gu[:, D_EXPERT:2 * D_EXPERT], shdn_ref[...])

    def drain(t, carry):
        for kq in range(TOP_K):
            copy(dest_ref, slot, t, kq).wait()
        return carry

    lax.fori_loop(0, COMBINE_TOK, drain, 0)
    w = w_ref[...]
    for kq in range(TOP_K):
        acc = acc + _rows_load(buf.at[slot, kq]) * w[:, kq:kq + 1]
    out = x_ref[...] + mod_ref[:, 5 * D:6 * D] * acc
    if final_norm:
        out = out * lax.rsqrt(jnp.mean(out * out, axis=-1, keepdims=True) + NORM_EPS) * gfin_ref[...]

        @pl.when(i < ctx_tiles)
        def _():
            o_refs[0][...] = out

        @pl.when(i >= ctx_tiles)
        def _():
            o_refs[1][...] = out
    else:
        o_refs[0][...] = out


def _moe_combine(dest_flat, x, h, w_cols, mod, p_tiles, tiles_per_seq, sh_gu, sh_down, g_final, ys, final_norm):
    n = x.shape[0]
    ratio = TM // COMBINE_TOK
    tok = lambda w: pl.BlockSpec((COMBINE_TOK, w), lambda i: (i, 0))
    full = lambda a: pl.BlockSpec(a.shape, lambda i: (0,) * a.ndim)
    n_tiles = n // COMBINE_TOK
    ctx_tiles = p_tiles * ratio
    if final_norm:
        out_specs = [pl.BlockSpec((COMBINE_TOK, D), lambda i: (jnp.minimum(i, ctx_tiles - 1), 0)),
                     pl.BlockSpec((COMBINE_TOK, D), lambda i: (jnp.maximum(i - ctx_tiles, 0), 0))]
        out_shape = [jax.ShapeDtypeStruct((ctx_tiles * COMBINE_TOK, D), F32),
                     jax.ShapeDtypeStruct((n - ctx_tiles * COMBINE_TOK, D), F32)]
    else:
        out_specs, out_shape = tok(D), jax.ShapeDtypeStruct((n, D), F32)
    return pl.pallas_call(
        functools.partial(_moe_combine_kernel, final_norm=final_norm, ctx_tiles=ctx_tiles),
        grid=(n_tiles,),
        in_specs=[pl.BlockSpec((COMBINE_TOK * TOP_K,), lambda i: (i,), memory_space=pltpu.SMEM),
                  pl.BlockSpec((COMBINE_TOK * TOP_K,), lambda i: (jnp.minimum(i + 1, n_tiles - 1),),
                               memory_space=pltpu.SMEM),
                  tok(D), pl.BlockSpec((COMBINE_TOK * ROW_SUB, LANES), lambda i: (i, 0)), tok(LANES),
                  pl.BlockSpec((None, 1, N_MOD * D), lambda i: (_mod_row(i // ratio, p_tiles, tiles_per_seq), 0, 0)),
                  full(sh_gu), full(sh_down), full(g_final),
                  pl.BlockSpec(memory_space=pl.ANY)],
        out_specs=out_specs,
        out_shape=out_shape,
        scratch_shapes=[pltpu.VMEM((2, TOP_K, COMBINE_TOK * ROW_SUB, LANES), F32), pltpu.SemaphoreType.DMA((2,))],
        compiler_params=_cparams(1),
        name="moe_combine",
    )(dest_flat, dest_flat, x, h, w_cols, mod, sh_gu, sh_down, g_final, ys)


def _moe_layer(x, mod, p_tiles, tiles_per_seq, norm_g, router, bias, w_gu, w_down, layer, sh_gu, sh_down, g_final,
               final_norm):
    n = x.shape[0]
    h, e_cols, w_cols, p_cols, counts = _moe_route(x, mod, p_tiles, tiles_per_seq, norm_g, router, bias)
    counts = counts[0, :N_EXPERTS].astype(I32)
    padded = (counts + EXPERT_BLOCK - 1) // EXPERT_BLOCK * EXPERT_BLOCK
    pad_end = jnp.cumsum(padded)
    pad_start = pad_end - padded
    chosen = e_cols[:, :TOP_K, None] == jnp.arange(N_EXPERTS, dtype=I32)
    dest = (jnp.sum(jnp.where(chosen, pad_start, 0), axis=-1) + p_cols[:, :TOP_K]).reshape(-1)
    n_blocks = n * TOP_K // EXPERT_BLOCK + N_EXPERTS
    n_slots = n_blocks * EXPERT_BLOCK
    block_start = jnp.arange(n_blocks, dtype=I32) * EXPERT_BLOCK
    block_e = jnp.minimum(jnp.sum((pad_end[None, :] <= block_start[:, None]).astype(I32), axis=1), N_EXPERTS - 1)
    pad_n = (padded - counts).at[N_EXPERTS - 1].add(n_slots - pad_end[N_EXPERTS - 1])
    xs = _moe_dispatch(pad_start + counts, pad_n, dest, h, n_slots)
    ys = _moe_expert(block_e, xs, w_gu, w_down, layer)
    return _moe_combine(dest, x, h, w_cols, mod, p_tiles, tiles_per_seq, sh_gu, sh_down, g_final.reshape(1, D), ys,
                        final_norm)


def _pair_pack(s):
    lead = s.shape[:-3]
    s = s.reshape(lead + (N_PAIR, 2, HEAD_A, HEAD_A))
    z = jnp.zeros_like(s[..., 0, :, :])
    top = jnp.concatenate([s[..., 0, :, :], z], axis=-1)
    bot = jnp.concatenate([z, s[..., 1, :, :]], axis=-1)
    return jnp.concatenate([top, bot], axis=-2)


def _pair_unpack(s):
    a = s[..., :HEAD_A, :HEAD_A]
    b = s[..., HEAD_A:, HEAD_A:]
    return jnp.stack([a, b], axis=-3).reshape(s.shape[:-3] + (H_A, HEAD_A, HEAD_A))


def _rope_tables(rows):
    half = DK_B // 2
    n_freq = half // 2
    inv = ROPE_BASE ** (-jnp.arange(n_freq, dtype=F32) / n_freq)
    pos_r = jnp.repeat(jnp.arange(rows, dtype=F32), GRID_W)
    pos_c = jnp.tile(jnp.arange(GRID_W, dtype=F32), rows)

    def tab(pos):
        ang = pos[:, None] * inv[None, :]
        c, s = jnp.cos(ang), jnp.sin(ang)
        return jnp.concatenate([c, c], -1), jnp.concatenate([-s, s], -1)

    cr, sr = tab(pos_r)
    cc, sc = tab(pos_c)
    cos = jnp.concatenate([cr, cc], -1)
    sin = jnp.concatenate([sr, sc], -1)
    cos = jnp.concatenate([jnp.ones((TM, DK_B), F32), cos], 0)
    sin = jnp.concatenate([jnp.zeros((TM, DK_B), F32), sin], 0)
    return cos, sin


def kernel(x_prompt, x_sample, state_rwkv, state_ret, c, c_ctx, ada_w, ada_b, norm_mix, norm_ffn, norm_final, rwkv_mu, rwkv_w0, rwkv_w1, rwkv_w2, rwkv_a0, rwkv_a1, rwkv_a2, rwkv_wrkv, rwkv_wo, rwkv_g1, rwkv_g2, rwkv_k_k, rwkv_k_a, rwkv_r_k, rwkv_ln_w, rwkv_ln_b, ret_w_in, ret_w_out, ret_decay_logit, moe_router, moe_bias, moe_w_gu, moe_w_down, moe_sh_gu, moe_sh_down):
    bp, tp, _ = x_prompt.shape
    bs, ts, _ = x_sample.shape
    n_p, n_s = bp * tp, bs * ts
    assert tp == TM and ts % TM == 0 and TM % GRID_W == 0
    p_tiles = n_p // TM
    tiles_per_seq = ts // TM
    rows = ts // GRID_W
    x_ctx, x_lat = x_prompt.reshape(n_p, D), x_sample.reshape(n_s, D)

    n_cond = 16
    cond = jnp.zeros((n_cond, D), F32).at[0].set(c_ctx).at[1:1 + bs].set(c)
    mod = _modulation(cond, ada_w, ada_b).reshape(ada_w.shape[0], n_cond, 1, N_MOD * D)

    head_of = jnp.arange(D, dtype=I32) // HEAD_A
    hsum = (head_of[:, None] == jnp.arange(LANES, dtype=I32)[None, :]).astype(BF16)
    hexp = hsum.T

    gl = rwkv_g1.shape[-1]
    glp = -(-gl // LANES) * LANES
    zb = lambda a: jnp.zeros_like(a)
    wts = {
        "mu": jnp.pad(rwkv_mu[0], ((0, 8 - N_MOD), (0, 0))),
        "wrkv": rwkv_wrkv[0].astype(BF16),
        "g1": jnp.pad(rwkv_g1[0], ((0, 0), (0, glp - gl))).astype(BF16),
        "g2": jnp.pad(rwkv_g2[0], ((0, glp - gl), (0, 0))).astype(BF16),
        "w1": jnp.concatenate([rwkv_w1[0, 0], rwkv_w1[0, 1]], axis=1).astype(BF16),
        "w2": jnp.concatenate([jnp.concatenate([rwkv_w2[0, 0], zb(rwkv_w2[0, 1])], 1),
                               jnp.concatenate([zb(rwkv_w2[0, 0]), rwkv_w2[0, 1]], 1)], 0).astype(BF16),
        "w0": rwkv_w0[0].reshape(1, 2 * D),
        "a1": jnp.concatenate([rwkv_a1[0, 0], rwkv_a1[0, 1]], axis=1).astype(BF16),
        "a2": jnp.concatenate([jnp.concatenate([rwkv_a2[0, 0], zb(rwkv_a2[0, 1])], 1),
                               jnp.concatenate([zb(rwkv_a2[0, 0]), rwkv_a2[0, 1]], 1)], 0).astype(BF16),
        "a0": rwkv_a0[0].reshape(1, 2 * D),
        "k_k": rwkv_k_k[0], "k_a": rwkv_k_a[0], "r_k": rwkv_r_k[0].reshape(D),
        "hsum": hsum, "hexp": hexp,
    }
    r, v, kk, lw, kd, b, gate, bonus = _rwkv_pre(x_ctx, x_lat, mod[0], p_tiles, tiles_per_seq, norm_mix[0], wts)
    s0_lat = _pair_pack(jnp.moveaxis(state_rwkv[:, 0], 1, 0))
    y_fwd, y_bwd, s_fin = _rwkv_scan(r, v, kk, lw, kd, b, s0_lat,
                                     _ScanItems(bp, tp // RWKV_CHUNK, bs, ts // RWKV_CHUNK))
    new_state_rwkv = jnp.moveaxis(_pair_unpack(s_fin), 0, 1)[:, None]
    x = _rwkv_post(x_ctx, x_lat, y_fwd, y_bwd, gate, bonus, mod[0], p_tiles, tiles_per_seq, rwkv_ln_w[0], rwkv_ln_b[0],
                   rwkv_wo[0].astype(BF16), hsum, hexp)
    x = _moe_layer(x, mod[0], p_tiles, tiles_per_seq, norm_ffn[0], moe_router[0], moe_bias[0],
                   moe_w_gu, moe_w_down, 0, moe_sh_gu[0].astype(BF16),
                   moe_sh_down[0].astype(BF16), norm_final, False)

    cos_t, sin_t = _rope_tables(rows)
    q, k, vv, rgate = _ret_pre(x, mod[1], p_tiles, tiles_per_seq, norm_mix[1], cos_t, sin_t, ret_w_in[0].astype(BF16))
    log_gamma = jax.nn.log_sigmoid(ret_decay_logit[0].astype(F32))
    o_fwd, o_bwd, r_fin = _ret_scan(log_gamma, q, k, vv, jnp.moveaxis(state_ret[:, 0], 1, 0),
                                    _ScanItems(bp, tp // RET_CHUNK, bs, ts // RET_CHUNK))
    new_state_ret = jnp.moveaxis(r_fin, 0, 1)[:, None]
    x = _ret_post(x, o_fwd, o_bwd, rgate, mod[1], p_tiles, tiles_per_seq, ret_w_out[0].astype(BF16))
    y_ctx, y_lat = _moe_layer(x, mod[1], p_tiles, tiles_per_seq, norm_ffn[1], moe_router[1], moe_bias[1],
                              moe_w_gu, moe_w_down, 1, moe_sh_gu[1].astype(BF16),
                              moe_sh_down[1].astype(BF16), norm_final, True)

    return (y_ctx.reshape(bp, tp, D), y_lat.reshape(bs, ts, D), new_state_rwkv, new_state_ret)
```

```python
import functools

import jax
import jax.numpy as jnp
from jax import lax
from jax.experimental import pallas as pl
from jax.experimental.pallas import tpu as pltpu

F32, BF16, I32 = jnp.float32, jnp.bfloat16, jnp.int32

D = 1024
N_MOD = 6
NORM_EPS = 1e-6
GRID_W = 64
HEAD_A = 64
H_A = D // HEAD_A
LNX_EPS = 64e-5
RWKV_CHUNK = 64
N_PAIR = H_A // 2
H_B = 4
DK_B = D // H_B
DV_B = 2 * DK_B
RET_CHUNK = 128
ROPE_BASE = 10000.0
N_EXPERTS = 64
TOP_K = 8
N_GROUPS = 8
TOPK_GROUPS = 4
PER_GROUP = N_EXPERTS // N_GROUPS
D_EXPERT = 256
ROUTED_SCALE = 2.5
EXPERT_BLOCK = 512

TM = 256
LANES = 128
ROW_SUB = D // LANES
VMEM_LIMIT = 56 * 1024 * 1024


def _cparams(n_grid_axes, **kw):
    return pltpu.CompilerParams(dimension_semantics=("arbitrary",) * n_grid_axes, vmem_limit_bytes=VMEM_LIMIT, **kw)


def _bdot(a, b):
    return jnp.dot(a.astype(BF16), b.astype(BF16), preferred_element_type=F32)


def _bdot_nt(a, b):
    return lax.dot_general(a.astype(BF16), b.astype(BF16), (((1,), (1,)), ((), ())), preferred_element_type=F32)


def _bdot_tn(a, b):
    return lax.dot_general(a.astype(BF16), b.astype(BF16), (((0,), (0,)), ((), ())), preferred_element_type=F32)


def _split3(x):
    hi = x.astype(BF16)
    r1 = x - hi.astype(F32)
    mid = r1.astype(BF16)
    lo = (r1 - mid.astype(F32)).astype(BF16)
    return hi, mid, lo


def _dot_f32(a, b):
    ah, am, al = _split3(a)
    bh, bm, bl = _split3(b)
    d = lambda x, y: jnp.dot(x, y, preferred_element_type=F32)
    return d(ah, bh) + (d(ah, bm) + d(am, bh)) + (d(ah, bl) + d(al, bh) + d(am, bm))


def _dot_f32_rhs01(a, m01):
    ah, am, al = _split3(a)
    d = lambda x: jnp.dot(x, m01, preferred_element_type=F32)
    return d(ah) + d(am) + d(al)


def _dot_f32_lhs01(m01, b):
    bh, bm, bl = _split3(b)
    d = lambda x: jnp.dot(m01, x, preferred_element_type=F32)
    return d(bh) + d(bm) + d(bl)


def _norm_mod(x, g, shift, scale):
    ms = jnp.mean(x * x, axis=-1, keepdims=True)
    y = x * lax.rsqrt(ms + NORM_EPS) * g
    return y * (1.0 + scale) + shift


def _silu(x):
    return x * jax.nn.sigmoid(x)


def _rows_load(ref):
    m = ref.shape[0] // ROW_SUB
    return jnp.concatenate([ref[pl.ds(j, m, stride=ROW_SUB), :] for j in range(ROW_SUB)], axis=1)


def _rows_store(ref, val):
    m = val.shape[0]
    for j in range(ROW_SUB):
        ref[pl.ds(j, m, stride=ROW_SUB), :] = val[:, j * LANES:(j + 1) * LANES]


def _row_tile(ref, idx):
    return ref.at[pl.ds(pl.multiple_of(idx * ROW_SUB, ROW_SUB), ROW_SUB)]


def _mod_row(i, p_tiles, tiles_per_seq):
    return jnp.where(i < p_tiles, 0, 1 + (i - p_tiles) // tiles_per_seq)


def _mod_kernel(cond_ref, w_ref, b_ref, o_ref):
    o_ref[...] = _dot_f32(_silu(cond_ref[...]), w_ref[...]) + b_ref[...]


def _modulation(cond, ada_w, ada_b):
    depth, _, n6 = ada_w.shape
    tn = 1536
    return pl.pallas_call(
        _mod_kernel,
        grid=(depth, n6 // tn),
        in_specs=[
            pl.BlockSpec(cond.shape, lambda l, j: (0, 0)),
            pl.BlockSpec((None, D, tn), lambda l, j: (l, 0, j)),
            pl.BlockSpec((None, 1, tn), lambda l, j: (l, 0, j)),
        ],
        out_specs=pl.BlockSpec((None, cond.shape[0], tn), lambda l, j: (l, 0, j)),
        out_shape=jax.ShapeDtypeStruct((depth, cond.shape[0], n6), F32),
        compiler_params=_cparams(2),
        name="adaln_mod",
    )(cond, ada_w, ada_b.reshape(depth, 1, n6))


def _rwkv_pre_kernel(xc_ref, xl_ref, xu_ref, xd_ref, mod_ref, g_ref, mu_ref, wrkv_ref, g1_ref, g2_ref, w1_ref, w2_ref,
                     w0_ref, a1_ref, a2_ref, a0_ref, kk_ref, ka_ref, rk_ref, hsum_ref, hexp_ref,
                     r_o, v_o, kk_o, lw_o, kd_o, b_o, gate_o, bonus_o, *, p_tiles, tiles_per_seq):
    i = pl.program_id(0)
    is_p = i < p_tiles
    sub = (i - p_tiles) % tiles_per_seq
    mod = mod_ref[...]
    shift, scale = mod[:, 0:D], mod[:, D:2 * D]
    g = g_ref[...]
    h = _norm_mod(jnp.where(is_p, xc_ref[...], xl_ref[...]), g, shift, scale)
    hu = _norm_mod(xu_ref[...], g, shift, scale)
    hd = _norm_mod(xd_ref[...], g, shift, scale)

    q = D // 4
    row = lax.broadcasted_iota(I32, (TM, 1), 0)
    per = jnp.where(is_p, TM, GRID_W)
    pos = row & (per - 1)

    def prev1(a):
        return jnp.where(pos == 0, 0.0, pltpu.roll(a, 1, 0))

    def next1(a):
        return jnp.where(pos == per - 1, 0.0, pltpu.roll(a, TM - 1, 0))

    h0, h1, h2, h3 = (h[:, j * q:(j + 1) * q] for j in range(4))
    up = jnp.concatenate([jnp.where(sub == 0, 0.0, hu[:, 2 * q:3 * q]), h2[0:TM - GRID_W]], axis=0)
    down = jnp.concatenate([h3[GRID_W:TM], jnp.where(sub == tiles_per_seq - 1, 0.0, hd[:, 3 * q:4 * q])], axis=0)
    s0 = prev1(h0)
    s1 = jnp.where(is_p, prev1(h1), next1(h1))
    s2 = jnp.where(is_p, next1(h2), up)
    s3 = jnp.where(is_p, next1(h3), down)
    xx = jnp.concatenate([s0, s1, s2, s3], axis=1) - h

    mu = mu_ref[...]
    mix = lambda j: h + xx * mu[j:j + 1]
    r = _bdot(mix(0), wrkv_ref[0])
    k = _bdot(mix(2), wrkv_ref[1])
    v = _bdot(mix(3), wrkv_ref[2])
    gate = _bdot(jax.nn.sigmoid(_bdot(mix(5), g1_ref[...])), g2_ref[...])
    w_all = w0_ref[...] + _bdot(jnp.tanh(_bdot(mix(1), w1_ref[...])), w2_ref[...])
    a_all = jax.nn.sigmoid(a0_ref[...] + _bdot(_bdot(mix(4), a1_ref[...]), a2_ref[...]))

    hsum, hexp = hsum_ref[...], hexp_ref[...]
    head_sum = lambda t: _dot_f32_rhs01(_dot_f32_rhs01(t, hsum), hexp)

    kkr = k * kk_ref[...]
    kk = kkr / jnp.maximum(jnp.sqrt(head_sum(kkr * kkr)), 1e-12)
    ka = ka_ref[...]
    kd_sum = jnp.zeros_like(k)
    for d in range(2):
        wd = w_all[:, d * D:(d + 1) * D]
        z = -wd
        softplus = jnp.maximum(z, 0.0) + jnp.log(1.0 + jnp.exp(-jnp.abs(z)))
        lw_o[d] = -jnp.exp(-softplus - 0.5)
        a = a_all[:, d * D:(d + 1) * D]
        kd = k * (1.0 + (a - 1.0) * ka)
        kd_o[d] = kd
        b_o[d] = kk * a
        kd_sum = kd_sum + kd
    r_o[...] = r
    v_o[...] = v
    kk_o[...] = kk
    gate_o[...] = gate.astype(BF16)
    bonus_o[...] = (head_sum(r * kd_sum * rk_ref[...]) * v).astype(BF16)


def _rwkv_pre(x_ctx, x_lat, mod, p_tiles, tiles_per_seq, norm_g, wts):
    n = x_ctx.shape[0] + x_lat.shape[0]
    n_tiles = n // TM
    hb = TM // GRID_W
    n_hblk = x_lat.shape[0] // GRID_W
    lat = lambda i: jnp.maximum(i - p_tiles, 0)
    row = lambda a: a.reshape(1, -1)
    full = lambda a: pl.BlockSpec(a.shape, lambda i: (0,) * a.ndim)
    tok = pl.BlockSpec((TM, D), lambda i: (i, 0))
    tok2 = pl.BlockSpec((2, TM, D), lambda i: (0, i, 0))
    consts = [row(norm_g), wts["mu"], wts["wrkv"], wts["g1"], wts["g2"], wts["w1"], wts["w2"], wts["w0"],
              wts["a1"], wts["a2"], wts["a0"], row(wts["k_k"]), row(wts["k_a"]), row(wts["r_k"]),
              wts["hsum"], wts["hexp"]]
    return pl.pallas_call(
        functools.partial(_rwkv_pre_kernel, p_tiles=p_tiles, tiles_per_seq=tiles_per_seq),
        grid=(n_tiles,),
        in_specs=[
            pl.BlockSpec((TM, D), lambda i: (jnp.minimum(i, p_tiles - 1), 0)),
            pl.BlockSpec((TM, D), lambda i: (lat(i), 0)),
            pl.BlockSpec((GRID_W, D), lambda i: (jnp.maximum(lat(i) * hb - 1, 0), 0)),
            pl.BlockSpec((GRID_W, D), lambda i: (jnp.minimum(lat(i) * hb + hb, n_hblk - 1), 0)),
            pl.BlockSpec((None, 1, N_MOD * D), lambda i: (_mod_row(i, p_tiles, tiles_per_seq), 0, 0)),
        ] + [full(a) for a in consts],
        out_specs=[tok, tok, tok, tok2, tok2, tok2, tok, tok],
        out_shape=[jax.ShapeDtypeStruct((n, D), F32)] * 3 + [jax.ShapeDtypeStruct((2, n, D), F32)] * 3
        + [jax.ShapeDtypeStruct((n, D), BF16)] * 2,
        compiler_params=_cparams(1),
        name="rwkv_pre",
    )(x_ctx, x_lat, x_lat, x_lat, mod, *consts)


class _ScanItems:
    def __init__(self, n_ctx, ctx_chunks, n_lat, lat_chunks):
        self.n_ctx, self.ctx_chunks, self.n_lat, self.lat_chunks = n_ctx, ctx_chunks, n_lat, lat_chunks
        self.ctx_items = n_ctx * ctx_chunks
        self.n_items = self.ctx_items + n_lat * lat_chunks

    def decode(self, j):
        is_ctx = j < self.ctx_items
        jl = jnp.maximum(j - self.ctx_items, 0)
        c = jnp.where(is_ctx, j % self.ctx_chunks, jl % self.lat_chunks)
        return is_ctx, c, jnp.where(is_ctx, self.ctx_chunks, self.lat_chunks)

    def block(self, d, j):
        _, c, nc = self.decode(j)
        return jnp.where(d == 0, j, j - c + (nc - 1 - c))

    def ctx_seq(self, j):
        return jnp.minimum(j // self.ctx_chunks, self.n_ctx - 1)

    def lat_seq(self, j):
        return jnp.maximum(j - self.ctx_items, 0) // self.lat_chunks


def _rwkv_scan_kernel(rf_ref, vf_ref, kkf_ref, lwf_ref, kdf_ref, bf_ref, rb_ref, vb_ref, kkb_ref, lwb_ref, kdb_ref,
                      bb_ref, s0_ref, yf_ref, yb_ref, sf_ref, s_scr, *, items):
    is_ctx, c, nc = items.decode(pl.program_id(0))
    ch = RWKV_CHUNK

    @pl.when(c == 0)
    def _():
        s_scr[...] = jnp.where(is_ctx, 0.0, s0_ref[...])

    ti = lax.broadcasted_iota(I32, (ch, ch), 0)
    tj = lax.broadcasted_iota(I32, (ch, ch), 1)
    si = lax.broadcasted_iota(I32, (2 * ch, 2 * ch), 0)
    sj = lax.broadcasted_iota(I32, (2 * ch, 2 * ch), 1)
    same = (si < ch) == (sj < ch)
    ui, uj = si & (ch - 1), sj & (ch - 1)
    first = lax.broadcasted_iota(I32, (1, LANES), 1) < HEAD_A

    def stack(xp):
        return jnp.concatenate([jnp.where(first, xp, 0.0), jnp.where(first, 0.0, xp)], axis=0)

    def prologue(rev, r_ref, v_ref, kk_ref, lw_ref, kd_ref, b_ref):
        tri = jnp.where((tj >= ti) if rev else (tj <= ti), 1.0, 0.0).astype(BF16)
        lw = lw_ref[...]
        cum = _dot_f32_lhs01(tri, lw)
        last = cum[0:1] if rev else cum[ch - 1:ch]
        kk, kd, b = kk_ref[...], kd_ref[...], b_ref[...]
        e_neg = jnp.exp(-cum)
        e_rel = jnp.exp(last - cum)
        return dict(at=-kk * jnp.exp(cum - lw), rt=r_ref[...] * jnp.exp(cum), bt=b * e_neg, kt=kd * e_neg,
                    bh=b * e_rel, kh=kd * e_rel, g_all=jnp.exp(last), v=v_ref[...],
                    strict=same & ((uj > ui) if rev else (uj < ui)),
                    incl=same & ((uj >= ui) if rev else (uj <= ui)))

    dirs = [prologue(False, rf_ref, vf_ref, kkf_ref, lwf_ref, kdf_ref, bf_ref),
            prologue(True, rb_ref, vb_ref, kkb_ref, lwb_ref, kdb_ref, bb_ref)]

    units = [(d, p, slice(p * LANES, (p + 1) * LANES)) for d in range(2) for p in range(N_PAIR)]
    pairs = range(len(units))
    s_old = [s_scr[d, p] for d, p, _ in units]
    m1 = [_bdot_nt(jnp.concatenate([stack(dirs[d]["at"][:, sl]), stack(dirs[d]["rt"][:, sl])], axis=0),
                   jnp.concatenate([stack(dirs[d]["bt"][:, sl]), stack(dirs[d]["kt"][:, sl])], axis=0))
          for d, _, sl in units]
    m2 = [_bdot_nt(jnp.concatenate([dirs[d]["at"][:, sl], dirs[d]["rt"][:, sl]], axis=0), s_old[u])
          for u, (d, _, sl) in enumerate(units)]
    vs = [stack(dirs[d]["v"][:, sl]) for d, _, sl in units]
    strict = [dirs[d]["strict"] for d, _, _ in units]
    incl = [dirs[d]["incl"] for d, _, _ in units]
    x = [stack(m2[p][0:ch]) + _bdot(jnp.where(strict[p], m1[p][0:2 * ch, 2 * ch:4 * ch], 0.0), vs[p]) for p in pairs]
    nm = [jnp.where(strict[p], m1[p][0:2 * ch, 0:2 * ch], 0.0).astype(BF16) for p in pairs]
    for _ in range(5):
        sq_ap = [_bdot(nm[p], jnp.concatenate([nm[p], x[p].astype(BF16)], axis=1)) for p in pairs]
        nm = [sq_ap[p][:, 0:2 * ch].astype(BF16) for p in pairs]
        x = [x[p] + sq_ap[p][:, 2 * ch:4 * ch] for p in pairs]
    x = [x[p] + _bdot(nm[p], x[p]) for p in pairs]
    uv = [jnp.concatenate([x[p], vs[p]], axis=0).astype(BF16) for p in pairs]
    ys = [_bdot(jnp.concatenate([jnp.where(incl[p], m1[p][2 * ch:4 * ch, 0:2 * ch], 0.0),
                                 jnp.where(incl[p], m1[p][2 * ch:4 * ch, 2 * ch:4 * ch], 0.0)], axis=1), uv[p])
          for p in pairs]
    s_new = [s_old[u] * dirs[d]["g_all"][:, sl]
             + _bdot_tn(uv[u], jnp.concatenate([stack(dirs[d]["bh"][:, sl]), stack(dirs[d]["kh"][:, sl])], axis=0))
             for u, (d, _, sl) in enumerate(units)]
    for u, (d, _, sl) in enumerate(units):
        (yf_ref, yb_ref)[d][:, sl] = (ys[u][0:ch] + ys[u][ch:2 * ch] + m2[u][ch:2 * ch]).astype(BF16)
    for u, (d, p, _) in enumerate(units):
        s_scr[d, p] = s_new[u]

    @pl.when(is_ctx & (c == nc - 1))
    def _():
        sf_ref[...] = s_scr[...]


def _rwkv_scan(r, v, kk, lw, kd, b, s0, items):
    n = r.shape[0]
    ch = RWKV_CHUNK
    tok = lambda d: pl.BlockSpec((ch, D), lambda j: (items.block(d, j), 0))
    tokd = lambda d: pl.BlockSpec((None, ch, D), lambda j: (d, items.block(d, j), 0))
    st = lambda seq: pl.BlockSpec((2, None, N_PAIR, LANES, LANES), lambda j: (0, seq(j), 0, 0, 0))
    per_dir = lambda d: [tok(d), tok(d), tok(d), tokd(d), tokd(d), tokd(d)]
    return pl.pallas_call(
        functools.partial(_rwkv_scan_kernel, items=items),
        grid=(items.n_items,),
        in_specs=per_dir(0) + per_dir(1) + [st(items.lat_seq)],
        out_specs=[tok(0), tok(1), st(items.ctx_seq)],
        out_shape=[jax.ShapeDtypeStruct((n, D), BF16), jax.ShapeDtypeStruct((n, D), BF16),
                   jax.ShapeDtypeStruct((2, items.n_ctx, N_PAIR, LANES, LANES), F32)],
        scratch_shapes=[pltpu.VMEM((2, N_PAIR, LANES, LANES), F32)],
        compiler_params=_cparams(1),
        name="rwkv_scan",
    )(r, v, kk, lw, kd, b, r, v, kk, lw, kd, b, s0)


def _rwkv_post_kernel(xc_ref, xl_ref, yf_ref, yb_ref, gate_ref, bonus_ref, mod_ref, lnw_ref, lnb_ref, wo_ref, hsum_ref,
                      hexp_ref, o_ref, *, p_tiles):
    x = jnp.where(pl.program_id(0) < p_tiles, xc_ref[...], xl_ref[...])
    hsum, hexp = hsum_ref[...], hexp_ref[...]
    head_mean = lambda t: _dot_f32_rhs01(_dot_f32_rhs01(t, hsum), hexp) * (1.0 / HEAD_A)
    y = yf_ref[...].astype(F32) + yb_ref[...].astype(F32)
    yc = y - head_mean(y)
    yn = yc * lax.rsqrt(head_mean(yc * yc) + LNX_EPS)
    z = (yn * lnw_ref[...] + lnb_ref[...] + bonus_ref[...].astype(F32)) * gate_ref[...].astype(F32)
    o_ref[...] = x + mod_ref[:, 2 * D:3 * D] * _bdot(z, wo_ref[...])


def _rwkv_post(x_ctx, x_lat, y_fwd, y_bwd, gate, bonus, mod, p_tiles, tiles_per_seq, ln_w, ln_b, wo, hsum, hexp):
    n = x_ctx.shape[0] + x_lat.shape[0]
    full = lambda a: pl.BlockSpec(a.shape, lambda i: (0,) * a.ndim)
    tok = pl.BlockSpec((TM, D), lambda i: (i, 0))
    consts = [ln_w.reshape(1, D), ln_b.reshape(1, D), wo, hsum, hexp]
    return pl.pallas_call(
        functools.partial(_rwkv_post_kernel, p_tiles=p_tiles),
        grid=(n // TM,),
        in_specs=[pl.BlockSpec((TM, D), lambda i: (jnp.minimum(i, p_tiles - 1), 0)),
                  pl.BlockSpec((TM, D), lambda i: (jnp.maximum(i - p_tiles, 0), 0)),
                  tok, tok, tok, tok,
                  pl.BlockSpec((None, 1, N_MOD * D), lambda i: (_mod_row(i, p_tiles, tiles_per_seq), 0, 0))]
        + [full(a) for a in consts],
        out_specs=tok,
        out_shape=jax.ShapeDtypeStruct((n, D), F32),
        compiler_params=_cparams(1),
        name="rwkv_post",
    )(x_ctx, x_lat, y_fwd, y_bwd, gate, bonus, mod, *consts)


def _ret_pre_kernel(x_ref, mod_ref, g_ref, cos_ref, sin_ref, win_ref, q_o, k_o, v_o, gate_o):
    mod = mod_ref[...]
    h = _norm_mod(x_ref[...], g_ref[...], mod[:, 0:D], mod[:, D:2 * D])
    proj = _bdot(h, win_ref[...])
    cos, sin = cos_ref[...], sin_ref[...]

    def rope(t):
        outs = []
        for j in range(D // LANES):
            tj = t[:, j * LANES:(j + 1) * LANES]
            cj = cos[:, (j % 2) * LANES:(j % 2 + 1) * LANES]
            sj = sin[:, (j % 2) * LANES:(j % 2 + 1) * LANES]
            outs.append(tj * cj + pltpu.roll(tj, LANES // 2, 1) * sj)
        return jnp.concatenate(outs, axis=1)

    q_o[...] = rope(proj[:, 0:D]).astype(BF16)
    k_o[...] = rope(proj[:, D:2 * D] * (DK_B ** -0.5)).astype(BF16)
    v_o[...] = proj[:, 2 * D:4 * D].astype(BF16)
    gate_o[...] = proj[:, 4 * D:6 * D].astype(BF16)


def _ret_pre(x, mod, p_tiles, tiles_per_seq, norm_g, cos_t, sin_t, w_in):
    n = x.shape[0]
    full = lambda a: pl.BlockSpec(a.shape, lambda i: (0,) * a.ndim)
    tok = lambda w: pl.BlockSpec((TM, w), lambda i: (i, 0))
    tab = pl.BlockSpec((TM, DK_B), lambda i: (jnp.where(i < p_tiles, 0, 1 + (i - p_tiles) % tiles_per_seq), 0))
    return pl.pallas_call(
        _ret_pre_kernel,
        grid=(n // TM,),
        in_specs=[tok(D), pl.BlockSpec((None, 1, N_MOD * D), lambda i: (_mod_row(i, p_tiles, tiles_per_seq), 0, 0)),
                  full(norm_g.reshape(1, D)), tab, tab, full(w_in)],
        out_specs=[tok(D), tok(D), tok(2 * D), tok(2 * D)],
        out_shape=[jax.ShapeDtypeStruct((n, D), BF16), jax.ShapeDtypeStruct((n, D), BF16),
                   jax.ShapeDtypeStruct((n, 2 * D), BF16), jax.ShapeDtypeStruct((n, 2 * D), BF16)],
        compiler_params=_cparams(1),
        name="ret_pre",
    )(x, mod, norm_g.reshape(1, D), cos_t, sin_t, w_in)


def _ret_scan_kernel(lg_ref, qf_ref, kf_ref, vf_ref, qb_ref, kb_ref, vb_ref, s0_ref, of_ref, ob_ref, sf_ref, s_scr, *,
                     items):
    is_ctx, c, nc = items.decode(pl.program_id(0))
    ch = RET_CHUNK

    @pl.when(c == 0)
    def _():
        s_scr[...] = jnp.where(is_ctx, 0.0, s0_ref[...])

    ti = lax.broadcasted_iota(I32, (ch, ch), 0)
    tj = lax.broadcasted_iota(I32, (ch, ch), 1)
    io = ((qf_ref, kf_ref, vf_ref, of_ref), (qb_ref, kb_ref, vb_ref, ob_ref))
    units = [(d, hd) for d in range(2) for hd in range(H_B)]

    def decays(d, hd):
        rev = d == 1
        lg = lg_ref[d, hd]
        rel = ((tj - ti) if rev else (ti - tj)).astype(F32)
        steps_q = ((ch - ti) if rev else (ti + 1)).astype(F32)
        steps_k = (ti if rev else (ch - 1 - ti)).astype(F32)
        return dict(mask=jnp.where(rel >= 0, jnp.exp(jnp.maximum(rel, 0.0) * lg), 0.0),
                    q_dec=jnp.exp(steps_q * lg), k_dec=jnp.exp(steps_k * lg),
                    chunk_dec=jnp.exp(jnp.full((1, DV_B), float(ch), F32) * lg))

    dec = [decays(d, hd) for d, hd in units]
    qh = [io[d][0][:, hd * DK_B:(hd + 1) * DK_B] for d, hd in units]
    kh = [io[d][1][:, hd * DK_B:(hd + 1) * DK_B] for d, hd in units]
    vh = [io[d][2][:, hd * DV_B:(hd + 1) * DV_B] for d, hd in units]
    s_old = [s_scr[d, hd] for d, hd in units]
    rng = range(len(units))
    scores = [_bdot_nt(qh[u], kh[u]) * dec[u]["mask"] for u in rng]
    cross = [_bdot(qh[u], s_old[u]) * jnp.concatenate([dec[u]["q_dec"]] * (DV_B // ch), axis=1) for u in rng]
    out = [_bdot(scores[u], vh[u]) + cross[u] for u in rng]
    s_new = [s_old[u] * dec[u]["chunk_dec"]
             + _bdot_tn(kh[u].astype(F32) * jnp.concatenate([dec[u]["k_dec"]] * (DK_B // ch), axis=1), vh[u])
             for u in rng]
    for u, (d, hd) in enumerate(units):
        io[d][3][:, hd * DV_B:(hd + 1) * DV_B] = out[u].astype(BF16)
    for u, (d, hd) in enumerate(units):
        s_scr[d, hd] = s_new[u]

    @pl.when(is_ctx & (c == nc - 1))
    def _():
        sf_ref[...] = s_scr[...]


def _ret_scan(log_gamma, q, k, v, s0, items):
    n = q.shape[0]
    ch = RET_CHUNK
    tok = lambda d, w: pl.BlockSpec((ch, w), lambda j: (items.block(d, j), 0))
    st = lambda seq: pl.BlockSpec((2, None, H_B, DK_B, DV_B), lambda j: (0, seq(j), 0, 0, 0))
    per_dir = lambda d: [tok(d, D), tok(d, D), tok(d, 2 * D)]
    return pl.pallas_call(
        functools.partial(_ret_scan_kernel, items=items),
        grid=(items.n_items,),
        in_specs=[pl.BlockSpec(memory_space=pltpu.SMEM)] + per_dir(0) + per_dir(1) + [st(items.lat_seq)],
        out_specs=[tok(0, 2 * D), tok(1, 2 * D), st(items.ctx_seq)],
        out_shape=[jax.ShapeDtypeStruct((n, 2 * D), BF16), jax.ShapeDtypeStruct((n, 2 * D), BF16),
                   jax.ShapeDtypeStruct((2, items.n_ctx, H_B, DK_B, DV_B), F32)],
        scratch_shapes=[pltpu.VMEM((2, H_B, DK_B, DV_B), F32)],
        compiler_params=_cparams(1),
        name="ret_scan",
    )(log_gamma, q, k, v, q, k, v, s0)


def _ret_post_kernel(x_ref, of_ref, ob_ref, gate_ref, mod_ref, wout_ref, out_ref):
    o = of_ref[...].astype(F32) + ob_ref[...].astype(F32)
    parts = []
    for hd in range(H_B):
        oh = o[:, hd * DV_B:(hd + 1) * DV_B]
        parts.append(oh * lax.rsqrt(jnp.mean(oh * oh, axis=-1, keepdims=True) + NORM_EPS))
    y = _silu(gate_ref[...].astype(F32)) * jnp.concatenate(parts, axis=1)
    out_ref[...] = x_ref[...] + mod_ref[:, 2 * D:3 * D] * _bdot(y, wout_ref[...])


def _ret_post(x, o_fwd, o_bwd, gate, mod, p_tiles, tiles_per_seq, w_out):
    n = x.shape[0]
    tok = lambda w: pl.BlockSpec((TM, w), lambda i: (i, 0))
    return pl.pallas_call(
        _ret_post_kernel,
        grid=(n // TM,),
        in_specs=[tok(D), tok(2 * D), tok(2 * D), tok(2 * D),
                  pl.BlockSpec((None, 1, N_MOD * D), lambda i: (_mod_row(i, p_tiles, tiles_per_seq), 0, 0)),
                  pl.BlockSpec(w_out.shape, lambda i: (0, 0))],
        out_specs=tok(D),
        out_shape=jax.ShapeDtypeStruct((n, D), F32),
        compiler_params=_cparams(1),
        name="ret_post",
    )(x, o_fwd, o_bwd, gate, mod, w_out)


def _moe_route_kernel(x_ref, mod_ref, g_ref, router_ref, bias_ref, h_o, e_o, w_o, p_o, cnt_o, carry):
    i = pl.program_id(0)

    @pl.when(i == 0)
    def _():
        carry[...] = jnp.zeros_like(carry)

    mod = mod_ref[...]
    h = _norm_mod(x_ref[...], g_ref[...], mod[:, 3 * D:4 * D], mod[:, 4 * D:5 * D])
    _rows_store(h_o, h)
    lane =lax.broadcasted_iota(I32, (TM, LANES), 1)
    valid = lane < N_EXPERTS
    neg = -jnp.inf
    scores = jax.nn.sigmoid(_dot_f32(h, router_ref[...]))
    biased = jnp.where(valid, scores + bias_ref[...], neg)

    def group_reduce(t, op):
        s = 1
        while s < PER_GROUP:
            partner = jnp.where((lane & s) == 0, pltpu.roll(t, LANES - s, 1), pltpu.roll(t, s, 1))
            t = op(t, partner)
            s *= 2
        return t

    lane_f = lane.astype(F32)
    group_f = jnp.floor(lane_f * (1.0 / PER_GROUP))

    def first_lane_of_max(t):
        m = jnp.max(t, axis=-1, keepdims=True)
        return jnp.min(jnp.where(t == m, lane_f, float(LANES)), axis=-1, keepdims=True)

    m1 = group_reduce(biased, jnp.maximum)
    first1 = group_reduce(jnp.where(biased == m1, lane_f, float(LANES)), jnp.minimum)
    m2 = group_reduce(jnp.where(lane_f == first1, neg, biased), jnp.maximum)
    gscore = jnp.where(valid, m1 + m2, neg)
    cand = jnp.full((TM, LANES), neg, F32)
    for _ in range(TOPK_GROUPS):
        gsel = group_f == jnp.floor(first_lane_of_max(gscore) * (1.0 / PER_GROUP))
        cand = jnp.where(gsel, biased, cand)
        gscore = jnp.where(gsel, neg, gscore)
    hits = []
    sel01 = jnp.zeros((TM, LANES), F32)
    e_cols = jnp.zeros((TM, LANES), F32)
    for j in range(TOP_K):
        fl = first_lane_of_max(cand)
        hit = lane_f == fl
        hits.append(hit)
        sel01 = jnp.where(hit, 1.0, sel01)
        cand = jnp.where(hit, neg, cand)
        e_cols = jnp.where(lane == j, fl, e_cols)
    wsum = jnp.sum(sel01 * scores, axis=-1, keepdims=True)

    ri = lax.broadcasted_iota(I32, (TM, TM), 0)
    rj = lax.broadcasted_iota(I32, (TM, TM), 1)
    below = jnp.where(rj < ri, 1.0, 0.0).astype(BF16)
    rank = jnp.dot(below, sel01.astype(BF16), preferred_element_type=F32) + carry[...]
    carry[...] = carry[...] + jnp.sum(sel01, axis=0, keepdims=True)
    w_cols = jnp.zeros((TM, LANES), F32)
    p_cols = jnp.zeros((TM, LANES), F32)
    for j in range(TOP_K):
        wj = jnp.sum(jnp.where(hits[j], scores, 0.0), axis=-1, keepdims=True)
        w_cols = jnp.where(lane == j, wj / wsum * ROUTED_SCALE, w_cols)
        p_cols = jnp.where(lane == j, jnp.sum(jnp.where(hits[j], rank, 0.0), axis=-1, keepdims=True), p_cols)
    e_o[...] = e_cols.astype(I32)
    w_o[...] = w_cols
    p_o[...] = p_cols.astype(I32)
    cnt_o[...] = jnp.broadcast_to(carry[...], cnt_o.shape)


def _moe_route(x, mod, p_tiles, tiles_per_seq, norm_g, router, bias):
    n = x.shape[0]
    full = lambda a: pl.BlockSpec(a.shape, lambda i: (0,) * a.ndim)
    tok = lambda w: pl.BlockSpec((TM, w), lambda i: (i, 0))
    router_p = jnp.pad(router, ((0, 0), (0, LANES - N_EXPERTS)))
    bias_p = jnp.pad(bias, (0, LANES - N_EXPERTS)).reshape(1, LANES)
    return pl.pallas_call(
        _moe_route_kernel,
        grid=(n // TM,),
        in_specs=[tok(D), pl.BlockSpec((None, 1, N_MOD * D), lambda i: (_mod_row(i, p_tiles, tiles_per_seq), 0, 0)),
                  full(norm_g.reshape(1, D)), full(router_p), full(bias_p)],
        out_specs=[pl.BlockSpec((TM * ROW_SUB, LANES), lambda i: (i, 0)), tok(LANES), tok(LANES), tok(LANES),
                   pl.BlockSpec((8, LANES), lambda i: (0, 0))],
        out_shape=[jax.ShapeDtypeStruct((n * ROW_SUB, LANES), F32), jax.ShapeDtypeStruct((n, LANES), I32),
                   jax.ShapeDtypeStruct((n, LANES), F32), jax.ShapeDtypeStruct((n, LANES), I32),
                   jax.ShapeDtypeStruct((8, LANES), F32)],
        scratch_shapes=[pltpu.VMEM((1, LANES), F32)],
        compiler_params=_cparams(1),
        name="moe_route",
    )(x, mod, norm_g.reshape(1, D), router_p, bias_p)


DISPATCH_TOK = 512


def _moe_dispatch_kernel(pad_lo_ref, pad_n_ref, dest_ref, h_ref, xs_hbm, zero_scr, sem):
    @pl.when(pl.program_id(0) == 0)
    def _():
        zero_scr[...] = jnp.zeros_like(zero_scr)

        def pad_copy(e, s):
            return pltpu.make_async_copy(zero_scr, _row_tile(xs_hbm, pad_lo_ref[e] + s), sem)

        def per_expert(e, carry):
            def issue(s, c):
                pad_copy(e, s).start()
                return c

            def drain(s, c):
                pad_copy(e, s).wait()
                return c

            lax.fori_loop(0, pad_n_ref[e], issue, 0)
            lax.fori_loop(0, pad_n_ref[e], drain, 0)
            return carry

        lax.fori_loop(0, N_EXPERTS, per_expert, 0)

    def copy(t, kq):
        return pltpu.make_async_copy(_row_tile(h_ref, t), _row_tile(xs_hbm, dest_ref[t * TOP_K + kq]), sem)

    def issue(t, carry):
        for kq in range(TOP_K):
            copy(t, kq).start(priority=kq % 2)
        return carry

    def drain(t, carry):
        for kq in range(TOP_K):
            copy(t, kq).wait()
        return carry

    lax.fori_loop(0, DISPATCH_TOK, issue, 0)
    lax.fori_loop(0, DISPATCH_TOK, drain, 0)


def _moe_dispatch(pad_lo, pad_n, dest_flat, h, n_slots):
    n = h.shape[0] // ROW_SUB
    grid_spec = pltpu.PrefetchScalarGridSpec(
        num_scalar_prefetch=2,
        grid=(n // DISPATCH_TOK,),
        in_specs=[pl.BlockSpec((DISPATCH_TOK * TOP_K,), lambda i, *_: (i,), memory_space=pltpu.SMEM),
                  pl.BlockSpec((DISPATCH_TOK * ROW_SUB, LANES), lambda i, *_: (i, 0))],
        out_specs=pl.BlockSpec(memory_space=pl.ANY),
        scratch_shapes=[pltpu.VMEM((ROW_SUB, LANES), F32), pltpu.SemaphoreType.DMA],
    )
    return pl.pallas_call(
        _moe_dispatch_kernel,
        grid_spec=grid_spec,
        out_shape=jax.ShapeDtypeStruct((n_slots * ROW_SUB, LANES), F32),
        compiler_params=_cparams(1, has_side_effects=True),
        name="moe_dispatch",
    )(pad_lo, pad_n, dest_flat, h)


def _moe_expert_kernel(be_ref, xs_ref, wgu_ref, wdn_ref, o_ref, wgu_bf, wdn_bf):
    i = pl.program_id(0)

    @pl.when((i == 0) | (be_ref[i] != be_ref[jnp.maximum(i - 1, 0)]))
    def _():
        wgu_bf[...] = wgu_ref[...].astype(BF16)
        wdn_bf[...] = wdn_ref[...].astype(BF16)

    gu = _bdot(_rows_load(xs_ref), wgu_bf[...])
    act = _silu(gu[:, 0:D_EXPERT]) * gu[:, D_EXPERT:2 * D_EXPERT]
    _rows_store(o_ref, _bdot(act, wdn_bf[...]))


def _moe_expert(block_e, xs, w_gu, w_down, layer):
    n_slots = xs.shape[0] // ROW_SUB
    n_blocks = n_slots // EXPERT_BLOCK
    rows = pl.BlockSpec((EXPERT_BLOCK * ROW_SUB, LANES), lambda i, be: (i, 0))
    grid_spec = pltpu.PrefetchScalarGridSpec(
        num_scalar_prefetch=1,
        grid=(n_blocks,),
        in_specs=[rows,
                  pl.BlockSpec((None, None, D, 2 * D_EXPERT), lambda i, be: (layer, be[i], 0, 0)),
                  pl.BlockSpec((None, None, D_EXPERT, D), lambda i, be: (layer, be[i], 0, 0))],
        out_specs=rows,
        scratch_shapes=[pltpu.VMEM((D, 2 * D_EXPERT), BF16), pltpu.VMEM((D_EXPERT, D), BF16)],
    )
    return pl.pallas_call(
        _moe_expert_kernel,
        grid_spec=grid_spec,
        out_shape=jax.ShapeDtypeStruct((n_slots * ROW_SUB, LANES), F32),
        compiler_params=_cparams(1),
        name="moe_expert",
    )(block_e, xs, w_gu, w_down)


COMBINE_TOK = 128


def _moe_combine_kernel(dest_ref, dest_next_ref, x_ref, h_ref, w_ref, mod_ref, shgu_ref, shdn_ref, gfin_ref, ys_hbm,
                        *refs, final_norm, ctx_tiles):
    o_refs, (buf, sems) = refs[:-2], refs[-2:]
    i = pl.program_id(0)
    slot = i % 2

    def copy(idx_ref, s, t, kq):
        return pltpu.make_async_copy(_row_tile(ys_hbm, idx_ref[t * TOP_K + kq]), _row_tile(buf.at[s, kq], t),
                                     sems.at[s])

    def fetch(idx_ref, s):
        def body(t, carry):
            for kq in range(TOP_K):
                copy(idx_ref, s, t, kq).start(priority=kq % 2)
            return carry
        lax.fori_loop(0, COMBINE_TOK, body, 0, unroll=2)

    @pl.when(i == 0)
    def _():
        fetch(dest_ref, slot)

    @pl.when(i + 1 < pl.num_programs(0))
    def _():
        fetch(dest_next_ref, 1 - slot)

    gu = _bdot(_rows_load(h_ref), shgu_ref[...])
    acc = _bdot(_silu(gu[:, 0:D_EXPERT]) * gu[:, D_EXPERT:2 * D_EXPERT], shdn_ref[...])

    def drain(t, carry):
        for kq in range(TOP_K):
            copy(dest_ref, slot, t, kq).wait()
        return carry

    lax.fori_loop(0, COMBINE_TOK, drain, 0)
    w = w_ref[...]
    for kq in range(TOP_K):
        acc = acc + _rows_load(buf.at[slot, kq]) * w[:, kq:kq + 1]
    out = x_ref[...] + mod_ref[:, 5 * D:6 * D] * acc
    if final_norm:
        out = out * lax.rsqrt(jnp.mean(out * out, axis=-1, keepdims=True) + NORM_EPS) * gfin_ref[...]

        @pl.when(i < ctx_tiles)
        def _():
            o_refs[0][...] = out

        @pl.when(i >= ctx_tiles)
        def _():
            o_refs[1][...] = out
    else:
        o_refs[0][...] = out


def _moe_combine(dest_flat, x, h, w_cols, mod, p_tiles, tiles_per_seq, sh_gu, sh_down, g_final, ys, final_norm):
    n = x.shape[0]
    ratio = TM // COMBINE_TOK
    tok = lambda w: pl.BlockSpec((COMBINE_TOK, w), lambda i: (i, 0))
    full = lambda a: pl.BlockSpec(a.shape, lambda i: (0,) * a.ndim)
    n_tiles = n // COMBINE_TOK
    ctx_tiles = p_tiles * ratio
    if final_norm:
        out_specs = [pl.BlockSpec((COMBINE_TOK, D), lambda i: (jnp.minimum(i, ctx_tiles - 1), 0)),
                     pl.BlockSpec((COMBINE_TOK, D), lambda i: (jnp.maximum(i - ctx_tiles, 0), 0))]
        out_shape = [jax.ShapeDtypeStruct((ctx_tiles * COMBINE_TOK, D), F32),
                     jax.ShapeDtypeStruct((n - ctx_tiles * COMBINE_TOK, D), F32)]
    else:
        out_specs, out_shape = tok(D), jax.ShapeDtypeStruct((n, D), F32)
    return pl.pallas_call(
        functools.partial(_moe_combine_kernel, final_norm=final_norm, ctx_tiles=ctx_tiles),
        grid=(n_tiles,),
        in_specs=[pl.BlockSpec((COMBINE_TOK * TOP_K,), lambda i: (i,), memory_space=pltpu.SMEM),
                  pl.BlockSpec((COMBINE_TOK * TOP_K,), lambda i: (jnp.minimum(i + 1, n_tiles - 1),),
                               memory_space=pltpu.SMEM),
                  tok(D), pl.BlockSpec((COMBINE_TOK * ROW_SUB, LANES), lambda i: (i, 0)), tok(LANES),
                  pl.BlockSpec((None, 1, N_MOD * D), lambda i: (_mod_row(i // ratio, p_tiles, tiles_per_seq), 0, 0)),
                  full(sh_gu), full(sh_down), full(g_final),
                  pl.BlockSpec(memory_space=pl.ANY)],
        out_specs=out_specs,
        out_shape=out_shape,
        scratch_shapes=[pltpu.VMEM((2, TOP_K, COMBINE_TOK * ROW_SUB, LANES), F32), pltpu.SemaphoreType.DMA((2,))],
        compiler_params=_cparams(1),
        name="moe_combine",
    )(dest_flat, dest_flat, x, h, w_cols, mod, sh_gu, sh_down, g_final, ys)


def _moe_layer(x, mod, p_tiles, tiles_per_seq, norm_g, router, bias, w_gu, w_down, layer, sh_gu, sh_down, g_final,
               final_norm):
    n = x.shape[0]
    h, e_cols, w_cols, p_cols, counts = _moe_route(x, mod, p_tiles, tiles_per_seq, norm_g, router, bias)
    counts = counts[0, :N_EXPERTS].astype(I32)
    padded = (counts + EXPERT_BLOCK - 1) // EXPERT_BLOCK * EXPERT_BLOCK
    pad_end = jnp.cumsum(padded)
    pad_start = pad_end - padded
    chosen = e_cols[:, :TOP_K, None] == jnp.arange(N_EXPERTS, dtype=I32)
    dest = (jnp.sum(jnp.where(chosen, pad_start, 0), axis=-1) + p_cols[:, :TOP_K]).reshape(-1)
    n_blocks = n * TOP_K // EXPERT_BLOCK + N_EXPERTS
    n_slots = n_blocks * EXPERT_BLOCK
    block_start = jnp.arange(n_blocks, dtype=I32) * EXPERT_BLOCK
    block_e = jnp.minimum(jnp.sum((pad_end[None, :] <= block_start[:, None]).astype(I32), axis=1), N_EXPERTS - 1)
    pad_n = (padded - counts).at[N_EXPERTS - 1].add(n_slots - pad_end[N_EXPERTS - 1])
    xs = _moe_dispatch(pad_start + counts, pad_n, dest, h, n_slots)
    ys = _moe_expert(block_e, xs, w_gu, w_down, layer)
    return _moe_combine(dest, x, h, w_cols, mod, p_tiles, tiles_per_seq, sh_gu, sh_down, g_final.reshape(1, D), ys,
                        final_norm)


def _pair_pack(s):
    lead = s.shape[:-3]
    s = s.reshape(lead + (N_PAIR, 2, HEAD_A, HEAD_A))
    z = jnp.zeros_like(s[..., 0, :, :])
    top = jnp.concatenate([s[..., 0, :, :], z], axis=-1)
    bot = jnp.concatenate([z, s[..., 1, :, :]], axis=-1)
    return jnp.concatenate([top, bot], axis=-2)


def _pair_unpack(s):
    a = s[..., :HEAD_A, :HEAD_A]
    b = s[..., HEAD_A:, HEAD_A:]
    return jnp.stack([a, b], axis=-3).reshape(s.shape[:-3] + (H_A, HEAD_A, HEAD_A))


def _rope_tables(rows):
    half = DK_B // 2
    n_freq = half // 2
    inv = ROPE_BASE ** (-jnp.arange(n_freq, dtype=F32) / n_freq)
    pos_r = jnp.repeat(jnp.arange(rows, dtype=F32), GRID_W)
    pos_c = jnp.tile(jnp.arange(GRID_W, dtype=F32), rows)

    def tab(pos):
        ang = pos[:, None] * inv[None, :]
        c, s = jnp.cos(ang), jnp.sin(ang)
        return jnp.concatenate([c, c], -1), jnp.concatenate([-s, s], -1)

    cr, sr = tab(pos_r)
    cc, sc = tab(pos_c)
    cos = jnp.concatenate([cr, cc], -1)
    sin = jnp.concatenate([sr, sc], -1)
    cos = jnp.concatenate([jnp.ones((TM, DK_B), F32), cos], 0)
    sin = jnp.concatenate([jnp.zeros((TM, DK_B), F32), sin], 0)
    return cos, sin


def kernel(x_prompt, x_sample, state_rwkv, state_ret, c, c_ctx, ada_w, ada_b, norm_mix, norm_ffn, norm_final, rwkv_mu, rwkv_w0, rwkv_w1, rwkv_w2, rwkv_a0, rwkv_a1, rwkv_a2, rwkv_wrkv, rwkv_wo, rwkv_g1, rwkv_g2, rwkv_k_k, rwkv_k_a, rwkv_r_k, rwkv_ln_w, rwkv_ln_b, ret_w_in, ret_w_out, ret_decay_logit, moe_router, moe_bias, moe_w_gu, moe_w_down, moe_sh_gu, moe_sh_down):
    bp, tp, _ = x_prompt.shape
    bs, ts, _ = x_sample.shape
    n_p, n_s = bp * tp, bs * ts
    assert tp == TM and ts % TM == 0 and TM % GRID_W == 0
    p_tiles = n_p // TM
    tiles_per_seq = ts // TM
    rows = ts // GRID_W
    x_ctx, x_lat = x_prompt.reshape(n_p, D), x_sample.reshape(n_s, D)

    n_cond = 16
    cond = jnp.zeros((n_cond, D), F32).at[0].set(c_ctx).at[1:1 + bs].set(c)
    mod = _modulation(cond, ada_w, ada_b).reshape(ada_w.shape[0], n_cond, 1, N_MOD * D)

    head_of = jnp.arange(D, dtype=I32) // HEAD_A
    hsum = (head_of[:, None] == jnp.arange(LANES, dtype=I32)[None, :]).astype(BF16)
    hexp = hsum.T

    gl = rwkv_g1.shape[-1]
    glp = -(-gl // LANES) * LANES
    zb = lambda a: jnp.zeros_like(a)
    wts = {
        "mu": jnp.pad(rwkv_mu[0], ((0, 8 - N_MOD), (0, 0))),
        "wrkv": rwkv_wrkv[0].astype(BF16),
        "g1": jnp.pad(rwkv_g1[0], ((0, 0), (0, glp - gl))).astype(BF16),
        "g2": jnp.pad(rwkv_g2[0], ((0, glp - gl), (0, 0))).astype(BF16),
        "w1": jnp.concatenate([rwkv_w1[0, 0], rwkv_w1[0, 1]], axis=1).astype(BF16),
        "w2": jnp.concatenate([jnp.concatenate([rwkv_w2[0, 0], zb(rwkv_w2[0, 1])], 1),
                               jnp.concatenate([zb(rwkv_w2[0, 0]), rwkv_w2[0, 1]], 1)], 0).astype(BF16),
        "w0": rwkv_w0[0].reshape(1, 2 * D),
        "a1": jnp.concatenate([rwkv_a1[0, 0], rwkv_a1[0, 1]], axis=1).astype(BF16),
        "a2": jnp.concatenate([jnp.concatenate([rwkv_a2[0, 0], zb(rwkv_a2[0, 1])], 1),
                               jnp.concatenate([zb(rwkv_a2[0, 0]), rwkv_a2[0, 1]], 1)], 0).astype(BF16),
        "a0": rwkv_a0[0].reshape(1, 2 * D),
        "k_k": rwkv_k_k[0], "k_a": rwkv_k_a[0], "r_k": rwkv_r_k[0].reshape(D),
        "hsum": hsum, "hexp": hexp,
    }
    r, v, kk, lw, kd, b, gate, bonus = _rwkv_pre(x_ctx, x_lat, mod[0], p_tiles, tiles_per_seq, norm_mix[0], wts)
    s0_lat = _pair_pack(jnp.moveaxis(state_rwkv[:, 0], 1, 0))
    y_fwd, y_bwd, s_fin = _rwkv_scan(r, v, kk, lw, kd, b, s0_lat,
                                     _ScanItems(bp, tp // RWKV_CHUNK, bs, ts // RWKV_CHUNK))
    new_state_rwkv = jnp.moveaxis(_pair_unpack(s_fin), 0, 1)[:, None]
    x = _rwkv_post(x_ctx, x_lat, y_fwd, y_bwd, gate, bonus, mod[0], p_tiles, tiles_per_seq, rwkv_ln_w[0], rwkv_ln_b[0],
                   rwkv_wo[0].astype(BF16), hsum, hexp)
    x = _moe_layer(x, mod[0], p_tiles, tiles_per_seq, norm_ffn[0], moe_router[0], moe_bias[0],
                   moe_w_gu, moe_w_down, 0, moe_sh_gu[0].astype(BF16),
                   moe_sh_down[0].astype(BF16), norm_final, False)

    cos_t, sin_t = _rope_tables(rows)
    q, k, vv, rgate = _ret_pre(x, mod[1], p_tiles, tiles_per_seq, norm_mix[1], cos_t, sin_t, ret_w_in[0].astype(BF16))
    log_gamma = jax.nn.log_sigmoid(ret_decay_logit[0].astype(F32))
    o_fwd, o_bwd, r_fin = _ret_scan(log_gamma, q, k, vv, jnp.moveaxis(state_ret[:, 0], 1, 0),
                                    _ScanItems(bp, tp // RET_CHUNK, bs, ts // RET_CHUNK))
    new_state_ret = jnp.moveaxis(r_fin, 0, 1)[:, None]
    x = _ret_post(x, o_fwd, o_bwd, rgate, mod[1], p_tiles, tiles_per_seq, ret_w_out[0].astype(BF16))
    y_ctx, y_lat = _moe_layer(x, mod[1], p_tiles, tiles_per_seq, norm_ffn[1], moe_router[1], moe_bias[1],
                              moe_w_gu, moe_w_down, 1, moe_sh_gu[1].astype(BF16),
                              moe_sh_down[1].astype(BF16), norm_final, True)

    return (y_ctx.reshape(bp, tp, D), y_lat.reshape(bs, ts, D), new_state_rwkv, new_state_ret)
```

```python
import functools

import jax
import jax.numpy as jnp
from jax import lax
from jax.experimental import pallas as pl
from jax.experimental.pallas import tpu as pltpu

F32, BF16, I32 = jnp.float32, jnp.bfloat16, jnp.int32

D = 1024
N_MOD = 6
NORM_EPS = 1e-6
GRID_W = 64
HEAD_A = 64
H_A = D // HEAD_A
LNX_EPS = 64e-5
RWKV_CHUNK = 64
N_PAIR = H_A // 2
H_B = 4
DK_B = D // H_B
DV_B = 2 * DK_B
RET_CHUNK = 128
ROPE_BASE = 10000.0
N_EXPERTS = 64
TOP_K = 8
N_GROUPS = 8
TOPK_GROUPS = 4
PER_GROUP = N_EXPERTS // N_GROUPS
D_EXPERT = 256
ROUTED_SCALE = 2.5
EXPERT_BLOCK = 512

TM = 256
LANES = 128
ROW_SUB = D // LANES
VMEM_LIMIT = 56 * 1024 * 1024


def _cparams(n_grid_axes, **kw):
    return pltpu.CompilerParams(dimension_semantics=("arbitrary",) * n_grid_axes, vmem_limit_bytes=VMEM_LIMIT, **kw)


def _bdot(a, b):
    return jnp.dot(a.astype(BF16), b.astype(BF16), preferred_element_type=F32)


def _bdot_nt(a, b):
    return lax.dot_general(a.astype(BF16), b.astype(BF16), (((1,), (1,)), ((), ())), preferred_element_type=F32)


def _bdot_tn(a, b):
    return lax.dot_general(a.astype(BF16), b.astype(BF16), (((0,), (0,)), ((), ())), preferred_element_type=F32)


def _split3(x):
    hi = x.astype(BF16)
    r1 = x - hi.astype(F32)
    mid = r1.astype(BF16)
    lo = (r1 - mid.astype(F32)).astype(BF16)
    return hi, mid, lo


def _dot_f32(a, b):
    ah, am, al = _split3(a)
    bh, bm, bl = _split3(b)
    d = lambda x, y: jnp.dot(x, y, preferred_element_type=F32)
    return d(ah, bh) + (d(ah, bm) + d(am, bh)) + (d(ah, bl) + d(al, bh) + d(am, bm))


def _dot_f32_rhs01(a, m01):
    ah, am, al = _split3(a)
    d = lambda x: jnp.dot(x, m01, preferred_element_type=F32)
    return d(ah) + d(am) + d(al)


def _dot_f32_lhs01(m01, b):
    bh, bm, bl = _split3(b)
    d = lambda x: jnp.dot(m01, x, preferred_element_type=F32)
    return d(bh) + d(bm) + d(bl)


def _norm_mod(x, g, shift, scale):
    ms = jnp.mean(x * x, axis=-1, keepdims=True)
    y = x * lax.rsqrt(ms + NORM_EPS) * g
    return y * (1.0 + scale) + shift


def _silu(x):
    return x * jax.nn.sigmoid(x)


def _rows_load(ref):
    m = ref.shape[0] // ROW_SUB
    return jnp.concatenate([ref[pl.ds(j, m, stride=ROW_SUB), :] for j in range(ROW_SUB)], axis=1)


def _rows_store(ref, val):
    m = val.shape[0]
    for j in range(ROW_SUB):
        ref[pl.ds(j, m, stride=ROW_SUB), :] = val[:, j * LANES:(j + 1) * LANES]


def _row_tile(ref, idx):
    return ref.at[pl.ds(pl.multiple_of(idx * ROW_SUB, ROW_SUB), ROW_SUB)]


def _mod_row(i, p_tiles, tiles_per_seq):
    return jnp.where(i < p_tiles, 0, 1 + (i - p_tiles) // tiles_per_seq)


def _mod_kernel(cond_ref, w_ref, b_ref, o_ref):
    o_ref[...] = _dot_f32(_silu(cond_ref[...]), w_ref[...]) + b_ref[...]


def _modulation(cond, ada_w, ada_b):
    depth, _, n6 = ada_w.shape
    tn = 1536
    return pl.pallas_call(
        _mod_kernel,
        grid=(depth, n6 // tn),
        in_specs=[
            pl.BlockSpec(cond.shape, lambda l, j: (0, 0)),
            pl.BlockSpec((None, D, tn), lambda l, j: (l, 0, j)),
            pl.BlockSpec((None, 1, tn), lambda l, j: (l, 0, j)),
        ],
        out_specs=pl.BlockSpec((None, cond.shape[0], tn), lambda l, j: (l, 0, j)),
        out_shape=jax.ShapeDtypeStruct((depth, cond.shape[0], n6), F32),
        compiler_params=_cparams(2),
        name="adaln_mod",
    )(cond, ada_w, ada_b.reshape(depth, 1, n6))


def _rwkv_pre_kernel(xc_ref, xl_ref, xu_ref, xd_ref, mod_ref, g_ref, mu_ref, wrkv_ref, g1_ref, g2_ref, w1_ref, w2_ref,
                     w0_ref, a1_ref, a2_ref, a0_ref, kk_ref, ka_ref, rk_ref, hsum_ref, hexp_ref,
                     r_o, v_o, kk_o, lw_o, kd_o, b_o, gate_o, bonus_o, *, p_tiles, tiles_per_seq):
    i = pl.program_id(0)
    is_p = i < p_tiles
    sub = (i - p_tiles) % tiles_per_seq
    mod = mod_ref[...]
    shift, scale = mod[:, 0:D], mod[:, D:2 * D]
    g = g_ref[...]
    h = _norm_mod(jnp.where(is_p, xc_ref[...], xl_ref[...]), g, shift, scale)
    hu = _norm_mod(xu_ref[...], g, shift, scale)
    hd = _norm_mod(xd_ref[...], g, shift, scale)

    q = D // 4
    row = lax.broadcasted_iota(I32, (TM, 1), 0)
    per = jnp.where(is_p, TM, GRID_W)
    pos = row & (per - 1)

    def prev1(a):
        return jnp.where(pos == 0, 0.0, pltpu.roll(a, 1, 0))

    def next1(a):
        return jnp.where(pos == per - 1, 0.0, pltpu.roll(a, TM - 1, 0))

    h0, h1, h2, h3 = (h[:, j * q:(j + 1) * q] for j in range(4))
    up = jnp.concatenate([jnp.where(sub == 0, 0.0, hu[:, 2 * q:3 * q]), h2[0:TM - GRID_W]], axis=0)
    down = jnp.concatenate([h3[GRID_W:TM], jnp.where(sub == tiles_per_seq - 1, 0.0, hd[:, 3 * q:4 * q])], axis=0)
    s0 = prev1(h0)
    s1 = jnp.where(is_p, prev1(h1), next1(h1))
    s2 = jnp.where(is_p, next1(h2), up)
    s3 = jnp.where(is_p, next1(h3), down)
    xx = jnp.concatenate([s0, s1, s2, s3], axis=1) - h

    mu = mu_ref[...]
    mix = lambda j: h + xx * mu[j:j + 1]
    r = _bdot(mix(0), wrkv_ref[0])
    k = _bdot(mix(2), wrkv_ref[1])
    v = _bdot(mix(3), wrkv_ref[2])
    gate = _bdot(jax.nn.sigmoid(_bdot(mix(5), g1_ref[...])), g2_ref[...])
    w_all = w0_ref[...] + _bdot(jnp.tanh(_bdot(mix(1), w1_ref[...])), w2_ref[...])
    a_all = jax.nn.sigmoid(a0_ref[...] + _bdot(_bdot(mix(4), a1_ref[...]), a2_ref[...]))

    hsum, hexp = hsum_ref[...], hexp_ref[...]
    head_sum = lambda t: _dot_f32_rhs01(_dot_f32_rhs01(t, hsum), hexp)

    kkr = k * kk_ref[...]
    kk = kkr / jnp.maximum(jnp.sqrt(head_sum(kkr * kkr)), 1e-12)
    ka = ka_ref[...]
    kd_sum = jnp.zeros_like(k)
    for d in range(2):
        wd = w_all[:, d * D:(d + 1) * D]
        z = -wd
        softplus = jnp.maximum(z, 0.0) + jnp.log(1.0 + jnp.exp(-jnp.abs(z)))
        lw_o[d] = -jnp.exp(-softplus - 0.5)
        a = a_all[:, d * D:(d + 1) * D]
        kd = k * (1.0 + (a - 1.0) * ka)
        kd_o[d] = kd
        b_o[d] = kk * a
        kd_sum = kd_sum + kd
    r_o[...] = r
    v_o[...] = v
    kk_o[...] = kk
    gate_o[...] = gate.astype(BF16)
    bonus_o[...] = (head_sum(r * kd_sum * rk_ref[...]) * v).astype(BF16)


def _rwkv_pre(x_ctx, x_lat, mod, p_tiles, tiles_per_seq, norm_g, wts):
    n = x_ctx.shape[0] + x_lat.shape[0]
    n_tiles = n // TM
    hb = TM // GRID_W
    n_hblk = x_lat.shape[0] // GRID_W
    lat = lambda i: jnp.maximum(i - p_tiles, 0)
    row = lambda a: a.reshape(1, -1)
    full = lambda a: pl.BlockSpec(a.shape, lambda i: (0,) * a.ndim)
    tok = pl.BlockSpec((TM, D), lambda i: (i, 0))
    tok2 = pl.BlockSpec((2, TM, D), lambda i: (0, i, 0))
    consts = [row(norm_g), wts["mu"], wts["wrkv"], wts["g1"], wts["g2"], wts["w1"], wts["w2"], wts["w0"],
              wts["a1"], wts["a2"], wts["a0"], row(wts["k_k"]), row(wts["k_a"]), row(wts["r_k"]),
              wts["hsum"], wts["hexp"]]
    return pl.pallas_call(
        functools.partial(_rwkv_pre_kernel, p_tiles=p_tiles, tiles_per_seq=tiles_per_seq),
        grid=(n_tiles,),
        in_specs=[
            pl.BlockSpec((TM, D), lambda i: (jnp.minimum(i, p_tiles - 1), 0)),
            pl.BlockSpec((TM, D), lambda i: (lat(i), 0)),
            pl.BlockSpec((GRID_W, D), lambda i: (jnp.maximum(lat(i) * hb - 1, 0), 0)),
            pl.BlockSpec((GRID_W, D), lambda i: (jnp.minimum(lat(i) * hb + hb, n_hblk - 1), 0)),
            pl.BlockSpec((None, 1, N_MOD * D), lambda i: (_mod_row(i, p_tiles, tiles_per_seq), 0, 0)),
        ] + [full(a) for a in consts],
        out_specs=[tok, tok, tok, tok2, tok2, tok2, tok, tok],
        out_shape=[jax.ShapeDtypeStruct((n, D), F32)] * 3 + [jax.ShapeDtypeStruct((2, n, D), F32)] * 3
        + [jax.ShapeDtypeStruct((n, D), BF16)] * 2,
        compiler_params=_cparams(1),
        name="rwkv_pre",
    )(x_ctx, x_lat, x_lat, x_lat, mod, *consts)


class _ScanItems:
    def __init__(self, n_ctx, ctx_chunks, n_lat, lat_chunks):
        self.n_ctx, self.ctx_chunks, self.n_lat, self.lat_chunks = n_ctx, ctx_chunks, n_lat, lat_chunks
        self.ctx_items = n_ctx * ctx_chunks
        self.n_items = self.ctx_items + n_lat * lat_chunks

    def decode(self, j):
        is_ctx = j < self.ctx_items
        jl = jnp.maximum(j - self.ctx_items, 0)
        c = jnp.where(is_ctx, j % self.ctx_chunks, jl % self.lat_chunks)
        return is_ctx, c, jnp.where(is_ctx, self.ctx_chunks, self.lat_chunks)

    def block(self, d, j):
        _, c, nc = self.decode(j)
        return jnp.where(d == 0, j, j - c + (nc - 1 - c))

    def ctx_seq(self, j):
        return jnp.minimum(j // self.ctx_chunks, self.n_ctx - 1)

    def lat_seq(self, j):
        return jnp.maximum(j - self.ctx_items, 0) // self.lat_chunks


def _rwkv_scan_kernel(rf_ref, vf_ref, kkf_ref, lwf_ref, kdf_ref, bf_ref, rb_ref, vb_ref, kkb_ref, lwb_ref, kdb_ref,
                      bb_ref, s0_ref, yf_ref, yb_ref, sf_ref, s_scr, *, items):
    is_ctx, c, nc = items.decode(pl.program_id(0))
    ch = RWKV_CHUNK

    @pl.when(c == 0)
    def _():
        s_scr[...] = jnp.where(is_ctx, 0.0, s0_ref[...])

    ti = lax.broadcasted_iota(I32, (ch, ch), 0)
    tj = lax.broadcasted_iota(I32, (ch, ch), 1)
    si = lax.broadcasted_iota(I32, (2 * ch, 2 * ch), 0)
    sj = lax.broadcasted_iota(I32, (2 * ch, 2 * ch), 1)
    same = (si < ch) == (sj < ch)
    ui, uj = si & (ch - 1), sj & (ch - 1)
    first = lax.broadcasted_iota(I32, (1, LANES), 1) < HEAD_A

    def stack(xp):
        return jnp.concatenate([jnp.where(first, xp, 0.0), jnp.where(first, 0.0, xp)], axis=0)

    def prologue(rev, r_ref, v_ref, kk_ref, lw_ref, kd_ref, b_ref):
        tri = jnp.where((tj >= ti) if rev else (tj <= ti), 1.0, 0.0).astype(BF16)
        lw = lw_ref[...]
        cum = _dot_f32_lhs01(tri, lw)
        last = cum[0:1] if rev else cum[ch - 1:ch]
        kk, kd, b = kk_ref[...], kd_ref[...], b_ref[...]
        e_neg = jnp.exp(-cum)
        e_rel = jnp.exp(last - cum)
        return dict(at=-kk * jnp.exp(cum - lw), rt=r_ref[...] * jnp.exp(cum), bt=b * e_neg, kt=kd * e_neg,
                    bh=b * e_rel, kh=kd * e_rel, g_all=jnp.exp(last), v=v_ref[...],
                    strict=same & ((uj > ui) if rev else (uj < ui)),
                    incl=same & ((uj >= ui) if rev else (uj <= ui)))

    dirs = [prologue(False, rf_ref, vf_ref, kkf_ref, lwf_ref, kdf_ref, bf_ref),
            prologue(True, rb_ref, vb_ref, kkb_ref, lwb_ref, kdb_ref, bb_ref)]

    units = [(d, p, slice(p * LANES, (p + 1) * LANES)) for d in range(2) for p in range(N_PAIR)]
    pairs = range(len(units))
    s_old = [s_scr[d, p] for d, p, _ in units]
    m1 = [_bdot_nt(jnp.concatenate([stack(dirs[d]["at"][:, sl]), stack(dirs[d]["rt"][:, sl])], axis=0),
                   jnp.concatenate([stack(dirs[d]["bt"][:, sl]), stack(dirs[d]["kt"][:, sl])], axis=0))
          for d, _, sl in units]
    m2 = [_bdot_nt(jnp.concatenate([dirs[d]["at"][:, sl], dirs[d]["rt"][:, sl]], axis=0), s_old[u])
          for u, (d, _, sl) in enumerate(units)]
    vs = [stack(dirs[d]["v"][:, sl]) for d, _, sl in units]
    strict = [dirs[d]["strict"] for d, _, _ in units]
    incl = [dirs[d]["incl"] for d, _, _ in units]
    x = [stack(m2[p][0:ch]) + _bdot(jnp.where(strict[p], m1[p][0:2 * ch, 2 * ch:4 * ch], 0.0), vs[p]) for p in pairs]
    nm = [jnp.where(strict[p], m1[p][0:2 * ch, 0:2 * ch], 0.0).astype(BF16) for p in pairs]
    for _ in range(5):
        sq_ap = [_bdot(nm[p], jnp.concatenate([nm[p], x[p].astype(BF16)], axis=1)) for p in pairs]
        nm = [sq_ap[p][:, 0:2 * ch].astype(BF16) for p in pairs]
        x = [x[p] + sq_ap[p][:, 2 * ch:4 * ch] for p in pairs]
    x = [x[p] + _bdot(nm[p], x[p]) for p in pairs]
    uv = [jnp.concatenate([x[p], vs[p]], axis=0).astype(BF16) for p in pairs]
    ys = [_bdot(jnp.concatenate([jnp.where(incl[p], m1[p][2 * ch:4 * ch, 0:2 * ch], 0.0),
                                 jnp.where(incl[p], m1[p][2 * ch:4 * ch, 2 * ch:4 * ch], 0.0)], axis=1), uv[p])
          for p in pairs]
    s_new = [s_old[u] * dirs[d]["g_all"][:, sl]
             + _bdot_tn(uv[u], jnp.concatenate([stack(dirs[d]["bh"][:, sl]), stack(dirs[d]["kh"][:, sl])], axis=0))
             for u, (d, _, sl) in enumerate(units)]
    for u, (d, _, sl) in enumerate(units):
        (yf_ref, yb_ref)[d][:, sl] = (ys[u][0:ch] + ys[u][ch:2 * ch] + m2[u][ch:2 * ch]).astype(BF16)
    for u, (d, p, _) in enumerate(units):
        s_scr[d, p] = s_new[u]

    @pl.when(is_ctx & (c == nc - 1))
    def _():
        sf_ref[...] = s_scr[...]


def _rwkv_scan(r, v, kk, lw, kd, b, s0, items):
    n = r.shape[0]
    ch = RWKV_CHUNK
    tok = lambda d: pl.BlockSpec((ch, D), lambda j: (items.block(d, j), 0))
    tokd = lambda d: pl.BlockSpec((None, ch, D), lambda j: (d, items.block(d, j), 0))
    st = lambda seq: pl.BlockSpec((2, None, N_PAIR, LANES, LANES), lambda j: (0, seq(j), 0, 0, 0))
    per_dir = lambda d: [tok(d), tok(d), tok(d), tokd(d), tokd(d), tokd(d)]
    return pl.pallas_call(
        functools.partial(_rwkv_scan_kernel, items=items),
        grid=(items.n_items,),
        in_specs=per_dir(0) + per_dir(1) + [st(items.lat_seq)],
        out_specs=[tok(0), tok(1), st(items.ctx_seq)],
        out_shape=[jax.ShapeDtypeStruct((n, D), BF16), jax.ShapeDtypeStruct((n, D), BF16),
                   jax.ShapeDtypeStruct((2, items.n_ctx, N_PAIR, LANES, LANES), F32)],
        scratch_shapes=[pltpu.VMEM((2, N_PAIR, LANES, LANES), F32)],
        compiler_params=_cparams(1),
        name="rwkv_scan",
    )(r, v, kk, lw, kd, b, r, v, kk, lw, kd, b, s0)


def _rwkv_post_kernel(xc_ref, xl_ref, yf_ref, yb_ref, gate_ref, bonus_ref, mod_ref, lnw_ref, lnb_ref, wo_ref, hsum_ref,
                      hexp_ref, o_ref, *, p_tiles):
    x = jnp.where(pl.program_id(0) < p_tiles, xc_ref[...], xl_ref[...])
    hsum, hexp = hsum_ref[...], hexp_ref[...]
    head_mean = lambda t: _dot_f32_rhs01(_dot_f32_rhs01(t, hsum), hexp) * (1.0 / HEAD_A)
    y = yf_ref[...].astype(F32) + yb_ref[...].astype(F32)
    yc = y - head_mean(y)
    yn = yc * lax.rsqrt(head_mean(yc * yc) + LNX_EPS)
    z = (yn * lnw_ref[...] + lnb_ref[...] + bonus_ref[...].astype(F32)) * gate_ref[...].astype(F32)
    o_ref[...] = x + mod_ref[:, 2 * D:3 * D] * _bdot(z, wo_ref[...])


def _rwkv_post(x_ctx, x_lat, y_fwd, y_bwd, gate, bonus, mod, p_tiles, tiles_per_seq, ln_w, ln_b, wo, hsum, hexp):
    n = x_ctx.shape[0] + x_lat.shape[0]
    full = lambda a: pl.BlockSpec(a.shape, lambda i: (0,) * a.ndim)
    tok = pl.BlockSpec((TM, D), lambda i: (i, 0))
    consts = [ln_w.reshape(1, D), ln_b.reshape(1, D), wo, hsum, hexp]
    return pl.pallas_call(
        functools.partial(_rwkv_post_kernel, p_tiles=p_tiles),
        grid=(n // TM,),
        in_specs=[pl.BlockSpec((TM, D), lambda i: (jnp.minimum(i, p_tiles - 1), 0)),
                  pl.BlockSpec((TM, D), lambda i: (jnp.maximum(i - p_tiles, 0), 0)),
                  tok, tok, tok, tok,
                  pl.BlockSpec((None, 1, N_MOD * D), lambda i: (_mod_row(i, p_tiles, tiles_per_seq), 0, 0))]
        + [full(a) for a in consts],
        out_specs=tok,
        out_shape=jax.ShapeDtypeStruct((n, D), F32),
        compiler_params=_cparams(1),
        name="rwkv_post",
    )(x_ctx, x_lat, y_fwd, y_bwd, gate, bonus, mod, *consts)


def _ret_pre_kernel(x_ref, mod_ref, g_ref, cos_ref, sin_ref, win_ref, q_o, k_o, v_o, gate_o):
    mod = mod_ref[...]
    h = _norm_mod(x_ref[...], g_ref[...], mod[:, 0:D], mod[:, D:2 * D])
    proj = _bdot(h, win_ref[...])
    cos, sin = cos_ref[...], sin_ref[...]

    def rope(t):
        outs = []
        for j in range(D // LANES):
            tj = t[:, j * LANES:(j + 1) * LANES]
            cj = cos[:, (j % 2) * LANES:(j % 2 + 1) * LANES]
            sj = sin[:, (j % 2) * LANES:(j % 2 + 1) * LANES]
            outs.append(tj * cj + pltpu.roll(tj, LANES // 2, 1) * sj)
        return jnp.concatenate(outs, axis=1)

    q_o[...] = rope(proj[:, 0:D]).astype(BF16)
    k_o[...] = rope(proj[:, D:2 * D] * (DK_B ** -0.5)).astype(BF16)
    v_o[...] = proj[:, 2 * D:4 * D].astype(BF16)
    gate_o[...] = proj[:, 4 * D:6 * D].astype(BF16)


def _ret_pre(x, mod, p_tiles, tiles_per_seq, norm_g, cos_t, sin_t, w_in):
    n = x.shape[0]
    full = lambda a: pl.BlockSpec(a.shape, lambda i: (0,) * a.ndim)
    tok = lambda w: pl.BlockSpec((TM, w), lambda i: (i, 0))
    tab = pl.BlockSpec((TM, DK_B), lambda i: (jnp.where(i < p_tiles, 0, 1 + (i - p_tiles) % tiles_per_seq), 0))
    return pl.pallas_call(
        _ret_pre_kernel,
        grid=(n // TM,),
        in_specs=[tok(D), pl.BlockSpec((None, 1, N_MOD * D), lambda i: (_mod_row(i, p_tiles, tiles_per_seq), 0, 0)),
                  full(norm_g.reshape(1, D)), tab, tab, full(w_in)],
        out_specs=[tok(D), tok(D), tok(2 * D), tok(2 * D)],
        out_shape=[jax.ShapeDtypeStruct((n, D), BF16), jax.ShapeDtypeStruct((n, D), BF16),
                   jax.ShapeDtypeStruct((n, 2 * D), BF16), jax.ShapeDtypeStruct((n, 2 * D), BF16)],
        compiler_params=_cparams(1),
        name="ret_pre",
    )(x, mod, norm_g.reshape(1, D), cos_t, sin_t, w_in)


def _ret_scan_kernel(lg_ref, qf_ref, kf_ref, vf_ref, qb_ref, kb_ref, vb_ref, s0_ref, of_ref, ob_ref, sf_ref, s_scr, *,
                     items):
    is_ctx, c, nc = items.decode(pl.program_id(0))
    ch = RET_CHUNK

    @pl.when(c == 0)
    def _():
        s_scr[...] = jnp.where(is_ctx, 0.0, s0_ref[...])

    ti = lax.broadcasted_iota(I32, (ch, ch), 0)
    tj = lax.broadcasted_iota(I32, (ch, ch), 1)
    io = ((qf_ref, kf_ref, vf_ref, of_ref), (qb_ref, kb_ref, vb_ref, ob_ref))
    units = [(d, hd) for d in range(2) for hd in range(H_B)]

    def decays(d, hd):
        rev = d == 1
        lg = lg_ref[d, hd]
        rel = ((tj - ti) if rev else (ti - tj)).astype(F32)
        steps_q = ((ch - ti) if rev else (ti + 1)).astype(F32)
        steps_k = (ti if rev else (ch - 1 - ti)).astype(F32)
        return dict(mask=jnp.where(rel >= 0, jnp.exp(jnp.maximum(rel, 0.0) * lg), 0.0),
                    q_dec=jnp.exp(steps_q * lg), k_dec=jnp.exp(steps_k * lg),
                    chunk_dec=jnp.exp(jnp.full((1, DV_B), float(ch), F32) * lg))

    dec = [decays(d, hd) for d, hd in units]
    qh = [io[d][0][:, hd * DK_B:(hd + 1) * DK_B] for d, hd in units]
    kh = [io[d][1][:, hd * DK_B:(hd + 1) * DK_B] for d, hd in units]
    vh = [io[d][2][:, hd * DV_B:(hd + 1) * DV_B] for d, hd in units]
    s_old = [s_scr[d, hd] for d, hd in units]
    rng = range(len(units))
    scores = [_bdot_nt(qh[u], kh[u]) * dec[u]["mask"] for u in rng]
    cross = [_bdot(qh[u], s_old[u]) * jnp.concatenate([dec[u]["q_dec"]] * (DV_B // ch), axis=1) for u in rng]
    out = [_bdot(scores[u], vh[u]) + cross[u] for u in rng]
    s_new = [s_old[u] * dec[u]["chunk_dec"]
             + _bdot_tn(kh[u].astype(F32) * jnp.concatenate([dec[u]["k_dec"]] * (DK_B // ch), axis=1), vh[u])
             for u in rng]
    for u, (d, hd) in enumerate(units):
        io[d][3][:, hd * DV_B:(hd + 1) * DV_B] = out[u].astype(BF16)
    for u, (d, hd) in enumerate(units):
        s_scr[d, hd] = s_new[u]

    @pl.when(is_ctx & (c == nc - 1))
    def _():
        sf_ref[...] = s_scr[...]


def _ret_scan(log_gamma, q, k, v, s0, items):
    n = q.shape[0]
    ch = RET_CHUNK
    tok = lambda d, w: pl.BlockSpec((ch, w), lambda j: (items.block(d, j), 0))
    st = lambda seq: pl.BlockSpec((2, None, H_B, DK_B, DV_B), lambda j: (0, seq(j), 0, 0, 0))
    per_dir = lambda d: [tok(d, D), tok(d, D), tok(d, 2 * D)]
    return pl.pallas_call(
        functools.partial(_ret_scan_kernel, items=items),
        grid=(items.n_items,),
        in_specs=[pl.BlockSpec(memory_space=pltpu.SMEM)] + per_dir(0) + per_dir(1) + [st(items.lat_seq)],
        out_specs=[tok(0, 2 * D), tok(1, 2 * D), st(items.ctx_seq)],
        out_shape=[jax.ShapeDtypeStruct((n, 2 * D), BF16), jax.ShapeDtypeStruct((n, 2 * D), BF16),
                   jax.ShapeDtypeStruct((2, items.n_ctx, H_B, DK_B, DV_B), F32)],
        scratch_shapes=[pltpu.VMEM((2, H_B, DK_B, DV_B), F32)],
        compiler_params=_cparams(1),
        name="ret_scan",
    )(log_gamma, q, k, v, q, k, v, s0)


def _ret_post_kernel(x_ref, of_ref, ob_ref, gate_ref, mod_ref, wout_ref, out_ref):
    o = of_ref[...].astype(F32) + ob_ref[...].astype(F32)
    parts = []
    for hd in range(H_B):
        oh = o[:, hd * DV_B:(hd + 1) * DV_B]
        parts.append(oh * lax.rsqrt(jnp.mean(oh * oh, axis=-1, keepdims=True) + NORM_EPS))
    y = _silu(gate_ref[...].astype(F32)) * jnp.concatenate(parts, axis=1)
    out_ref[...] = x_ref[...] + mod_ref[:, 2 * D:3 * D] * _bdot(y, wout_ref[...])


def _ret_post(x, o_fwd, o_bwd, gate, mod, p_tiles, tiles_per_seq, w_out):
    n = x.shape[0]
    tok = lambda w: pl.BlockSpec((TM, w), lambda i: (i, 0))
    return pl.pallas_call(
        _ret_post_kernel,
        grid=(n // TM,),
        in_specs=[tok(D), tok(2 * D), tok(2 * D), tok(2 * D),
                  pl.BlockSpec((None, 1, N_MOD * D), lambda i: (_mod_row(i, p_tiles, tiles_per_seq), 0, 0)),
                  pl.BlockSpec(w_out.shape, lambda i: (0, 0))],
        out_specs=tok(D),
        out_shape=jax.ShapeDtypeStruct((n, D), F32),
        compiler_params=_cparams(1),
        name="ret_post",
    )(x, o_fwd, o_bwd, gate, mod, w_out)


def _moe_route_kernel(x_ref, mod_ref, g_ref, router_ref, bias_ref, h_o, e_o, w_o, p_o, cnt_o, carry):
    i = pl.program_id(0)

    @pl.when(i == 0)
    def _():
        carry[...] = jnp.zeros_like(carry)

    mod = mod_ref[...]
    h = _norm_mod(x_ref[...], g_ref[...], mod[:, 3 * D:4 * D], mod[:, 4 * D:5 * D])
    _rows_store(h_o, h)
    lane =lax.broadcasted_iota(I32, (TM, LANES), 1)
    valid = lane < N_EXPERTS
    neg = -jnp.inf
    scores = jax.nn.sigmoid(_dot_f32(h, router_ref[...]))
    biased = jnp.where(valid, scores + bias_ref[...], neg)

    def group_reduce(t, op):
        s = 1
        while s < PER_GROUP:
            partner = jnp.where((lane & s) == 0, pltpu.roll(t, LANES - s, 1), pltpu.roll(t, s, 1))
            t = op(t, partner)
            s *= 2
        return t

    lane_f = lane.astype(F32)
    group_f = jnp.floor(lane_f * (1.0 / PER_GROUP))

    def first_lane_of_max(t):
        m = jnp.max(t, axis=-1, keepdims=True)
        return jnp.min(jnp.where(t == m, lane_f, float(LANES)), axis=-1, keepdims=True)

    m1 = group_reduce(biased, jnp.maximum)
    first1 = group_reduce(jnp.where(biased == m1, lane_f, float(LANES)), jnp.minimum)
    m2 = group_reduce(jnp.where(lane_f == first1, neg, biased), jnp.maximum)
    gscore = jnp.where(valid, m1 + m2, neg)
    cand = jnp.full((TM, LANES), neg, F32)
    for _ in range(TOPK_GROUPS):
        gsel = group_f == jnp.floor(first_lane_of_max(gscore) * (1.0 / PER_GROUP))
        cand = jnp.where(gsel, biased, cand)
        gscore = jnp.where(gsel, neg, gscore)
    hits = []
    sel01 = jnp.zeros((TM, LANES), F32)
    e_cols = jnp.zeros((TM, LANES), F32)
    for j in range(TOP_K):
        fl = first_lane_of_max(cand)
        hit = lane_f == fl
        hits.append(hit)
        sel01 = jnp.where(hit, 1.0, sel01)
        cand = jnp.where(hit, neg, cand)
        e_cols = jnp.where(lane == j, fl, e_cols)
    wsum = jnp.sum(sel01 * scores, axis=-1, keepdims=True)

    ri = lax.broadcasted_iota(I32, (TM, TM), 0)
    rj = lax.broadcasted_iota(I32, (TM, TM), 1)
    below = jnp.where(rj < ri, 1.0, 0.0).astype(BF16)
    rank = jnp.dot(below, sel01.astype(BF16), preferred_element_type=F32) + carry[...]
    carry[...] = carry[...] + jnp.sum(sel01, axis=0, keepdims=True)
    w_cols = jnp.zeros((TM, LANES), F32)
    p_cols = jnp.zeros((TM, LANES), F32)
    for j in range(TOP_K):
        wj = jnp.sum(jnp.where(hits[j], scores, 0.0), axis=-1, keepdims=True)
        w_cols = jnp.where(lane == j, wj / wsum * ROUTED_SCALE, w_cols)
        p_cols = jnp.where(lane == j, jnp.sum(jnp.where(hits[j], rank, 0.0), axis=-1, keepdims=True), p_cols)
    e_o[...] = e_cols.astype(I32)
    w_o[...] = w_cols
    p_o[...] = p_cols.astype(I32)
    cnt_o[...] = jnp.broadcast_to(carry[...], cnt_o.shape)


def _moe_route(x, mod, p_tiles, tiles_per_seq, norm_g, router, bias):
    n = x.shape[0]
    full = lambda a: pl.BlockSpec(a.shape, lambda i: (0,) * a.ndim)
    tok = lambda w: pl.BlockSpec((TM, w), lambda i: (i, 0))
    router_p = jnp.pad(router, ((0, 0), (0, LANES - N_EXPERTS)))
    bias_p = jnp.pad(bias, (0, LANES - N_EXPERTS)).reshape(1, LANES)
    return pl.pallas_call(
        _moe_route_kernel,
        grid=(n // TM,),
        in_specs=[tok(D), pl.BlockSpec((None, 1, N_MOD * D), lambda i: (_mod_row(i, p_tiles, tiles_per_seq), 0, 0)),
                  full(norm_g.reshape(1, D)), full(router_p), full(bias_p)],
        out_specs=[pl.BlockSpec((TM * ROW_SUB, LANES), lambda i: (i, 0)), tok(LANES), tok(LANES), tok(LANES),
                   pl.BlockSpec((8, LANES), lambda i: (0, 0))],
        out_shape=[jax.ShapeDtypeStruct((n * ROW_SUB, LANES), F32), jax.ShapeDtypeStruct((n, LANES), I32),
                   jax.ShapeDtypeStruct((n, LANES), F32), jax.ShapeDtypeStruct((n, LANES), I32),
                   jax.ShapeDtypeStruct((8, LANES), F32)],
        scratch_shapes=[pltpu.VMEM((1, LANES), F32)],
        compiler_params=_cparams(1),
        name="moe_route",
    )(x, mod, norm_g.reshape(1, D), router_p, bias_p)


DISPATCH_TOK = 512


def _moe_dispatch_kernel(pad_lo_ref, pad_n_ref, dest_ref, h_ref, xs_hbm, zero_scr, sem):
    @pl.when(pl.program_id(0) == 0)
    def _():
        zero_scr[...] = jnp.zeros_like(zero_scr)
        sizes = [EXPERT_BLOCK >> s for s in range(EXPERT_BLOCK.bit_length())]

        def region(e, act):
            n = pad_n_ref[e]
            off = pad_lo_ref[e]
            for sz in sizes:
                take = (n & sz) != 0

                @pl.when(take)
                def _(off=off, sz=sz):
                    dst = xs_hbm.at[pl.ds(pl.multiple_of(off * ROW_SUB, ROW_SUB), sz * ROW_SUB)]
                    act(pltpu.make_async_copy(zero_scr.at[pl.ds(0, sz * ROW_SUB)], dst, sem))

                off = off + jnp.where(take, sz, 0)

        def start_all(e, carry):
            region(e, lambda cp: cp.start())
            return carry

        def wait_all(e, carry):
            region(e, lambda cp: cp.wait())
            return carry

        lax.fori_loop(0, pad_n_ref.shape[0], start_all, 0)
        lax.fori_loop(0, pad_n_ref.shape[0], wait_all, 0)

    def copy(t, kq):
        return pltpu.make_async_copy(_row_tile(h_ref, t), _row_tile(xs_hbm, dest_ref[t * TOP_K + kq]), sem)

    def issue(t, carry):
        for kq in range(TOP_K):
            copy(t, kq).start(priority=kq % 2)
        return carry

    def drain(t, carry):
        for kq in range(TOP_K):
            copy(t, kq).wait()
        return carry

    lax.fori_loop(0, DISPATCH_TOK, issue, 0)
    lax.fori_loop(0, DISPATCH_TOK, drain, 0)


def _moe_dispatch(pad_lo, pad_n, dest_flat, h, n_slots):
    n = h.shape[0] // ROW_SUB
    grid_spec = pltpu.PrefetchScalarGridSpec(
        num_scalar_prefetch=2,
        grid=(n // DISPATCH_TOK,),
        in_specs=[pl.BlockSpec((DISPATCH_TOK * TOP_K,), lambda i, *_: (i,), memory_space=pltpu.SMEM),
                  pl.BlockSpec((DISPATCH_TOK * ROW_SUB, LANES), lambda i, *_: (i, 0))],
        out_specs=pl.BlockSpec(memory_space=pl.ANY),
        scratch_shapes=[pltpu.VMEM((EXPERT_BLOCK * ROW_SUB, LANES), F32), pltpu.SemaphoreType.DMA],
    )
    return pl.pallas_call(
        _moe_dispatch_kernel,
        grid_spec=grid_spec,
        out_shape=jax.ShapeDtypeStruct((n_slots * ROW_SUB, LANES), F32),
        compiler_params=_cparams(1, has_side_effects=True),
        name="moe_dispatch",
    )(pad_lo, pad_n, dest_flat, h)


def _moe_expert_kernel(be_ref, xs_ref, wgu_ref, wdn_ref, o_ref, wgu_bf, wdn_bf):
    i = pl.program_id(0)

    @pl.when((i == 0) | (be_ref[i] != be_ref[jnp.maximum(i - 1, 0)]))
    def _():
        wgu_bf[...] = wgu_ref[...].astype(BF16)
        wdn_bf[...] = wdn_ref[...].astype(BF16)

    gu = _bdot(_rows_load(xs_ref), wgu_bf[...])
    act = _silu(gu[:, 0:D_EXPERT]) * gu[:, D_EXPERT:2 * D_EXPERT]
    _rows_store(o_ref, _bdot(act, wdn_bf[...]))


def _moe_expert(block_e, xs, w_gu, w_down, layer):
    n_slots = xs.shape[0] // ROW_SUB
    n_blocks = n_slots // EXPERT_BLOCK
    rows = pl.BlockSpec((EXPERT_BLOCK * ROW_SUB, LANES), lambda i, be: (i, 0))
    grid_spec = pltpu.PrefetchScalarGridSpec(
        num_scalar_prefetch=1,
        grid=(n_blocks,),
        in_specs=[rows,
                  pl.BlockSpec((None, None, D, 2 * D_EXPERT), lambda i, be: (layer, be[i], 0, 0)),
                  pl.BlockSpec((None, None, D_EXPERT, D), lambda i, be: (layer, be[i], 0, 0))],
        out_specs=rows,
        scratch_shapes=[pltpu.VMEM((D, 2 * D_EXPERT), BF16), pltpu.VMEM((D_EXPERT, D), BF16)],
    )
    return pl.pallas_call(
        _moe_expert_kernel,
        grid_spec=grid_spec,
        out_shape=jax.ShapeDtypeStruct((n_slots * ROW_SUB, LANES), F32),
        compiler_params=_cparams(1),
        name="moe_expert",
    )(block_e, xs, w_gu, w_down)


COMBINE_TOK = 128


def _moe_combine_kernel(dest_ref, dest_next_ref, x_ref, h_ref, w_ref, mod_ref, shgu_ref, shdn_ref, gfin_ref, ys_hbm,
                        *refs, final_norm, ctx_tiles):
    o_refs, (buf, sems) = refs[:-2], refs[-2:]
    i = pl.program_id(0)
    slot = i % 2

    def copy(idx_ref, s, t, kq):
        return pltpu.make_async_copy(_row_tile(ys_hbm, idx_ref[t * TOP_K + kq]), _row_tile(buf.at[s, kq], t),
                                     sems.at[s])

    def fetch(idx_ref, s):
        def body(t, carry):
            for kq in range(TOP_K):
                copy(idx_ref, s, t, kq).start(priority=kq % 2)
            return carry
        lax.fori_loop(0, COMBINE_TOK, body, 0, unroll=2)

    @pl.when(i == 0)
    def _():
        fetch(dest_ref, slot)

    @pl.when(i + 1 < pl.num_programs(0))
    def _():
        fetch(dest_next_ref, 1 - slot)

    gu = _bdot(_rows_load(h_ref), shgu_ref[...])
    acc = _bdot(_silu(gu[:, 0:D_EXPERT]) * gu[:, D_EXPERT:2 * D_EXPERT], shdn_ref[...])

    def drain(t, carry):
        for kq in range(TOP_K):
            copy(dest_ref, slot, t, kq).wait()
        return carry

    lax.fori_loop(0, COMBINE_TOK, drain, 0)
    w = w_ref[...]
    for kq in range(TOP_K):
        acc = acc + _rows_load(buf.at[slot, kq]) * w[:, kq:kq + 1]
    out = x_ref[...] + mod_ref[:, 5 * D:6 * D] * acc
    if final_norm:
        out = out * lax.rsqrt(jnp.mean(out * out, axis=-1, keepdims=True) + NORM_EPS) * gfin_ref[...]

        @pl.when(i < ctx_tiles)
        def _():
            o_refs[0][...] = out

        @pl.when(i >= ctx_tiles)
        def _():
            o_refs[1][...] = out
    else:
        o_refs[0][...] = out


def _moe_combine(dest_flat, x, h, w_cols, mod, p_tiles, tiles_per_seq, sh_gu, sh_down, g_final, ys, final_norm):
    n = x.shape[0]
    ratio = TM // COMBINE_TOK
    tok = lambda w: pl.BlockSpec((COMBINE_TOK, w), lambda i: (i, 0))
    full = lambda a: pl.BlockSpec(a.shape, lambda i: (0,) * a.ndim)
    n_tiles = n // COMBINE_TOK
    ctx_tiles = p_tiles * ratio
    if final_norm:
        out_specs = [pl.BlockSpec((COMBINE_TOK, D), lambda i: (jnp.minimum(i, ctx_tiles - 1), 0)),
                     pl.BlockSpec((COMBINE_TOK, D), lambda i: (jnp.maximum(i - ctx_tiles, 0), 0))]
        out_shape = [jax.ShapeDtypeStruct((ctx_tiles * COMBINE_TOK, D), F32),
                     jax.ShapeDtypeStruct((n - ctx_tiles * COMBINE_TOK, D), F32)]
    else:
        out_specs, out_shape = tok(D), jax.ShapeDtypeStruct((n, D), F32)
    return pl.pallas_call(
        functools.partial(_moe_combine_kernel, final_norm=final_norm, ctx_tiles=ctx_tiles),
        grid=(n_tiles,),
        in_specs=[pl.BlockSpec((COMBINE_TOK * TOP_K,), lambda i: (i,), memory_space=pltpu.SMEM),
                  pl.BlockSpec((COMBINE_TOK * TOP_K,), lambda i: (jnp.minimum(i + 1, n_tiles - 1),),
                               memory_space=pltpu.SMEM),
                  tok(D), pl.BlockSpec((COMBINE_TOK * ROW_SUB, LANES), lambda i: (i, 0)), tok(LANES),
                  pl.BlockSpec((None, 1, N_MOD * D), lambda i: (_mod_row(i // ratio, p_tiles, tiles_per_seq), 0, 0)),
                  full(sh_gu), full(sh_down), full(g_final),
                  pl.BlockSpec(memory_space=pl.ANY)],
        out_specs=out_specs,
        out_shape=out_shape,
        scratch_shapes=[pltpu.VMEM((2, TOP_K, COMBINE_TOK * ROW_SUB, LANES), F32), pltpu.SemaphoreType.DMA((2,))],
        compiler_params=_cparams(1),
        name="moe_combine",
    )(dest_flat, dest_flat, x, h, w_cols, mod, sh_gu, sh_down, g_final, ys)


def _moe_layer(x, mod, p_tiles, tiles_per_seq, norm_g, router, bias, w_gu, w_down, layer, sh_gu, sh_down, g_final,
               final_norm):
    n = x.shape[0]
    h, e_cols, w_cols, p_cols, counts = _moe_route(x, mod, p_tiles, tiles_per_seq, norm_g, router, bias)
    counts = counts[0, :N_EXPERTS].astype(I32)
    padded = (counts + EXPERT_BLOCK - 1) // EXPERT_BLOCK * EXPERT_BLOCK
    pad_end = jnp.cumsum(padded)
    pad_start = pad_end - padded
    chosen = e_cols[:, :TOP_K, None] == jnp.arange(N_EXPERTS, dtype=I32)
    dest = (jnp.sum(jnp.where(chosen, pad_start, 0), axis=-1) + p_cols[:, :TOP_K]).reshape(-1)
    n_blocks = n * TOP_K // EXPERT_BLOCK + N_EXPERTS
    n_slots = n_blocks * EXPERT_BLOCK
    block_start = jnp.arange(n_blocks, dtype=I32) * EXPERT_BLOCK
    block_e = jnp.minimum(jnp.sum((pad_end[None, :] <= block_start[:, None]).astype(I32), axis=1), N_EXPERTS - 1)
    tail = jnp.arange(N_EXPERTS, dtype=I32) * EXPERT_BLOCK + pad_end[N_EXPERTS - 1]
    pad_lo = jnp.concatenate([pad_start + counts, tail])
    pad_n = jnp.concatenate([padded - counts, jnp.where(tail < n_slots, EXPERT_BLOCK, 0)])
    xs = _moe_dispatch(pad_lo, pad_n, dest, h, n_slots)
    ys = _moe_expert(block_e, xs, w_gu, w_down, layer)
    return _moe_combine(dest, x, h, w_cols, mod, p_tiles, tiles_per_seq, sh_gu, sh_down, g_final.reshape(1, D), ys,
                        final_norm)


def _pair_pack(s):
    lead = s.shape[:-3]
    s = s.reshape(lead + (N_PAIR, 2, HEAD_A, HEAD_A))
    z = jnp.zeros_like(s[..., 0, :, :])
    top = jnp.concatenate([s[..., 0, :, :], z], axis=-1)
    bot = jnp.concatenate([z, s[..., 1, :, :]], axis=-1)
    return jnp.concatenate([top, bot], axis=-2)


def _pair_unpack(s):
    a = s[..., :HEAD_A, :HEAD_A]
    b = s[..., HEAD_A:, HEAD_A:]
    return jnp.stack([a, b], axis=-3).reshape(s.shape[:-3] + (H_A, HEAD_A, HEAD_A))


def _rope_tables(rows):
    half = DK_B // 2
    n_freq = half // 2
    inv = ROPE_BASE ** (-jnp.arange(n_freq, dtype=F32) / n_freq)
    pos_r = jnp.repeat(jnp.arange(rows, dtype=F32), GRID_W)
    pos_c = jnp.tile(jnp.arange(GRID_W, dtype=F32), rows)

    def tab(pos):
        ang = pos[:, None] * inv[None, :]
        c, s = jnp.cos(ang), jnp.sin(ang)
        return jnp.concatenate([c, c], -1), jnp.concatenate([-s, s], -1)

    cr, sr = tab(pos_r)
    cc, sc = tab(pos_c)
    cos = jnp.concatenate([cr, cc], -1)
    sin = jnp.concatenate([sr, sc], -1)
    cos = jnp.concatenate([jnp.ones((TM, DK_B), F32), cos], 0)
    sin = jnp.concatenate([jnp.zeros((TM, DK_B), F32), sin], 0)
    return cos, sin


def kernel(x_prompt, x_sample, state_rwkv, state_ret, c, c_ctx, ada_w, ada_b, norm_mix, norm_ffn, norm_final, rwkv_mu, rwkv_w0, rwkv_w1, rwkv_w2, rwkv_a0, rwkv_a1, rwkv_a2, rwkv_wrkv, rwkv_wo, rwkv_g1, rwkv_g2, rwkv_k_k, rwkv_k_a, rwkv_r_k, rwkv_ln_w, rwkv_ln_b, ret_w_in, ret_w_out, ret_decay_logit, moe_router, moe_bias, moe_w_gu, moe_w_down, moe_sh_gu, moe_sh_down):
    bp, tp, _ = x_prompt.shape
    bs, ts, _ = x_sample.shape
    n_p, n_s = bp * tp, bs * ts
    assert tp == TM and ts % TM == 0 and TM % GRID_W == 0
    p_tiles = n_p // TM
    tiles_per_seq = ts // TM
    rows = ts // GRID_W
    x_ctx, x_lat = x_prompt.reshape(n_p, D), x_sample.reshape(n_s, D)

    n_cond = 16
    cond = jnp.zeros((n_cond, D), F32).at[0].set(c_ctx).at[1:1 + bs].set(c)
    mod = _modulation(cond, ada_w, ada_b).reshape(ada_w.shape[0], n_cond, 1, N_MOD * D)

    head_of = jnp.arange(D, dtype=I32) // HEAD_A
    hsum = (head_of[:, None] == jnp.arange(LANES, dtype=I32)[None, :]).astype(BF16)
    hexp = hsum.T

    gl = rwkv_g1.shape[-1]
    glp = -(-gl // LANES) * LANES
    zb = lambda a: jnp.zeros_like(a)
    wts = {
        "mu": jnp.pad(rwkv_mu[0], ((0, 8 - N_MOD), (0, 0))),
        "wrkv": rwkv_wrkv[0].astype(BF16),
        "g1": jnp.pad(rwkv_g1[0], ((0, 0), (0, glp - gl))).astype(BF16),
        "g2": jnp.pad(rwkv_g2[0], ((0, glp - gl), (0, 0))).astype(BF16),
        "w1": jnp.concatenate([rwkv_w1[0, 0], rwkv_w1[0, 1]], axis=1).astype(BF16),
        "w2": jnp.concatenate([jnp.concatenate([rwkv_w2[0, 0], zb(rwkv_w2[0, 1])], 1),
                               jnp.concatenate([zb(rwkv_w2[0, 0]), rwkv_w2[0, 1]], 1)], 0).astype(BF16),
        "w0": rwkv_w0[0].reshape(1, 2 * D),
        "a1": jnp.concatenate([rwkv_a1[0, 0], rwkv_a1[0, 1]], axis=1).astype(BF16),
        "a2": jnp.concatenate([jnp.concatenate([rwkv_a2[0, 0], zb(rwkv_a2[0, 1])], 1),
                               jnp.concatenate([zb(rwkv_a2[0, 0]), rwkv_a2[0, 1]], 1)], 0).astype(BF16),
        "a0": rwkv_a0[0].reshape(1, 2 * D),
        "k_k": rwkv_k_k[0], "k_a": rwkv_k_a[0], "r_k": rwkv_r_k[0].reshape(D),
        "hsum": hsum, "hexp": hexp,
    }
    r, v, kk, lw, kd, b, gate, bonus = _rwkv_pre(x_ctx, x_lat, mod[0], p_tiles, tiles_per_seq, norm_mix[0], wts)
    s0_lat = _pair_pack(jnp.moveaxis(state_rwkv[:, 0], 1, 0))
    y_fwd, y_bwd, s_fin = _rwkv_scan(r, v, kk, lw, kd, b, s0_lat,
                                     _ScanItems(bp, tp // RWKV_CHUNK, bs, ts // RWKV_CHUNK))
    new_state_rwkv = jnp.moveaxis(_pair_unpack(s_fin), 0, 1)[:, None]
    x = _rwkv_post(x_ctx, x_lat, y_fwd, y_bwd, gate, bonus, mod[0], p_tiles, tiles_per_seq, rwkv_ln_w[0], rwkv_ln_b[0],
                   rwkv_wo[0].astype(BF16), hsum, hexp)
    x = _moe_layer(x, mod[0], p_tiles, tiles_per_seq, norm_ffn[0], moe_router[0], moe_bias[0],
                   moe_w_gu, moe_w_down, 0, moe_sh_gu[0].astype(BF16),
                   moe_sh_down[0].astype(BF16), norm_final, False)

    cos_t, sin_t = _rope_tables(rows)
    q, k, vv, rgate = _ret_pre(x, mod[1], p_tiles, tiles_per_seq, norm_mix[1], cos_t, sin_t, ret_w_in[0].astype(BF16))
    log_gamma = jax.nn.log_sigmoid(ret_decay_logit[0].astype(F32))
    o_fwd, o_bwd, r_fin = _ret_scan(log_gamma, q, k, vv, jnp.moveaxis(state_ret[:, 0], 1, 0),
                                    _ScanItems(bp, tp // RET_CHUNK, bs, ts // RET_CHUNK))
    new_state_ret = jnp.moveaxis(r_fin, 0, 1)[:, None]
    x = _ret_post(x, o_fwd, o_bwd, rgate, mod[1], p_tiles, tiles_per_seq, ret_w_out[0].astype(BF16))
    y_ctx, y_lat = _moe_layer(x, mod[1], p_tiles, tiles_per_seq, norm_ffn[1], moe_router[1], moe_bias[1],
                              moe_w_gu, moe_w_down, 1, moe_sh_gu[1].astype(BF16),
                              moe_sh_down[1].astype(BF16), norm_final, True)

    return (y_ctx.reshape(bp, tp, D), y_lat.reshape(bs, ts, D), new_state_rwkv, new_state_ret)
```

```python
import functools

import jax
import jax.numpy as jnp
from jax import lax
from jax.experimental import pallas as pl
from jax.experimental.pallas import tpu as pltpu

F32, BF16, I32 = jnp.float32, jnp.bfloat16, jnp.int32

D = 1024
N_MOD = 6
NORM_EPS = 1e-6
GRID_W = 64
HEAD_A = 64
H_A = D // HEAD_A
LNX_EPS = 64e-5
RWKV_CHUNK = 64
N_PAIR = H_A // 2
H_B = 4
DK_B = D // H_B
DV_B = 2 * DK_B
RET_CHUNK = 128
ROPE_BASE = 10000.0
N_EXPERTS = 64
TOP_K = 8
N_GROUPS = 8
TOPK_GROUPS = 4
PER_GROUP = N_EXPERTS // N_GROUPS
D_EXPERT = 256
ROUTED_SCALE = 2.5
EXPERT_BLOCK = 1024

TM = 256
LANES = 128
ROW_SUB = D // LANES
VMEM_LIMIT = 56 * 1024 * 1024


def _cparams(n_grid_axes, **kw):
    return pltpu.CompilerParams(dimension_semantics=("arbitrary",) * n_grid_axes, vmem_limit_bytes=VMEM_LIMIT, **kw)


def _bdot(a, b):
    return jnp.dot(a.astype(BF16), b.astype(BF16), preferred_element_type=F32)


def _bdot_nt(a, b):
    return lax.dot_general(a.astype(BF16), b.astype(BF16), (((1,), (1,)), ((), ())), preferred_element_type=F32)


def _bdot_tn(a, b):
    return lax.dot_general(a.astype(BF16), b.astype(BF16), (((0,), (0,)), ((), ())), preferred_element_type=F32)


def _split3(x):
    hi = x.astype(BF16)
    r1 = x - hi.astype(F32)
    mid = r1.astype(BF16)
    lo = (r1 - mid.astype(F32)).astype(BF16)
    return hi, mid, lo


def _dot_f32(a, b):
    ah, am, al = _split3(a)
    bh, bm, bl = _split3(b)
    d = lambda x, y: jnp.dot(x, y, preferred_element_type=F32)
    return d(ah, bh) + (d(ah, bm) + d(am, bh)) + (d(ah, bl) + d(al, bh) + d(am, bm))


def _dot_f32_rhs01(a, m01):
    ah, am, al = _split3(a)
    d = lambda x: jnp.dot(x, m01, preferred_element_type=F32)
    return d(ah) + d(am) + d(al)


def _dot_f32_lhs01(m01, b):
    bh, bm, bl = _split3(b)
    d = lambda x: jnp.dot(m01, x, preferred_element_type=F32)
    return d(bh) + d(bm) + d(bl)


def _norm_mod(x, g, shift, scale):
    ms = jnp.mean(x * x, axis=-1, keepdims=True)
    y = x * lax.rsqrt(ms + NORM_EPS) * g
    return y * (1.0 + scale) + shift


def _silu(x):
    return x * jax.nn.sigmoid(x)


def _rows_load(ref):
    m = ref.shape[0] // ROW_SUB
    return jnp.concatenate([ref[pl.ds(j, m, stride=ROW_SUB), :] for j in range(ROW_SUB)], axis=1)


def _rows_store(ref, val):
    m = val.shape[0]
    for j in range(ROW_SUB):
        ref[pl.ds(j, m, stride=ROW_SUB), :] = val[:, j * LANES:(j + 1) * LANES]


def _row_tile(ref, idx):
    return ref.at[pl.ds(pl.multiple_of(idx * ROW_SUB, ROW_SUB), ROW_SUB)]


def _mod_row(i, p_tiles, tiles_per_seq):
    return jnp.where(i < p_tiles, 0, 1 + (i - p_tiles) // tiles_per_seq)


def _mod_kernel(cond_ref, w_ref, b_ref, o_ref):
    o_ref[...] = _dot_f32(_silu(cond_ref[...]), w_ref[...]) + b_ref[...]


def _modulation(cond, ada_w, ada_b):
    depth, _, n6 = ada_w.shape
    tn = 1536
    return pl.pallas_call(
        _mod_kernel,
        grid=(depth, n6 // tn),
        in_specs=[
            pl.BlockSpec(cond.shape, lambda l, j: (0, 0)),
            pl.BlockSpec((None, D, tn), lambda l, j: (l, 0, j)),
            pl.BlockSpec((None, 1, tn), lambda l, j: (l, 0, j)),
        ],
        out_specs=pl.BlockSpec((None, cond.shape[0], tn), lambda l, j: (l, 0, j)),
        out_shape=jax.ShapeDtypeStruct((depth, cond.shape[0], n6), F32),
        compiler_params=_cparams(2),
        name="adaln_mod",
    )(cond, ada_w, ada_b.reshape(depth, 1, n6))


def _rwkv_pre_kernel(xc_ref, xl_ref, xu_ref, xd_ref, mod_ref, g_ref, mu_ref, wrkv_ref, g1_ref, g2_ref, w1_ref, w2_ref,
                     w0_ref, a1_ref, a2_ref, a0_ref, kk_ref, ka_ref, rk_ref, hsum_ref, hexp_ref,
                     r_o, v_o, kk_o, lw_o, kd_o, b_o, gate_o, bonus_o, *, p_tiles, tiles_per_seq):
    i = pl.program_id(0)
    is_p = i < p_tiles
    sub = (i - p_tiles) % tiles_per_seq
    mod = mod_ref[...]
    shift, scale = mod[:, 0:D], mod[:, D:2 * D]
    g = g_ref[...]
    h = _norm_mod(jnp.where(is_p, xc_ref[...], xl_ref[...]), g, shift, scale)
    hu = _norm_mod(xu_ref[...], g, shift, scale)
    hd = _norm_mod(xd_ref[...], g, shift, scale)

    q = D // 4
    row = lax.broadcasted_iota(I32, (TM, 1), 0)
    per = jnp.where(is_p, TM, GRID_W)
    pos = row & (per - 1)

    def prev1(a):
        return jnp.where(pos == 0, 0.0, pltpu.roll(a, 1, 0))

    def next1(a):
        return jnp.where(pos == per - 1, 0.0, pltpu.roll(a, TM - 1, 0))

    h0, h1, h2, h3 = (h[:, j * q:(j + 1) * q] for j in range(4))
    up = jnp.concatenate([jnp.where(sub == 0, 0.0, hu[:, 2 * q:3 * q]), h2[0:TM - GRID_W]], axis=0)
    down = jnp.concatenate([h3[GRID_W:TM], jnp.where(sub == tiles_per_seq - 1, 0.0, hd[:, 3 * q:4 * q])], axis=0)
    s0 = prev1(h0)
    s1 = jnp.where(is_p, prev1(h1), next1(h1))
    s2 = jnp.where(is_p, next1(h2), up)
    s3 = jnp.where(is_p, next1(h3), down)
    xx = jnp.concatenate([s0, s1, s2, s3], axis=1) - h

    mu = mu_ref[...]
    mix = lambda j: h + xx * mu[j:j + 1]
    r = _bdot(mix(0), wrkv_ref[0])
    k = _bdot(mix(2), wrkv_ref[1])
    v = _bdot(mix(3), wrkv_ref[2])
    gate = _bdot(jax.nn.sigmoid(_bdot(mix(5), g1_ref[...])), g2_ref[...])
    w_all = w0_ref[...] + _bdot(jnp.tanh(_bdot(mix(1), w1_ref[...])), w2_ref[...])
    a_all = jax.nn.sigmoid(a0_ref[...] + _bdot(_bdot(mix(4), a1_ref[...]), a2_ref[...]))

    hsum, hexp = hsum_ref[...], hexp_ref[...]
    head_sum = lambda t: _dot_f32_rhs01(_dot_f32_rhs01(t, hsum), hexp)

    kkr = k * kk_ref[...]
    kk = kkr / jnp.maximum(jnp.sqrt(head_sum(kkr * kkr)), 1e-12)
    ka = ka_ref[...]
    kd_sum = jnp.zeros_like(k)
    for d in range(2):
        wd = w_all[:, d * D:(d + 1) * D]
        z = -wd
        softplus = jnp.maximum(z, 0.0) + jnp.log(1.0 + jnp.exp(-jnp.abs(z)))
        lw_o[d] = -jnp.exp(-softplus - 0.5)
        a = a_all[:, d * D:(d + 1) * D]
        kd = k * (1.0 + (a - 1.0) * ka)
        kd_o[d] = kd
        b_o[d] = kk * a
        kd_sum = kd_sum + kd
    r_o[...] = r
    v_o[...] = v
    kk_o[...] = kk
    gate_o[...] = gate.astype(BF16)
    bonus_o[...] = (head_sum(r * kd_sum * rk_ref[...]) * v).astype(BF16)


def _rwkv_pre(x_ctx, x_lat, mod, p_tiles, tiles_per_seq, norm_g, wts):
    n = x_ctx.shape[0] + x_lat.shape[0]
    n_tiles = n // TM
    hb = TM // GRID_W
    n_hblk = x_lat.shape[0] // GRID_W
    lat = lambda i: jnp.maximum(i - p_tiles, 0)
    row = lambda a: a.reshape(1, -1)
    full = lambda a: pl.BlockSpec(a.shape, lambda i: (0,) * a.ndim)
    tok = pl.BlockSpec((TM, D), lambda i: (i, 0))
    tok2 = pl.BlockSpec((2, TM, D), lambda i: (0, i, 0))
    consts = [row(norm_g), wts["mu"], wts["wrkv"], wts["g1"], wts["g2"], wts["w1"], wts["w2"], wts["w0"],
              wts["a1"], wts["a2"], wts["a0"], row(wts["k_k"]), row(wts["k_a"]), row(wts["r_k"]),
              wts["hsum"], wts["hexp"]]
    return pl.pallas_call(
        functools.partial(_rwkv_pre_kernel, p_tiles=p_tiles, tiles_per_seq=tiles_per_seq),
        grid=(n_tiles,),
        in_specs=[
            pl.BlockSpec((TM, D), lambda i: (jnp.minimum(i, p_tiles - 1), 0)),
            pl.BlockSpec((TM, D), lambda i: (lat(i), 0)),
            pl.BlockSpec((GRID_W, D), lambda i: (jnp.maximum(lat(i) * hb - 1, 0), 0)),
            pl.BlockSpec((GRID_W, D), lambda i: (jnp.minimum(lat(i) * hb + hb, n_hblk - 1), 0)),
            pl.BlockSpec((None, 1, N_MOD * D), lambda i: (_mod_row(i, p_tiles, tiles_per_seq), 0, 0)),
        ] + [full(a) for a in consts],
        out_specs=[tok, tok, tok, tok2, tok2, tok2, tok, tok],
        out_shape=[jax.ShapeDtypeStruct((n, D), F32)] * 3 + [jax.ShapeDtypeStruct((2, n, D), F32)] * 3
        + [jax.ShapeDtypeStruct((n, D), BF16)] * 2,
        compiler_params=_cparams(1),
        name="rwkv_pre",
    )(x_ctx, x_lat, x_lat, x_lat, mod, *consts)


class _ScanItems:
    def __init__(self, n_ctx, ctx_chunks, n_lat, lat_chunks):
        self.n_ctx, self.ctx_chunks, self.n_lat, self.lat_chunks = n_ctx, ctx_chunks, n_lat, lat_chunks
        self.ctx_items = n_ctx * ctx_chunks
        self.n_items = self.ctx_items + n_lat * lat_chunks

    def decode(self, j):
        is_ctx = j < self.ctx_items
        jl = jnp.maximum(j - self.ctx_items, 0)
        c = jnp.where(is_ctx, j % self.ctx_chunks, jl % self.lat_chunks)
        return is_ctx, c, jnp.where(is_ctx, self.ctx_chunks, self.lat_chunks)

    def block(self, d, j):
        _, c, nc = self.decode(j)
        return jnp.where(d == 0, j, j - c + (nc - 1 - c))

    def ctx_seq(self, j):
        return jnp.minimum(j // self.ctx_chunks, self.n_ctx - 1)

    def lat_seq(self, j):
        return jnp.maximum(j - self.ctx_items, 0) // self.lat_chunks


def _rwkv_scan_kernel(rf_ref, vf_ref, kkf_ref, lwf_ref, kdf_ref, bf_ref, rb_ref, vb_ref, kkb_ref, lwb_ref, kdb_ref,
                      bb_ref, s0_ref, yf_ref, yb_ref, sf_ref, s_scr, *, items):
    is_ctx, c, nc = items.decode(pl.program_id(0))
    ch = RWKV_CHUNK

    @pl.when(c == 0)
    def _():
        s_scr[...] = jnp.where(is_ctx, 0.0, s0_ref[...])

    ti = lax.broadcasted_iota(I32, (ch, ch), 0)
    tj = lax.broadcasted_iota(I32, (ch, ch), 1)
    si = lax.broadcasted_iota(I32, (2 * ch, 2 * ch), 0)
    sj = lax.broadcasted_iota(I32, (2 * ch, 2 * ch), 1)
    same = (si < ch) == (sj < ch)
    ui, uj = si & (ch - 1), sj & (ch - 1)
    first = lax.broadcasted_iota(I32, (1, LANES), 1) < HEAD_A

    def stack(xp):
        return jnp.concatenate([jnp.where(first, xp, 0.0), jnp.where(first, 0.0, xp)], axis=0)

    def prologue(rev, r_ref, v_ref, kk_ref, lw_ref, kd_ref, b_ref):
        tri = jnp.where((tj >= ti) if rev else (tj <= ti), 1.0, 0.0).astype(BF16)
        lw = lw_ref[...]
        cum = _dot_f32_lhs01(tri, lw)
        last = cum[0:1] if rev else cum[ch - 1:ch]
        kk, kd, b = kk_ref[...], kd_ref[...], b_ref[...]
        e_neg = jnp.exp(-cum)
        e_rel = jnp.exp(last - cum)
        return dict(at=-kk * jnp.exp(cum - lw), rt=r_ref[...] * jnp.exp(cum), bt=b * e_neg, kt=kd * e_neg,
                    bh=b * e_rel, kh=kd * e_rel, g_all=jnp.exp(last), v=v_ref[...],
                    strict=same & ((uj > ui) if rev else (uj < ui)),
                    incl=same & ((uj >= ui) if rev else (uj <= ui)))

    dirs = [prologue(False, rf_ref, vf_ref, kkf_ref, lwf_ref, kdf_ref, bf_ref),
            prologue(True, rb_ref, vb_ref, kkb_ref, lwb_ref, kdb_ref, bb_ref)]

    units = [(d, p, slice(p * LANES, (p + 1) * LANES)) for d in range(2) for p in range(N_PAIR)]
    pairs = range(len(units))
    s_old = [s_scr[d, p] for d, p, _ in units]
    m1 = [_bdot_nt(jnp.concatenate([stack(dirs[d]["at"][:, sl]), stack(dirs[d]["rt"][:, sl])], axis=0),
                   jnp.concatenate([stack(dirs[d]["bt"][:, sl]), stack(dirs[d]["kt"][:, sl])], axis=0))
          for d, _, sl in units]
    m2 = [_bdot_nt(jnp.concatenate([dirs[d]["at"][:, sl], dirs[d]["rt"][:, sl]], axis=0), s_old[u])
          for u, (d, _, sl) in enumerate(units)]
    vs = [stack(dirs[d]["v"][:, sl]) for d, _, sl in units]
    strict = [dirs[d]["strict"] for d, _, _ in units]
    incl = [dirs[d]["incl"] for d, _, _ in units]
    x = [stack(m2[p][0:ch]) + _bdot(jnp.where(strict[p], m1[p][0:2 * ch, 2 * ch:4 * ch], 0.0), vs[p]) for p in pairs]
    nm = [jnp.where(strict[p], m1[p][0:2 * ch, 0:2 * ch], 0.0).astype(BF16) for p in pairs]
    for _ in range(5):
        sq_ap = [_bdot(nm[p], jnp.concatenate([nm[p], x[p].astype(BF16)], axis=1)) for p in pairs]
        nm = [sq_ap[p][:, 0:2 * ch].astype(BF16) for p in pairs]
        x = [x[p] + sq_ap[p][:, 2 * ch:4 * ch] for p in pairs]
    x = [x[p] + _bdot(nm[p], x[p]) for p in pairs]
    uv = [jnp.concatenate([x[p], vs[p]], axis=0).astype(BF16) for p in pairs]
    ys = [_bdot(jnp.concatenate([jnp.where(incl[p], m1[p][2 * ch:4 * ch, 0:2 * ch], 0.0),
                                 jnp.where(incl[p], m1[p][2 * ch:4 * ch, 2 * ch:4 * ch], 0.0)], axis=1), uv[p])
          for p in pairs]
    s_new = [s_old[u] * dirs[d]["g_all"][:, sl]
             + _bdot_tn(uv[u], jnp.concatenate([stack(dirs[d]["bh"][:, sl]), stack(dirs[d]["kh"][:, sl])], axis=0))
             for u, (d, _, sl) in enumerate(units)]
    for u, (d, _, sl) in enumerate(units):
        (yf_ref, yb_ref)[d][:, sl] = (ys[u][0:ch] + ys[u][ch:2 * ch] + m2[u][ch:2 * ch]).astype(BF16)
    for u, (d, p, _) in enumerate(units):
        s_scr[d, p] = s_new[u]

    @pl.when(is_ctx & (c == nc - 1))
    def _():
        sf_ref[...] = s_scr[...]


def _rwkv_scan(r, v, kk, lw, kd, b, s0, items):
    n = r.shape[0]
    ch = RWKV_CHUNK
    tok = lambda d: pl.BlockSpec((ch, D), lambda j: (items.block(d, j), 0))
    tokd = lambda d: pl.BlockSpec((None, ch, D), lambda j: (d, items.block(d, j), 0))
    st = lambda seq: pl.BlockSpec((2, None, N_PAIR, LANES, LANES), lambda j: (0, seq(j), 0, 0, 0))
    per_dir = lambda d: [tok(d), tok(d), tok(d), tokd(d), tokd(d), tokd(d)]
    return pl.pallas_call(
        functools.partial(_rwkv_scan_kernel, items=items),
        grid=(items.n_items,),
        in_specs=per_dir(0) + per_dir(1) + [st(items.lat_seq)],
        out_specs=[tok(0), tok(1), st(items.ctx_seq)],
        out_shape=[jax.ShapeDtypeStruct((n, D), BF16), jax.ShapeDtypeStruct((n, D), BF16),
                   jax.ShapeDtypeStruct((2, items.n_ctx, N_PAIR, LANES, LANES), F32)],
        scratch_shapes=[pltpu.VMEM((2, N_PAIR, LANES, LANES), F32)],
        compiler_params=_cparams(1),
        name="rwkv_scan",
    )(r, v, kk, lw, kd, b, r, v, kk, lw, kd, b, s0)


def _rwkv_post_kernel(xc_ref, xl_ref, yf_ref, yb_ref, gate_ref, bonus_ref, mod_ref, lnw_ref, lnb_ref, wo_ref, hsum_ref,
                      hexp_ref, o_ref, *, p_tiles):
    x = jnp.where(pl.program_id(0) < p_tiles, xc_ref[...], xl_ref[...])
    hsum, hexp = hsum_ref[...], hexp_ref[...]
    head_mean = lambda t: _dot_f32_rhs01(_dot_f32_rhs01(t, hsum), hexp) * (1.0 / HEAD_A)
    y = yf_ref[...].astype(F32) + yb_ref[...].astype(F32)
    yc = y - head_mean(y)
    yn = yc * lax.rsqrt(head_mean(yc * yc) + LNX_EPS)
    z = (yn * lnw_ref[...] + lnb_ref[...] + bonus_ref[...].astype(F32)) * gate_ref[...].astype(F32)
    o_ref[...] = x + mod_ref[:, 2 * D:3 * D] * _bdot(z, wo_ref[...])


def _rwkv_post(x_ctx, x_lat, y_fwd, y_bwd, gate, bonus, mod, p_tiles, tiles_per_seq, ln_w, ln_b, wo, hsum, hexp):
    n = x_ctx.shape[0] + x_lat.shape[0]
    full = lambda a: pl.BlockSpec(a.shape, lambda i: (0,) * a.ndim)
    tok = pl.BlockSpec((TM, D), lambda i: (i, 0))
    consts = [ln_w.reshape(1, D), ln_b.reshape(1, D), wo, hsum, hexp]
    return pl.pallas_call(
        functools.partial(_rwkv_post_kernel, p_tiles=p_tiles),
        grid=(n // TM,),
        in_specs=[pl.BlockSpec((TM, D), lambda i: (jnp.minimum(i, p_tiles - 1), 0)),
                  pl.BlockSpec((TM, D), lambda i: (jnp.maximum(i - p_tiles, 0), 0)),
                  tok, tok, tok, tok,
                  pl.BlockSpec((None, 1, N_MOD * D), lambda i: (_mod_row(i, p_tiles, tiles_per_seq), 0, 0))]
        + [full(a) for a in consts],
        out_specs=tok,
        out_shape=jax.ShapeDtypeStruct((n, D), F32),
        compiler_params=_cparams(1),
        name="rwkv_post",
    )(x_ctx, x_lat, y_fwd, y_bwd, gate, bonus, mod, *consts)


def _ret_pre_kernel(x_ref, mod_ref, g_ref, cos_ref, sin_ref, win_ref, q_o, k_o, v_o, gate_o):
    mod = mod_ref[...]
    h = _norm_mod(x_ref[...], g_ref[...], mod[:, 0:D], mod[:, D:2 * D])
    proj = _bdot(h, win_ref[...])
    cos, sin = cos_ref[...], sin_ref[...]

    def rope(t):
        outs = []
        for j in range(D // LANES):
            tj = t[:, j * LANES:(j + 1) * LANES]
            cj = cos[:, (j % 2) * LANES:(j % 2 + 1) * LANES]
            sj = sin[:, (j % 2) * LANES:(j % 2 + 1) * LANES]
            outs.append(tj * cj + pltpu.roll(tj, LANES // 2, 1) * sj)
        return jnp.concatenate(outs, axis=1)

    q_o[...] = rope(proj[:, 0:D]).astype(BF16)
    k_o[...] = rope(proj[:, D:2 * D] * (DK_B ** -0.5)).astype(BF16)
    v_o[...] = proj[:, 2 * D:4 * D].astype(BF16)
    gate_o[...] = proj[:, 4 * D:6 * D].astype(BF16)


def _ret_pre(x, mod, p_tiles, tiles_per_seq, norm_g, cos_t, sin_t, w_in):
    n = x.shape[0]
    full = lambda a: pl.BlockSpec(a.shape, lambda i: (0,) * a.ndim)
    tok = lambda w: pl.BlockSpec((TM, w), lambda i: (i, 0))
    tab = pl.BlockSpec((TM, DK_B), lambda i: (jnp.where(i < p_tiles, 0, 1 + (i - p_tiles) % tiles_per_seq), 0))
    return pl.pallas_call(
        _ret_pre_kernel,
        grid=(n // TM,),
        in_specs=[tok(D), pl.BlockSpec((None, 1, N_MOD * D), lambda i: (_mod_row(i, p_tiles, tiles_per_seq), 0, 0)),
                  full(norm_g.reshape(1, D)), tab, tab, full(w_in)],
        out_specs=[tok(D), tok(D), tok(2 * D), tok(2 * D)],
        out_shape=[jax.ShapeDtypeStruct((n, D), BF16), jax.ShapeDtypeStruct((n, D), BF16),
                   jax.ShapeDtypeStruct((n, 2 * D), BF16), jax.ShapeDtypeStruct((n, 2 * D), BF16)],
        compiler_params=_cparams(1),
        name="ret_pre",
    )(x, mod, norm_g.reshape(1, D), cos_t, sin_t, w_in)


def _ret_scan_kernel(lg_ref, qf_ref, kf_ref, vf_ref, qb_ref, kb_ref, vb_ref, s0_ref, of_ref, ob_ref, sf_ref, s_scr, *,
                     items):
    is_ctx, c, nc = items.decode(pl.program_id(0))
    ch = RET_CHUNK

    @pl.when(c == 0)
    def _():
        s_scr[...] = jnp.where(is_ctx, 0.0, s0_ref[...])

    ti = lax.broadcasted_iota(I32, (ch, ch), 0)
    tj = lax.broadcasted_iota(I32, (ch, ch), 1)
    io = ((qf_ref, kf_ref, vf_ref, of_ref), (qb_ref, kb_ref, vb_ref, ob_ref))
    units = [(d, hd) for d in range(2) for hd in range(H_B)]

    def decays(d, hd):
        rev = d == 1
        lg = lg_ref[d, hd]
        rel = ((tj - ti) if rev else (ti - tj)).astype(F32)
        steps_q = ((ch - ti) if rev else (ti + 1)).astype(F32)
        steps_k = (ti if rev else (ch - 1 - ti)).astype(F32)
        return dict(mask=jnp.where(rel >= 0, jnp.exp(jnp.maximum(rel, 0.0) * lg), 0.0),
                    q_dec=jnp.exp(steps_q * lg), k_dec=jnp.exp(steps_k * lg),
                    chunk_dec=jnp.exp(jnp.full((1, DV_B), float(ch), F32) * lg))

    dec = [decays(d, hd) for d, hd in units]
    qh = [io[d][0][:, hd * DK_B:(hd + 1) * DK_B] for d, hd in units]
    kh = [io[d][1][:, hd * DK_B:(hd + 1) * DK_B] for d, hd in units]
    vh = [io[d][2][:, hd * DV_B:(hd + 1) * DV_B] for d, hd in units]
    s_old = [s_scr[d, hd] for d, hd in units]
    rng = range(len(units))
    scores = [_bdot_nt(qh[u], kh[u]) * dec[u]["mask"] for u in rng]
    cross = [_bdot(qh[u], s_old[u]) * jnp.concatenate([dec[u]["q_dec"]] * (DV_B // ch), axis=1) for u in rng]
    out = [_bdot(scores[u], vh[u]) + cross[u] for u in rng]
    s_new = [s_old[u] * dec[u]["chunk_dec"]
             + _bdot_tn(kh[u].astype(F32) * jnp.concatenate([dec[u]["k_dec"]] * (DK_B // ch), axis=1), vh[u])
             for u in rng]
    for u, (d, hd) in enumerate(units):
        io[d][3][:, hd * DV_B:(hd + 1) * DV_B] = out[u].astype(BF16)
    for u, (d, hd) in enumerate(units):
        s_scr[d, hd] = s_new[u]

    @pl.when(is_ctx & (c == nc - 1))
    def _():
        sf_ref[...] = s_scr[...]


def _ret_scan(log_gamma, q, k, v, s0, items):
    n = q.shape[0]
    ch = RET_CHUNK
    tok = lambda d, w: pl.BlockSpec((ch, w), lambda j: (items.block(d, j), 0))
    st = lambda seq: pl.BlockSpec((2, None, H_B, DK_B, DV_B), lambda j: (0, seq(j), 0, 0, 0))
    per_dir = lambda d: [tok(d, D), tok(d, D), tok(d, 2 * D)]
    return pl.pallas_call(
        functools.partial(_ret_scan_kernel, items=items),
        grid=(items.n_items,),
        in_specs=[pl.BlockSpec(memory_space=pltpu.SMEM)] + per_dir(0) + per_dir(1) + [st(items.lat_seq)],
        out_specs=[tok(0, 2 * D), tok(1, 2 * D), st(items.ctx_seq)],
        out_shape=[jax.ShapeDtypeStruct((n, 2 * D), BF16), jax.ShapeDtypeStruct((n, 2 * D), BF16),
                   jax.ShapeDtypeStruct((2, items.n_ctx, H_B, DK_B, DV_B), F32)],
        scratch_shapes=[pltpu.VMEM((2, H_B, DK_B, DV_B), F32)],
        compiler_params=_cparams(1),
        name="ret_scan",
    )(log_gamma, q, k, v, q, k, v, s0)


def _ret_post_kernel(x_ref, of_ref, ob_ref, gate_ref, mod_ref, wout_ref, out_ref):
    o = of_ref[...].astype(F32) + ob_ref[...].astype(F32)
    parts = []
    for hd in range(H_B):
        oh = o[:, hd * DV_B:(hd + 1) * DV_B]
        parts.append(oh * lax.rsqrt(jnp.mean(oh * oh, axis=-1, keepdims=True) + NORM_EPS))
    y = _silu(gate_ref[...].astype(F32)) * jnp.concatenate(parts, axis=1)
    out_ref[...] = x_ref[...] + mod_ref[:, 2 * D:3 * D] * _bdot(y, wout_ref[...])


def _ret_post(x, o_fwd, o_bwd, gate, mod, p_tiles, tiles_per_seq, w_out):
    n = x.shape[0]
    tok = lambda w: pl.BlockSpec((TM, w), lambda i: (i, 0))
    return pl.pallas_call(
        _ret_post_kernel,
        grid=(n // TM,),
        in_specs=[tok(D), tok(2 * D), tok(2 * D), tok(2 * D),
                  pl.BlockSpec((None, 1, N_MOD * D), lambda i: (_mod_row(i, p_tiles, tiles_per_seq), 0, 0)),
                  pl.BlockSpec(w_out.shape, lambda i: (0, 0))],
        out_specs=tok(D),
        out_shape=jax.ShapeDtypeStruct((n, D), F32),
        compiler_params=_cparams(1),
        name="ret_post",
    )(x, o_fwd, o_bwd, gate, mod, w_out)


def _moe_route_kernel(x_ref, mod_ref, g_ref, router_ref, bias_ref, h_o, e_o, w_o, p_o, cnt_o, carry):
    i = pl.program_id(0)

    @pl.when(i == 0)
    def _():
        carry[...] = jnp.zeros_like(carry)

    mod = mod_ref[...]
    h = _norm_mod(x_ref[...], g_ref[...], mod[:, 3 * D:4 * D], mod[:, 4 * D:5 * D])
    _rows_store(h_o, h)
    lane =lax.broadcasted_iota(I32, (TM, LANES), 1)
    valid = lane < N_EXPERTS
    neg = -jnp.inf
    scores = jax.nn.sigmoid(_dot_f32(h, router_ref[...]))
    biased = jnp.where(valid, scores + bias_ref[...], neg)

    def group_reduce(t, op):
        s = 1
        while s < PER_GROUP:
            partner = jnp.where((lane & s) == 0, pltpu.roll(t, LANES - s, 1), pltpu.roll(t, s, 1))
            t = op(t, partner)
            s *= 2
        return t

    lane_f = lane.astype(F32)
    group_f = jnp.floor(lane_f * (1.0 / PER_GROUP))

    def first_lane_of_max(t):
        m = jnp.max(t, axis=-1, keepdims=True)
        return jnp.min(jnp.where(t == m, lane_f, float(LANES)), axis=-1, keepdims=True)

    m1 = group_reduce(biased, jnp.maximum)
    first1 = group_reduce(jnp.where(biased == m1, lane_f, float(LANES)), jnp.minimum)
    m2 = group_reduce(jnp.where(lane_f == first1, neg, biased), jnp.maximum)
    gscore = jnp.where(valid, m1 + m2, neg)
    cand = jnp.full((TM, LANES), neg, F32)
    for _ in range(TOPK_GROUPS):
        gsel = group_f == jnp.floor(first_lane_of_max(gscore) * (1.0 / PER_GROUP))
        cand = jnp.where(gsel, biased, cand)
        gscore = jnp.where(gsel, neg, gscore)
    hits = []
    sel01 = jnp.zeros((TM, LANES), F32)
    e_cols = jnp.zeros((TM, LANES), F32)
    for j in range(TOP_K):
        fl = first_lane_of_max(cand)
        hit = lane_f == fl
        hits.append(hit)
        sel01 = jnp.where(hit, 1.0, sel01)
        cand = jnp.where(hit, neg, cand)
        e_cols = jnp.where(lane == j, fl, e_cols)
    wsum = jnp.sum(sel01 * scores, axis=-1, keepdims=True)

    ri = lax.broadcasted_iota(I32, (TM, TM), 0)
    rj = lax.broadcasted_iota(I32, (TM, TM), 1)
    below = jnp.where(rj < ri, 1.0, 0.0).astype(BF16)
    rank = jnp.dot(below, sel01.astype(BF16), preferred_element_type=F32) + carry[...]
    carry[...] = carry[...] + jnp.sum(sel01, axis=0, keepdims=True)
    w_cols = jnp.zeros((TM, LANES), F32)
    p_cols = jnp.zeros((TM, LANES), F32)
    for j in range(TOP_K):
        wj = jnp.sum(jnp.where(hits[j], scores, 0.0), axis=-1, keepdims=True)
        w_cols = jnp.where(lane == j, wj / wsum * ROUTED_SCALE, w_cols)
        p_cols = jnp.where(lane == j, jnp.sum(jnp.where(hits[j], rank, 0.0), axis=-1, keepdims=True), p_cols)
    e_o[...] = e_cols.astype(I32)
    w_o[...] = w_cols
    p_o[...] = p_cols.astype(I32)
    cnt_o[...] = jnp.broadcast_to(carry[...], cnt_o.shape)


def _moe_route(x, mod, p_tiles, tiles_per_seq, norm_g, router, bias):
    n = x.shape[0]
    full = lambda a: pl.BlockSpec(a.shape, lambda i: (0,) * a.ndim)
    tok = lambda w: pl.BlockSpec((TM, w), lambda i: (i, 0))
    router_p = jnp.pad(router, ((0, 0), (0, LANES - N_EXPERTS)))
    bias_p = jnp.pad(bias, (0, LANES - N_EXPERTS)).reshape(1, LANES)
    return pl.pallas_call(
        _moe_route_kernel,
        grid=(n // TM,),
        in_specs=[tok(D), pl.BlockSpec((None, 1, N_MOD * D), lambda i: (_mod_row(i, p_tiles, tiles_per_seq), 0, 0)),
                  full(norm_g.reshape(1, D)), full(router_p), full(bias_p)],
        out_specs=[pl.BlockSpec((TM * ROW_SUB, LANES), lambda i: (i, 0)), tok(LANES), tok(LANES), tok(LANES),
                   pl.BlockSpec((8, LANES), lambda i: (0, 0))],
        out_shape=[jax.ShapeDtypeStruct((n * ROW_SUB, LANES), F32), jax.ShapeDtypeStruct((n, LANES), I32),
                   jax.ShapeDtypeStruct((n, LANES), F32), jax.ShapeDtypeStruct((n, LANES), I32),
                   jax.ShapeDtypeStruct((8, LANES), F32)],
        scratch_shapes=[pltpu.VMEM((1, LANES), F32)],
        compiler_params=_cparams(1),
        name="moe_route",
    )(x, mod, norm_g.reshape(1, D), router_p, bias_p)


DISPATCH_TOK = 512


def _moe_dispatch_kernel(pad_lo_ref, pad_n_ref, dest_ref, h_ref, xs_hbm, zero_scr, sem):
    @pl.when(pl.program_id(0) == 0)
    def _():
        zero_scr[...] = jnp.zeros_like(zero_scr)
        sizes = [EXPERT_BLOCK >> s for s in range(EXPERT_BLOCK.bit_length())]

        def region(e, act):
            n = pad_n_ref[e]
            off = pad_lo_ref[e]
            for sz in sizes:
                take = (n & sz) != 0

                @pl.when(take)
                def _(off=off, sz=sz):
                    dst = xs_hbm.at[pl.ds(pl.multiple_of(off * ROW_SUB, ROW_SUB), sz * ROW_SUB)]
                    act(pltpu.make_async_copy(zero_scr.at[pl.ds(0, sz * ROW_SUB)], dst, sem))

                off = off + jnp.where(take, sz, 0)

        def start_all(e, carry):
            region(e, lambda cp: cp.start())
            return carry

        def wait_all(e, carry):
            region(e, lambda cp: cp.wait())
            return carry

        lax.fori_loop(0, pad_n_ref.shape[0], start_all, 0)
        lax.fori_loop(0, pad_n_ref.shape[0], wait_all, 0)

    def copy(t, kq):
        return pltpu.make_async_copy(_row_tile(h_ref, t), _row_tile(xs_hbm, dest_ref[t * TOP_K + kq]), sem)

    def issue(t, carry):
        for kq in range(TOP_K):
            copy(t, kq).start(priority=kq % 2)
        return carry

    def drain(t, carry):
        for kq in range(TOP_K):
            copy(t, kq).wait()
        return carry

    lax.fori_loop(0, DISPATCH_TOK, issue, 0)
    lax.fori_loop(0, DISPATCH_TOK, drain, 0)


def _moe_dispatch(pad_lo, pad_n, dest_flat, h, n_slots):
    n = h.shape[0] // ROW_SUB
    grid_spec = pltpu.PrefetchScalarGridSpec(
        num_scalar_prefetch=2,
        grid=(n // DISPATCH_TOK,),
        in_specs=[pl.BlockSpec((DISPATCH_TOK * TOP_K,), lambda i, *_: (i,), memory_space=pltpu.SMEM),
                  pl.BlockSpec((DISPATCH_TOK * ROW_SUB, LANES), lambda i, *_: (i, 0))],
        out_specs=pl.BlockSpec(memory_space=pl.ANY),
        scratch_shapes=[pltpu.VMEM((EXPERT_BLOCK * ROW_SUB, LANES), F32), pltpu.SemaphoreType.DMA],
    )
    return pl.pallas_call(
        _moe_dispatch_kernel,
        grid_spec=grid_spec,
        out_shape=jax.ShapeDtypeStruct((n_slots * ROW_SUB, LANES), F32),
        compiler_params=_cparams(1, has_side_effects=True),
        name="moe_dispatch",
    )(pad_lo, pad_n, dest_flat, h)


def _moe_expert_kernel(be_ref, xs_ref, wgu_ref, wdn_ref, o_ref, wgu_bf, wdn_bf):
    i = pl.program_id(0)

    @pl.when((i == 0) | (be_ref[i] != be_ref[jnp.maximum(i - 1, 0)]))
    def _():
        wgu_bf[...] = wgu_ref[...].astype(BF16)
        wdn_bf[...] = wdn_ref[...].astype(BF16)

    gu = _bdot(_rows_load(xs_ref), wgu_bf[...])
    act = _silu(gu[:, 0:D_EXPERT]) * gu[:, D_EXPERT:2 * D_EXPERT]
    _rows_store(o_ref, _bdot(act, wdn_bf[...]))


def _moe_expert(block_e, xs, w_gu, w_down, layer):
    n_slots = xs.shape[0] // ROW_SUB
    n_blocks = n_slots // EXPERT_BLOCK
    rows = pl.BlockSpec((EXPERT_BLOCK * ROW_SUB, LANES), lambda i, be: (i, 0))
    grid_spec = pltpu.PrefetchScalarGridSpec(
        num_scalar_prefetch=1,
        grid=(n_blocks,),
        in_specs=[rows,
                  pl.BlockSpec((None, None, D, 2 * D_EXPERT), lambda i, be: (layer, be[i], 0, 0)),
                  pl.BlockSpec((None, None, D_EXPERT, D), lambda i, be: (layer, be[i], 0, 0))],
        out_specs=rows,
        scratch_shapes=[pltpu.VMEM((D, 2 * D_EXPERT), BF16), pltpu.VMEM((D_EXPERT, D), BF16)],
    )
    return pl.pallas_call(
        _moe_expert_kernel,
        grid_spec=grid_spec,
        out_shape=jax.ShapeDtypeStruct((n_slots * ROW_SUB, LANES), F32),
        compiler_params=_cparams(1),
        name="moe_expert",
    )(block_e, xs, w_gu, w_down)


COMBINE_TOK = 128


def _moe_combine_kernel(dest_ref, dest_next_ref, x_ref, h_ref, w_ref, mod_ref, shgu_ref, shdn_ref, gfin_ref, ys_hbm,
                        *refs, final_norm, ctx_tiles):
    o_refs, (buf, sems) = refs[:-2], refs[-2:]
    i = pl.program_id(0)
    slot = i % 2

    def copy(idx_ref, s, t, kq):
        return pltpu.make_async_copy(_row_tile(ys_hbm, idx_ref[t * TOP_K + kq]), _row_tile(buf.at[s, kq], t),
                                     sems.at[s])

    def fetch(idx_ref, s):
        def body(t, carry):
            for kq in range(TOP_K):
                copy(idx_ref, s, t, kq).start(priority=kq % 2)
            return carry
        lax.fori_loop(0, COMBINE_TOK, body, 0, unroll=2)

    @pl.when(i == 0)
    def _():
        fetch(dest_ref, slot)

    @pl.when(i + 1 < pl.num_programs(0))
    def _():
        fetch(dest_next_ref, 1 - slot)

    gu = _bdot(_rows_load(h_ref), shgu_ref[...])
    acc = _bdot(_silu(gu[:, 0:D_EXPERT]) * gu[:, D_EXPERT:2 * D_EXPERT], shdn_ref[...])

    def drain(t, carry):
        for kq in range(TOP_K):
            copy(dest_ref, slot, t, kq).wait()
        return carry

    lax.fori_loop(0, COMBINE_TOK, drain, 0)
    w = w_ref[...]
    for kq in range(TOP_K):
        acc = acc + _rows_load(buf.at[slot, kq]) * w[:, kq:kq + 1]
    out = x_ref[...] + mod_ref[:, 5 * D:6 * D] * acc
    if final_norm:
        out = out * lax.rsqrt(jnp.mean(out * out, axis=-1, keepdims=True) + NORM_EPS) * gfin_ref[...]

        @pl.when(i < ctx_tiles)
        def _():
            o_refs[0][...] = out

        @pl.when(i >= ctx_tiles)
        def _():
            o_refs[1][...] = out
    else:
        o_refs[0][...] = out


def _moe_combine(dest_flat, x, h, w_cols, mod, p_tiles, tiles_per_seq, sh_gu, sh_down, g_final, ys, final_norm):
    n = x.shape[0]
    ratio = TM // COMBINE_TOK
    tok = lambda w: pl.BlockSpec((COMBINE_TOK, w), lambda i: (i, 0))
    full = lambda a: pl.BlockSpec(a.shape, lambda i: (0,) * a.ndim)
    n_tiles = n // COMBINE_TOK
    ctx_tiles = p_tiles * ratio
    if final_norm:
        out_specs = [pl.BlockSpec((COMBINE_TOK, D), lambda i: (jnp.minimum(i, ctx_tiles - 1), 0)),
                     pl.BlockSpec((COMBINE_TOK, D), lambda i: (jnp.maximum(i - ctx_tiles, 0), 0))]
        out_shape = [jax.ShapeDtypeStruct((ctx_tiles * COMBINE_TOK, D), F32),
                     jax.ShapeDtypeStruct((n - ctx_tiles * COMBINE_TOK, D), F32)]
    else:
        out_specs, out_shape = tok(D), jax.ShapeDtypeStruct((n, D), F32)
    return pl.pallas_call(
        functools.partial(_moe_combine_kernel, final_norm=final_norm, ctx_tiles=ctx_tiles),
        grid=(n_tiles,),
        in_specs=[pl.BlockSpec((COMBINE_TOK * TOP_K,), lambda i: (i,), memory_space=pltpu.SMEM),
                  pl.BlockSpec((COMBINE_TOK * TOP_K,), lambda i: (jnp.minimum(i + 1, n_tiles - 1),),
                               memory_space=pltpu.SMEM),
                  tok(D), pl.BlockSpec((COMBINE_TOK * ROW_SUB, LANES), lambda i: (i, 0)), tok(LANES),
                  pl.BlockSpec((None, 1, N_MOD * D), lambda i: (_mod_row(i // ratio, p_tiles, tiles_per_seq), 0, 0)),
                  full(sh_gu), full(sh_down), full(g_final),
                  pl.BlockSpec(memory_space=pl.ANY)],
        out_specs=out_specs,
        out_shape=out_shape,
        scratch_shapes=[pltpu.VMEM((2, TOP_K, COMBINE_TOK * ROW_SUB, LANES), F32), pltpu.SemaphoreType.DMA((2,))],
        compiler_params=_cparams(1),
        name="moe_combine",
    )(dest_flat, dest_flat, x, h, w_cols, mod, sh_gu, sh_down, g_final, ys)


def _moe_layer(x, mod, p_tiles, tiles_per_seq, norm_g, router, bias, w_gu, w_down, layer, sh_gu, sh_down, g_final,
               final_norm):
    n = x.shape[0]
    h, e_cols, w_cols, p_cols, counts = _moe_route(x, mod, p_tiles, tiles_per_seq, norm_g, router, bias)
    counts = counts[0, :N_EXPERTS].astype(I32)
    padded = (counts + EXPERT_BLOCK - 1) // EXPERT_BLOCK * EXPERT_BLOCK
    pad_end = jnp.cumsum(padded)
    pad_start = pad_end - padded
    chosen = e_cols[:, :TOP_K, None] == jnp.arange(N_EXPERTS, dtype=I32)
    dest = (jnp.sum(jnp.where(chosen, pad_start, 0), axis=-1) + p_cols[:, :TOP_K]).reshape(-1)
    n_blocks = n * TOP_K // EXPERT_BLOCK + N_EXPERTS
    n_slots = n_blocks * EXPERT_BLOCK
    block_start = jnp.arange(n_blocks, dtype=I32) * EXPERT_BLOCK
    block_e = jnp.minimum(jnp.sum((pad_end[None, :] <= block_start[:, None]).astype(I32), axis=1), N_EXPERTS - 1)
    tail = jnp.arange(N_EXPERTS, dtype=I32) * EXPERT_BLOCK + pad_end[N_EXPERTS - 1]
    pad_lo = jnp.concatenate([pad_start + counts, tail])
    pad_n = jnp.concatenate([padded - counts, jnp.where(tail < n_slots, EXPERT_BLOCK, 0)])
    xs = _moe_dispatch(pad_lo, pad_n, dest, h, n_slots)
    ys = _moe_expert(block_e, xs, w_gu, w_down, layer)
    return _moe_combine(dest, x, h, w_cols, mod, p_tiles, tiles_per_seq, sh_gu, sh_down, g_final.reshape(1, D), ys,
                        final_norm)


def _pair_pack(s):
    lead = s.shape[:-3]
    s = s.reshape(lead + (N_PAIR, 2, HEAD_A, HEAD_A))
    z = jnp.zeros_like(s[..., 0, :, :])
    top = jnp.concatenate([s[..., 0, :, :], z], axis=-1)
    bot = jnp.concatenate([z, s[..., 1, :, :]], axis=-1)
    return jnp.concatenate([top, bot], axis=-2)


def _pair_unpack(s):
    a = s[..., :HEAD_A, :HEAD_A]
    b = s[..., HEAD_A:, HEAD_A:]
    return jnp.stack([a, b], axis=-3).reshape(s.shape[:-3] + (H_A, HEAD_A, HEAD_A))


def _rope_tables(rows):
    half = DK_B // 2
    n_freq = half // 2
    inv = ROPE_BASE ** (-jnp.arange(n_freq, dtype=F32) / n_freq)
    pos_r = jnp.repeat(jnp.arange(rows, dtype=F32), GRID_W)
    pos_c = jnp.tile(jnp.arange(GRID_W, dtype=F32), rows)

    def tab(pos):
        ang = pos[:, None] * inv[None, :]
        c, s = jnp.cos(ang), jnp.sin(ang)
        return jnp.concatenate([c, c], -1), jnp.concatenate([-s, s], -1)

    cr, sr = tab(pos_r)
    cc, sc = tab(pos_c)
    cos = jnp.concatenate([cr, cc], -1)
    sin = jnp.concatenate([sr, sc], -1)
    cos = jnp.concatenate([jnp.ones((TM, DK_B), F32), cos], 0)
    sin = jnp.concatenate([jnp.zeros((TM, DK_B), F32), sin], 0)
    return cos, sin


def kernel(x_prompt, x_sample, state_rwkv, state_ret, c, c_ctx, ada_w, ada_b, norm_mix, norm_ffn, norm_final, rwkv_mu, rwkv_w0, rwkv_w1, rwkv_w2, rwkv_a0, rwkv_a1, rwkv_a2, rwkv_wrkv, rwkv_wo, rwkv_g1, rwkv_g2, rwkv_k_k, rwkv_k_a, rwkv_r_k, rwkv_ln_w, rwkv_ln_b, ret_w_in, ret_w_out, ret_decay_logit, moe_router, moe_bias, moe_w_gu, moe_w_down, moe_sh_gu, moe_sh_down):
    bp, tp, _ = x_prompt.shape
    bs, ts, _ = x_sample.shape
    n_p, n_s = bp * tp, bs * ts
    assert tp == TM and ts % TM == 0 and TM % GRID_W == 0
    p_tiles = n_p // TM
    tiles_per_seq = ts // TM
    rows = ts // GRID_W
    x_ctx, x_lat = x_prompt.reshape(n_p, D), x_sample.reshape(n_s, D)

    n_cond = 16
    cond = jnp.zeros((n_cond, D), F32).at[0].set(c_ctx).at[1:1 + bs].set(c)
    mod = _modulation(cond, ada_w, ada_b).reshape(ada_w.shape[0], n_cond, 1, N_MOD * D)

    head_of = jnp.arange(D, dtype=I32) // HEAD_A
    hsum = (head_of[:, None] == jnp.arange(LANES, dtype=I32)[None, :]).astype(BF16)
    hexp = hsum.T

    gl = rwkv_g1.shape[-1]
    glp = -(-gl // LANES) * LANES
    zb = lambda a: jnp.zeros_like(a)
    wts = {
        "mu": jnp.pad(rwkv_mu[0], ((0, 8 - N_MOD), (0, 0))),
        "wrkv": rwkv_wrkv[0].astype(BF16),
        "g1": jnp.pad(rwkv_g1[0], ((0, 0), (0, glp - gl))).astype(BF16),
        "g2": jnp.pad(rwkv_g2[0], ((0, glp - gl), (0, 0))).astype(BF16),
        "w1": jnp.concatenate([rwkv_w1[0, 0], rwkv_w1[0, 1]], axis=1).astype(BF16),
        "w2": jnp.concatenate([jnp.concatenate([rwkv_w2[0, 0], zb(rwkv_w2[0, 1])], 1),
                               jnp.concatenate([zb(rwkv_w2[0, 0]), rwkv_w2[0, 1]], 1)], 0).astype(BF16),
        "w0": rwkv_w0[0].reshape(1, 2 * D),
        "a1": jnp.concatenate([rwkv_a1[0, 0], rwkv_a1[0, 1]], axis=1).astype(BF16),
        "a2": jnp.concatenate([jnp.concatenate([rwkv_a2[0, 0], zb(rwkv_a2[0, 1])], 1),
                               jnp.concatenate([zb(rwkv_a2[0, 0]), rwkv_a2[0, 1]], 1)], 0).astype(BF16),
        "a0": rwkv_a0[0].reshape(1, 2 * D),
        "k_k": rwkv_k_k[0], "k_a": rwkv_k_a[0], "r_k": rwkv_r_k[0].reshape(D),
        "hsum": hsum, "hexp": hexp,
    }
    r, v, kk, lw, kd, b, gate, bonus = _rwkv_pre(x_ctx, x_lat, mod[0], p_tiles, tiles_per_seq, norm_mix[0], wts)
    s0_lat = _pair_pack(jnp.moveaxis(state_rwkv[:, 0], 1, 0))
    y_fwd, y_bwd, s_fin = _rwkv_scan(r, v, kk, lw, kd, b, s0_lat,
                                     _ScanItems(bp, tp // RWKV_CHUNK, bs, ts // RWKV_CHUNK))
    new_state_rwkv = jnp.moveaxis(_pair_unpack(s_fin), 0, 1)[:, None]
    x = _rwkv_post(x_ctx, x_lat, y_fwd, y_bwd, gate, bonus, mod[0], p_tiles, tiles_per_seq, rwkv_ln_w[0], rwkv_ln_b[0],
                   rwkv_wo[0].astype(BF16), hsum, hexp)
    x = _moe_layer(x, mod[0], p_tiles, tiles_per_seq, norm_ffn[0], moe_router[0], moe_bias[0],
                   moe_w_gu, moe_w_down, 0, moe_sh_gu[0].astype(BF16),
                   moe_sh_down[0].astype(BF16), norm_final, False)

    cos_t, sin_t = _rope_tables(rows)
    q, k, vv, rgate = _ret_pre(x, mod[1], p_tiles, tiles_per_seq, norm_mix[1], cos_t, sin_t, ret_w_in[0].astype(BF16))
    log_gamma = jax.nn.log_sigmoid(ret_decay_logit[0].astype(F32))
    o_fwd, o_bwd, r_fin = _ret_scan(log_gamma, q, k, vv, jnp.moveaxis(state_ret[:, 0], 1, 0),
                                    _ScanItems(bp, tp // RET_CHUNK, bs, ts // RET_CHUNK))
    new_state_ret = jnp.moveaxis(r_fin, 0, 1)[:, None]
    x = _ret_post(x, o_fwd, o_bwd, rgate, mod[1], p_tiles, tiles_per_seq, ret_w_out[0].astype(BF16))
    y_ctx, y_lat = _moe_layer(x, mod[1], p_tiles, tiles_per_seq, norm_ffn[1], moe_router[1], moe_bias[1],
                              moe_w_gu, moe_w_down, 1, moe_sh_gu[1].astype(BF16),
                              moe_sh_down[1].astype(BF16), norm_final, True)

    return (y_ctx.reshape(bp, tp, D), y_lat.reshape(bs, ts, D), new_state_rwkv, new_state_ret)
```
